```python
import functools
import jax
import jax.numpy as jnp
from jax import lax
import numpy as np

D_MODEL = 1024
BATCH = 4
SEQ = 4096
DEPTH = 1
DEC_BATCH = 32
DEC_SEQ = 1
PAST_LEN = 16384
PAGE_SIZE = 128

N_HEADS = 8
HEAD_DIM = 128
N_KV_HEADS = 2
GROUP = N_HEADS // N_KV_HEADS
IDX_HEADS = 8
IDX_DIM = 64
TOPK_KEYS = 256
Q_BLOCK = 128
IDX_SCALE = IDX_DIM ** -0.5 * IDX_HEADS ** -0.5
D_CONV = D_MODEL
CONV_W = 3
N_EXPERTS = 256
TOP_K = 8
N_GROUPS = 8
TOPK_GROUPS = 4
EXPERTS_PER_GROUP = N_EXPERTS // N_GROUPS
D_EXPERT = D_MODEL // 4
D_SHARED = D_EXPERT
ROUTED_SCALE = 2.5
MOE_BLOCK = 128
ALPHA = (2 * DEPTH) ** 0.25
BETA = (8 * DEPTH) ** -0.25
LN_EPS = 1e-5
NEG = -1e30
IN_WIDTHS = (N_HEADS * HEAD_DIM, N_KV_HEADS * HEAD_DIM, N_KV_HEADS * HEAD_DIM,
             IDX_HEADS * IDX_DIM, IDX_DIM, IDX_HEADS,
             D_CONV, D_CONV, D_CONV, D_MODEL, D_MODEL)
D_IN = sum(IN_WIDTHS)

kernel_name = 'hybrid_dsa_shortconv_moe_deepnorm_step'


def split_in(z):
    parts, off = [], 0
    for w in IN_WIDTHS:
        parts.append(z[..., off:off + w])
        off += w
    return parts


def layer_norm(x, g, b):
    xf = x.astype(jnp.float32)
    mu = jnp.mean(xf, axis=-1, keepdims=True)
    var = jnp.mean(jnp.square(xf - mu), axis=-1, keepdims=True)
    return ((xf - mu) * lax.rsqrt(var + LN_EPS) * g.astype(jnp.float32) + b.astype(jnp.float32)).astype(x.dtype)


def take_rows(a, idx):
    return jax.vmap(lambda ab, ib: ab[ib])(a, idx)


def sparse_attention(q, qi, wi, kidx, qpos, gather_kv, n_keep):
    B, T = q.shape[:2]
    L = kidx.shape[1]
    dots = jnp.einsum('bthd,bsd->bths', qi.astype(jnp.float32), kidx.astype(jnp.float32))
    score = jnp.einsum('bths,bth->bts', jax.nn.relu(dots), wi.astype(jnp.float32)) * IDX_SCALE
    causal = jnp.arange(L, dtype=jnp.int32)[None, :] <= qpos[:, None]
    score = jnp.where(causal[None], score, NEG)
    top_val, top_idx = lax.top_k(score, n_keep)
    valid = top_val > NEG * 0.5
    k_sel, v_sel = gather_kv(top_idx)
    qg = q.reshape(B, T, N_KV_HEADS, GROUP, HEAD_DIM).astype(jnp.float32)
    logits = jnp.einsum('btgrd,btkgd->btgrk', qg, k_sel.astype(jnp.float32)) * (HEAD_DIM ** -0.5)
    logits = jnp.where(valid[:, :, None, None, :], logits, NEG)
    p = jax.nn.softmax(logits, axis=-1)
    out = jnp.einsum('btgrk,btkgd->btgrd', p, v_sel.astype(jnp.float32))
    return out.reshape(B, T, N_HEADS * HEAD_DIM).astype(q.dtype)


def to_blocks(a):
    B, T = a.shape[:2]
    return jnp.moveaxis(a.reshape((B, T // Q_BLOCK, Q_BLOCK) + a.shape[2:]), 1, 0)


def attend_prompt(q, k, v, qi, ki, wi):
    B, T = q.shape[:2]
    n_keep = min(TOPK_KEYS, T // 4)
    gather = lambda idx: (take_rows(k, idx), take_rows(v, idx))

    def block(args):
        qb, qib, wib, start = args
        qpos = start + jnp.arange(Q_BLOCK, dtype=jnp.int32)
        return sparse_attention(qb, qib, wib, ki, qpos, gather, n_keep)

    starts = jnp.arange(T // Q_BLOCK, dtype=jnp.int32) * Q_BLOCK
    out = lax.map(block, (to_blocks(q), to_blocks(qi), to_blocks(wi), starts))
    return jnp.moveaxis(out, 0, 1).reshape(B, T, N_HEADS * HEAD_DIM)


def attend_sample(cache_k_l, cache_v_l, cache_kidx_l, page_table, q, k, v, qi, ki, wi):
    B, T = q.shape[:2]
    past = page_table.shape[1] * PAGE_SIZE
    kidx_past = cache_kidx_l[page_table].reshape(B, past, IDX_DIM)
    kidx_all = jnp.concatenate([kidx_past, ki.astype(kidx_past.dtype)], axis=1)
    n_keep = min(TOPK_KEYS, (past + T) // 4)

    def gather(idx):
        in_past = (idx < past)[..., None, None]
        ic = jnp.minimum(idx, past - 1)
        phys = jax.vmap(lambda pt, pg: pt[pg])(page_table, ic // PAGE_SIZE)
        slot = ic % PAGE_SIZE
        inew = jnp.clip(idx - past, 0, T - 1)
        k_sel = jnp.where(in_past, cache_k_l[phys, slot], take_rows(k, inew))
        v_sel = jnp.where(in_past, cache_v_l[phys, slot], take_rows(v, inew))
        return k_sel, v_sel

    qpos = past + jnp.arange(T, dtype=jnp.int32)
    return sparse_attention(q, qi, wi, kidx_all, qpos, gather, n_keep)


def route(x, router_w, router_bias):
    s = jax.nn.sigmoid(x.astype(jnp.float32) @ router_w.astype(jnp.float32))
    sb = s + router_bias.astype(jnp.float32)
    grouped = sb.reshape(-1, N_GROUPS, EXPERTS_PER_GROUP)
    group_score = lax.top_k(grouped, 2)[0].sum(-1)
    _, top_groups = lax.top_k(group_score, TOPK_GROUPS)
    group_mask = jax.nn.one_hot(top_groups, N_GROUPS, dtype=jnp.float32).sum(1) > 0
    expert_mask = jnp.repeat(group_mask, EXPERTS_PER_GROUP, axis=-1)
    _, e_idx = lax.top_k(jnp.where(expert_mask, sb, NEG), TOP_K)
    w = jnp.take_along_axis(s, e_idx, axis=-1)
    gates = w / jnp.sum(w, axis=-1, keepdims=True) * ROUTED_SCALE
    return e_idx.astype(jnp.int32), gates


def routed_experts(x, e_idx, gates, w_gate, w_up, w_down):
    T, D = x.shape
    A = T * TOP_K
    e_flat = e_idx.reshape(-1)
    tok_flat = jnp.arange(A, dtype=jnp.int32) // TOP_K
    g_flat = gates.reshape(-1)
    order = jnp.argsort(e_flat)
    e_sorted = e_flat[order]
    counts = jnp.zeros((N_EXPERTS,), jnp.int32).at[e_flat].add(1)
    padded = (counts + MOE_BLOCK - 1) // MOE_BLOCK * MOE_BLOCK
    start = jnp.cumsum(counts) - counts
    pend = jnp.cumsum(padded)
    pstart = pend - padded
    rank = jnp.arange(A, dtype=jnp.int32) - start[e_sorted]
    dest = pstart[e_sorted] + rank
    n_blocks = (A + N_EXPERTS * (MOE_BLOCK - 1) + MOE_BLOCK - 1) // MOE_BLOCK
    m_pad = n_blocks * MOE_BLOCK
    buf_tok = jnp.full((m_pad,), T, jnp.int32).at[dest].set(tok_flat[order])
    buf_gate = jnp.zeros((m_pad,), g_flat.dtype).at[dest].set(g_flat[order])
    block_start = jnp.arange(n_blocks, dtype=jnp.int32) * MOE_BLOCK
    block_e = jnp.minimum(jnp.searchsorted(pend, block_start, side='right'), N_EXPERTS - 1)
    xpad = jnp.concatenate([x, jnp.zeros((1, D), x.dtype)], axis=0)

    def block_fn(args):
        toks, e, g = args
        xb = xpad[toks]
        h = jax.nn.silu(xb @ w_gate[e]) * (xb @ w_up[e])
        return (h @ w_down[e]) * g[:, None].astype(h.dtype)

    out = lax.map(block_fn, (buf_tok.reshape(n_blocks, MOE_BLOCK), block_e,
                             buf_gate.reshape(n_blocks, MOE_BLOCK)))
    y = jnp.zeros((T + 1, D), out.dtype).at[buf_tok].add(out.reshape(m_pad, D))
    return y[:T].astype(x.dtype)


def swiglu(x, w_g, w_u, w_d):
    return (jax.nn.silu(x @ w_g) * (x @ w_u)) @ w_d


def moe(h, router_w, router_bias, w_gate, w_up, w_down, sh_g, sh_u, sh_d):
    B, T, D = h.shape
    x = h.reshape(B * T, D)
    e_idx, gates = route(x, router_w, router_bias)
    y = routed_experts(x, e_idx, gates, w_gate, w_up, w_down) + swiglu(x, sh_g, sh_u, sh_d)
    return y.reshape(B, T, D)


def layer_forward(x, attend, conv_prev, w_in, conv_w, w_o_attn, w_o_conv, w_o, ln1_g, ln1_b,
                  router_w, router_bias, moe_w_gate, moe_w_up, moe_w_down,
                  shared_w_gate, shared_w_up, shared_w_down, ln2_g, ln2_b):
    B, T, _ = x.shape
    q, k, v, qi, ki, wi, gb, gc, xv, ga, gcv = split_in(x @ w_in)
    q = q.reshape(B, T, N_HEADS, HEAD_DIM)
    k = k.reshape(B, T, N_KV_HEADS, HEAD_DIM)
    v = v.reshape(B, T, N_KV_HEADS, HEAD_DIM)
    qi = qi.reshape(B, T, IDX_HEADS, IDX_DIM)
    attn = attend(q, k, v, qi, ki, wi)
    u = gc * xv
    upad = jnp.concatenate([conv_prev.astype(u.dtype), u], axis=1)
    conv = sum(conv_w[j] * upad[:, j:j + T] for j in range(CONV_W))
    conv_out = gb * conv
    merged = jax.nn.sigmoid(ga) * (attn @ w_o_attn) + jax.nn.sigmoid(gcv) * (conv_out @ w_o_conv)
    h = layer_norm(ALPHA * x + merged @ w_o, ln1_g, ln1_b)
    f = moe(h, router_w, router_bias, moe_w_gate, moe_w_up, moe_w_down, shared_w_gate, shared_w_up, shared_w_down)
    y = layer_norm(ALPHA * h + f, ln2_g, ln2_b)
    return y, k, v, ki, upad[:, -(CONV_W - 1):]


def setup_inputs(seed: int = 0) -> dict:
    key = jax.random.key(seed)
    ks = jax.random.split(key, 24)
    n_pages = PAST_LEN // PAGE_SIZE
    n_used = DEC_BATCH * n_pages
    n_pool = (n_used * 5 + 3) // 4

    def nrm(k, shape, s=1.0):
        return jax.random.normal(k, shape, jnp.float32) * s

    return {
        'x_prompt': nrm(ks[0], (BATCH, SEQ, D_MODEL)),
        'x_sample': nrm(ks[1], (DEC_BATCH, DEC_SEQ, D_MODEL)),
        'cache_k': nrm(ks[2], (DEPTH, n_pool, PAGE_SIZE, N_KV_HEADS, HEAD_DIM)),
        'cache_v': nrm(ks[3], (DEPTH, n_pool, PAGE_SIZE, N_KV_HEADS, HEAD_DIM)),
        'cache_kidx': nrm(ks[4], (DEPTH, n_pool, PAGE_SIZE, IDX_DIM)),
        'page_table': jax.random.permutation(ks[5], n_pool)[:n_used].reshape(DEC_BATCH, n_pages).astype(jnp.int32),
        'state_conv': nrm(ks[6], (DEPTH, DEC_BATCH, CONV_W - 1, D_CONV)),
        'w_in': nrm(ks[7], (DEPTH, D_MODEL, D_IN), D_MODEL ** -0.5),
        'conv_w': nrm(ks[8], (DEPTH, CONV_W, D_CONV), CONV_W ** -0.5),
        'w_o_attn': nrm(ks[9], (DEPTH, N_HEADS * HEAD_DIM, D_MODEL), (N_HEADS * HEAD_DIM) ** -0.5),
        'w_o_conv': nrm(ks[10], (DEPTH, D_CONV, D_MODEL), D_CONV ** -0.5),
        'w_o': nrm(ks[11], (DEPTH, D_MODEL, D_MODEL), BETA * D_MODEL ** -0.5),
        'ln1_g': 1.0 + nrm(ks[12], (DEPTH, D_MODEL), 0.05),
        'ln1_b': nrm(ks[13], (DEPTH, D_MODEL), 0.02),
        'router_w': nrm(ks[14], (DEPTH, D_MODEL, N_EXPERTS), D_MODEL ** -0.5),
        'router_bias': nrm(ks[15], (DEPTH, N_EXPERTS), 0.01),
        'moe_w_gate': nrm(ks[16], (DEPTH, N_EXPERTS, D_MODEL, D_EXPERT), D_MODEL ** -0.5),
        'moe_w_up': nrm(ks[17], (DEPTH, N_EXPERTS, D_MODEL, D_EXPERT), D_MODEL ** -0.5),
        'moe_w_down': nrm(ks[18], (DEPTH, N_EXPERTS, D_EXPERT, D_MODEL), BETA * D_EXPERT ** -0.5),
        'shared_w_gate': nrm(ks[19], (DEPTH, D_MODEL, D_SHARED), D_MODEL ** -0.5),
        'shared_w_up': nrm(ks[20], (DEPTH, D_MODEL, D_SHARED), D_MODEL ** -0.5),
        'shared_w_down': nrm(ks[21], (DEPTH, D_SHARED, D_MODEL), BETA * D_SHARED ** -0.5),
        'ln2_g': 1.0 + nrm(ks[22], (DEPTH, D_MODEL), 0.05),
        'ln2_b': nrm(ks[23], (DEPTH, D_MODEL), 0.02),
    }


def reference(x_prompt, x_sample, cache_k, cache_v, cache_kidx, page_table, state_conv,
              w_in, conv_w, w_o_attn, w_o_conv, w_o, ln1_g, ln1_b, router_w, router_bias,
              moe_w_gate, moe_w_up, moe_w_down, shared_w_gate, shared_w_up, shared_w_down,
              ln2_g, ln2_b):
    xp, xs = x_prompt, x_sample
    kp_l, vp_l, ip_l, cp_l = [], [], [], []
    ks_l, vs_l, is_l, cs_l = [], [], [], []
    for l in range(DEPTH):
        params = (w_in[l], conv_w[l], w_o_attn[l], w_o_conv[l], w_o[l], ln1_g[l], ln1_b[l],
                  router_w[l], router_bias[l], moe_w_gate[l], moe_w_up[l], moe_w_down[l],
                  shared_w_gate[l], shared_w_up[l], shared_w_down[l], ln2_g[l], ln2_b[l])
        conv0 = jnp.zeros((xp.shape[0], CONV_W - 1, D_CONV), xp.dtype)
        xp, kp, vp, ip, cp = layer_forward(xp, attend_prompt, conv0, *params)
        att_s = functools.partial(attend_sample, cache_k[l], cache_v[l], cache_kidx[l], page_table)
        xs, ksn, vsn, isn, csn = layer_forward(xs, att_s, state_conv[l], *params)
        kp_l.append(kp); vp_l.append(vp); ip_l.append(ip); cp_l.append(cp)
        ks_l.append(ksn); vs_l.append(vsn); is_l.append(isn); cs_l.append(csn)
    return (xp, xs, jnp.stack(kp_l), jnp.stack(vp_l), jnp.stack(ip_l), jnp.stack(cp_l),
            jnp.stack(ks_l), jnp.stack(vs_l), jnp.stack(is_l), jnp.stack(cs_l))
```

```python
import functools

import jax
import jax.numpy as jnp
import numpy as np
from jax import lax
from jax.experimental import pallas as pl
from jax.experimental.pallas import tpu as pltpu

F32 = jnp.float32
BF16 = jnp.bfloat16
I32 = jnp.int32

D_MODEL = 1024
N_HEADS = 8
HEAD_DIM = 128
N_KV_HEADS = 2
GROUP = N_HEADS // N_KV_HEADS
IDX_HEADS = 8
IDX_DIM = 64
TOPK_KEYS = 256
IDX_SCALE = IDX_DIM ** -0.5 * IDX_HEADS ** -0.5
Q_SCALE = HEAD_DIM ** -0.5
PAGE_SIZE = 128
D_CONV = D_MODEL
CONV_W = 3
TOP_K = 8
N_GROUPS = 8
TOPK_GROUPS = 4
ROUTED_SCALE = 2.5
DEPTH = 1
ALPHA = (2 * DEPTH) ** 0.25
LN_EPS = 1e-5
NEG = -1e30
IN_WIDTHS = (N_HEADS * HEAD_DIM, N_KV_HEADS * HEAD_DIM, N_KV_HEADS * HEAD_DIM,
             IDX_HEADS * IDX_DIM, IDX_DIM, IDX_HEADS,
             D_CONV, D_CONV, D_CONV, D_MODEL, D_MODEL)

LANES = 128
SUBLANES = 8
VMEM_LIMIT_BYTES = 56 * 1024 * 1024

PROJ_ROWS = 512
ATT_Q = 256
SEARCH_ROWS = 64
TOKEN_TILE = 256
MOE_BLOCK = 256
CONV_CHUNK = 256
INT_MIN = -2 ** 31


def _np_key(v):
    b = int(np.float32(v).view(np.int32))
    return b ^ ((b >> 31) & 0x7FFFFFFF)


KEY_HALF = _np_key(NEG * 0.5)


def _sort_key(s):
    b = lax.bitcast_convert_type(s + 0.0, I32)
    return b ^ ((b >> 31) & jnp.int32(0x7FFFFFFF))


def _dot(a, b):
    return jnp.dot(a, b, preferred_element_type=F32)


def _dot_nt(a, b):
    return lax.dot_general(a, b, (((1,), (1,)), ((), ())), preferred_element_type=F32)


def _params(n_grid):
    return pltpu.CompilerParams(dimension_semantics=("arbitrary",) * n_grid,
                                vmem_limit_bytes=VMEM_LIMIT_BYTES)


def _const_spec(shape):
    nd = len(shape)
    return pl.BlockSpec(shape, lambda *_: (0,) * nd, pipeline_mode=pl.Buffered(1))


def _proj_common(xb, w1, w2, w3, wga, wgc, q_o, k_o, v_o, kv_o, qi_o, idx_o, kib_o, sga_o, sgc_o):
    nq = N_HEADS * HEAD_DIM
    nkv = N_KV_HEADS * HEAD_DIM
    z1 = _dot(xb, w1[...])
    q_o[...] = (z1[:, :nq] * Q_SCALE).astype(BF16)
    k = z1[:, nq:nq + nkv]
    v = z1[:, nq + nkv:nq + 2 * nkv]
    k_o[...] = k
    v_o[...] = v
    kv_o[:, :nkv] = k.astype(BF16)
    kv_o[:, nkv:] = v.astype(BF16)
    qi_o[...] = _dot(xb, w2[...]).astype(BF16)
    z3 = _dot(xb, w3[...])
    idx_o[...] = z3
    kib_o[...] = z3.astype(BF16)
    sga_o[...] = jax.nn.sigmoid(_dot(xb, wga[...])).astype(BF16)
    sgc_o[...] = jax.nn.sigmoid(_dot(xb, wgc[...])).astype(BF16)


def _in_proj_prompt_kernel(x_ref, w1, w2, w3, wb, wc, wx, wga, wgc, cw_ref,
                           q_o, k_o, v_o, kv_o, qi_o, idx_o, kib_o, conv_o, sga_o, sgc_o, ulast_o,
                           tail_ref):
    j = pl.program_id(1)

    @pl.when(j == 0)
    def _():
        tail_ref[...] = jnp.zeros_like(tail_ref)

    xb = x_ref[...].astype(BF16)
    _proj_common(xb, w1, w2, w3, wga, wgc, q_o, k_o, v_o, kv_o, qi_o, idx_o, kib_o, sga_o, sgc_o)
    tm = xb.shape[0]
    row = lax.broadcasted_iota(I32, (tm, CONV_CHUNK), 0)
    for c in range(D_CONV // CONV_CHUNK):
        cs = slice(c * CONV_CHUNK, (c + 1) * CONV_CHUNK)
        gb = _dot(xb, wb[:, cs])
        u = _dot(xb, wc[:, cs]) * _dot(xb, wx[:, cs])
        p1 = tail_ref[SUBLANES - 1:SUBLANES, cs]
        p2 = tail_ref[SUBLANES - 2:SUBLANES - 1, cs]
        u1 = jnp.where(row == 0, p1, pltpu.roll(u, 1, 0))
        u2 = jnp.where(row == 0, p2, jnp.where(row == 1, p1, pltpu.roll(u, 2, 0)))
        conv = cw_ref[0:1, cs] * u2 + cw_ref[1:2, cs] * u1 + cw_ref[2:3, cs] * u
        conv_o[:, cs] = (gb * conv).astype(BF16)
        tail_ref[:, cs] = u[tm - SUBLANES:, :]
    ulast_o[0] = tail_ref[...]


def _in_proj_decode_kernel(x_ref, p2_ref, p1_ref, w1, w2, w3, wb, wc, wx, wga, wgc, cw_ref,
                           q_o, k_o, v_o, kv_o, qi_o, idx_o, kib_o, conv_o, sga_o, sgc_o, u_o):
    xb = x_ref[...].astype(BF16)
    _proj_common(xb, w1, w2, w3, wga, wgc, q_o, k_o, v_o, kv_o, qi_o, idx_o, kib_o, sga_o, sgc_o)
    for c in range(D_CONV // CONV_CHUNK):
        cs = slice(c * CONV_CHUNK, (c + 1) * CONV_CHUNK)
        gb = _dot(xb, wb[:, cs])
        u = _dot(xb, wc[:, cs]) * _dot(xb, wx[:, cs])
        conv = cw_ref[0:1, cs] * p2_ref[:, cs] + cw_ref[1:2, cs] * p1_ref[:, cs] + cw_ref[2:3, cs] * u
        conv_o[:, cs] = (gb * conv).astype(BF16)
        u_o[:, cs] = u


def _prep_in_weights(w_in):
    w = w_in.astype(BF16)
    offs = np.concatenate([[0], np.cumsum(IN_WIDTHS)])
    part = [w[:, int(offs[i]):int(offs[i + 1])] for i in range(len(IN_WIDTHS))]
    q, k, v, qi, ki, wi, gb, gc, xv, ga, gcv = part
    d = w.shape[0]
    w1 = jnp.concatenate([q, k, v], axis=1)
    w2 = jnp.pad(qi.reshape(d, IDX_HEADS, IDX_DIM),
                 ((0, 0), (0, 0), (0, LANES - IDX_DIM))).reshape(d, IDX_HEADS * LANES)
    w3 = jnp.concatenate([ki, wi, jnp.zeros((d, LANES - IDX_DIM - IDX_HEADS), BF16)], axis=1)
    return (w1, w2, w3, gb, gc, xv, ga, gcv)


def _proj_out_shapes(t):
    nkv = N_KV_HEADS * HEAD_DIM
    sd = jax.ShapeDtypeStruct
    return [sd((t, N_HEADS * HEAD_DIM), BF16), sd((t, nkv), F32), sd((t, nkv), F32), sd((t, 2 * nkv), BF16),
            sd((t, IDX_HEADS * LANES), BF16), sd((t, LANES), F32), sd((t, LANES), BF16),
            sd((t, D_CONV), BF16), sd((t, D_MODEL), BF16), sd((t, D_MODEL), BF16)]


def _in_proj_prompt(x2d, weights, conv_w, nb):
    t = x2d.shape[0]
    s = t // nb
    tm = min(PROJ_ROWS, s)
    nj = s // tm
    row_spec = lambda w: pl.BlockSpec((tm, w), lambda b, j: (b * nj + j, 0))
    out_shapes = _proj_out_shapes(t) + [jax.ShapeDtypeStruct((nb, SUBLANES, D_CONV), F32)]
    out_specs = [row_spec(o.shape[1]) for o in out_shapes[:-1]]
    out_specs.append(pl.BlockSpec((1, SUBLANES, D_CONV), lambda b, j: (b, 0, 0)))
    return pl.pallas_call(
        _in_proj_prompt_kernel,
        grid=(nb, nj),
        in_specs=[row_spec(D_MODEL)] + [_const_spec(w.shape) for w in weights] + [_const_spec(conv_w.shape)],
        out_specs=out_specs,
        out_shape=out_shapes,
        scratch_shapes=[pltpu.VMEM((SUBLANES, D_CONV), F32)],
        compiler_params=_params(2),
        name="in_proj_prompt",
    )(x2d, *weights, conv_w)


def _in_proj_decode(x2d, p2, p1, weights, conv_w):
    t = x2d.shape[0]
    row_spec = lambda w: pl.BlockSpec((t, w), lambda i: (0, 0))
    out_shapes = _proj_out_shapes(t) + [jax.ShapeDtypeStruct((t, D_CONV), F32)]
    return pl.pallas_call(
        _in_proj_decode_kernel,
        grid=(1,),
        in_specs=[row_spec(D_MODEL)] * 3 + [_const_spec(w.shape) for w in weights] + [_const_spec(conv_w.shape)],
        out_specs=[row_spec(o.shape[1]) for o in out_shapes],
        out_shape=out_shapes,
        compiler_params=_params(1),
        name="in_proj_decode",
    )(x2d, p2, p1, *weights, conv_w)


def _attn_prompt_kernel(q_ref, kv_ref, qi_ref, kib_ref, idx_ref, o_ref,
                        key_ref, bias_ref, t_ref, j_ref, need_ref, m_ref, l_ref, acc_ref,
                        *, n_keep, seq_len):
    tq = q_ref.shape[0]
    i = pl.program_id(1)
    nchunk = i + 1
    nsub = tq // LANES
    lane = lax.broadcasted_iota(I32, (tq, LANES), 1)

    wi_cols = [idx_ref[:, IDX_DIM + h:IDX_DIM + h + 1] * IDX_SCALE for h in range(IDX_HEADS)]
    row_i = lax.broadcasted_iota(I32, (tq, tq), 0)
    col_i = lax.broadcasted_iota(I32, (tq, tq), 1)

    def score_chunk(c, carry):
        kic = kib_ref[pl.ds(pl.multiple_of(c * tq, tq), tq), :]
        acc = jnp.zeros((tq, tq), F32)
        for h in range(IDX_HEADS):
            d = _dot_nt(qi_ref[:, h * LANES:(h + 1) * LANES], kic)
            acc = acc + jnp.maximum(d, 0.0) * wi_cols[h]
        s = jnp.where((c < i) | (col_i <= row_i), acc, NEG)
        key_ref[c] = _sort_key(s)
        return carry

    lax.fori_loop(0, nchunk, score_chunk, 0)

    rb = SEARCH_ROWS

    def count(r0, pred):
        def body(c, cnt):
            for k in range(nsub):
                blk = key_ref[c, r0:r0 + rb, k * LANES:(k + 1) * LANES]
                cnt = cnt + pred(blk, c * tq + k * LANES).astype(I32)
            return cnt
        cnt = lax.fori_loop(0, nchunk, body, jnp.zeros((rb, LANES), I32))
        return jnp.sum(cnt, axis=1, keepdims=True)

    any_excess = jnp.int32(0)
    for r in range(tq // rb):
        r0 = r * rb

        def bit_body(bi, t):
            cand = t + lax.shift_left(jnp.int32(1), 31 - bi)
            cand_b = jnp.broadcast_to(cand, (rb, LANES))
            cnt = count(r0, lambda blk, _: blk >= cand_b)
            return jnp.where(cnt >= n_keep, cand, t)

        t = lax.fori_loop(0, 32, bit_body, jnp.full((rb, 1), INT_MIN, I32))
        t_b = jnp.broadcast_to(t, (rb, LANES))
        cnt_gt = count(r0, lambda blk, _: blk > t_b)
        cnt_eq = count(r0, lambda blk, _: blk == t_b)
        need = n_keep - cnt_gt
        excess = (cnt_eq > need) & (t > KEY_HALF)
        any_excess = any_excess + jnp.max(excess.astype(I32))
        t_ref[r0:r0 + rb, :] = t_b
        need_ref[r0:r0 + rb, :] = jnp.broadcast_to(need, (rb, LANES))

    j_ref[...] = jnp.full(j_ref.shape, seq_len, I32)

    @pl.when(any_excess > 0)
    def _():
        nbits = max(1, (seq_len - 1).bit_length())
        lane_rb = lax.broadcasted_iota(I32, (rb, LANES), 1)
        for r in range(tq // rb):
            r0 = r * rb
            t_b = t_ref[r0:r0 + rb, :]
            need = need_ref[r0:r0 + rb, 0:1]

            def jbit(bi, jj):
                cand = jj | lax.shift_left(jnp.int32(1), nbits - 1 - bi)
                cand_b = jnp.broadcast_to(cand, (rb, LANES))
                g = count(r0, lambda blk, base: (blk == t_b) & ((base + lane_rb) < cand_b))
                return jnp.where(g < need, cand, jj)

            jj = lax.fori_loop(0, nbits, jbit, jnp.zeros((rb, 1), I32))
            j_ref[r0:r0 + rb, :] = jnp.broadcast_to(jj, (rb, LANES))

    t_all = t_ref[...]
    j_all = j_ref[...]

    def bias_chunk(c, carry):
        for k in range(nsub):
            blk = key_ref[c, :, k * LANES:(k + 1) * LANES]
            col = c * tq + k * LANES + lane
            sel = (blk > KEY_HALF) & ((blk > t_all) | ((blk == t_all) & (col <= j_all)))
            bias_ref[c, :, k * LANES:(k + 1) * LANES] = jnp.where(sel, 0.0, NEG)
        return carry

    lax.fori_loop(0, nchunk, bias_chunk, 0)

    m_ref[...] = jnp.full(m_ref.shape, NEG, F32)
    l_ref[...] = jnp.zeros_like(l_ref)
    acc_ref[...] = jnp.zeros_like(acc_ref)
    nkv = N_KV_HEADS * HEAD_DIM

    def attn_chunk(c, carry):
        rows = pl.ds(pl.multiple_of(c * tq, tq), tq)
        bias = bias_ref[c]
        for h in range(N_HEADS):
            g = h // GROUP
            kc = kv_ref[rows, g * HEAD_DIM:(g + 1) * HEAD_DIM]
            vc = kv_ref[rows, nkv + g * HEAD_DIM:nkv + (g + 1) * HEAD_DIM]
            s = _dot_nt(q_ref[:, h * HEAD_DIM:(h + 1) * HEAD_DIM], kc) + bias
            m_old = m_ref[h][:, 0:1]
            m_new = jnp.maximum(m_old, jnp.max(s, axis=1, keepdims=True))
            alpha = jnp.exp(m_old - m_new)
            p = jnp.exp(s - m_new)
            l_ref[h] = jnp.broadcast_to(alpha * l_ref[h][:, 0:1] + jnp.sum(p, axis=1, keepdims=True),
                                        (tq, LANES))
            acc_ref[h] = alpha * acc_ref[h] + _dot(p.astype(BF16), vc)
            m_ref[h] = jnp.broadcast_to(m_new, (tq, LANES))
        return carry

    lax.fori_loop(0, nchunk, attn_chunk, 0)
    for h in range(N_HEADS):
        o_ref[:, h * HEAD_DIM:(h + 1) * HEAD_DIM] = (acc_ref[h] / l_ref[h][:, 0:1]).astype(BF16)


def _attn_prompt(q, kvb, qi, kib, idx, nb):
    t = q.shape[0]
    s = t // nb
    tq = min(ATT_Q, s)
    nq = s // tq
    n_keep = min(TOPK_KEYS, s // 4)
    blk = lambda w: pl.BlockSpec((tq, w), lambda b, i: (b * nq + i, 0))
    seq = lambda w: pl.BlockSpec((s, w), lambda b, i: (b, 0))
    return pl.pallas_call(
        functools.partial(_attn_prompt_kernel, n_keep=n_keep, seq_len=s),
        grid=(nb, nq),
        in_specs=[blk(q.shape[1]), seq(kvb.shape[1]), blk(qi.shape[1]), seq(kib.shape[1]), blk(idx.shape[1])],
        out_specs=blk(N_HEADS * HEAD_DIM),
        out_shape=jax.ShapeDtypeStruct((t, N_HEADS * HEAD_DIM), BF16),
        scratch_shapes=[pltpu.VMEM((nq, tq, tq), I32), pltpu.VMEM((nq, tq, tq), F32),
                        pltpu.VMEM((tq, LANES), I32), pltpu.VMEM((tq, LANES), I32), pltpu.VMEM((tq, LANES), I32),
                        pltpu.VMEM((N_HEADS, tq, LANES), F32), pltpu.VMEM((N_HEADS, tq, LANES), F32),
                        pltpu.VMEM((N_HEADS, tq, HEAD_DIM), F32)],
        compiler_params=_params(2),
        name="attn_prompt",
    )(q, kvb, qi, kib, idx)


def _attn_decode_kernel(pt_ref, q_ref, qi_ref, wi_ref, kin_ref, kn_ref, vn_ref, ck_hbm, cv_hbm, cx_hbm, o_ref,
                        kbuf, vbuf, xbuf, key_ref, bias_ref, s_ref, sem, *, n_pages, n_keep):
    b = pl.program_id(0)
    kv_rows = PAGE_SIZE * N_KV_HEADS
    past = n_pages * PAGE_SIZE
    cw = 2 * PAGE_SIZE
    nrow = past // cw

    def page_copies(p):
        phys = pt_ref[b, p]
        src_kv = pl.ds(pl.multiple_of(phys * kv_rows, kv_rows), kv_rows)
        dst_kv = pl.ds(pl.multiple_of(p * kv_rows, kv_rows), kv_rows)
        src_x = pl.ds(pl.multiple_of(phys * PAGE_SIZE, PAGE_SIZE), PAGE_SIZE)
        dst_x = pl.ds(pl.multiple_of(p * PAGE_SIZE, PAGE_SIZE), PAGE_SIZE)
        return (pltpu.make_async_copy(ck_hbm.at[src_kv], kbuf.at[dst_kv], sem.at[0]),
                pltpu.make_async_copy(cv_hbm.at[src_kv], vbuf.at[dst_kv], sem.at[1]),
                pltpu.make_async_copy(cx_hbm.at[src_x], xbuf.at[dst_x], sem.at[2]))

    def start_page(p, carry):
        for cp in page_copies(p):
            cp.start()
        return carry

    def wait_page(p, carry):
        for cp in page_copies(p):
            cp.wait()
        return carry

    lax.fori_loop(0, n_pages, start_page, 0)
    lax.fori_loop(0, n_pages, wait_page, 0)

    q8 = q_ref[0]
    qi8 = qi_ref[0]
    wi8 = wi_ref[0] * IDX_SCALE

    def score_row(r, carry):
        kx = xbuf[pl.ds(pl.multiple_of(r * cw, cw), cw), :].astype(BF16)
        d = _dot_nt(qi8, kx)
        sc = jnp.sum(jnp.maximum(d, 0.0) * wi8, axis=0, keepdims=True)
        key_ref[pl.ds(r, 1), :] = _sort_key(sc)
        return carry

    lax.fori_loop(0, nrow, score_row, 0)
    dn = jnp.sum(qi8.astype(F32) * kin_ref[0].astype(F32), axis=1, keepdims=True)
    key_new = _sort_key(jnp.sum(jnp.maximum(dn, 0.0) * wi8, axis=0, keepdims=True))

    keys = key_ref[...]
    col = (lax.broadcasted_iota(I32, keys.shape, 0) * cw + lax.broadcasted_iota(I32, keys.shape, 1))

    def total(x):
        return jnp.sum(jnp.sum(x.astype(I32), axis=1, keepdims=True), axis=0, keepdims=True)

    def bit_body(bi, t):
        cand = t + lax.shift_left(jnp.int32(1), 31 - bi)
        cnt = total(keys >= cand) + (key_new >= cand).astype(I32)
        return jnp.where(cnt >= n_keep, cand, t)

    t = lax.fori_loop(0, 32, bit_body, jnp.full((1, 1), INT_MIN, I32))
    need = n_keep - (total(keys > t) + (key_new > t).astype(I32))
    nbits = past.bit_length()

    def jbit(bi, jj):
        cand = jj | lax.shift_left(jnp.int32(1), nbits - 1 - bi)
        g = total((keys == t) & (col < cand)) + ((key_new == t) & (past < cand)).astype(I32)
        return jnp.where(g < need, cand, jj)

    jj = lax.fori_loop(0, nbits, jbit, jnp.zeros((1, 1), I32))
    sel = (keys > t) | ((keys == t) & (col <= jj))
    sel_new = (key_new > t) | ((key_new == t) & (past <= jj))

    e_r = lax.broadcasted_iota(I32, (cw, N_KV_HEADS * cw), 0)
    e_c = lax.broadcasted_iota(I32, (cw, N_KV_HEADS * cw), 1)
    expand = (e_c // N_KV_HEADS == e_r).astype(BF16)
    sel2 = _dot(sel.astype(BF16), expand)
    bias_ref[...] = jnp.where(sel2 > 0.5, 0.0, NEG)

    ncol = N_KV_HEADS * cw
    head_i = lax.broadcasted_iota(I32, (N_HEADS, ncol), 0)
    col_i = lax.broadcasted_iota(I32, (N_HEADS, ncol), 1)
    same_group = (col_i % N_KV_HEADS) == (head_i // GROUP)

    def logit_row(r, m):
        kc = kbuf[pl.ds(pl.multiple_of(r * ncol, ncol), ncol), :].astype(BF16)
        s = jnp.where(same_group, _dot_nt(q8, kc) + bias_ref[pl.ds(r, 1), :], NEG)
        s_ref[r] = s
        return jnp.maximum(m, jnp.max(s, axis=1, keepdims=True))

    s_new = jnp.sum(q8.astype(F32) * kn_ref[0].astype(F32), axis=1, keepdims=True)
    s_new = jnp.where(sel_new, s_new, NEG)
    m = lax.fori_loop(0, nrow, logit_row, s_new)

    def pv_row(r, carry):
        l, acc = carry
        p = jnp.exp(s_ref[r] - m)
        vc = vbuf[pl.ds(pl.multiple_of(r * ncol, ncol), ncol), :].astype(BF16)
        return l + jnp.sum(p, axis=1, keepdims=True), acc + _dot(p.astype(BF16), vc)

    p_new = jnp.exp(s_new - m)
    l0 = p_new
    acc0 = p_new.astype(BF16).astype(F32) * vn_ref[0].astype(F32)
    l, acc = lax.fori_loop(0, nrow, pv_row, (l0, acc0))
    o_ref[0] = (acc / l).astype(BF16)


def _attn_decode(page_table, q8, qi8, wi8, kin, kn8, vn8, ck2, cv2, cx2):
    db, n_pages = page_table.shape
    past = n_pages * PAGE_SIZE
    n_keep = min(TOPK_KEYS, (past + 1) // 4)
    cw = 2 * PAGE_SIZE
    nrow = past // cw
    per_b = lambda a: pl.BlockSpec((1,) + a.shape[1:], lambda b, pt: (b, 0, 0))
    any_spec = pl.BlockSpec(memory_space=pl.ANY)
    grid_spec = pltpu.PrefetchScalarGridSpec(
        num_scalar_prefetch=1,
        grid=(db,),
        in_specs=[per_b(q8), per_b(qi8), per_b(wi8), per_b(kin), per_b(kn8), per_b(vn8),
                  any_spec, any_spec, any_spec],
        out_specs=pl.BlockSpec((1, N_HEADS, HEAD_DIM), lambda b, pt: (b, 0, 0)),
        scratch_shapes=[pltpu.VMEM((past * N_KV_HEADS, HEAD_DIM), F32),
                        pltpu.VMEM((past * N_KV_HEADS, HEAD_DIM), F32),
                        pltpu.VMEM((past, IDX_DIM), F32),
                        pltpu.VMEM((nrow, cw), I32),
                        pltpu.VMEM((nrow, N_KV_HEADS * cw), F32),
                        pltpu.VMEM((nrow, N_HEADS, N_KV_HEADS * cw), F32),
                        pltpu.SemaphoreType.DMA((3,))],
    )
    return pl.pallas_call(
        functools.partial(_attn_decode_kernel, n_pages=n_pages, n_keep=n_keep),
        grid_spec=grid_spec,
        out_shape=jax.ShapeDtypeStruct((db, N_HEADS, HEAD_DIM), BF16),
        compiler_params=_params(1),
        name="attn_decode",
    )(page_table, q8, qi8, wi8, kin, kn8, vn8, ck2, cv2, cx2)


def _layer_norm(r, g, b):
    mu = jnp.mean(r, axis=-1, keepdims=True)
    d = r - mu
    var = jnp.mean(d * d, axis=-1, keepdims=True)
    return d * lax.rsqrt(var + LN_EPS) * g + b


def _route(logits_t, rbias):
    n_exp, tm = logits_t.shape
    epg = n_exp // N_GROUPS
    s = jax.nn.sigmoid(logits_t)
    sb = s + rbias
    ie = lax.broadcasted_iota(I32, (epg, tm), 0)
    gs_rows = []
    for g in range(N_GROUPS):
        blk = sb[g * epg:(g + 1) * epg, :]
        m1 = jnp.max(blk, axis=0, keepdims=True)
        i1 = jnp.min(jnp.where(blk == m1, ie, epg), axis=0, keepdims=True)
        m2 = jnp.max(jnp.where(ie == i1, -jnp.inf, blk), axis=0, keepdims=True)
        gs_rows.append(m1 + m2)
    picked = [jnp.zeros((1, tm), jnp.bool_) for _ in range(N_GROUPS)]
    cur = list(gs_rows)
    for _ in range(TOPK_GROUPS):
        mx = cur[0]
        for g in range(1, N_GROUPS):
            mx = jnp.maximum(mx, cur[g])
        found = jnp.zeros((1, tm), jnp.bool_)
        for g in range(N_GROUPS):
            hit = (cur[g] == mx) & jnp.logical_not(found)
            found = found | hit
            picked[g] = picked[g] | hit
            cur[g] = jnp.where(hit, -jnp.inf, cur[g])
    masked = jnp.concatenate(
        [jnp.where(picked[g], sb[g * epg:(g + 1) * epg, :], NEG) for g in range(N_GROUPS)], axis=0)
    iall = lax.broadcasted_iota(I32, (n_exp, tm), 0)
    e_rows, w_rows = [], []
    for _ in range(TOP_K):
        mx = jnp.max(masked, axis=0, keepdims=True)
        ix = jnp.min(jnp.where(masked == mx, iall, n_exp), axis=0, keepdims=True)
        hit = iall == ix
        w_rows.append(jnp.sum(jnp.where(hit, s, 0.0), axis=0, keepdims=True))
        e_rows.append(ix)
        masked = jnp.where(hit, -jnp.inf, masked)
    wsum = w_rows[0]
    for w in w_rows[1:]:
        wsum = wsum + w
    gates = [w / wsum * ROUTED_SCALE for w in w_rows]
    return jnp.concatenate(e_rows, axis=0), jnp.concatenate(gates, axis=0)


def _post_attn_kernel(xp, xs, ap, as_, cp, cs, gap, gas, gcp, gcs, woa, woc, wo, g1, b1, rwt, rb,
                      h2_o, e_o, gate_o, *, n_prompt_tiles):
    i = pl.program_id(0)
    is_p = i < n_prompt_tiles
    pick = lambda a, b: jnp.where(is_p, a[...], b[...])
    a = _dot(pick(ap, as_), woa[...])
    c = _dot(pick(cp, cs), woc[...])
    merged = pick(gap, gas).astype(F32) * a + pick(gcp, gcs).astype(F32) * c
    r = ALPHA * pick(xp, xs) + _dot(merged.astype(BF16), wo[...])
    h = _layer_norm(r, g1[...], b1[...])
    tm = h.shape[0]
    for k in range(D_MODEL // LANES):
        h2_o[pl.ds(k, tm, stride=SUBLANES), :] = h[:, k * LANES:(k + 1) * LANES]
    logits_t = lax.dot_general(rwt[...], h, (((1,), (1,)), ((), ())),
                               precision=lax.Precision.HIGHEST, preferred_element_type=F32)
    e_idx, gates = _route(logits_t, rb[...])
    e_o[...] = e_idx
    gate_o[...] = gates


def _post_attn(x_p, x_s, attn_p, attn_s, conv_p, conv_s, ga_p, ga_s, gc_p, gc_s,
               woa, woc, wo, g1, b1, rwt, rb):
    tp, ts = x_p.shape[0], x_s.shape[0]
    tm = TOKEN_TILE
    npt, nst = tp // tm, ts // tm
    n_tok = tp + ts
    p_spec = pl.BlockSpec((tm, D_MODEL), lambda i: (jnp.minimum(i, npt - 1), 0))
    s_spec = pl.BlockSpec((tm, D_MODEL), lambda i: (jnp.maximum(i - npt, 0), 0))
    consts = [woa, woc, wo, g1, b1, rwt, rb]
    return pl.pallas_call(
        functools.partial(_post_attn_kernel, n_prompt_tiles=npt),
        grid=(npt + nst,),
        in_specs=[p_spec, s_spec] * 5 + [_const_spec(c.shape) for c in consts],
        out_specs=[pl.BlockSpec((tm * SUBLANES, LANES), lambda i: (i, 0)),
                   pl.BlockSpec((TOP_K, tm), lambda i: (0, i)),
                   pl.BlockSpec((TOP_K, tm), lambda i: (0, i))],
        out_shape=[jax.ShapeDtypeStruct((n_tok * SUBLANES, LANES), F32),
                   jax.ShapeDtypeStruct((TOP_K, n_tok), I32),
                   jax.ShapeDtypeStruct((TOP_K, n_tok), F32)],
        compiler_params=_params(1),
        name="post_attn",
    )(x_p, x_s, attn_p, attn_s, conv_p, conv_s, ga_p, ga_s, gc_p, gc_s, *consts)


def _rank_kernel(e_ref, rank_o, cnt_o, carry_ref):
    i = pl.program_id(0)

    @pl.when(i == 0)
    def _():
        carry_ref[...] = jnp.zeros_like(carry_ref)

    n_exp = carry_ref.shape[0]
    e = e_ref[...]
    tk = e.shape[1]
    ie = lax.broadcasted_iota(I32, (n_exp, tk), 0)
    onehot = jnp.zeros((n_exp, tk), F32)
    for j in range(TOP_K):
        onehot = onehot + (ie == e[j:j + 1, :]).astype(F32)
    before = (lax.broadcasted_iota(I32, (tk, tk), 0) < lax.broadcasted_iota(I32, (tk, tk), 1)).astype(BF16)
    prefix = _dot(onehot.astype(BF16), before) + carry_ref[:, 0:1]
    rows = [jnp.sum(jnp.where(ie == e[j:j + 1, :], prefix, 0.0), axis=0, keepdims=True) for j in range(TOP_K)]
    rank_o[...] = jnp.concatenate(rows, axis=0).astype(I32)
    carry_ref[...] = carry_ref[...] + jnp.sum(onehot, axis=1, keepdims=True)
    cnt_o[...] = carry_ref[...].astype(I32)


def _rank(e_t, n_exp):
    n_tok = e_t.shape[1]
    tk = TOKEN_TILE
    return pl.pallas_call(
        _rank_kernel,
        grid=(n_tok // tk,),
        in_specs=[pl.BlockSpec((TOP_K, tk), lambda i: (0, i))],
        out_specs=[pl.BlockSpec((TOP_K, tk), lambda i: (0, i)),
                   pl.BlockSpec((n_exp, LANES), lambda i: (0, 0))],
        out_shape=[jax.ShapeDtypeStruct((TOP_K, n_tok), I32), jax.ShapeDtypeStruct((n_exp, LANES), I32)],
        scratch_shapes=[pltpu.VMEM((n_exp, LANES), F32)],
        compiler_params=_params(1),
        name="moe_rank",
    )(e_t)


def _slot_kernel(e_ref, rank_ref, pstart_ref, slot_o):
    e = e_ref[...]
    n_exp = pstart_ref.shape[0]
    tk = e.shape[1]
    ie = lax.broadcasted_iota(I32, (n_exp, tk), 0)
    ps = pstart_ref[:, 0:1]
    rows = [jnp.sum(jnp.where(ie == e[j:j + 1, :], ps, 0.0), axis=0, keepdims=True) for j in range(TOP_K)]
    slot_o[...] = rank_ref[...] + jnp.concatenate(rows, axis=0).astype(I32)


def _slots(e_t, rank_t, pstart_f):
    n_tok = e_t.shape[1]
    tk = TOKEN_TILE
    spec = pl.BlockSpec((TOP_K, tk), lambda i: (0, i))
    return pl.pallas_call(
        _slot_kernel,
        grid=(n_tok // tk,),
        in_specs=[spec, spec, _const_spec(pstart_f.shape)],
        out_specs=spec,
        out_shape=jax.ShapeDtypeStruct((TOP_K, n_tok), I32),
        compiler_params=_params(1),
        name="moe_slots",
    )(e_t, rank_t, pstart_f)


def _dispatch_kernel(pend_ref, padded_ref, slot_ref, h2_hbm, xs_hbm, zbuf, sem, zsem, *, n_exp):
    i = pl.program_id(0)
    td = slot_ref.shape[1]
    blk_rows = MOE_BLOCK * SUBLANES

    def zero_copy(e):
        dst = pl.ds(pl.multiple_of((pend_ref[e] - MOE_BLOCK) * SUBLANES, SUBLANES), blk_rows)
        return pltpu.make_async_copy(zbuf, xs_hbm.at[dst], zsem)

    @pl.when(i == 0)
    def _():
        zbuf[...] = jnp.zeros_like(zbuf)

        def start(e, carry):
            @pl.when(padded_ref[e] > 0)
            def _():
                zero_copy(e).start()
            return carry

        def wait(e, carry):
            @pl.when(padded_ref[e] > 0)
            def _():
                zero_copy(e).wait()
            return carry

        lax.fori_loop(0, n_exp, start, 0)
        lax.fori_loop(0, n_exp, wait, 0)

    def row_copy(t, j):
        src = pl.ds(pl.multiple_of((i * td + t) * SUBLANES, SUBLANES), SUBLANES)
        dst = pl.ds(pl.multiple_of(slot_ref[j, t] * SUBLANES, SUBLANES), SUBLANES)
        return pltpu.make_async_copy(h2_hbm.at[src], xs_hbm.at[dst], sem)

    def start_tok(t, carry):
        for j in range(TOP_K):
            row_copy(t, j).start()
        return carry

    def wait_tok(t, carry):
        for j in range(TOP_K):
            row_copy(t, j).wait()
        return carry

    lax.fori_loop(0, td, start_tok, 0)
    lax.fori_loop(0, td, wait_tok, 0)


def _dispatch(pend, padded, slot_t, h2, m_pad):
    n_tok = slot_t.shape[1]
    td = TOKEN_TILE
    n_exp = pend.shape[0]
    grid_spec = pltpu.PrefetchScalarGridSpec(
        num_scalar_prefetch=2,
        grid=(n_tok // td,),
        in_specs=[pl.BlockSpec((TOP_K, td), lambda i, *_: (0, i), memory_space=pltpu.SMEM),
                  pl.BlockSpec(memory_space=pl.ANY)],
        out_specs=pl.BlockSpec(memory_space=pl.ANY),
        scratch_shapes=[pltpu.VMEM((MOE_BLOCK * SUBLANES, LANES), F32),
                        pltpu.SemaphoreType.DMA(()), pltpu.SemaphoreType.DMA(())],
    )
    return pl.pallas_call(
        functools.partial(_dispatch_kernel, n_exp=n_exp),
        grid_spec=grid_spec,
        out_shape=jax.ShapeDtypeStruct((m_pad * SUBLANES, LANES), F32),
        compiler_params=_params(1),
        name="moe_dispatch",
    )(pend, padded, slot_t, h2)


def _tile_rows(ref, n):
    return jnp.concatenate([ref[pl.ds(k, n, stride=SUBLANES), :] for k in range(D_MODEL // LANES)], axis=1)


def _expert_kernel(be_ref, nused_ref, x_ref, wg_ref, wu_ref, wd_ref, y_ref, wgu_b, wd_b):
    i = pl.program_id(0)
    d_exp = wd_b.shape[0]

    @pl.when(i < nused_ref[0])
    def _():
        changed = (i == 0) | (be_ref[i] != be_ref[jnp.maximum(i - 1, 0)])

        @pl.when(changed)
        def _():
            wgu_b[:, :d_exp] = wg_ref[0].astype(BF16)
            wgu_b[:, d_exp:] = wu_ref[0].astype(BF16)
            wd_b[...] = wd_ref[0].astype(BF16)

        x = _tile_rows(x_ref, MOE_BLOCK).astype(BF16)
        gu = _dot(x, wgu_b[...])
        hh = jax.nn.silu(gu[:, :d_exp]) * gu[:, d_exp:]
        y = _dot(hh.astype(BF16), wd_b[...])
        for k in range(D_MODEL // LANES):
            y_ref[pl.ds(k, MOE_BLOCK, stride=SUBLANES), :] = y[:, k * LANES:(k + 1) * LANES]


def _experts(block_e, nused, xs, w_gate, w_up, w_down):
    n_blocks = block_e.shape[0]
    d_exp = w_gate.shape[2]
    rows = MOE_BLOCK * SUBLANES
    last = lambda i, nu: jnp.minimum(i, nu[0] - 1)
    grid_spec = pltpu.PrefetchScalarGridSpec(
        num_scalar_prefetch=2,
        grid=(n_blocks,),
        in_specs=[pl.BlockSpec((rows, LANES), lambda i, be, nu: (last(i, nu), 0)),
                  pl.BlockSpec((1, D_MODEL, d_exp), lambda i, be, nu: (be[i], 0, 0)),
                  pl.BlockSpec((1, D_MODEL, d_exp), lambda i, be, nu: (be[i], 0, 0)),
                  pl.BlockSpec((1, d_exp, D_MODEL), lambda i, be, nu: (be[i], 0, 0))],
        out_specs=pl.BlockSpec((rows, LANES), lambda i, be, nu: (last(i, nu), 0)),
        scratch_shapes=[pltpu.VMEM((D_MODEL, 2 * d_exp), BF16), pltpu.VMEM((d_exp, D_MODEL), BF16)],
    )
    return pl.pallas_call(
        _expert_kernel,
        grid_spec=grid_spec,
        out_shape=jax.ShapeDtypeStruct(xs.shape, F32),
        compiler_params=_params(1),
        name="moe_experts",
    )(block_e, nused, xs, w_gate, w_up, w_down)


def _combine_kernel(slot_ref, gate_ref, h2_ref, ys_hbm, shg, shu, shd, g2, b2, yp_o, ys_o, buf, sem,
                    *, n_prompt_tiles):
    i = pl.program_id(0)
    tc = gate_ref.shape[0]

    def row_copy(t, j):
        src = pl.ds(pl.multiple_of(slot_ref[j, t] * SUBLANES, SUBLANES), SUBLANES)
        dst = pl.ds(pl.multiple_of(t * SUBLANES, SUBLANES), SUBLANES)
        return pltpu.make_async_copy(ys_hbm.at[src], buf.at[j, dst], sem)

    def start_tok(t, carry):
        for j in range(TOP_K):
            row_copy(t, j).start()
        return carry

    def wait_tok(t, carry):
        for j in range(TOP_K):
            row_copy(t, j).wait()
        return carry

    lax.fori_loop(0, tc, start_tok, 0)
    h = _tile_rows(h2_ref, tc)
    hb = h.astype(BF16)
    shared = _dot((jax.nn.silu(_dot(hb, shg[...])) * _dot(hb, shu[...])).astype(BF16), shd[...])
    lax.fori_loop(0, tc, wait_tok, 0)
    routed = jnp.zeros((tc, D_MODEL), F32)
    for j in range(TOP_K):
        routed = routed + _tile_rows(buf.at[j], tc) * gate_ref[:, j:j + 1]
    y = _layer_norm(ALPHA * h + (routed + shared), g2[...], b2[...])

    @pl.when(i < n_prompt_tiles)
    def _():
        yp_o[...] = y

    @pl.when(i >= n_prompt_tiles)
    def _():
        ys_o[...] = y


def _combine(slot_t, gates, h2, ys, shg, shu, shd, g2, b2, n_prompt, n_decode):
    tc = TOKEN_TILE
    npt, nst = n_prompt // tc, n_decode // tc
    consts = [shg, shu, shd, g2, b2]
    return pl.pallas_call(
        functools.partial(_combine_kernel, n_prompt_tiles=npt),
        grid=(npt + nst,),
        in_specs=[pl.BlockSpec((TOP_K, tc), lambda i: (0, i), memory_space=pltpu.SMEM),
                  pl.BlockSpec((tc, TOP_K), lambda i: (i, 0)),
                  pl.BlockSpec((tc * SUBLANES, LANES), lambda i: (i, 0)),
                  pl.BlockSpec(memory_space=pl.ANY)] + [_const_spec(c.shape) for c in consts],
        out_specs=[pl.BlockSpec((tc, D_MODEL), lambda i: (jnp.minimum(i, npt - 1), 0)),
                   pl.BlockSpec((tc, D_MODEL), lambda i: (jnp.maximum(i - npt, 0), 0))],
        out_shape=[jax.ShapeDtypeStruct((n_prompt, D_MODEL), F32),
                   jax.ShapeDtypeStruct((n_decode, D_MODEL), F32)],
        scratch_shapes=[pltpu.VMEM((TOP_K, tc * SUBLANES, LANES), F32), pltpu.SemaphoreType.DMA(())],
        compiler_params=_params(1),
        name="moe_combine",
    )(slot_t, gates, h2, ys, *consts)


def _pad_rows(a, n):
    return jnp.pad(a, ((0, n - a.shape[0]),) + ((0, 0),) * (a.ndim - 1))


def kernel(x_prompt, x_sample, cache_k, cache_v, cache_kidx, page_table, state_conv, w_in, conv_w, w_o_attn,
           w_o_conv, w_o, ln1_g, ln1_b, router_w, router_bias, moe_w_gate, moe_w_up, moe_w_down,
           shared_w_gate, shared_w_up, shared_w_down, ln2_g, ln2_b):
    nb, seq, _ = x_prompt.shape
    db = x_sample.shape[0]
    n_pool = cache_k.shape[1]
    n_exp = router_w.shape[-1]
    tp = nb * seq
    ts = TOKEN_TILE
    nkv = N_KV_HEADS * HEAD_DIM

    weights = _prep_in_weights(w_in[0])
    cw = conv_w[0]

    xp2 = x_prompt.reshape(tp, D_MODEL)
    (q_p, k_p, v_p, kvb_p, qi_p, idx_p, kib_p, conv_p, ga_p, gc_p, ulast_p) = _in_proj_prompt(xp2, weights, cw, nb)
    xs2 = _pad_rows(x_sample.reshape(db, D_MODEL), ts)
    prev2 = _pad_rows(state_conv[0, :, 0, :], ts)
    prev1 = _pad_rows(state_conv[0, :, 1, :], ts)
    (q_s, k_s, v_s, _, qi_s, idx_s, _, conv_s, ga_s, gc_s, u_s) = _in_proj_decode(xs2, prev2, prev1, weights, cw)

    attn_p = _attn_prompt(q_p, kvb_p, qi_p, kib_p, idx_p, nb)
    q8 = q_s[:db].reshape(db, N_HEADS, HEAD_DIM)
    qi8 = qi_s[:db].reshape(db, IDX_HEADS, LANES)[:, :, :IDX_DIM]
    wi8 = idx_s[:db, IDX_DIM:IDX_DIM + IDX_HEADS].reshape(db, IDX_HEADS, 1)
    kin = idx_s[:db, :IDX_DIM].astype(BF16).reshape(db, 1, IDX_DIM)
    kn8 = jnp.repeat(k_s[:db].reshape(db, N_KV_HEADS, HEAD_DIM), GROUP, axis=1).astype(BF16)
    vn8 = jnp.repeat(v_s[:db].reshape(db, N_KV_HEADS, HEAD_DIM), GROUP, axis=1).astype(BF16)
    ck2 = cache_k[0].reshape(n_pool * PAGE_SIZE * N_KV_HEADS, HEAD_DIM)
    cv2 = cache_v[0].reshape(n_pool * PAGE_SIZE * N_KV_HEADS, HEAD_DIM)
    cx2 = cache_kidx[0].reshape(n_pool * PAGE_SIZE, IDX_DIM)
    attn_s8 = _attn_decode(page_table, q8, qi8, wi8, kin, kn8, vn8, ck2, cv2, cx2)
    attn_s = _pad_rows(attn_s8.reshape(db, N_HEADS * HEAD_DIM), ts)

    h2, e_t, gate_t = _post_attn(
        xp2, xs2, attn_p, attn_s, conv_p, conv_s, ga_p, ga_s, gc_p, gc_s,
        w_o_attn[0].astype(BF16), w_o_conv[0].astype(BF16), w_o[0].astype(BF16),
        ln1_g[0].reshape(1, D_MODEL), ln1_b[0].reshape(1, D_MODEL),
        router_w[0].T, router_bias[0].reshape(n_exp, 1))

    n_tok = tp + ts
    rank_t, cnt = _rank(e_t, n_exp)
    counts = cnt[:, 0]
    padded = (counts + MOE_BLOCK - 1) // MOE_BLOCK * MOE_BLOCK
    pend = jnp.cumsum(padded)
    pstart = pend - padded
    n_blocks = (n_tok * TOP_K + n_exp * (MOE_BLOCK - 1) + MOE_BLOCK - 1) // MOE_BLOCK
    nused = (pend[-1] // MOE_BLOCK).astype(I32)
    blk = jnp.minimum(jnp.arange(n_blocks, dtype=I32), nused - 1)
    block_e = jnp.minimum(jnp.searchsorted(pend, blk * MOE_BLOCK, side='right'), n_exp - 1).astype(I32)
    slot_t = _slots(e_t, rank_t, jnp.broadcast_to(pstart.astype(F32)[:, None], (n_exp, LANES)))
    xs = _dispatch(pend.astype(I32), padded.astype(I32), slot_t, h2, n_blocks * MOE_BLOCK)
    ys = _experts(block_e, nused.reshape(1), xs, moe_w_gate[0], moe_w_up[0], moe_w_down[0])
    y_p, y_s = _combine(slot_t, gate_t.T, h2, ys,
                        shared_w_gate[0].astype(BF16), shared_w_up[0].astype(BF16),
                        shared_w_down[0].astype(BF16),
                        ln2_g[0].reshape(1, D_MODEL), ln2_b[0].reshape(1, D_MODEL), tp, ts)

    conv_sample = jnp.stack([state_conv[0, :, 1, :], u_s[:db]], axis=1)[None]
    return (y_p.reshape(nb, seq, D_MODEL),
            y_s[:db].reshape(db, 1, D_MODEL),
            k_p.reshape(1, nb, seq, N_KV_HEADS, HEAD_DIM),
            v_p.reshape(1, nb, seq, N_KV_HEADS, HEAD_DIM),
            idx_p[:, :IDX_DIM].reshape(1, nb, seq, IDX_DIM),
            ulast_p[:, SUBLANES - (CONV_W - 1):, :][None],
            k_s[:db].reshape(1, db, 1, N_KV_HEADS, HEAD_DIM),
            v_s[:db].reshape(1, db, 1, N_KV_HEADS, HEAD_DIM),
            idx_s[:db, :IDX_DIM].reshape(1, db, 1, IDX_DIM),
            conv_sample)
```

```python
import functools

import jax
import jax.numpy as jnp
import numpy as np
from jax import lax
from jax.experimental import pallas as pl
from jax.experimental.pallas import tpu as pltpu

F32 = jnp.float32
BF16 = jnp.bfloat16
I32 = jnp.int32

D_MODEL = 1024
N_HEADS = 8
HEAD_DIM = 128
N_KV_HEADS = 2
GROUP = N_HEADS // N_KV_HEADS
IDX_HEADS = 8
IDX_DIM = 64
TOPK_KEYS = 256
IDX_SCALE = IDX_DIM ** -0.5 * IDX_HEADS ** -0.5
Q_SCALE = HEAD_DIM ** -0.5
PAGE_SIZE = 128
D_CONV = D_MODEL
CONV_W = 3
TOP_K = 8
N_GROUPS = 8
TOPK_GROUPS = 4
ROUTED_SCALE = 2.5
DEPTH = 1
ALPHA = (2 * DEPTH) ** 0.25
LN_EPS = 1e-5
NEG = -1e30
IN_WIDTHS = (N_HEADS * HEAD_DIM, N_KV_HEADS * HEAD_DIM, N_KV_HEADS * HEAD_DIM,
             IDX_HEADS * IDX_DIM, IDX_DIM, IDX_HEADS,
             D_CONV, D_CONV, D_CONV, D_MODEL, D_MODEL)

LANES = 128
SUBLANES = 8
VMEM_LIMIT_BYTES = 56 * 1024 * 1024

PROJ_ROWS = 512
ATT_Q = 256
SEARCH_ROWS = 128
TOKEN_TILE = 256
MOE_BLOCK = 256
CONV_CHUNK = 256
INT_MIN = -2 ** 31


def _np_key(v):
    b = int(np.float32(v).view(np.int32))
    return b ^ ((b >> 31) & 0x7FFFFFFF)


KEY_HALF = _np_key(NEG * 0.5)


def _sort_key(s):
    b = lax.bitcast_convert_type(s + 0.0, I32)
    return b ^ ((b >> 31) & jnp.int32(0x7FFFFFFF))


def _dot(a, b):
    return jnp.dot(a, b, preferred_element_type=F32)


def _dot_nt(a, b):
    return lax.dot_general(a, b, (((1,), (1,)), ((), ())), preferred_element_type=F32)


def _params(n_grid):
    return pltpu.CompilerParams(dimension_semantics=("arbitrary",) * n_grid,
                                vmem_limit_bytes=VMEM_LIMIT_BYTES)


def _const_spec(shape):
    nd = len(shape)
    return pl.BlockSpec(shape, lambda *_: (0,) * nd, pipeline_mode=pl.Buffered(1))


def _proj_common(xb, w1, w2, w3, wga, wgc, q_o, k_o, v_o, kv_o, qi_o, idx_o, kib_o, sga_o, sgc_o):
    nq = N_HEADS * HEAD_DIM
    nkv = N_KV_HEADS * HEAD_DIM
    z1 = _dot(xb, w1[...])
    q_o[...] = (z1[:, :nq] * Q_SCALE).astype(BF16)
    k = z1[:, nq:nq + nkv]
    v = z1[:, nq + nkv:nq + 2 * nkv]
    k_o[...] = k
    v_o[...] = v
    kv_o[:, :nkv] = k.astype(BF16)
    kv_o[:, nkv:] = v.astype(BF16)
    qi_o[...] = _dot(xb, w2[...]).astype(BF16)
    z3 = _dot(xb, w3[...])
    idx_o[...] = z3
    kib_o[...] = z3.astype(BF16)
    sga_o[...] = jax.nn.sigmoid(_dot(xb, wga[...])).astype(BF16)
    sgc_o[...] = jax.nn.sigmoid(_dot(xb, wgc[...])).astype(BF16)


def _in_proj_prompt_kernel(x_ref, w1, w2, w3, wb, wc, wx, wga, wgc, cw_ref,
                           q_o, k_o, v_o, kv_o, qi_o, idx_o, kib_o, conv_o, sga_o, sgc_o, ulast_o,
                           tail_ref):
    j = pl.program_id(1)

    @pl.when(j == 0)
    def _():
        tail_ref[...] = jnp.zeros_like(tail_ref)

    xb = x_ref[...].astype(BF16)
    _proj_common(xb, w1, w2, w3, wga, wgc, q_o, k_o, v_o, kv_o, qi_o, idx_o, kib_o, sga_o, sgc_o)
    tm = xb.shape[0]
    row = lax.broadcasted_iota(I32, (tm, CONV_CHUNK), 0)
    for c in range(D_CONV // CONV_CHUNK):
        cs = slice(c * CONV_CHUNK, (c + 1) * CONV_CHUNK)
        gb = _dot(xb, wb[:, cs])
        u = _dot(xb, wc[:, cs]) * _dot(xb, wx[:, cs])
        p1 = tail_ref[SUBLANES - 1:SUBLANES, cs]
        p2 = tail_ref[SUBLANES - 2:SUBLANES - 1, cs]
        u1 = jnp.where(row == 0, p1, pltpu.roll(u, 1, 0))
        u2 = jnp.where(row == 0, p2, jnp.where(row == 1, p1, pltpu.roll(u, 2, 0)))
        conv = cw_ref[0:1, cs] * u2 + cw_ref[1:2, cs] * u1 + cw_ref[2:3, cs] * u
        conv_o[:, cs] = (gb * conv).astype(BF16)
        tail_ref[:, cs] = u[tm - SUBLANES:, :]
    ulast_o[0] = tail_ref[...]


def _in_proj_decode_kernel(x_ref, p2_ref, p1_ref, w1, w2, w3, wb, wc, wx, wga, wgc, cw_ref,
                           q_o, k_o, v_o, kv_o, qi_o, idx_o, kib_o, conv_o, sga_o, sgc_o, u_o):
    xb = x_ref[...].astype(BF16)
    _proj_common(xb, w1, w2, w3, wga, wgc, q_o, k_o, v_o, kv_o, qi_o, idx_o, kib_o, sga_o, sgc_o)
    for c in range(D_CONV // CONV_CHUNK):
        cs = slice(c * CONV_CHUNK, (c + 1) * CONV_CHUNK)
        gb = _dot(xb, wb[:, cs])
        u = _dot(xb, wc[:, cs]) * _dot(xb, wx[:, cs])
        conv = cw_ref[0:1, cs] * p2_ref[:, cs] + cw_ref[1:2, cs] * p1_ref[:, cs] + cw_ref[2:3, cs] * u
        conv_o[:, cs] = (gb * conv).astype(BF16)
        u_o[:, cs] = u


def _prep_in_weights(w_in):
    w = w_in.astype(BF16)
    offs = np.concatenate([[0], np.cumsum(IN_WIDTHS)])
    part = [w[:, int(offs[i]):int(offs[i + 1])] for i in range(len(IN_WIDTHS))]
    q, k, v, qi, ki, wi, gb, gc, xv, ga, gcv = part
    d = w.shape[0]
    w1 = jnp.concatenate([q, k, v], axis=1)
    w2 = jnp.pad(qi.reshape(d, IDX_HEADS, IDX_DIM),
                 ((0, 0), (0, 0), (0, LANES - IDX_DIM))).reshape(d, IDX_HEADS * LANES)
    w3 = jnp.concatenate([ki, wi, jnp.zeros((d, LANES - IDX_DIM - IDX_HEADS), BF16)], axis=1)
    return (w1, w2, w3, gb, gc, xv, ga, gcv)


def _proj_out_shapes(t):
    nkv = N_KV_HEADS * HEAD_DIM
    sd = jax.ShapeDtypeStruct
    return [sd((t, N_HEADS * HEAD_DIM), BF16), sd((t, nkv), F32), sd((t, nkv), F32), sd((t, 2 * nkv), BF16),
            sd((t, IDX_HEADS * LANES), BF16), sd((t, LANES), F32), sd((t, LANES), BF16),
            sd((t, D_CONV), BF16), sd((t, D_MODEL), BF16), sd((t, D_MODEL), BF16)]


def _in_proj_prompt(x2d, weights, conv_w, nb):
    t = x2d.shape[0]
    s = t // nb
    tm = min(PROJ_ROWS, s)
    nj = s // tm
    row_spec = lambda w: pl.BlockSpec((tm, w), lambda b, j: (b * nj + j, 0))
    out_shapes = _proj_out_shapes(t) + [jax.ShapeDtypeStruct((nb, SUBLANES, D_CONV), F32)]
    out_specs = [row_spec(o.shape[1]) for o in out_shapes[:-1]]
    out_specs.append(pl.BlockSpec((1, SUBLANES, D_CONV), lambda b, j: (b, 0, 0)))
    return pl.pallas_call(
        _in_proj_prompt_kernel,
        grid=(nb, nj),
        in_specs=[row_spec(D_MODEL)] + [_const_spec(w.shape) for w in weights] + [_const_spec(conv_w.shape)],
        out_specs=out_specs,
        out_shape=out_shapes,
        scratch_shapes=[pltpu.VMEM((SUBLANES, D_CONV), F32)],
        compiler_params=_params(2),
        name="in_proj_prompt",
    )(x2d, *weights, conv_w)


def _in_proj_decode(x2d, p2, p1, weights, conv_w):
    t = x2d.shape[0]
    row_spec = lambda w: pl.BlockSpec((t, w), lambda i: (0, 0))
    out_shapes = _proj_out_shapes(t) + [jax.ShapeDtypeStruct((t, D_CONV), F32)]
    return pl.pallas_call(
        _in_proj_decode_kernel,
        grid=(1,),
        in_specs=[row_spec(D_MODEL)] * 3 + [_const_spec(w.shape) for w in weights] + [_const_spec(conv_w.shape)],
        out_specs=[row_spec(o.shape[1]) for o in out_shapes],
        out_shape=out_shapes,
        compiler_params=_params(1),
        name="in_proj_decode",
    )(x2d, p2, p1, *weights, conv_w)


def _attn_prompt_kernel(q_ref, kv_ref, qi_ref, kib_ref, idx_ref, o_ref,
                        key_ref, bias_ref, t_ref, j_ref, need_ref, m_ref, acc_ref,
                        *, n_keep, seq_len):
    tq = q_ref.shape[0]
    i = pl.program_id(1)
    nchunk = i + 1
    nsub = tq // LANES
    lane = lax.broadcasted_iota(I32, (tq, LANES), 1)

    wi_cols = [idx_ref[:, IDX_DIM + h:IDX_DIM + h + 1] * IDX_SCALE for h in range(IDX_HEADS)]
    row_i = lax.broadcasted_iota(I32, (tq, tq), 0)
    col_i = lax.broadcasted_iota(I32, (tq, tq), 1)

    def score_chunk(c, carry):
        kic = kib_ref[pl.ds(pl.multiple_of(c * tq, tq), tq), :]
        acc = jnp.zeros((tq, tq), F32)
        for h in range(IDX_HEADS):
            d = _dot_nt(qi_ref[:, h * LANES:(h + 1) * LANES], kic)
            acc = acc + jnp.maximum(d, 0.0) * wi_cols[h]
        s = jnp.where((c < i) | (col_i <= row_i), acc, NEG)
        key_ref[c] = _sort_key(s)
        return carry

    lax.fori_loop(0, nchunk, score_chunk, 0)

    rb = SEARCH_ROWS

    def count(r0, pred):
        def body(c, cnt):
            for k in range(nsub):
                blk = key_ref[c, r0:r0 + rb, k * LANES:(k + 1) * LANES]
                cnt = cnt + pred(blk, c * tq + k * LANES).astype(I32)
            return cnt
        cnt = lax.fori_loop(0, nchunk, body, jnp.zeros((rb, LANES), I32))
        return jnp.sum(cnt, axis=1, keepdims=True)

    any_excess = jnp.int32(0)
    for r in range(tq // rb):
        r0 = r * rb

        def bit_body(bi, t):
            cand = t + lax.shift_left(jnp.int32(1), 31 - bi)
            cand_b = jnp.broadcast_to(cand, (rb, LANES))
            cnt = count(r0, lambda blk, _: blk >= cand_b)
            return jnp.where(cnt >= n_keep, cand, t)

        t = lax.fori_loop(0, 32, bit_body, jnp.full((rb, 1), INT_MIN, I32))
        t_b = jnp.broadcast_to(t, (rb, LANES))
        cnt_gt = count(r0, lambda blk, _: blk > t_b)
        cnt_eq = count(r0, lambda blk, _: blk == t_b)
        need = n_keep - cnt_gt
        excess = (cnt_eq > need) & (t > KEY_HALF)
        any_excess = any_excess + jnp.max(excess.astype(I32))
        t_ref[r0:r0 + rb, :] = t_b
        need_ref[r0:r0 + rb, :] = jnp.broadcast_to(need, (rb, LANES))

    j_ref[...] = jnp.full(j_ref.shape, seq_len, I32)

    @pl.when(any_excess > 0)
    def _():
        nbits = max(1, (seq_len - 1).bit_length())
        lane_rb = lax.broadcasted_iota(I32, (rb, LANES), 1)
        for r in range(tq // rb):
            r0 = r * rb
            t_b = t_ref[r0:r0 + rb, :]
            need = need_ref[r0:r0 + rb, 0:1]

            def jbit(bi, jj):
                cand = jj | lax.shift_left(jnp.int32(1), nbits - 1 - bi)
                cand_b = jnp.broadcast_to(cand, (rb, LANES))
                g = count(r0, lambda blk, base: (blk == t_b) & ((base + lane_rb) < cand_b))
                return jnp.where(g < need, cand, jj)

            jj = lax.fori_loop(0, nbits, jbit, jnp.zeros((rb, 1), I32))
            j_ref[r0:r0 + rb, :] = jnp.broadcast_to(jj, (rb, LANES))

    t_all = t_ref[...]
    j_all = j_ref[...]

    def bias_chunk(c, carry):
        for k in range(nsub):
            blk = key_ref[c, :, k * LANES:(k + 1) * LANES]
            col = c * tq + k * LANES + lane
            sel = (blk > KEY_HALF) & ((blk > t_all) | ((blk == t_all) & (col <= j_all)))
            bias_ref[c, :, k * LANES:(k + 1) * LANES] = jnp.where(sel, 0.0, NEG)
        return carry

    lax.fori_loop(0, nchunk, bias_chunk, 0)

    nkv = N_KV_HEADS * HEAD_DIM
    gq = GROUP * tq

    def group_logits(c, g):
        rows = pl.ds(pl.multiple_of(c * tq, tq), tq)
        kc = kv_ref[rows, g * HEAD_DIM:(g + 1) * HEAD_DIM]
        qg = jnp.concatenate([q_ref[:, h * HEAD_DIM:(h + 1) * HEAD_DIM]
                              for h in range(g * GROUP, (g + 1) * GROUP)], axis=0)
        s = _dot_nt(qg, kc).reshape(GROUP, tq, tq) + bias_ref[c][None]
        return s.reshape(gq, tq)

    m_ref[...] = jnp.full(m_ref.shape, NEG, F32)

    def max_chunk(c, carry):
        for g in range(N_KV_HEADS):
            s = group_logits(c, g)
            mx = s[:, :LANES]
            for k in range(1, nsub):
                mx = jnp.maximum(mx, s[:, k * LANES:(k + 1) * LANES])
            m_ref[g] = jnp.maximum(m_ref[g], mx)
        return carry

    lax.fori_loop(0, nchunk, max_chunk, 0)
    for g in range(N_KV_HEADS):
        m_ref[g] = jnp.broadcast_to(jnp.max(m_ref[g], axis=1, keepdims=True), (gq, LANES))

    acc_ref[...] = jnp.zeros_like(acc_ref)
    ones = jnp.ones((tq, HEAD_DIM), BF16)

    def pv_chunk(c, carry):
        rows = pl.ds(pl.multiple_of(c * tq, tq), tq)
        for g in range(N_KV_HEADS):
            s = group_logits(c, g)
            m = m_ref[g]
            p = jnp.concatenate([jnp.exp(s[:, k * LANES:(k + 1) * LANES] - m) for k in range(nsub)], axis=1)
            vc = kv_ref[rows, nkv + g * HEAD_DIM:nkv + (g + 1) * HEAD_DIM]
            acc_ref[g] = acc_ref[g] + _dot(p.astype(BF16), jnp.concatenate([vc, ones], axis=1))
        return carry

    lax.fori_loop(0, nchunk, pv_chunk, 0)
    for h in range(N_HEADS):
        a = acc_ref[h // GROUP, (h % GROUP) * tq:(h % GROUP + 1) * tq, :]
        o_ref[:, h * HEAD_DIM:(h + 1) * HEAD_DIM] = (a[:, :HEAD_DIM] / a[:, HEAD_DIM:]).astype(BF16)


def _attn_prompt(q, kvb, qi, kib, idx, nb):
    t = q.shape[0]
    s = t // nb
    tq = min(ATT_Q, s)
    nq = s // tq
    n_keep = min(TOPK_KEYS, s // 4)
    blk = lambda w: pl.BlockSpec((tq, w), lambda b, i: (b * nq + i, 0))
    seq = lambda w: pl.BlockSpec((s, w), lambda b, i: (b, 0))
    return pl.pallas_call(
        functools.partial(_attn_prompt_kernel, n_keep=n_keep, seq_len=s),
        grid=(nb, nq),
        in_specs=[blk(q.shape[1]), seq(kvb.shape[1]), blk(qi.shape[1]), seq(kib.shape[1]), blk(idx.shape[1])],
        out_specs=blk(N_HEADS * HEAD_DIM),
        out_shape=jax.ShapeDtypeStruct((t, N_HEADS * HEAD_DIM), BF16),
        scratch_shapes=[pltpu.VMEM((nq, tq, tq), I32), pltpu.VMEM((nq, tq, tq), F32),
                        pltpu.VMEM((tq, LANES), I32), pltpu.VMEM((tq, LANES), I32), pltpu.VMEM((tq, LANES), I32),
                        pltpu.VMEM((N_KV_HEADS, GROUP * tq, LANES), F32),
                        pltpu.VMEM((N_KV_HEADS, GROUP * tq, 2 * HEAD_DIM), F32)],
        compiler_params=_params(2),
        name="attn_prompt",
    )(q, kvb, qi, kib, idx)


def _attn_decode_kernel(pt_ref, q_ref, qi_ref, wi_ref, kin_ref, kn_ref, vn_ref, ck_hbm, cv_hbm, cx_hbm, o_ref,
                        kbuf, vbuf, xbuf, key_ref, bias_ref, s_ref, sem, *, n_pages, n_keep):
    b = pl.program_id(0)
    kv_rows = PAGE_SIZE * N_KV_HEADS
    past = n_pages * PAGE_SIZE
    cw = 2 * PAGE_SIZE
    nrow = past // cw

    def page_copies(p):
        phys = pt_ref[b, p]
        src_kv = pl.ds(pl.multiple_of(phys * kv_rows, kv_rows), kv_rows)
        dst_kv = pl.ds(pl.multiple_of(p * kv_rows, kv_rows), kv_rows)
        src_x = pl.ds(pl.multiple_of(phys * PAGE_SIZE, PAGE_SIZE), PAGE_SIZE)
        dst_x = pl.ds(pl.multiple_of(p * PAGE_SIZE, PAGE_SIZE), PAGE_SIZE)
        return (pltpu.make_async_copy(ck_hbm.at[src_kv], kbuf.at[dst_kv], sem.at[0]),
                pltpu.make_async_copy(cv_hbm.at[src_kv], vbuf.at[dst_kv], sem.at[1]),
                pltpu.make_async_copy(cx_hbm.at[src_x], xbuf.at[dst_x], sem.at[2]))

    def start_page(p, carry):
        for cp in page_copies(p):
            cp.start()
        return carry

    def wait_page(p, carry):
        for cp in page_copies(p):
            cp.wait()
        return carry

    lax.fori_loop(0, n_pages, start_page, 0)
    lax.fori_loop(0, n_pages, wait_page, 0)

    q8 = q_ref[0]
    qi8 = qi_ref[0]
    wi8 = wi_ref[0] * IDX_SCALE

    def score_row(r, carry):
        kx = xbuf[pl.ds(pl.multiple_of(r * cw, cw), cw), :].astype(BF16)
        d = _dot_nt(qi8, kx)
        sc = jnp.sum(jnp.maximum(d, 0.0) * wi8, axis=0, keepdims=True)
        key_ref[pl.ds(r, 1), :] = _sort_key(sc)
        return carry

    lax.fori_loop(0, nrow, score_row, 0)
    dn = jnp.sum(qi8.astype(F32) * kin_ref[0].astype(F32), axis=1, keepdims=True)
    key_new = _sort_key(jnp.sum(jnp.maximum(dn, 0.0) * wi8, axis=0, keepdims=True))

    keys = key_ref[...]
    col = (lax.broadcasted_iota(I32, keys.shape, 0) * cw + lax.broadcasted_iota(I32, keys.shape, 1))

    def total(x):
        return jnp.sum(jnp.sum(x.astype(I32), axis=1, keepdims=True), axis=0, keepdims=True)

    def bit_body(bi, t):
        cand = t + lax.shift_left(jnp.int32(1), 31 - bi)
        cnt = total(keys >= cand) + (key_new >= cand).astype(I32)
        return jnp.where(cnt >= n_keep, cand, t)

    t = lax.fori_loop(0, 32, bit_body, jnp.full((1, 1), INT_MIN, I32))
    need = n_keep - (total(keys > t) + (key_new > t).astype(I32))
    nbits = past.bit_length()

    def jbit(bi, jj):
        cand = jj | lax.shift_left(jnp.int32(1), nbits - 1 - bi)
        g = total((keys == t) & (col < cand)) + ((key_new == t) & (past < cand)).astype(I32)
        return jnp.where(g < need, cand, jj)

    jj = lax.fori_loop(0, nbits, jbit, jnp.zeros((1, 1), I32))
    sel = (keys > t) | ((keys == t) & (col <= jj))
    sel_new = (key_new > t) | ((key_new == t) & (past <= jj))

    e_r = lax.broadcasted_iota(I32, (cw, N_KV_HEADS * cw), 0)
    e_c = lax.broadcasted_iota(I32, (cw, N_KV_HEADS * cw), 1)
    expand = (e_c // N_KV_HEADS == e_r).astype(BF16)
    sel2 = _dot(sel.astype(BF16), expand)
    bias_ref[...] = jnp.where(sel2 > 0.5, 0.0, NEG)

    ncol = N_KV_HEADS * cw
    head_i = lax.broadcasted_iota(I32, (N_HEADS, ncol), 0)
    col_i = lax.broadcasted_iota(I32, (N_HEADS, ncol), 1)
    same_group = (col_i % N_KV_HEADS) == (head_i // GROUP)

    def logit_row(r, m):
        kc = kbuf[pl.ds(pl.multiple_of(r * ncol, ncol), ncol), :].astype(BF16)
        s = jnp.where(same_group, _dot_nt(q8, kc) + bias_ref[pl.ds(r, 1), :], NEG)
        s_ref[r] = s
        return jnp.maximum(m, jnp.max(s, axis=1, keepdims=True))

    s_new = jnp.sum(q8.astype(F32) * kn_ref[0].astype(F32), axis=1, keepdims=True)
    s_new = jnp.where(sel_new, s_new, NEG)
    m = lax.fori_loop(0, nrow, logit_row, s_new)

    def pv_row(r, carry):
        l, acc = carry
        p = jnp.exp(s_ref[r] - m)
        vc = vbuf[pl.ds(pl.multiple_of(r * ncol, ncol), ncol), :].astype(BF16)
        return l + jnp.sum(p, axis=1, keepdims=True), acc + _dot(p.astype(BF16), vc)

    p_new = jnp.exp(s_new - m)
    l0 = p_new
    acc0 = p_new.astype(BF16).astype(F32) * vn_ref[0].astype(F32)
    l, acc = lax.fori_loop(0, nrow, pv_row, (l0, acc0))
    o_ref[0] = (acc / l).astype(BF16)


def _attn_decode(page_table, q8, qi8, wi8, kin, kn8, vn8, ck2, cv2, cx2):
    db, n_pages = page_table.shape
    past = n_pages * PAGE_SIZE
    n_keep = min(TOPK_KEYS, (past + 1) // 4)
    cw = 2 * PAGE_SIZE
    nrow = past // cw
    per_b = lambda a: pl.BlockSpec((1,) + a.shape[1:], lambda b, pt: (b, 0, 0))
    any_spec = pl.BlockSpec(memory_space=pl.ANY)
    grid_spec = pltpu.PrefetchScalarGridSpec(
        num_scalar_prefetch=1,
        grid=(db,),
        in_specs=[per_b(q8), per_b(qi8), per_b(wi8), per_b(kin), per_b(kn8), per_b(vn8),
                  any_spec, any_spec, any_spec],
        out_specs=pl.BlockSpec((1, N_HEADS, HEAD_DIM), lambda b, pt: (b, 0, 0)),
        scratch_shapes=[pltpu.VMEM((past * N_KV_HEADS, HEAD_DIM), F32),
                        pltpu.VMEM((past * N_KV_HEADS, HEAD_DIM), F32),
                        pltpu.VMEM((past, IDX_DIM), F32),
                        pltpu.VMEM((nrow, cw), I32),
                        pltpu.VMEM((nrow, N_KV_HEADS * cw), F32),
                        pltpu.VMEM((nrow, N_HEADS, N_KV_HEADS * cw), F32),
                        pltpu.SemaphoreType.DMA((3,))],
    )
    return pl.pallas_call(
        functools.partial(_attn_decode_kernel, n_pages=n_pages, n_keep=n_keep),
        grid_spec=grid_spec,
        out_shape=jax.ShapeDtypeStruct((db, N_HEADS, HEAD_DIM), BF16),
        compiler_params=_params(1),
        name="attn_decode",
    )(page_table, q8, qi8, wi8, kin, kn8, vn8, ck2, cv2, cx2)


def _layer_norm(r, g, b):
    mu = jnp.mean(r, axis=-1, keepdims=True)
    d = r - mu
    var = jnp.mean(d * d, axis=-1, keepdims=True)
    return d * lax.rsqrt(var + LN_EPS) * g + b


def _route(logits_t, rbias):
    n_exp, tm = logits_t.shape
    epg = n_exp // N_GROUPS
    s = jax.nn.sigmoid(logits_t)
    sb = s + rbias
    ie = lax.broadcasted_iota(I32, (epg, tm), 0)
    gs_rows = []
    for g in range(N_GROUPS):
        blk = sb[g * epg:(g + 1) * epg, :]
        m1 = jnp.max(blk, axis=0, keepdims=True)
        i1 = jnp.min(jnp.where(blk == m1, ie, epg), axis=0, keepdims=True)
        m2 = jnp.max(jnp.where(ie == i1, -jnp.inf, blk), axis=0, keepdims=True)
        gs_rows.append(m1 + m2)
    picked = [jnp.zeros((1, tm), jnp.bool_) for _ in range(N_GROUPS)]
    cur = list(gs_rows)
    for _ in range(TOPK_GROUPS):
        mx = cur[0]
        for g in range(1, N_GROUPS):
            mx = jnp.maximum(mx, cur[g])
        found = jnp.zeros((1, tm), jnp.bool_)
        for g in range(N_GROUPS):
            hit = (cur[g] == mx) & jnp.logical_not(found)
            found = found | hit
            picked[g] = picked[g] | hit
            cur[g] = jnp.where(hit, -jnp.inf, cur[g])
    masked = jnp.concatenate(
        [jnp.where(picked[g], sb[g * epg:(g + 1) * epg, :], NEG) for g in range(N_GROUPS)], axis=0)
    iall = lax.broadcasted_iota(I32, (n_exp, tm), 0)
    e_rows, w_rows = [], []
    for _ in range(TOP_K):
        mx = jnp.max(masked, axis=0, keepdims=True)
        ix = jnp.min(jnp.where(masked == mx, iall, n_exp), axis=0, keepdims=True)
        hit = iall == ix
        w_rows.append(jnp.sum(jnp.where(hit, s, 0.0), axis=0, keepdims=True))
        e_rows.append(ix)
        masked = jnp.where(hit, -jnp.inf, masked)
    wsum = w_rows[0]
    for w in w_rows[1:]:
        wsum = wsum + w
    gates = [w / wsum * ROUTED_SCALE for w in w_rows]
    return jnp.concatenate(e_rows, axis=0), jnp.concatenate(gates, axis=0)


def _post_attn_kernel(xp, xs, ap, as_, cp, cs, gap, gas, gcp, gcs, woa, woc, wo, g1, b1, rwt, rb,
                      h2_o, e_o, gate_o, *, n_prompt_tiles):
    i = pl.program_id(0)
    is_p = i < n_prompt_tiles
    pick = lambda a, b: jnp.where(is_p, a[...], b[...])
    a = _dot(pick(ap, as_), woa[...])
    c = _dot(pick(cp, cs), woc[...])
    merged = pick(gap, gas).astype(F32) * a + pick(gcp, gcs).astype(F32) * c
    r = ALPHA * pick(xp, xs) + _dot(merged.astype(BF16), wo[...])
    h = _layer_norm(r, g1[...], b1[...])
    tm = h.shape[0]
    for k in range(D_MODEL // LANES):
        h2_o[pl.ds(k, tm, stride=SUBLANES), :] = h[:, k * LANES:(k + 1) * LANES]
    logits_t = lax.dot_general(rwt[...], h, (((1,), (1,)), ((), ())),
                               precision=lax.Precision.HIGHEST, preferred_element_type=F32)
    e_idx, gates = _route(logits_t, rb[...])
    e_o[...] = e_idx
    gate_o[...] = gates


def _post_attn(x_p, x_s, attn_p, attn_s, conv_p, conv_s, ga_p, ga_s, gc_p, gc_s,
               woa, woc, wo, g1, b1, rwt, rb):
    tp, ts = x_p.shape[0], x_s.shape[0]
    tm = TOKEN_TILE
    npt, nst = tp // tm, ts // tm
    n_tok = tp + ts
    p_spec = pl.BlockSpec((tm, D_MODEL), lambda i: (jnp.minimum(i, npt - 1), 0))
    s_spec = pl.BlockSpec((tm, D_MODEL), lambda i: (jnp.maximum(i - npt, 0), 0))
    consts = [woa, woc, wo, g1, b1, rwt, rb]
    return pl.pallas_call(
        functools.partial(_post_attn_kernel, n_prompt_tiles=npt),
        grid=(npt + nst,),
        in_specs=[p_spec, s_spec] * 5 + [_const_spec(c.shape) for c in consts],
        out_specs=[pl.BlockSpec((tm * SUBLANES, LANES), lambda i: (i, 0)),
                   pl.BlockSpec((TOP_K, tm), lambda i: (0, i)),
                   pl.BlockSpec((TOP_K, tm), lambda i: (0, i))],
        out_shape=[jax.ShapeDtypeStruct((n_tok * SUBLANES, LANES), F32),
                   jax.ShapeDtypeStruct((TOP_K, n_tok), I32),
                   jax.ShapeDtypeStruct((TOP_K, n_tok), F32)],
        compiler_params=_params(1),
        name="post_attn",
    )(x_p, x_s, attn_p, attn_s, conv_p, conv_s, ga_p, ga_s, gc_p, gc_s, *consts)


def _rank_kernel(e_ref, rank_o, cnt_o, carry_ref):
    i = pl.program_id(0)

    @pl.when(i == 0)
    def _():
        carry_ref[...] = jnp.zeros_like(carry_ref)

    n_exp = carry_ref.shape[0]
    e = e_ref[...]
    tk = e.shape[1]
    ie = lax.broadcasted_iota(I32, (n_exp, tk), 0)
    onehot = jnp.zeros((n_exp, tk), F32)
    for j in range(TOP_K):
        onehot = onehot + (ie == e[j:j + 1, :]).astype(F32)
    before = (lax.broadcasted_iota(I32, (tk, tk), 0) < lax.broadcasted_iota(I32, (tk, tk), 1)).astype(BF16)
    prefix = _dot(onehot.astype(BF16), before) + carry_ref[:, 0:1]
    rows = [jnp.sum(jnp.where(ie == e[j:j + 1, :], prefix, 0.0), axis=0, keepdims=True) for j in range(TOP_K)]
    rank_o[...] = jnp.concatenate(rows, axis=0).astype(I32)
    carry_ref[...] = carry_ref[...] + jnp.sum(onehot, axis=1, keepdims=True)
    cnt_o[...] = carry_ref[...].astype(I32)


def _rank(e_t, n_exp):
    n_tok = e_t.shape[1]
    tk = TOKEN_TILE
    return pl.pallas_call(
        _rank_kernel,
        grid=(n_tok // tk,),
        in_specs=[pl.BlockSpec((TOP_K, tk), lambda i: (0, i))],
        out_specs=[pl.BlockSpec((TOP_K, tk), lambda i: (0, i)),
                   pl.BlockSpec((n_exp, LANES), lambda i: (0, 0))],
        out_shape=[jax.ShapeDtypeStruct((TOP_K, n_tok), I32), jax.ShapeDtypeStruct((n_exp, LANES), I32)],
        scratch_shapes=[pltpu.VMEM((n_exp, LANES), F32)],
        compiler_params=_params(1),
        name="moe_rank",
    )(e_t)


def _slot_kernel(e_ref, rank_ref, pstart_ref, slot_o):
    e = e_ref[...]
    n_exp = pstart_ref.shape[0]
    tk = e.shape[1]
    ie = lax.broadcasted_iota(I32, (n_exp, tk), 0)
    ps = pstart_ref[:, 0:1]
    rows = [jnp.sum(jnp.where(ie == e[j:j + 1, :], ps, 0.0), axis=0, keepdims=True) for j in range(TOP_K)]
    slot_o[...] = rank_ref[...] + jnp.concatenate(rows, axis=0).astype(I32)


def _slots(e_t, rank_t, pstart_f):
    n_tok = e_t.shape[1]
    tk = TOKEN_TILE
    spec = pl.BlockSpec((TOP_K, tk), lambda i: (0, i))
    return pl.pallas_call(
        _slot_kernel,
        grid=(n_tok // tk,),
        in_specs=[spec, spec, _const_spec(pstart_f.shape)],
        out_specs=spec,
        out_shape=jax.ShapeDtypeStruct((TOP_K, n_tok), I32),
        compiler_params=_params(1),
        name="moe_slots",
    )(e_t, rank_t, pstart_f)


def _dispatch_kernel(pend_ref, padded_ref, slot_ref, h2_ref, xs_hbm, zbuf, sem, zsem, *, n_exp):
    i = pl.program_id(0)
    td = slot_ref.shape[1]
    blk_rows = MOE_BLOCK * SUBLANES

    def zero_copy(e):
        dst = pl.ds(pl.multiple_of((pend_ref[e] - MOE_BLOCK) * SUBLANES, SUBLANES), blk_rows)
        return pltpu.make_async_copy(zbuf, xs_hbm.at[dst], zsem)

    @pl.when(i == 0)
    def _():
        zbuf[...] = jnp.zeros_like(zbuf)

        def start(e, carry):
            @pl.when(padded_ref[e] > 0)
            def _():
                zero_copy(e).start()
            return carry

        def wait(e, carry):
            @pl.when(padded_ref[e] > 0)
            def _():
                zero_copy(e).wait()
            return carry

        lax.fori_loop(0, n_exp, start, 0)
        lax.fori_loop(0, n_exp, wait, 0)

    def row_copy(t, j):
        src = pl.ds(pl.multiple_of(t * SUBLANES, SUBLANES), SUBLANES)
        dst = pl.ds(pl.multiple_of(slot_ref[j, t] * SUBLANES, SUBLANES), SUBLANES)
        return pltpu.make_async_copy(h2_ref.at[src], xs_hbm.at[dst], sem)

    def start_tok(t, carry):
        for j in range(TOP_K):
            row_copy(t, j).start()
        return carry

    def wait_tok(t, carry):
        for j in range(TOP_K):
            row_copy(t, j).wait()
        return carry

    lax.fori_loop(0, td, start_tok, 0)
    lax.fori_loop(0, td, wait_tok, 0)


def _dispatch(pend, padded, slot_t, h2, m_pad):
    n_tok = slot_t.shape[1]
    td = TOKEN_TILE
    n_exp = pend.shape[0]
    grid_spec = pltpu.PrefetchScalarGridSpec(
        num_scalar_prefetch=2,
        grid=(n_tok // td,),
        in_specs=[pl.BlockSpec((TOP_K, td), lambda i, *_: (0, i), memory_space=pltpu.SMEM),
                  pl.BlockSpec((td * SUBLANES, LANES), lambda i, *_: (i, 0))],
        out_specs=pl.BlockSpec(memory_space=pl.ANY),
        scratch_shapes=[pltpu.VMEM((MOE_BLOCK * SUBLANES, LANES), F32),
                        pltpu.SemaphoreType.DMA(()), pltpu.SemaphoreType.DMA(())],
    )
    return pl.pallas_call(
        functools.partial(_dispatch_kernel, n_exp=n_exp),
        grid_spec=grid_spec,
        out_shape=jax.ShapeDtypeStruct((m_pad * SUBLANES, LANES), F32),
        compiler_params=_params(1),
        name="moe_dispatch",
    )(pend, padded, slot_t, h2)


def _tile_rows(ref, n):
    return jnp.concatenate([ref[pl.ds(k, n, stride=SUBLANES), :] for k in range(D_MODEL // LANES)], axis=1)


def _expert_kernel(be_ref, nused_ref, x_ref, wg_ref, wu_ref, wd_ref, y_ref, wgu_b, wd_b):
    i = pl.program_id(0)
    d_exp = wd_b.shape[0]

    @pl.when(i < nused_ref[0])
    def _():
        changed = (i == 0) | (be_ref[i] != be_ref[jnp.maximum(i - 1, 0)])

        @pl.when(changed)
        def _():
            wgu_b[:, :d_exp] = wg_ref[0].astype(BF16)
            wgu_b[:, d_exp:] = wu_ref[0].astype(BF16)
            wd_b[...] = wd_ref[0].astype(BF16)

        x = _tile_rows(x_ref, MOE_BLOCK).astype(BF16)
        gu = _dot(x, wgu_b[...])
        hh = jax.nn.silu(gu[:, :d_exp]) * gu[:, d_exp:]
        y = _dot(hh.astype(BF16), wd_b[...])
        for k in range(D_MODEL // LANES):
            y_ref[pl.ds(k, MOE_BLOCK, stride=SUBLANES), :] = y[:, k * LANES:(k + 1) * LANES]


def _experts(block_e, nused, xs, w_gate, w_up, w_down):
    n_blocks = block_e.shape[0]
    d_exp = w_gate.shape[2]
    rows = MOE_BLOCK * SUBLANES
    last = lambda i, nu: jnp.minimum(i, nu[0] - 1)
    grid_spec = pltpu.PrefetchScalarGridSpec(
        num_scalar_prefetch=2,
        grid=(n_blocks,),
        in_specs=[pl.BlockSpec((rows, LANES), lambda i, be, nu: (last(i, nu), 0)),
                  pl.BlockSpec((1, D_MODEL, d_exp), lambda i, be, nu: (be[i], 0, 0)),
                  pl.BlockSpec((1, D_MODEL, d_exp), lambda i, be, nu: (be[i], 0, 0)),
                  pl.BlockSpec((1, d_exp, D_MODEL), lambda i, be, nu: (be[i], 0, 0))],
        out_specs=pl.BlockSpec((rows, LANES), lambda i, be, nu: (last(i, nu), 0)),
        scratch_shapes=[pltpu.VMEM((D_MODEL, 2 * d_exp), BF16), pltpu.VMEM((d_exp, D_MODEL), BF16)],
    )
    return pl.pallas_call(
        _expert_kernel,
        grid_spec=grid_spec,
        out_shape=jax.ShapeDtypeStruct(xs.shape, F32),
        compiler_params=_params(1),
        name="moe_experts",
    )(block_e, nused, xs, w_gate, w_up, w_down)


def _combine_kernel(slot_ref, gate_ref, h2_ref, ys_hbm, shg, shu, shd, g2, b2, yp_o, ys_o, buf, sem,
                    *, n_prompt_tiles):
    i = pl.program_id(0)
    tc = gate_ref.shape[0]

    def row_copy(t, j):
        src = pl.ds(pl.multiple_of(slot_ref[j, t] * SUBLANES, SUBLANES), SUBLANES)
        dst = pl.ds(pl.multiple_of(t * SUBLANES, SUBLANES), SUBLANES)
        return pltpu.make_async_copy(ys_hbm.at[src], buf.at[j, dst], sem)

    def start_tok(t, carry):
        for j in range(TOP_K):
            row_copy(t, j).start()
        return carry

    def wait_tok(t, carry):
        for j in range(TOP_K):
            row_copy(t, j).wait()
        return carry

    lax.fori_loop(0, tc, start_tok, 0)
    h = _tile_rows(h2_ref, tc)
    hb = h.astype(BF16)
    shared = _dot((jax.nn.silu(_dot(hb, shg[...])) * _dot(hb, shu[...])).astype(BF16), shd[...])
    lax.fori_loop(0, tc, wait_tok, 0)
    routed = jnp.zeros((tc, D_MODEL), F32)
    for j in range(TOP_K):
        routed = routed + _tile_rows(buf.at[j], tc) * gate_ref[:, j:j + 1]
    y = _layer_norm(ALPHA * h + (routed + shared), g2[...], b2[...])

    @pl.when(i < n_prompt_tiles)
    def _():
        yp_o[...] = y

    @pl.when(i >= n_prompt_tiles)
    def _():
        ys_o[...] = y


def _combine(slot_t, gates, h2, ys, shg, shu, shd, g2, b2, n_prompt, n_decode):
    tc = TOKEN_TILE
    npt, nst = n_prompt // tc, n_decode // tc
    consts = [shg, shu, shd, g2, b2]
    return pl.pallas_call(
        functools.partial(_combine_kernel, n_prompt_tiles=npt),
        grid=(npt + nst,),
        in_specs=[pl.BlockSpec((TOP_K, tc), lambda i: (0, i), memory_space=pltpu.SMEM),
                  pl.BlockSpec((tc, TOP_K), lambda i: (i, 0)),
                  pl.BlockSpec((tc * SUBLANES, LANES), lambda i: (i, 0)),
                  pl.BlockSpec(memory_space=pl.ANY)] + [_const_spec(c.shape) for c in consts],
        out_specs=[pl.BlockSpec((tc, D_MODEL), lambda i: (jnp.minimum(i, npt - 1), 0)),
                   pl.BlockSpec((tc, D_MODEL), lambda i: (jnp.maximum(i - npt, 0), 0))],
        out_shape=[jax.ShapeDtypeStruct((n_prompt, D_MODEL), F32),
                   jax.ShapeDtypeStruct((n_decode, D_MODEL), F32)],
        scratch_shapes=[pltpu.VMEM((TOP_K, tc * SUBLANES, LANES), F32), pltpu.SemaphoreType.DMA(())],
        compiler_params=_params(1),
        name="moe_combine",
    )(slot_t, gates, h2, ys, *consts)


def _pad_rows(a, n):
    return jnp.pad(a, ((0, n - a.shape[0]),) + ((0, 0),) * (a.ndim - 1))


def kernel(x_prompt, x_sample, cache_k, cache_v, cache_kidx, page_table, state_conv, w_in, conv_w, w_o_attn,
           w_o_conv, w_o, ln1_g, ln1_b, router_w, router_bias, moe_w_gate, moe_w_up, moe_w_down,
           shared_w_gate, shared_w_up, shared_w_down, ln2_g, ln2_b):
    nb, seq, _ = x_prompt.shape
    db = x_sample.shape[0]
    n_pool = cache_k.shape[1]
    n_exp = router_w.shape[-1]
    tp = nb * seq
    ts = TOKEN_TILE
    nkv = N_KV_HEADS * HEAD_DIM

    weights = _prep_in_weights(w_in[0])
    cw = conv_w[0]

    xp2 = x_prompt.reshape(tp, D_MODEL)
    (q_p, k_p, v_p, kvb_p, qi_p, idx_p, kib_p, conv_p, ga_p, gc_p, ulast_p) = _in_proj_prompt(xp2, weights, cw, nb)
    xs2 = _pad_rows(x_sample.reshape(db, D_MODEL), ts)
    prev2 = _pad_rows(state_conv[0, :, 0, :], ts)
    prev1 = _pad_rows(state_conv[0, :, 1, :], ts)
    (q_s, k_s, v_s, _, qi_s, idx_s, _, conv_s, ga_s, gc_s, u_s) = _in_proj_decode(xs2, prev2, prev1, weights, cw)

    attn_p = _attn_prompt(q_p, kvb_p, qi_p, kib_p, idx_p, nb)
    q8 = q_s[:db].reshape(db, N_HEADS, HEAD_DIM)
    qi8 = qi_s[:db].reshape(db, IDX_HEADS, LANES)[:, :, :IDX_DIM]
    wi8 = idx_s[:db, IDX_DIM:IDX_DIM + IDX_HEADS].reshape(db, IDX_HEADS, 1)
    kin = idx_s[:db, :IDX_DIM].astype(BF16).reshape(db, 1, IDX_DIM)
    kn8 = jnp.repeat(k_s[:db].reshape(db, N_KV_HEADS, HEAD_DIM), GROUP, axis=1).astype(BF16)
    vn8 = jnp.repeat(v_s[:db].reshape(db, N_KV_HEADS, HEAD_DIM), GROUP, axis=1).astype(BF16)
    ck2 = cache_k[0].reshape(n_pool * PAGE_SIZE * N_KV_HEADS, HEAD_DIM)
    cv2 = cache_v[0].reshape(n_pool * PAGE_SIZE * N_KV_HEADS, HEAD_DIM)
    cx2 = cache_kidx[0].reshape(n_pool * PAGE_SIZE, IDX_DIM)
    attn_s8 = _attn_decode(page_table, q8, qi8, wi8, kin, kn8, vn8, ck2, cv2, cx2)
    attn_s = _pad_rows(attn_s8.reshape(db, N_HEADS * HEAD_DIM), ts)

    h2, e_t, gate_t = _post_attn(
        xp2, xs2, attn_p, attn_s, conv_p, conv_s, ga_p, ga_s, gc_p, gc_s,
        w_o_attn[0].astype(BF16), w_o_conv[0].astype(BF16), w_o[0].astype(BF16),
        ln1_g[0].reshape(1, D_MODEL), ln1_b[0].reshape(1, D_MODEL),
        router_w[0].T, router_bias[0].reshape(n_exp, 1))

    n_tok = tp + ts
    rank_t, cnt = _rank(e_t, n_exp)
    counts = cnt[:, 0]
    padded = (counts + MOE_BLOCK - 1) // MOE_BLOCK * MOE_BLOCK
    pend = jnp.cumsum(padded)
    pstart = pend - padded
    n_blocks = (n_tok * TOP_K + n_exp * (MOE_BLOCK - 1) + MOE_BLOCK - 1) // MOE_BLOCK
    nused = (pend[-1] // MOE_BLOCK).astype(I32)
    blk = jnp.minimum(jnp.arange(n_blocks, dtype=I32), nused - 1)
    block_e = jnp.minimum(jnp.searchsorted(pend, blk * MOE_BLOCK, side='right'), n_exp - 1).astype(I32)
    slot_t = _slots(e_t, rank_t, jnp.broadcast_to(pstart.astype(F32)[:, None], (n_exp, LANES)))
    xs = _dispatch(pend.astype(I32), padded.astype(I32), slot_t, h2, n_blocks * MOE_BLOCK)
    ys = _experts(block_e, nused.reshape(1), xs, moe_w_gate[0], moe_w_up[0], moe_w_down[0])
    y_p, y_s = _combine(slot_t, gate_t.T, h2, ys,
                        shared_w_gate[0].astype(BF16), shared_w_up[0].astype(BF16),
                        shared_w_down[0].astype(BF16),
                        ln2_g[0].reshape(1, D_MODEL), ln2_b[0].reshape(1, D_MODEL), tp, ts)

    conv_sample = jnp.stack([state_conv[0, :, 1, :], u_s[:db]], axis=1)[None]
    return (y_p.reshape(nb, seq, D_MODEL),
            y_s[:db].reshape(db, 1, D_MODEL),
            k_p.reshape(1, nb, seq, N_KV_HEADS, HEAD_DIM),
            v_p.reshape(1, nb, seq, N_KV_HEADS, HEAD_DIM),
            idx_p[:, :IDX_DIM].reshape(1, nb, seq, IDX_DIM),
            ulast_p[:, SUBLANES - (CONV_W - 1):, :][None],
            k_s[:db].reshape(1, db, 1, N_KV_HEADS, HEAD_DIM),
            v_s[:db].reshape(1, db, 1, N_KV_HEADS, HEAD_DIM),
            idx_s[:db, :IDX_DIM].reshape(1, db, 1, IDX_DIM),
            conv_sample)
```

```python
import functools

import jax
import jax.numpy as jnp
import numpy as np
from jax import lax
from jax.experimental import pallas as pl
from jax.experimental.pallas import tpu as pltpu

F32 = jnp.float32
BF16 = jnp.bfloat16
I32 = jnp.int32

D_MODEL = 1024
N_HEADS = 8
HEAD_DIM = 128
N_KV_HEADS = 2
GROUP = N_HEADS // N_KV_HEADS
IDX_HEADS = 8
IDX_DIM = 64
TOPK_KEYS = 256
IDX_SCALE = IDX_DIM ** -0.5 * IDX_HEADS ** -0.5
Q_SCALE = HEAD_DIM ** -0.5
PAGE_SIZE = 128
D_CONV = D_MODEL
CONV_W = 3
TOP_K = 8
N_GROUPS = 8
TOPK_GROUPS = 4
ROUTED_SCALE = 2.5
DEPTH = 1
ALPHA = (2 * DEPTH) ** 0.25
LN_EPS = 1e-5
NEG = -1e30
IN_WIDTHS = (N_HEADS * HEAD_DIM, N_KV_HEADS * HEAD_DIM, N_KV_HEADS * HEAD_DIM,
             IDX_HEADS * IDX_DIM, IDX_DIM, IDX_HEADS,
             D_CONV, D_CONV, D_CONV, D_MODEL, D_MODEL)

LANES = 128
SUBLANES = 8
VMEM_LIMIT_BYTES = 56 * 1024 * 1024

PROJ_ROWS = 512
ATT_Q = 256
SEARCH_ROWS = 128
TOKEN_TILE = 256
MOE_BLOCK = 256
CONV_CHUNK = 256
INT_MIN = -2 ** 31


def _np_key(v):
    b = int(np.float32(v).view(np.int32))
    return b ^ ((b >> 31) & 0x7FFFFFFF)


KEY_HALF = _np_key(NEG * 0.5)


def _sort_key(s):
    b = lax.bitcast_convert_type(s + 0.0, I32)
    return b ^ ((b >> 31) & jnp.int32(0x7FFFFFFF))


def _dot(a, b):
    return jnp.dot(a, b, preferred_element_type=F32)


def _dot_nt(a, b):
    return lax.dot_general(a, b, (((1,), (1,)), ((), ())), preferred_element_type=F32)


def _params(n_grid):
    return pltpu.CompilerParams(dimension_semantics=("arbitrary",) * n_grid,
                                vmem_limit_bytes=VMEM_LIMIT_BYTES)


def _const_spec(shape):
    nd = len(shape)
    return pl.BlockSpec(shape, lambda *_: (0,) * nd, pipeline_mode=pl.Buffered(1))


def _proj_common(xb, w1, w2, w3, wga, wgc, q_o, k_o, v_o, kv_o, qi_o, idx_o, kib_o, sga_o, sgc_o):
    nq = N_HEADS * HEAD_DIM
    nkv = N_KV_HEADS * HEAD_DIM
    z1 = _dot(xb, w1[...])
    q_o[...] = (z1[:, :nq] * Q_SCALE).astype(BF16)
    k = z1[:, nq:nq + nkv]
    v = z1[:, nq + nkv:nq + 2 * nkv]
    k_o[...] = k
    v_o[...] = v
    kv_o[:, :nkv] = k.astype(BF16)
    kv_o[:, nkv:] = v.astype(BF16)
    qi_o[...] = _dot(xb, w2[...]).astype(BF16)
    z3 = _dot(xb, w3[...])
    idx_o[...] = z3
    kib_o[...] = z3.astype(BF16)
    sga_o[...] = jax.nn.sigmoid(_dot(xb, wga[...])).astype(BF16)
    sgc_o[...] = jax.nn.sigmoid(_dot(xb, wgc[...])).astype(BF16)


def _in_proj_prompt_kernel(x_ref, w1, w2, w3, wb, wc, wx, wga, wgc, cw_ref,
                           q_o, k_o, v_o, kv_o, qi_o, idx_o, kib_o, conv_o, sga_o, sgc_o, ulast_o,
                           tail_ref):
    j = pl.program_id(1)

    @pl.when(j == 0)
    def _():
        tail_ref[...] = jnp.zeros_like(tail_ref)

    xb = x_ref[...].astype(BF16)
    _proj_common(xb, w1, w2, w3, wga, wgc, q_o, k_o, v_o, kv_o, qi_o, idx_o, kib_o, sga_o, sgc_o)
    tm = xb.shape[0]
    row = lax.broadcasted_iota(I32, (tm, CONV_CHUNK), 0)
    for c in range(D_CONV // CONV_CHUNK):
        cs = slice(c * CONV_CHUNK, (c + 1) * CONV_CHUNK)
        gb = _dot(xb, wb[:, cs])
        u = _dot(xb, wc[:, cs]) * _dot(xb, wx[:, cs])
        p1 = tail_ref[SUBLANES - 1:SUBLANES, cs]
        p2 = tail_ref[SUBLANES - 2:SUBLANES - 1, cs]
        u1 = jnp.where(row == 0, p1, pltpu.roll(u, 1, 0))
        u2 = jnp.where(row == 0, p2, jnp.where(row == 1, p1, pltpu.roll(u, 2, 0)))
        conv = cw_ref[0:1, cs] * u2 + cw_ref[1:2, cs] * u1 + cw_ref[2:3, cs] * u
        conv_o[:, cs] = (gb * conv).astype(BF16)
        tail_ref[:, cs] = u[tm - SUBLANES:, :]
    ulast_o[0] = tail_ref[...]


def _in_proj_decode_kernel(x_ref, p2_ref, p1_ref, w1, w2, w3, wb, wc, wx, wga, wgc, cw_ref,
                           q_o, k_o, v_o, kv_o, qi_o, idx_o, kib_o, conv_o, sga_o, sgc_o, u_o):
    xb = x_ref[...].astype(BF16)
    _proj_common(xb, w1, w2, w3, wga, wgc, q_o, k_o, v_o, kv_o, qi_o, idx_o, kib_o, sga_o, sgc_o)
    for c in range(D_CONV // CONV_CHUNK):
        cs = slice(c * CONV_CHUNK, (c + 1) * CONV_CHUNK)
        gb = _dot(xb, wb[:, cs])
        u = _dot(xb, wc[:, cs]) * _dot(xb, wx[:, cs])
        conv = cw_ref[0:1, cs] * p2_ref[:, cs] + cw_ref[1:2, cs] * p1_ref[:, cs] + cw_ref[2:3, cs] * u
        conv_o[:, cs] = (gb * conv).astype(BF16)
        u_o[:, cs] = u


def _prep_in_weights(w_in):
    w = w_in.astype(BF16)
    offs = np.concatenate([[0], np.cumsum(IN_WIDTHS)])
    part = [w[:, int(offs[i]):int(offs[i + 1])] for i in range(len(IN_WIDTHS))]
    q, k, v, qi, ki, wi, gb, gc, xv, ga, gcv = part
    d = w.shape[0]
    w1 = jnp.concatenate([q, k, v], axis=1)
    w2 = jnp.pad(qi.reshape(d, IDX_HEADS, IDX_DIM),
                 ((0, 0), (0, 0), (0, LANES - IDX_DIM))).reshape(d, IDX_HEADS * LANES)
    w3 = jnp.concatenate([ki, wi, jnp.zeros((d, LANES - IDX_DIM - IDX_HEADS), BF16)], axis=1)
    return (w1, w2, w3, gb, gc, xv, ga, gcv)


def _proj_out_shapes(t):
    nkv = N_KV_HEADS * HEAD_DIM
    sd = jax.ShapeDtypeStruct
    return [sd((t, N_HEADS * HEAD_DIM), BF16), sd((t, nkv), F32), sd((t, nkv), F32), sd((t, 2 * nkv), BF16),
            sd((t, IDX_HEADS * LANES), BF16), sd((t, LANES), F32), sd((t, LANES), BF16),
            sd((t, D_CONV), BF16), sd((t, D_MODEL), BF16), sd((t, D_MODEL), BF16)]


def _in_proj_prompt(x2d, weights, conv_w, nb):
    t = x2d.shape[0]
    s = t // nb
    tm = min(PROJ_ROWS, s)
    nj = s // tm
    row_spec = lambda w: pl.BlockSpec((tm, w), lambda b, j: (b * nj + j, 0))
    out_shapes = _proj_out_shapes(t) + [jax.ShapeDtypeStruct((nb, SUBLANES, D_CONV), F32)]
    out_specs = [row_spec(o.shape[1]) for o in out_shapes[:-1]]
    out_specs.append(pl.BlockSpec((1, SUBLANES, D_CONV), lambda b, j: (b, 0, 0)))
    return pl.pallas_call(
        _in_proj_prompt_kernel,
        grid=(nb, nj),
        in_specs=[row_spec(D_MODEL)] + [_const_spec(w.shape) for w in weights] + [_const_spec(conv_w.shape)],
        out_specs=out_specs,
        out_shape=out_shapes,
        scratch_shapes=[pltpu.VMEM((SUBLANES, D_CONV), F32)],
        compiler_params=_params(2),
        name="in_proj_prompt",
    )(x2d, *weights, conv_w)


def _in_proj_decode(x2d, p2, p1, weights, conv_w):
    t = x2d.shape[0]
    row_spec = lambda w: pl.BlockSpec((t, w), lambda i: (0, 0))
    out_shapes = _proj_out_shapes(t) + [jax.ShapeDtypeStruct((t, D_CONV), F32)]
    return pl.pallas_call(
        _in_proj_decode_kernel,
        grid=(1,),
        in_specs=[row_spec(D_MODEL)] * 3 + [_const_spec(w.shape) for w in weights] + [_const_spec(conv_w.shape)],
        out_specs=[row_spec(o.shape[1]) for o in out_shapes],
        out_shape=out_shapes,
        compiler_params=_params(1),
        name="in_proj_decode",
    )(x2d, p2, p1, *weights, conv_w)


def _attn_prompt_kernel(q_ref, kv_ref, qi_ref, kib_ref, idx_ref, o_ref,
                        key_ref, bias_ref, t_ref, j_ref, need_ref, m_ref, acc_ref,
                        *, n_keep, seq_len):
    tq = q_ref.shape[0]
    i = pl.program_id(1)
    nchunk = i + 1
    nsub = tq // LANES
    lane = lax.broadcasted_iota(I32, (tq, LANES), 1)

    wi_cols = [idx_ref[:, IDX_DIM + h:IDX_DIM + h + 1] * IDX_SCALE for h in range(IDX_HEADS)]
    row_i = lax.broadcasted_iota(I32, (tq, tq), 0)
    col_i = lax.broadcasted_iota(I32, (tq, tq), 1)

    def score_chunk(c, carry):
        kic = kib_ref[pl.ds(pl.multiple_of(c * tq, tq), tq), :]
        acc = jnp.zeros((tq, tq), F32)
        for h in range(IDX_HEADS):
            d = _dot_nt(qi_ref[:, h * LANES:(h + 1) * LANES], kic)
            acc = acc + jnp.maximum(d, 0.0) * wi_cols[h]
        s = jnp.where((c < i) | (col_i <= row_i), acc, NEG)
        key_ref[c] = _sort_key(s)
        return carry

    lax.fori_loop(0, nchunk, score_chunk, 0)

    rb = SEARCH_ROWS

    def count(r0, pred):
        def body(c, cnt):
            for k in range(nsub):
                blk = key_ref[c, r0:r0 + rb, k * LANES:(k + 1) * LANES]
                cnt = cnt + pred(blk, c * tq + k * LANES).astype(I32)
            return cnt
        cnt = lax.fori_loop(0, nchunk, body, jnp.zeros((rb, LANES), I32))
        return jnp.sum(cnt, axis=1, keepdims=True)

    any_excess = jnp.int32(0)
    for r in range(tq // rb):
        r0 = r * rb

        def bit_body(bi, t):
            cand = t + lax.shift_left(jnp.int32(1), 31 - bi)
            cand_b = jnp.broadcast_to(cand, (rb, LANES))
            cnt = count(r0, lambda blk, _: blk >= cand_b)
            return jnp.where(cnt >= n_keep, cand, t)

        t = lax.fori_loop(0, 32, bit_body, jnp.full((rb, 1), INT_MIN, I32))
        t_b = jnp.broadcast_to(t, (rb, LANES))
        cnt_gt = count(r0, lambda blk, _: blk > t_b)
        cnt_eq = count(r0, lambda blk, _: blk == t_b)
        need = n_keep - cnt_gt
        excess = (cnt_eq > need) & (t > KEY_HALF)
        any_excess = any_excess + jnp.max(excess.astype(I32))
        t_ref[r0:r0 + rb, :] = t_b
        need_ref[r0:r0 + rb, :] = jnp.broadcast_to(need, (rb, LANES))

    j_ref[...] = jnp.full(j_ref.shape, seq_len, I32)

    @pl.when(any_excess > 0)
    def _():
        nbits = max(1, (seq_len - 1).bit_length())
        lane_rb = lax.broadcasted_iota(I32, (rb, LANES), 1)
        for r in range(tq // rb):
            r0 = r * rb
            t_b = t_ref[r0:r0 + rb, :]
            need = need_ref[r0:r0 + rb, 0:1]

            def jbit(bi, jj):
                cand = jj | lax.shift_left(jnp.int32(1), nbits - 1 - bi)
                cand_b = jnp.broadcast_to(cand, (rb, LANES))
                g = count(r0, lambda blk, base: (blk == t_b) & ((base + lane_rb) < cand_b))
                return jnp.where(g < need, cand, jj)

            jj = lax.fori_loop(0, nbits, jbit, jnp.zeros((rb, 1), I32))
            j_ref[r0:r0 + rb, :] = jnp.broadcast_to(jj, (rb, LANES))

    t_all = t_ref[...]
    j_all = j_ref[...]

    def bias_chunk(c, carry):
        for k in range(nsub):
            blk = key_ref[c, :, k * LANES:(k + 1) * LANES]
            col = c * tq + k * LANES + lane
            sel = (blk > KEY_HALF) & ((blk > t_all) | ((blk == t_all) & (col <= j_all)))
            bias_ref[c, :, k * LANES:(k + 1) * LANES] = jnp.where(sel, 0.0, NEG)
        return carry

    lax.fori_loop(0, nchunk, bias_chunk, 0)

    nkv = N_KV_HEADS * HEAD_DIM
    gq = GROUP * tq

    def group_logits(c, g):
        rows = pl.ds(pl.multiple_of(c * tq, tq), tq)
        kc = kv_ref[rows, g * HEAD_DIM:(g + 1) * HEAD_DIM]
        qg = jnp.concatenate([q_ref[:, h * HEAD_DIM:(h + 1) * HEAD_DIM]
                              for h in range(g * GROUP, (g + 1) * GROUP)], axis=0)
        s = _dot_nt(qg, kc).reshape(GROUP, tq, tq) + bias_ref[c][None]
        return s.reshape(gq, tq)

    m_ref[...] = jnp.full(m_ref.shape, NEG, F32)

    def max_chunk(c, carry):
        for g in range(N_KV_HEADS):
            s = group_logits(c, g)
            mx = s[:, :LANES]
            for k in range(1, nsub):
                mx = jnp.maximum(mx, s[:, k * LANES:(k + 1) * LANES])
            m_ref[g] = jnp.maximum(m_ref[g], mx)
        return carry

    lax.fori_loop(0, nchunk, max_chunk, 0)
    for g in range(N_KV_HEADS):
        m_ref[g] = jnp.broadcast_to(jnp.max(m_ref[g], axis=1, keepdims=True), (gq, LANES))

    acc_ref[...] = jnp.zeros_like(acc_ref)
    ones = jnp.ones((tq, HEAD_DIM), BF16)

    def pv_chunk(c, carry):
        rows = pl.ds(pl.multiple_of(c * tq, tq), tq)
        for g in range(N_KV_HEADS):
            s = group_logits(c, g)
            m = m_ref[g]
            p = jnp.concatenate([jnp.exp(s[:, k * LANES:(k + 1) * LANES] - m) for k in range(nsub)], axis=1)
            vc = kv_ref[rows, nkv + g * HEAD_DIM:nkv + (g + 1) * HEAD_DIM]
            acc_ref[g] = acc_ref[g] + _dot(p.astype(BF16), jnp.concatenate([vc, ones], axis=1))
        return carry

    lax.fori_loop(0, nchunk, pv_chunk, 0)
    for h in range(N_HEADS):
        a = acc_ref[h // GROUP, (h % GROUP) * tq:(h % GROUP + 1) * tq, :]
        o_ref[:, h * HEAD_DIM:(h + 1) * HEAD_DIM] = (a[:, :HEAD_DIM] / a[:, HEAD_DIM:]).astype(BF16)


def _attn_prompt(q, kvb, qi, kib, idx, nb):
    t = q.shape[0]
    s = t // nb
    tq = min(ATT_Q, s)
    nq = s // tq
    n_keep = min(TOPK_KEYS, s // 4)
    blk = lambda w: pl.BlockSpec((tq, w), lambda b, i: (b * nq + i, 0))
    seq = lambda w: pl.BlockSpec((s, w), lambda b, i: (b, 0))
    return pl.pallas_call(
        functools.partial(_attn_prompt_kernel, n_keep=n_keep, seq_len=s),
        grid=(nb, nq),
        in_specs=[blk(q.shape[1]), seq(kvb.shape[1]), blk(qi.shape[1]), seq(kib.shape[1]), blk(idx.shape[1])],
        out_specs=blk(N_HEADS * HEAD_DIM),
        out_shape=jax.ShapeDtypeStruct((t, N_HEADS * HEAD_DIM), BF16),
        scratch_shapes=[pltpu.VMEM((nq, tq, tq), I32), pltpu.VMEM((nq, tq, tq), F32),
                        pltpu.VMEM((tq, LANES), I32), pltpu.VMEM((tq, LANES), I32), pltpu.VMEM((tq, LANES), I32),
                        pltpu.VMEM((N_KV_HEADS, GROUP * tq, LANES), F32),
                        pltpu.VMEM((N_KV_HEADS, GROUP * tq, 2 * HEAD_DIM), F32)],
        compiler_params=_params(2),
        name="attn_prompt",
    )(q, kvb, qi, kib, idx)


DECODE_PAGES_PER_ROW = 4


def _decode_select_kernel(pt_ref, qi_ref, wi_ref, kin_ref, cx_hbm, meta_o, xbuf, key_ref, sem,
                          *, n_pages, n_keep):
    b = pl.program_id(0)
    nb = pl.num_programs(0)
    ppr = DECODE_PAGES_PER_ROW
    w = ppr * PAGE_SIZE
    nrow = n_pages // ppr
    past = n_pages * PAGE_SIZE

    def page_copy(bb, p, slot):
        return pltpu.make_async_copy(cx_hbm.at[pt_ref[bb, p]], xbuf.at[slot, p], sem.at[slot])

    def start_batch(bb, slot):
        def body(p, carry):
            page_copy(bb, p, slot).start()
            return carry
        lax.fori_loop(0, n_pages, body, 0)

    def wait_batch(bb, slot):
        def body(p, carry):
            page_copy(bb, p, slot).wait()
            return carry
        lax.fori_loop(0, n_pages, body, 0)

    @pl.when(b == 0)
    def _():
        start_batch(0, 0)

    @pl.when(b + 1 < nb)
    def _():
        start_batch(b + 1, (b + 1) % 2)

    slot = b % 2
    wait_batch(b, slot)

    qi8 = qi_ref[0]
    wi8 = wi_ref[0] * IDX_SCALE

    def score_row(r, carry):
        kx = jnp.concatenate([xbuf[slot, r * ppr + k] for k in range(ppr)], axis=1).astype(BF16)
        d = _dot(qi8, kx)
        sc = jnp.sum(jnp.maximum(d, 0.0) * wi8, axis=0, keepdims=True)
        key_ref[pl.ds(r, 1), :] = _sort_key(sc)
        return carry

    lax.fori_loop(0, nrow, score_row, 0)
    dn = jnp.sum(qi8.astype(F32) * kin_ref[0].astype(F32), axis=1, keepdims=True)
    key_new = _sort_key(jnp.sum(jnp.maximum(dn, 0.0) * wi8, axis=0, keepdims=True))

    keys = key_ref[...]
    col = (lax.broadcasted_iota(I32, keys.shape, 0) * w + lax.broadcasted_iota(I32, keys.shape, 1))

    def total(x):
        return jnp.sum(jnp.sum(x.astype(I32), axis=1, keepdims=True), axis=0, keepdims=True)

    def bit_body(bi, t):
        cand = t + lax.shift_left(jnp.int32(1), 31 - bi)
        cnt = total(keys >= cand) + (key_new >= cand).astype(I32)
        return jnp.where(cnt >= n_keep, cand, t)

    t = lax.fori_loop(0, 32, bit_body, jnp.full((1, 1), INT_MIN, I32))
    need = n_keep - (total(keys > t) + (key_new > t).astype(I32))
    nbits = past.bit_length()

    def jbit(bi, jj):
        cand = jj | lax.shift_left(jnp.int32(1), nbits - 1 - bi)
        g = total((keys == t) & (col < cand)) + ((key_new == t) & (past < cand)).astype(I32)
        return jnp.where(g < need, cand, jj)

    jj = lax.fori_loop(0, nbits, jbit, jnp.zeros((1, 1), I32))
    sel = (keys > t) | ((keys == t) & (col <= jj))
    sel_new = (key_new > t) | ((key_new == t) & (past <= jj))

    incl_lane = (lax.broadcasted_iota(I32, (w, w), 0) <= lax.broadcasted_iota(I32, (w, w), 1)).astype(BF16)
    cnt_in_row = _dot(sel.astype(BF16), incl_lane)
    row_tot = cnt_in_row[:, w - 1:w]
    rows_before = (lax.broadcasted_iota(I32, (nrow, nrow), 1) < lax.broadcasted_iota(I32, (nrow, nrow), 0))
    row_off = _dot(rows_before.astype(BF16), jnp.broadcast_to(row_tot, (nrow, LANES)).astype(BF16))[:, 0:1]
    jl = lax.broadcasted_iota(I32, (1, n_keep), 1).astype(F32)
    row_j = jnp.sum(((row_off + row_tot) <= jl).astype(F32), axis=0, keepdims=True)
    onehot = (lax.broadcasted_iota(I32, (nrow, n_keep), 0).astype(F32) == row_j)
    cnt_j = lax.dot_general(cnt_in_row.astype(BF16), onehot.astype(BF16), (((0,), (0,)), ((), ())),
                            preferred_element_type=F32)
    off_j = jnp.sum(jnp.where(onehot, row_off, 0.0), axis=0, keepdims=True)
    lane_j = jnp.sum((cnt_j <= (jl - off_j)).astype(F32), axis=0, keepdims=True)
    pos = jnp.minimum((row_j * w + lane_j).astype(I32), past - 1)
    is_pos_row = lax.broadcasted_iota(I32, (SUBLANES, n_keep), 0) == 0
    meta_o[0] = jnp.where(is_pos_row, pos, sel_new.astype(I32))


def _decode_select(page_table, qi8, wi8, kin, cxt):
    db, n_pages = page_table.shape
    past = n_pages * PAGE_SIZE
    n_keep = min(TOPK_KEYS, (past + 1) // 4)
    per_b = lambda a: pl.BlockSpec((1,) + a.shape[1:], lambda b, pt: (b, 0, 0))
    grid_spec = pltpu.PrefetchScalarGridSpec(
        num_scalar_prefetch=1,
        grid=(db,),
        in_specs=[per_b(qi8), per_b(wi8), per_b(kin), pl.BlockSpec(memory_space=pl.ANY)],
        out_specs=pl.BlockSpec((1, SUBLANES, n_keep), lambda b, pt: (b, 0, 0)),
        scratch_shapes=[pltpu.VMEM((2, n_pages, IDX_DIM, PAGE_SIZE), F32),
                        pltpu.VMEM((n_pages // DECODE_PAGES_PER_ROW, DECODE_PAGES_PER_ROW * PAGE_SIZE), I32),
                        pltpu.SemaphoreType.DMA((2,))],
    )
    return pl.pallas_call(
        functools.partial(_decode_select_kernel, n_pages=n_pages, n_keep=n_keep),
        grid_spec=grid_spec,
        out_shape=jax.ShapeDtypeStruct((db, SUBLANES, n_keep), I32),
        compiler_params=_params(1),
        name="decode_select",
    )(page_table, qi8, wi8, kin, cxt)


def _decode_attend_kernel(pt_ref, pos_ref, q_ref, meta_ref, kn_ref, vn_ref, ck_hbm, cv_hbm, o_ref,
                          kbuf, vbuf, expand_ref, sem, *, n_keep):
    b = pl.program_id(0)
    nb = pl.num_programs(0)
    pos_per_tile = SUBLANES // N_KV_HEADS
    ncol = n_keep * SUBLANES

    def item_copies(bb, j, slot):
        pos = pos_ref[bb, j]
        phys = pt_ref[bb, pos // PAGE_SIZE]
        tile = (phys * PAGE_SIZE + pos % PAGE_SIZE) // pos_per_tile
        src = pl.ds(pl.multiple_of(tile * SUBLANES, SUBLANES), SUBLANES)
        dst = pl.ds(pl.multiple_of(j * SUBLANES, SUBLANES), SUBLANES)
        return (pltpu.make_async_copy(ck_hbm.at[src], kbuf.at[slot, dst], sem.at[0, slot]),
                pltpu.make_async_copy(cv_hbm.at[src], vbuf.at[slot, dst], sem.at[1, slot]))

    def start_batch(bb, slot):
        def body(j, carry):
            for cp in item_copies(bb, j, slot):
                cp.start()
            return carry
        lax.fori_loop(0, n_keep, body, 0)

    def wait_batch(bb, slot):
        def body(j, carry):
            for cp in item_copies(bb, j, slot):
                cp.wait()
            return carry
        lax.fori_loop(0, n_keep, body, 0)

    @pl.when(b == 0)
    def _():
        start_batch(0, 0)
        item_of_col = lax.broadcasted_iota(I32, (n_keep, ncol), 1) // SUBLANES
        expand_ref[...] = (item_of_col == lax.broadcasted_iota(I32, (n_keep, ncol), 0)).astype(BF16)

    @pl.when(b + 1 < nb)
    def _():
        start_batch(b + 1, (b + 1) % 2)

    slot = b % 2
    wait_batch(b, slot)

    q8 = q_ref[0]
    meta = meta_ref[0]
    new_kept = meta[1:2, 0:1] > 0
    sub = (meta % pos_per_tile).astype(BF16)
    sub_col = _dot(sub, expand_ref[...])[0:1, :]
    head_i = lax.broadcasted_iota(I32, (N_HEADS, ncol), 0)
    col_i = lax.broadcasted_iota(I32, (N_HEADS, ncol), 1)
    want = sub_col * N_KV_HEADS + (head_i // GROUP).astype(F32)
    is_new_item = new_kept & (col_i // SUBLANES == n_keep - 1)
    ok = ((col_i % SUBLANES).astype(F32) == want) & jnp.logical_not(is_new_item)

    s = jnp.where(ok, _dot_nt(q8, kbuf[slot].astype(BF16)), NEG)
    s_new = jnp.sum(q8.astype(F32) * kn_ref[0].astype(F32), axis=1, keepdims=True)
    s_new = jnp.where(new_kept, s_new, NEG)
    m = jnp.maximum(jnp.max(s, axis=1, keepdims=True), s_new)
    p = jnp.exp(s - m)
    p_new = jnp.exp(s_new - m)
    l = jnp.sum(p, axis=1, keepdims=True) + p_new
    acc = _dot(p.astype(BF16), vbuf[slot].astype(BF16)) + p_new.astype(BF16).astype(F32) * vn_ref[0].astype(F32)
    o_ref[0] = (acc / l).astype(BF16)


def _decode_attend(page_table, pos, q8, meta, kn8, vn8, ck2, cv2):
    db, n_keep = pos.shape
    per_b = lambda a: pl.BlockSpec((1,) + a.shape[1:], lambda b, pt, ps: (b, 0, 0))
    any_spec = pl.BlockSpec(memory_space=pl.ANY)
    ncol = n_keep * SUBLANES
    grid_spec = pltpu.PrefetchScalarGridSpec(
        num_scalar_prefetch=2,
        grid=(db,),
        in_specs=[per_b(q8), per_b(meta), per_b(kn8), per_b(vn8), any_spec, any_spec],
        out_specs=pl.BlockSpec((1, N_HEADS, HEAD_DIM), lambda b, pt, ps: (b, 0, 0)),
        scratch_shapes=[pltpu.VMEM((2, ncol, HEAD_DIM), F32), pltpu.VMEM((2, ncol, HEAD_DIM), F32),
                        pltpu.VMEM((n_keep, ncol), BF16), pltpu.SemaphoreType.DMA((2, 2))],
    )
    return pl.pallas_call(
        functools.partial(_decode_attend_kernel, n_keep=n_keep),
        grid_spec=grid_spec,
        out_shape=jax.ShapeDtypeStruct((db, N_HEADS, HEAD_DIM), BF16),
        compiler_params=_params(1),
        name="decode_attend",
    )(page_table, pos, q8, meta, kn8, vn8, ck2, cv2)


def _layer_norm(r, g, b):
    mu = jnp.mean(r, axis=-1, keepdims=True)
    d = r - mu
    var = jnp.mean(d * d, axis=-1, keepdims=True)
    return d * lax.rsqrt(var + LN_EPS) * g + b


def _route(logits_t, rbias):
    n_exp, tm = logits_t.shape
    epg = n_exp // N_GROUPS
    s = jax.nn.sigmoid(logits_t)
    sb = s + rbias
    ie = lax.broadcasted_iota(I32, (epg, tm), 0)
    gs_rows = []
    for g in range(N_GROUPS):
        blk = sb[g * epg:(g + 1) * epg, :]
        m1 = jnp.max(blk, axis=0, keepdims=True)
        i1 = jnp.min(jnp.where(blk == m1, ie, epg), axis=0, keepdims=True)
        m2 = jnp.max(jnp.where(ie == i1, -jnp.inf, blk), axis=0, keepdims=True)
        gs_rows.append(m1 + m2)
    picked = [jnp.zeros((1, tm), jnp.bool_) for _ in range(N_GROUPS)]
    cur = list(gs_rows)
    for _ in range(TOPK_GROUPS):
        mx = cur[0]
        for g in range(1, N_GROUPS):
            mx = jnp.maximum(mx, cur[g])
        found = jnp.zeros((1, tm), jnp.bool_)
        for g in range(N_GROUPS):
            hit = (cur[g] == mx) & jnp.logical_not(found)
            found = found | hit
            picked[g] = picked[g] | hit
            cur[g] = jnp.where(hit, -jnp.inf, cur[g])
    masked = jnp.concatenate(
        [jnp.where(picked[g], sb[g * epg:(g + 1) * epg, :], NEG) for g in range(N_GROUPS)], axis=0)
    iall = lax.broadcasted_iota(I32, (n_exp, tm), 0)
    e_rows, w_rows = [], []
    for _ in range(TOP_K):
        mx = jnp.max(masked, axis=0, keepdims=True)
        ix = jnp.min(jnp.where(masked == mx, iall, n_exp), axis=0, keepdims=True)
        hit = iall == ix
        w_rows.append(jnp.sum(jnp.where(hit, s, 0.0), axis=0, keepdims=True))
        e_rows.append(ix)
        masked = jnp.where(hit, -jnp.inf, masked)
    wsum = w_rows[0]
    for w in w_rows[1:]:
        wsum = wsum + w
    gates = [w / wsum * ROUTED_SCALE for w in w_rows]
    return jnp.concatenate(e_rows, axis=0), jnp.concatenate(gates, axis=0)


def _post_attn_kernel(xp, xs, ap, as_, cp, cs, gap, gas, gcp, gcs, woa, woc, wo, g1, b1, rwt, rb,
                      h2_o, e_o, gate_o, *, n_prompt_tiles):
    i = pl.program_id(0)
    is_p = i < n_prompt_tiles
    pick = lambda a, b: jnp.where(is_p, a[...], b[...])
    a = _dot(pick(ap, as_), woa[...])
    c = _dot(pick(cp, cs), woc[...])
    merged = pick(gap, gas).astype(F32) * a + pick(gcp, gcs).astype(F32) * c
    r = ALPHA * pick(xp, xs) + _dot(merged.astype(BF16), wo[...])
    h = _layer_norm(r, g1[...], b1[...])
    tm = h.shape[0]
    for k in range(D_MODEL // LANES):
        h2_o[pl.ds(k, tm, stride=SUBLANES), :] = h[:, k * LANES:(k + 1) * LANES]
    logits_t = lax.dot_general(rwt[...], h, (((1,), (1,)), ((), ())),
                               precision=lax.Precision.HIGHEST, preferred_element_type=F32)
    e_idx, gates = _route(logits_t, rb[...])
    e_o[...] = e_idx
    gate_o[...] = gates


def _post_attn(x_p, x_s, attn_p, attn_s, conv_p, conv_s, ga_p, ga_s, gc_p, gc_s,
               woa, woc, wo, g1, b1, rwt, rb):
    tp, ts = x_p.shape[0], x_s.shape[0]
    tm = TOKEN_TILE
    npt, nst = tp // tm, ts // tm
    n_tok = tp + ts
    p_spec = pl.BlockSpec((tm, D_MODEL), lambda i: (jnp.minimum(i, npt - 1), 0))
    s_spec = pl.BlockSpec((tm, D_MODEL), lambda i: (jnp.maximum(i - npt, 0), 0))
    consts = [woa, woc, wo, g1, b1, rwt, rb]
    return pl.pallas_call(
        functools.partial(_post_attn_kernel, n_prompt_tiles=npt),
        grid=(npt + nst,),
        in_specs=[p_spec, s_spec] * 5 + [_const_spec(c.shape) for c in consts],
        out_specs=[pl.BlockSpec((tm * SUBLANES, LANES), lambda i: (i, 0)),
                   pl.BlockSpec((TOP_K, tm), lambda i: (0, i)),
                   pl.BlockSpec((TOP_K, tm), lambda i: (0, i))],
        out_shape=[jax.ShapeDtypeStruct((n_tok * SUBLANES, LANES), F32),
                   jax.ShapeDtypeStruct((TOP_K, n_tok), I32),
                   jax.ShapeDtypeStruct((TOP_K, n_tok), F32)],
        compiler_params=_params(1),
        name="post_attn",
    )(x_p, x_s, attn_p, attn_s, conv_p, conv_s, ga_p, ga_s, gc_p, gc_s, *consts)


def _rank_kernel(e_ref, rank_o, cnt_o, carry_ref):
    i = pl.program_id(0)

    @pl.when(i == 0)
    def _():
        carry_ref[...] = jnp.zeros_like(carry_ref)

    n_exp = carry_ref.shape[0]
    e = e_ref[...]
    tk = e.shape[1]
    ie = lax.broadcasted_iota(I32, (n_exp, tk), 0)
    onehot = jnp.zeros((n_exp, tk), F32)
    for j in range(TOP_K):
        onehot = onehot + (ie == e[j:j + 1, :]).astype(F32)
    before = (lax.broadcasted_iota(I32, (tk, tk), 0) < lax.broadcasted_iota(I32, (tk, tk), 1)).astype(BF16)
    prefix = _dot(onehot.astype(BF16), before) + carry_ref[:, 0:1]
    rows = [jnp.sum(jnp.where(ie == e[j:j + 1, :], prefix, 0.0), axis=0, keepdims=True) for j in range(TOP_K)]
    rank_o[...] = jnp.concatenate(rows, axis=0).astype(I32)
    carry_ref[...] = carry_ref[...] + jnp.sum(onehot, axis=1, keepdims=True)
    cnt_o[...] = carry_ref[...].astype(I32)


def _rank(e_t, n_exp):
    n_tok = e_t.shape[1]
    tk = TOKEN_TILE
    return pl.pallas_call(
        _rank_kernel,
        grid=(n_tok // tk,),
        in_specs=[pl.BlockSpec((TOP_K, tk), lambda i: (0, i))],
        out_specs=[pl.BlockSpec((TOP_K, tk), lambda i: (0, i)),
                   pl.BlockSpec((n_exp, LANES), lambda i: (0, 0))],
        out_shape=[jax.ShapeDtypeStruct((TOP_K, n_tok), I32), jax.ShapeDtypeStruct((n_exp, LANES), I32)],
        scratch_shapes=[pltpu.VMEM((n_exp, LANES), F32)],
        compiler_params=_params(1),
        name="moe_rank",
    )(e_t)


def _slot_kernel(e_ref, rank_ref, pstart_ref, slot_o):
    e = e_ref[...]
    n_exp = pstart_ref.shape[0]
    tk = e.shape[1]
    ie = lax.broadcasted_iota(I32, (n_exp, tk), 0)
    ps = pstart_ref[:, 0:1]
    rows = [jnp.sum(jnp.where(ie == e[j:j + 1, :], ps, 0.0), axis=0, keepdims=True) for j in range(TOP_K)]
    slot_o[...] = rank_ref[...] + jnp.concatenate(rows, axis=0).astype(I32)


def _slots(e_t, rank_t, pstart_f):
    n_tok = e_t.shape[1]
    tk = TOKEN_TILE
    spec = pl.BlockSpec((TOP_K, tk), lambda i: (0, i))
    return pl.pallas_call(
        _slot_kernel,
        grid=(n_tok // tk,),
        in_specs=[spec, spec, _const_spec(pstart_f.shape)],
        out_specs=spec,
        out_shape=jax.ShapeDtypeStruct((TOP_K, n_tok), I32),
        compiler_params=_params(1),
        name="moe_slots",
    )(e_t, rank_t, pstart_f)


def _dispatch_kernel(pend_ref, padded_ref, slot_ref, h2_ref, xs_hbm, zbuf, sem, zsem, *, n_exp):
    i = pl.program_id(0)
    td = slot_ref.shape[1]
    blk_rows = MOE_BLOCK * SUBLANES

    def zero_copy(e):
        dst = pl.ds(pl.multiple_of((pend_ref[e] - MOE_BLOCK) * SUBLANES, SUBLANES), blk_rows)
        return pltpu.make_async_copy(zbuf, xs_hbm.at[dst], zsem)

    @pl.when(i == 0)
    def _():
        zbuf[...] = jnp.zeros_like(zbuf)

        def start(e, carry):
            @pl.when(padded_ref[e] > 0)
            def _():
                zero_copy(e).start()
            return carry

        def wait(e, carry):
            @pl.when(padded_ref[e] > 0)
            def _():
                zero_copy(e).wait()
            return carry

        lax.fori_loop(0, n_exp, start, 0)
        lax.fori_loop(0, n_exp, wait, 0)

    def row_copy(t, j):
        src = pl.ds(pl.multiple_of(t * SUBLANES, SUBLANES), SUBLANES)
        dst = pl.ds(pl.multiple_of(slot_ref[j, t] * SUBLANES, SUBLANES), SUBLANES)
        return pltpu.make_async_copy(h2_ref.at[src], xs_hbm.at[dst], sem)

    def start_tok(t, carry):
        for j in range(TOP_K):
            row_copy(t, j).start()
        return carry

    def wait_tok(t, carry):
        for j in range(TOP_K):
            row_copy(t, j).wait()
        return carry

    lax.fori_loop(0, td, start_tok, 0)
    lax.fori_loop(0, td, wait_tok, 0)


def _dispatch(pend, padded, slot_t, h2, m_pad):
    n_tok = slot_t.shape[1]
    td = TOKEN_TILE
    n_exp = pend.shape[0]
    grid_spec = pltpu.PrefetchScalarGridSpec(
        num_scalar_prefetch=2,
        grid=(n_tok // td,),
        in_specs=[pl.BlockSpec((TOP_K, td), lambda i, *_: (0, i), memory_space=pltpu.SMEM),
                  pl.BlockSpec((td * SUBLANES, LANES), lambda i, *_: (i, 0))],
        out_specs=pl.BlockSpec(memory_space=pl.ANY),
        scratch_shapes=[pltpu.VMEM((MOE_BLOCK * SUBLANES, LANES), F32),
                        pltpu.SemaphoreType.DMA(()), pltpu.SemaphoreType.DMA(())],
    )
    return pl.pallas_call(
        functools.partial(_dispatch_kernel, n_exp=n_exp),
        grid_spec=grid_spec,
        out_shape=jax.ShapeDtypeStruct((m_pad * SUBLANES, LANES), F32),
        compiler_params=_params(1),
        name="moe_dispatch",
    )(pend, padded, slot_t, h2)


def _tile_rows(ref, n):
    return jnp.concatenate([ref[pl.ds(k, n, stride=SUBLANES), :] for k in range(D_MODEL // LANES)], axis=1)


def _expert_kernel(be_ref, nused_ref, x_ref, wg_ref, wu_ref, wd_ref, y_ref, wgu_b, wd_b):
    i = pl.program_id(0)
    d_exp = wd_b.shape[0]

    @pl.when(i < nused_ref[0])
    def _():
        changed = (i == 0) | (be_ref[i] != be_ref[jnp.maximum(i - 1, 0)])

        @pl.when(changed)
        def _():
            wgu_b[:, :d_exp] = wg_ref[0].astype(BF16)
            wgu_b[:, d_exp:] = wu_ref[0].astype(BF16)
            wd_b[...] = wd_ref[0].astype(BF16)

        x = _tile_rows(x_ref, MOE_BLOCK).astype(BF16)
        gu = _dot(x, wgu_b[...])
        hh = jax.nn.silu(gu[:, :d_exp]) * gu[:, d_exp:]
        y = _dot(hh.astype(BF16), wd_b[...])
        for k in range(D_MODEL // LANES):
            y_ref[pl.ds(k, MOE_BLOCK, stride=SUBLANES), :] = y[:, k * LANES:(k + 1) * LANES]


def _experts(block_e, nused, xs, w_gate, w_up, w_down):
    n_blocks = block_e.shape[0]
    d_exp = w_gate.shape[2]
    rows = MOE_BLOCK * SUBLANES
    last = lambda i, nu: jnp.minimum(i, nu[0] - 1)
    grid_spec = pltpu.PrefetchScalarGridSpec(
        num_scalar_prefetch=2,
        grid=(n_blocks,),
        in_specs=[pl.BlockSpec((rows, LANES), lambda i, be, nu: (last(i, nu), 0)),
                  pl.BlockSpec((1, D_MODEL, d_exp), lambda i, be, nu: (be[i], 0, 0)),
                  pl.BlockSpec((1, D_MODEL, d_exp), lambda i, be, nu: (be[i], 0, 0)),
                  pl.BlockSpec((1, d_exp, D_MODEL), lambda i, be, nu: (be[i], 0, 0))],
        out_specs=pl.BlockSpec((rows, LANES), lambda i, be, nu: (last(i, nu), 0)),
        scratch_shapes=[pltpu.VMEM((D_MODEL, 2 * d_exp), BF16), pltpu.VMEM((d_exp, D_MODEL), BF16)],
    )
    return pl.pallas_call(
        _expert_kernel,
        grid_spec=grid_spec,
        out_shape=jax.ShapeDtypeStruct(xs.shape, F32),
        compiler_params=_params(1),
        name="moe_experts",
    )(block_e, nused, xs, w_gate, w_up, w_down)


def _combine_kernel(slot_ref, gate_ref, h2_ref, ys_hbm, shg, shu, shd, g2, b2, yp_o, ys_o, buf, sem,
                    *, n_prompt_tiles):
    i = pl.program_id(0)
    tc = gate_ref.shape[0]

    def row_copy(t, j):
        src = pl.ds(pl.multiple_of(slot_ref[j, t] * SUBLANES, SUBLANES), SUBLANES)
        dst = pl.ds(pl.multiple_of(t * SUBLANES, SUBLANES), SUBLANES)
        return pltpu.make_async_copy(ys_hbm.at[src], buf.at[j, dst], sem)

    def start_tok(t, carry):
        for j in range(TOP_K):
            row_copy(t, j).start()
        return carry

    def wait_tok(t, carry):
        for j in range(TOP_K):
            row_copy(t, j).wait()
        return carry

    lax.fori_loop(0, tc, start_tok, 0)
    h = _tile_rows(h2_ref, tc)
    hb = h.astype(BF16)
    shared = _dot((jax.nn.silu(_dot(hb, shg[...])) * _dot(hb, shu[...])).astype(BF16), shd[...])
    lax.fori_loop(0, tc, wait_tok, 0)
    routed = jnp.zeros((tc, D_MODEL), F32)
    for j in range(TOP_K):
        routed = routed + _tile_rows(buf.at[j], tc) * gate_ref[:, j:j + 1]
    y = _layer_norm(ALPHA * h + (routed + shared), g2[...], b2[...])

    @pl.when(i < n_prompt_tiles)
    def _():
        yp_o[...] = y

    @pl.when(i >= n_prompt_tiles)
    def _():
        ys_o[...] = y


def _combine(slot_t, gates, h2, ys, shg, shu, shd, g2, b2, n_prompt, n_decode):
    tc = TOKEN_TILE
    npt, nst = n_prompt // tc, n_decode // tc
    consts = [shg, shu, shd, g2, b2]
    return pl.pallas_call(
        functools.partial(_combine_kernel, n_prompt_tiles=npt),
        grid=(npt + nst,),
        in_specs=[pl.BlockSpec((TOP_K, tc), lambda i: (0, i), memory_space=pltpu.SMEM),
                  pl.BlockSpec((tc, TOP_K), lambda i: (i, 0)),
                  pl.BlockSpec((tc * SUBLANES, LANES), lambda i: (i, 0)),
                  pl.BlockSpec(memory_space=pl.ANY)] + [_const_spec(c.shape) for c in consts],
        out_specs=[pl.BlockSpec((tc, D_MODEL), lambda i: (jnp.minimum(i, npt - 1), 0)),
                   pl.BlockSpec((tc, D_MODEL), lambda i: (jnp.maximum(i - npt, 0), 0))],
        out_shape=[jax.ShapeDtypeStruct((n_prompt, D_MODEL), F32),
                   jax.ShapeDtypeStruct((n_decode, D_MODEL), F32)],
        scratch_shapes=[pltpu.VMEM((TOP_K, tc * SUBLANES, LANES), F32), pltpu.SemaphoreType.DMA(())],
        compiler_params=_params(1),
        name="moe_combine",
    )(slot_t, gates, h2, ys, *consts)


def _pad_rows(a, n):
    return jnp.pad(a, ((0, n - a.shape[0]),) + ((0, 0),) * (a.ndim - 1))


def kernel(x_prompt, x_sample, cache_k, cache_v, cache_kidx, page_table, state_conv, w_in, conv_w, w_o_attn,
           w_o_conv, w_o, ln1_g, ln1_b, router_w, router_bias, moe_w_gate, moe_w_up, moe_w_down,
           shared_w_gate, shared_w_up, shared_w_down, ln2_g, ln2_b):
    nb, seq, _ = x_prompt.shape
    db = x_sample.shape[0]
    n_pool = cache_k.shape[1]
    n_exp = router_w.shape[-1]
    tp = nb * seq
    ts = TOKEN_TILE
    nkv = N_KV_HEADS * HEAD_DIM

    weights = _prep_in_weights(w_in[0])
    cw = conv_w[0]

    xp2 = x_prompt.reshape(tp, D_MODEL)
    (q_p, k_p, v_p, kvb_p, qi_p, idx_p, kib_p, conv_p, ga_p, gc_p, ulast_p) = _in_proj_prompt(xp2, weights, cw, nb)
    xs2 = _pad_rows(x_sample.reshape(db, D_MODEL), ts)
    prev2 = _pad_rows(state_conv[0, :, 0, :], ts)
    prev1 = _pad_rows(state_conv[0, :, 1, :], ts)
    (q_s, k_s, v_s, _, qi_s, idx_s, _, conv_s, ga_s, gc_s, u_s) = _in_proj_decode(xs2, prev2, prev1, weights, cw)

    attn_p = _attn_prompt(q_p, kvb_p, qi_p, kib_p, idx_p, nb)
    q8 = q_s[:db].reshape(db, N_HEADS, HEAD_DIM)
    qi8 = qi_s[:db].reshape(db, IDX_HEADS, LANES)[:, :, :IDX_DIM]
    wi8 = idx_s[:db, IDX_DIM:IDX_DIM + IDX_HEADS].reshape(db, IDX_HEADS, 1)
    kin = idx_s[:db, :IDX_DIM].astype(BF16).reshape(db, 1, IDX_DIM)
    kn8 = jnp.repeat(k_s[:db].reshape(db, N_KV_HEADS, HEAD_DIM), GROUP, axis=1).astype(BF16)
    vn8 = jnp.repeat(v_s[:db].reshape(db, N_KV_HEADS, HEAD_DIM), GROUP, axis=1).astype(BF16)
    ck2 = cache_k[0].reshape(n_pool * PAGE_SIZE * N_KV_HEADS, HEAD_DIM)
    cv2 = cache_v[0].reshape(n_pool * PAGE_SIZE * N_KV_HEADS, HEAD_DIM)
    meta = _decode_select(page_table, qi8, wi8, kin, jnp.swapaxes(cache_kidx[0], 1, 2))
    attn_s8 = _decode_attend(page_table, meta[:, 0, :], q8, meta, kn8, vn8, ck2, cv2)
    attn_s = _pad_rows(attn_s8.reshape(db, N_HEADS * HEAD_DIM), ts)

    h2, e_t, gate_t = _post_attn(
        xp2, xs2, attn_p, attn_s, conv_p, conv_s, ga_p, ga_s, gc_p, gc_s,
        w_o_attn[0].astype(BF16), w_o_conv[0].astype(BF16), w_o[0].astype(BF16),
        ln1_g[0].reshape(1, D_MODEL), ln1_b[0].reshape(1, D_MODEL),
        router_w[0].T, router_bias[0].reshape(n_exp, 1))

    n_tok = tp + ts
    rank_t, cnt = _rank(e_t, n_exp)
    counts = cnt[:, 0]
    padded = (counts + MOE_BLOCK - 1) // MOE_BLOCK * MOE_BLOCK
    pend = jnp.cumsum(padded)
    pstart = pend - padded
    n_blocks = (n_tok * TOP_K + n_exp * (MOE_BLOCK - 1) + MOE_BLOCK - 1) // MOE_BLOCK
    nused = (pend[-1] // MOE_BLOCK).astype(I32)
    blk = jnp.minimum(jnp.arange(n_blocks, dtype=I32), nused - 1)
    block_e = jnp.minimum(jnp.sum((pend[None, :] <= (blk * MOE_BLOCK)[:, None]).astype(I32), axis=1), n_exp - 1)
    slot_t = _slots(e_t, rank_t, jnp.broadcast_to(pstart.astype(F32)[:, None], (n_exp, LANES)))
    xs = _dispatch(pend.astype(I32), padded.astype(I32), slot_t, h2, n_blocks * MOE_BLOCK)
    ys = _experts(block_e, nused.reshape(1), xs, moe_w_gate[0], moe_w_up[0], moe_w_down[0])
    y_p, y_s = _combine(slot_t, gate_t.T, h2, ys,
                        shared_w_gate[0].astype(BF16), shared_w_up[0].astype(BF16),
                        shared_w_down[0].astype(BF16),
                        ln2_g[0].reshape(1, D_MODEL), ln2_b[0].reshape(1, D_MODEL), tp, ts)

    conv_sample = jnp.stack([state_conv[0, :, 1, :], u_s[:db]], axis=1)[None]
    return (y_p.reshape(nb, seq, D_MODEL),
            y_s[:db].reshape(db, 1, D_MODEL),
            k_p.reshape(1, nb, seq, N_KV_HEADS, HEAD_DIM),
            v_p.reshape(1, nb, seq, N_KV_HEADS, HEAD_DIM),
            idx_p[:, :IDX_DIM].reshape(1, nb, seq, IDX_DIM),
            ulast_p[:, SUBLANES - (CONV_W - 1):, :][None],
            k_s[:db].reshape(1, db, 1, N_KV_HEADS, HEAD_DIM),
            v_s[:db].reshape(1, db, 1, N_KV_HEADS, HEAD_DIM),
            idx_s[:db, :IDX_DIM].reshape(1, db, 1, IDX_DIM),
            conv_sample)
```

```python
import functools

import jax
import jax.numpy as jnp
import numpy as np
from jax import lax
from jax.experimental import pallas as pl
from jax.experimental.pallas import tpu as pltpu

F32 = jnp.float32
BF16 = jnp.bfloat16
I32 = jnp.int32

D_MODEL = 1024
N_HEADS = 8
HEAD_DIM = 128
N_KV_HEADS = 2
GROUP = N_HEADS // N_KV_HEADS
IDX_HEADS = 8
IDX_DIM = 64
TOPK_KEYS = 256
IDX_SCALE = IDX_DIM ** -0.5 * IDX_HEADS ** -0.5
Q_SCALE = HEAD_DIM ** -0.5
PAGE_SIZE = 128
D_CONV = D_MODEL
CONV_W = 3
TOP_K = 8
N_GROUPS = 8
TOPK_GROUPS = 4
ROUTED_SCALE = 2.5
DEPTH = 1
ALPHA = (2 * DEPTH) ** 0.25
LN_EPS = 1e-5
NEG = -1e30
IN_WIDTHS = (N_HEADS * HEAD_DIM, N_KV_HEADS * HEAD_DIM, N_KV_HEADS * HEAD_DIM,
             IDX_HEADS * IDX_DIM, IDX_DIM, IDX_HEADS,
             D_CONV, D_CONV, D_CONV, D_MODEL, D_MODEL)

LANES = 128
SUBLANES = 8
VMEM_LIMIT_BYTES = 56 * 1024 * 1024

PROJ_ROWS = 512
ATT_Q = 256
SEARCH_ROWS = 128
TOKEN_TILE = 256
MOE_BLOCK = 256
CONV_CHUNK = 256
INT_MIN = -2 ** 31


def _np_key(v):
    b = int(np.float32(v).view(np.int32))
    return b ^ ((b >> 31) & 0x7FFFFFFF)


KEY_HALF = _np_key(NEG * 0.5)


def _sort_key(s):
    b = lax.bitcast_convert_type(s + 0.0, I32)
    return b ^ ((b >> 31) & jnp.int32(0x7FFFFFFF))


def _dot(a, b):
    return jnp.dot(a, b, preferred_element_type=F32)


def _dot_nt(a, b):
    return lax.dot_general(a, b, (((1,), (1,)), ((), ())), preferred_element_type=F32)


def _params(n_grid):
    return pltpu.CompilerParams(dimension_semantics=("arbitrary",) * n_grid,
                                vmem_limit_bytes=VMEM_LIMIT_BYTES)


def _const_spec(shape):
    nd = len(shape)
    return pl.BlockSpec(shape, lambda *_: (0,) * nd, pipeline_mode=pl.Buffered(1))


def _proj_common(xb, w1, w2, w3, wga, wgc, q_o, k_o, v_o, kv_o, qi_o, idx_o, kib_o, sga_o, sgc_o):
    nq = N_HEADS * HEAD_DIM
    nkv = N_KV_HEADS * HEAD_DIM
    z1 = _dot(xb, w1[...])
    q_o[...] = (z1[:, :nq] * Q_SCALE).astype(BF16)
    k = z1[:, nq:nq + nkv]
    v = z1[:, nq + nkv:nq + 2 * nkv]
    k_o[...] = k
    v_o[...] = v
    kv_o[:, :nkv] = k.astype(BF16)
    kv_o[:, nkv:] = v.astype(BF16)
    qi_o[...] = _dot(xb, w2[...]).astype(BF16)
    z3 = _dot(xb, w3[...])
    idx_o[...] = z3
    kib_o[...] = z3.astype(BF16)
    sga_o[...] = jax.nn.sigmoid(_dot(xb, wga[...])).astype(BF16)
    sgc_o[...] = jax.nn.sigmoid(_dot(xb, wgc[...])).astype(BF16)


def _in_proj_prompt_kernel(x_ref, w1, w2, w3, wb, wc, wx, wga, wgc, cw_ref,
                           q_o, k_o, v_o, kv_o, qi_o, idx_o, kib_o, conv_o, sga_o, sgc_o, ulast_o,
                           tail_ref):
    j = pl.program_id(1)

    @pl.when(j == 0)
    def _():
        tail_ref[...] = jnp.zeros_like(tail_ref)

    xb = x_ref[...].astype(BF16)
    _proj_common(xb, w1, w2, w3, wga, wgc, q_o, k_o, v_o, kv_o, qi_o, idx_o, kib_o, sga_o, sgc_o)
    tm = xb.shape[0]
    row = lax.broadcasted_iota(I32, (tm, CONV_CHUNK), 0)
    for c in range(D_CONV // CONV_CHUNK):
        cs = slice(c * CONV_CHUNK, (c + 1) * CONV_CHUNK)
        gb = _dot(xb, wb[:, cs])
        u = _dot(xb, wc[:, cs]) * _dot(xb, wx[:, cs])
        p1 = tail_ref[SUBLANES - 1:SUBLANES, cs]
        p2 = tail_ref[SUBLANES - 2:SUBLANES - 1, cs]
        u1 = jnp.where(row == 0, p1, pltpu.roll(u, 1, 0))
        u2 = jnp.where(row == 0, p2, jnp.where(row == 1, p1, pltpu.roll(u, 2, 0)))
        conv = cw_ref[0:1, cs] * u2 + cw_ref[1:2, cs] * u1 + cw_ref[2:3, cs] * u
        conv_o[:, cs] = (gb * conv).astype(BF16)
        tail_ref[:, cs] = u[tm - SUBLANES:, :]
    ulast_o[0] = tail_ref[...]


def _in_proj_decode_kernel(x_ref, p2_ref, p1_ref, w1, w2, w3, wb, wc, wx, wga, wgc, cw_ref,
                           q_o, k_o, v_o, kv_o, qi_o, idx_o, kib_o, conv_o, sga_o, sgc_o, u_o):
    xb = x_ref[...].astype(BF16)
    _proj_common(xb, w1, w2, w3, wga, wgc, q_o, k_o, v_o, kv_o, qi_o, idx_o, kib_o, sga_o, sgc_o)
    for c in range(D_CONV // CONV_CHUNK):
        cs = slice(c * CONV_CHUNK, (c + 1) * CONV_CHUNK)
        gb = _dot(xb, wb[:, cs])
        u = _dot(xb, wc[:, cs]) * _dot(xb, wx[:, cs])
        conv = cw_ref[0:1, cs] * p2_ref[:, cs] + cw_ref[1:2, cs] * p1_ref[:, cs] + cw_ref[2:3, cs] * u
        conv_o[:, cs] = (gb * conv).astype(BF16)
        u_o[:, cs] = u


def _prep_in_weights(w_in):
    w = w_in.astype(BF16)
    offs = np.concatenate([[0], np.cumsum(IN_WIDTHS)])
    part = [w[:, int(offs[i]):int(offs[i + 1])] for i in range(len(IN_WIDTHS))]
    q, k, v, qi, ki, wi, gb, gc, xv, ga, gcv = part
    d = w.shape[0]
    w1 = jnp.concatenate([q, k, v], axis=1)
    w2 = jnp.pad(qi.reshape(d, IDX_HEADS, IDX_DIM),
                 ((0, 0), (0, 0), (0, LANES - IDX_DIM))).reshape(d, IDX_HEADS * LANES)
    w3 = jnp.concatenate([ki, wi, jnp.zeros((d, LANES - IDX_DIM - IDX_HEADS), BF16)], axis=1)
    return (w1, w2, w3, gb, gc, xv, ga, gcv)


def _proj_out_shapes(t):
    nkv = N_KV_HEADS * HEAD_DIM
    sd = jax.ShapeDtypeStruct
    return [sd((t, N_HEADS * HEAD_DIM), BF16), sd((t, nkv), F32), sd((t, nkv), F32), sd((t, 2 * nkv), BF16),
            sd((t, IDX_HEADS * LANES), BF16), sd((t, LANES), F32), sd((t, LANES), BF16),
            sd((t, D_CONV), BF16), sd((t, D_MODEL), BF16), sd((t, D_MODEL), BF16)]


def _in_proj_prompt(x2d, weights, conv_w, nb):
    t = x2d.shape[0]
    s = t // nb
    tm = min(PROJ_ROWS, s)
    nj = s // tm
    row_spec = lambda w: pl.BlockSpec((tm, w), lambda b, j: (b * nj + j, 0))
    out_shapes = _proj_out_shapes(t) + [jax.ShapeDtypeStruct((nb, SUBLANES, D_CONV), F32)]
    out_specs = [row_spec(o.shape[1]) for o in out_shapes[:-1]]
    out_specs.append(pl.BlockSpec((1, SUBLANES, D_CONV), lambda b, j: (b, 0, 0)))
    return pl.pallas_call(
        _in_proj_prompt_kernel,
        grid=(nb, nj),
        in_specs=[row_spec(D_MODEL)] + [_const_spec(w.shape) for w in weights] + [_const_spec(conv_w.shape)],
        out_specs=out_specs,
        out_shape=out_shapes,
        scratch_shapes=[pltpu.VMEM((SUBLANES, D_CONV), F32)],
        compiler_params=_params(2),
        name="in_proj_prompt",
    )(x2d, *weights, conv_w)


def _in_proj_decode(x2d, p2, p1, weights, conv_w):
    t = x2d.shape[0]
    row_spec = lambda w: pl.BlockSpec((t, w), lambda i: (0, 0))
    out_shapes = _proj_out_shapes(t) + [jax.ShapeDtypeStruct((t, D_CONV), F32)]
    return pl.pallas_call(
        _in_proj_decode_kernel,
        grid=(1,),
        in_specs=[row_spec(D_MODEL)] * 3 + [_const_spec(w.shape) for w in weights] + [_const_spec(conv_w.shape)],
        out_specs=[row_spec(o.shape[1]) for o in out_shapes],
        out_shape=out_shapes,
        compiler_params=_params(1),
        name="in_proj_decode",
    )(x2d, p2, p1, *weights, conv_w)


def _attn_prompt_kernel(q_ref, kv_ref, qi_ref, kib_ref, idx_ref, o_ref,
                        key_ref, bias_ref, t_ref, j_ref, need_ref, m_ref, acc_ref,
                        *, n_keep, seq_len):
    tq = q_ref.shape[0]
    i = pl.program_id(1)
    nchunk = i + 1
    nsub = tq // LANES
    lane = lax.broadcasted_iota(I32, (tq, LANES), 1)

    wi_cols = [idx_ref[:, IDX_DIM + h:IDX_DIM + h + 1] * IDX_SCALE for h in range(IDX_HEADS)]
    row_i = lax.broadcasted_iota(I32, (tq, tq), 0)
    col_i = lax.broadcasted_iota(I32, (tq, tq), 1)

    def score_chunk(c, carry):
        kic = kib_ref[pl.ds(pl.multiple_of(c * tq, tq), tq), :]
        acc = jnp.zeros((tq, tq), F32)
        for h in range(IDX_HEADS):
            d = _dot_nt(qi_ref[:, h * LANES:(h + 1) * LANES], kic)
            acc = acc + jnp.maximum(d, 0.0) * wi_cols[h]
        s = jnp.where((c < i) | (col_i <= row_i), acc, NEG)
        key_ref[c] = _sort_key(s)
        return carry

    lax.fori_loop(0, nchunk, score_chunk, 0)

    rb = SEARCH_ROWS

    def count(r0, pred):
        def body(c, cnt):
            for k in range(nsub):
                blk = key_ref[c, r0:r0 + rb, k * LANES:(k + 1) * LANES]
                cnt = cnt + pred(blk, c * tq + k * LANES).astype(I32)
            return cnt
        cnt = lax.fori_loop(0, nchunk, body, jnp.zeros((rb, LANES), I32))
        return jnp.sum(cnt, axis=1, keepdims=True)

    any_excess = jnp.int32(0)
    for r in range(tq // rb):
        r0 = r * rb

        def bit_body(bi, t):
            cand = t + lax.shift_left(jnp.int32(1), 31 - bi)
            cand_b = jnp.broadcast_to(cand, (rb, LANES))
            cnt = count(r0, lambda blk, _: blk >= cand_b)
            return jnp.where(cnt >= n_keep, cand, t)

        t = lax.fori_loop(0, 32, bit_body, jnp.full((rb, 1), INT_MIN, I32))
        t_b = jnp.broadcast_to(t, (rb, LANES))
        cnt_gt = count(r0, lambda blk, _: blk > t_b)
        cnt_eq = count(r0, lambda blk, _: blk == t_b)
        need = n_keep - cnt_gt
        excess = (cnt_eq > need) & (t > KEY_HALF)
        any_excess = any_excess + jnp.max(excess.astype(I32))
        t_ref[r0:r0 + rb, :] = t_b
        need_ref[r0:r0 + rb, :] = jnp.broadcast_to(need, (rb, LANES))

    j_ref[...] = jnp.full(j_ref.shape, seq_len, I32)

    @pl.when(any_excess > 0)
    def _():
        nbits = max(1, (seq_len - 1).bit_length())
        lane_rb = lax.broadcasted_iota(I32, (rb, LANES), 1)
        for r in range(tq // rb):
            r0 = r * rb
            t_b = t_ref[r0:r0 + rb, :]
            need = need_ref[r0:r0 + rb, 0:1]

            def jbit(bi, jj):
                cand = jj | lax.shift_left(jnp.int32(1), nbits - 1 - bi)
                cand_b = jnp.broadcast_to(cand, (rb, LANES))
                g = count(r0, lambda blk, base: (blk == t_b) & ((base + lane_rb) < cand_b))
                return jnp.where(g < need, cand, jj)

            jj = lax.fori_loop(0, nbits, jbit, jnp.zeros((rb, 1), I32))
            j_ref[r0:r0 + rb, :] = jnp.broadcast_to(jj, (rb, LANES))

    t_all = t_ref[...]
    j_all = j_ref[...]

    def bias_chunk(c, carry):
        for k in range(nsub):
            blk = key_ref[c, :, k * LANES:(k + 1) * LANES]
            col = c * tq + k * LANES + lane
            sel = (blk > KEY_HALF) & ((blk > t_all) | ((blk == t_all) & (col <= j_all)))
            bias_ref[c, :, k * LANES:(k + 1) * LANES] = jnp.where(sel, 0.0, NEG)
        return carry

    lax.fori_loop(0, nchunk, bias_chunk, 0)

    nkv = N_KV_HEADS * HEAD_DIM
    gq = GROUP * tq

    def group_logits(c, g):
        rows = pl.ds(pl.multiple_of(c * tq, tq), tq)
        kc = kv_ref[rows, g * HEAD_DIM:(g + 1) * HEAD_DIM]
        qg = jnp.concatenate([q_ref[:, h * HEAD_DIM:(h + 1) * HEAD_DIM]
                              for h in range(g * GROUP, (g + 1) * GROUP)], axis=0)
        s = _dot_nt(qg, kc).reshape(GROUP, tq, tq) + bias_ref[c][None]
        return s.reshape(gq, tq)

    m_ref[...] = jnp.full(m_ref.shape, NEG, F32)

    def max_chunk(c, carry):
        for g in range(N_KV_HEADS):
            s = group_logits(c, g)
            mx = s[:, :LANES]
            for k in range(1, nsub):
                mx = jnp.maximum(mx, s[:, k * LANES:(k + 1) * LANES])
            m_ref[g] = jnp.maximum(m_ref[g], mx)
        return carry

    lax.fori_loop(0, nchunk, max_chunk, 0)
    for g in range(N_KV_HEADS):
        m_ref[g] = jnp.broadcast_to(jnp.max(m_ref[g], axis=1, keepdims=True), (gq, LANES))

    acc_ref[...] = jnp.zeros_like(acc_ref)
    ones = jnp.ones((tq, HEAD_DIM), BF16)

    def pv_chunk(c, carry):
        rows = pl.ds(pl.multiple_of(c * tq, tq), tq)
        for g in range(N_KV_HEADS):
            s = group_logits(c, g)
            m = m_ref[g]
            p = jnp.concatenate([jnp.exp(s[:, k * LANES:(k + 1) * LANES] - m) for k in range(nsub)], axis=1)
            vc = kv_ref[rows, nkv + g * HEAD_DIM:nkv + (g + 1) * HEAD_DIM]
            acc_ref[g] = acc_ref[g] + _dot(p.astype(BF16), jnp.concatenate([vc, ones], axis=1))
        return carry

    lax.fori_loop(0, nchunk, pv_chunk, 0)
    for h in range(N_HEADS):
        a = acc_ref[h // GROUP, (h % GROUP) * tq:(h % GROUP + 1) * tq, :]
        o_ref[:, h * HEAD_DIM:(h + 1) * HEAD_DIM] = (a[:, :HEAD_DIM] / a[:, HEAD_DIM:]).astype(BF16)


def _attn_prompt(q, kvb, qi, kib, idx, nb):
    t = q.shape[0]
    s = t // nb
    tq = min(ATT_Q, s)
    nq = s // tq
    n_keep = min(TOPK_KEYS, s // 4)
    blk = lambda w: pl.BlockSpec((tq, w), lambda b, i: (b * nq + i, 0))
    seq = lambda w: pl.BlockSpec((s, w), lambda b, i: (b, 0))
    return pl.pallas_call(
        functools.partial(_attn_prompt_kernel, n_keep=n_keep, seq_len=s),
        grid=(nb, nq),
        in_specs=[blk(q.shape[1]), seq(kvb.shape[1]), blk(qi.shape[1]), seq(kib.shape[1]), blk(idx.shape[1])],
        out_specs=blk(N_HEADS * HEAD_DIM),
        out_shape=jax.ShapeDtypeStruct((t, N_HEADS * HEAD_DIM), BF16),
        scratch_shapes=[pltpu.VMEM((nq, tq, tq), I32), pltpu.VMEM((nq, tq, tq), F32),
                        pltpu.VMEM((tq, LANES), I32), pltpu.VMEM((tq, LANES), I32), pltpu.VMEM((tq, LANES), I32),
                        pltpu.VMEM((N_KV_HEADS, GROUP * tq, LANES), F32),
                        pltpu.VMEM((N_KV_HEADS, GROUP * tq, 2 * HEAD_DIM), F32)],
        compiler_params=_params(2),
        name="attn_prompt",
    )(q, kvb, qi, kib, idx)


DECODE_PAGES_PER_ROW = 4


def _decode_select_kernel(pt_ref, qi_ref, wi_ref, kin_ref, cx_hbm, meta_o, xbuf, key_ref, sem,
                          *, n_pages, n_keep):
    b = pl.program_id(0)
    nb = pl.num_programs(0)
    ppr = DECODE_PAGES_PER_ROW
    w = ppr * PAGE_SIZE
    nrow = n_pages // ppr
    past = n_pages * PAGE_SIZE

    def page_copy(bb, p, slot):
        return pltpu.make_async_copy(cx_hbm.at[pt_ref[bb, p]], xbuf.at[slot, p], sem.at[slot])

    def start_batch(bb, slot):
        def body(p, carry):
            page_copy(bb, p, slot).start()
            return carry
        lax.fori_loop(0, n_pages, body, 0)

    def wait_batch(bb, slot):
        def body(p, carry):
            page_copy(bb, p, slot).wait()
            return carry
        lax.fori_loop(0, n_pages, body, 0)

    @pl.when(b == 0)
    def _():
        start_batch(0, 0)

    @pl.when(b + 1 < nb)
    def _():
        start_batch(b + 1, (b + 1) % 2)

    slot = b % 2
    wait_batch(b, slot)

    qi8 = qi_ref[0]
    wi8 = wi_ref[0] * IDX_SCALE

    def score_row(r, carry):
        kx = jnp.concatenate([xbuf[slot, r * ppr + k] for k in range(ppr)], axis=1).astype(BF16)
        d = _dot(qi8, kx)
        sc = jnp.sum(jnp.maximum(d, 0.0) * wi8, axis=0, keepdims=True)
        key_ref[pl.ds(r, 1), :] = _sort_key(sc)
        return carry

    lax.fori_loop(0, nrow, score_row, 0, unroll=4)
    dn = jnp.sum(qi8.astype(F32) * kin_ref[0].astype(F32), axis=1, keepdims=True)
    key_new = _sort_key(jnp.sum(jnp.maximum(dn, 0.0) * wi8, axis=0, keepdims=True))

    keys = key_ref[...]
    col = (lax.broadcasted_iota(I32, keys.shape, 0) * w + lax.broadcasted_iota(I32, keys.shape, 1))

    def total(x):
        return jnp.sum(jnp.sum(x.astype(I32), axis=1, keepdims=True), axis=0, keepdims=True)

    def bit_body(bi, t):
        cand = t + lax.shift_left(jnp.int32(1), 31 - bi)
        cnt = total(keys >= cand) + (key_new >= cand).astype(I32)
        return jnp.where(cnt >= n_keep, cand, t)

    t = lax.fori_loop(0, 32, bit_body, jnp.full((1, 1), INT_MIN, I32))
    need = n_keep - (total(keys > t) + (key_new > t).astype(I32))
    nbits = past.bit_length()

    def jbit(bi, jj):
        cand = jj | lax.shift_left(jnp.int32(1), nbits - 1 - bi)
        g = total((keys == t) & (col < cand)) + ((key_new == t) & (past < cand)).astype(I32)
        return jnp.where(g < need, cand, jj)

    jj = lax.fori_loop(0, nbits, jbit, jnp.zeros((1, 1), I32))
    sel = (keys > t) | ((keys == t) & (col <= jj))
    sel_new = (key_new > t) | ((key_new == t) & (past <= jj))

    incl_lane = (lax.broadcasted_iota(I32, (w, w), 0) <= lax.broadcasted_iota(I32, (w, w), 1)).astype(BF16)
    cnt_in_row = _dot(sel.astype(BF16), incl_lane)
    row_tot = cnt_in_row[:, w - 1:w]
    rows_before = (lax.broadcasted_iota(I32, (nrow, nrow), 1) < lax.broadcasted_iota(I32, (nrow, nrow), 0))
    row_off = _dot(rows_before.astype(BF16), jnp.broadcast_to(row_tot, (nrow, LANES)).astype(BF16))[:, 0:1]
    jl = lax.broadcasted_iota(I32, (1, n_keep), 1).astype(F32)
    row_j = jnp.sum(((row_off + row_tot) <= jl).astype(F32), axis=0, keepdims=True)
    onehot = (lax.broadcasted_iota(I32, (nrow, n_keep), 0).astype(F32) == row_j)
    cnt_j = lax.dot_general(cnt_in_row.astype(BF16), onehot.astype(BF16), (((0,), (0,)), ((), ())),
                            preferred_element_type=F32)
    off_j = jnp.sum(jnp.where(onehot, row_off, 0.0), axis=0, keepdims=True)
    lane_j = jnp.sum((cnt_j <= (jl - off_j)).astype(F32), axis=0, keepdims=True)
    pos = jnp.minimum((row_j * w + lane_j).astype(I32), past - 1)
    is_pos_row = lax.broadcasted_iota(I32, (SUBLANES, n_keep), 0) == 0
    meta_o[0] = jnp.where(is_pos_row, pos, sel_new.astype(I32))


def _decode_select(page_table, qi8, wi8, kin, cxt):
    db, n_pages = page_table.shape
    past = n_pages * PAGE_SIZE
    n_keep = min(TOPK_KEYS, (past + 1) // 4)
    per_b = lambda a: pl.BlockSpec((1,) + a.shape[1:], lambda b, pt: (b, 0, 0))
    grid_spec = pltpu.PrefetchScalarGridSpec(
        num_scalar_prefetch=1,
        grid=(db,),
        in_specs=[per_b(qi8), per_b(wi8), per_b(kin), pl.BlockSpec(memory_space=pl.ANY)],
        out_specs=pl.BlockSpec((1, SUBLANES, n_keep), lambda b, pt: (b, 0, 0)),
        scratch_shapes=[pltpu.VMEM((2, n_pages, IDX_DIM, PAGE_SIZE), F32),
                        pltpu.VMEM((n_pages // DECODE_PAGES_PER_ROW, DECODE_PAGES_PER_ROW * PAGE_SIZE), I32),
                        pltpu.SemaphoreType.DMA((2,))],
    )
    return pl.pallas_call(
        functools.partial(_decode_select_kernel, n_pages=n_pages, n_keep=n_keep),
        grid_spec=grid_spec,
        out_shape=jax.ShapeDtypeStruct((db, SUBLANES, n_keep), I32),
        compiler_params=_params(1),
        name="decode_select",
    )(page_table, qi8, wi8, kin, cxt)


def _decode_attend_kernel(pt_ref, pos_ref, q_ref, meta_ref, kn_ref, vn_ref, ck_hbm, cv_hbm, o_ref,
                          kbuf, vbuf, expand_ref, sem, *, n_keep):
    b = pl.program_id(0)
    nb = pl.num_programs(0)
    pos_per_tile = SUBLANES // N_KV_HEADS
    ncol = n_keep * SUBLANES

    def item_copies(bb, j, slot):
        pos = pos_ref[bb, j]
        phys = pt_ref[bb, pos // PAGE_SIZE]
        tile = (phys * PAGE_SIZE + pos % PAGE_SIZE) // pos_per_tile
        src = pl.ds(pl.multiple_of(tile * SUBLANES, SUBLANES), SUBLANES)
        dst = pl.ds(pl.multiple_of(j * SUBLANES, SUBLANES), SUBLANES)
        return (pltpu.make_async_copy(ck_hbm.at[src], kbuf.at[slot, dst], sem.at[0, slot]),
                pltpu.make_async_copy(cv_hbm.at[src], vbuf.at[slot, dst], sem.at[1, slot]))

    def start_batch(bb, slot):
        def body(j, carry):
            for prio, cp in enumerate(item_copies(bb, j, slot)):
                cp.start(priority=prio)
            return carry
        lax.fori_loop(0, n_keep, body, 0)

    def wait_batch(slot):
        pltpu.make_async_copy(ck_hbm.at[pl.ds(0, ncol)], kbuf.at[slot], sem.at[0, slot]).wait()
        pltpu.make_async_copy(cv_hbm.at[pl.ds(0, ncol)], vbuf.at[slot], sem.at[1, slot]).wait()

    @pl.when(b == 0)
    def _():
        start_batch(0, 0)
        item_of_col = lax.broadcasted_iota(I32, (n_keep, ncol), 1) // SUBLANES
        expand_ref[...] = (item_of_col == lax.broadcasted_iota(I32, (n_keep, ncol), 0)).astype(BF16)

    @pl.when(b + 1 < nb)
    def _():
        start_batch(b + 1, (b + 1) % 2)

    slot = b % 2
    wait_batch(slot)

    q8 = q_ref[0]
    meta = meta_ref[0]
    new_kept = meta[1:2, 0:1] > 0
    sub = (meta % pos_per_tile).astype(BF16)
    sub_col = _dot(sub, expand_ref[...])[0:1, :]
    head_i = lax.broadcasted_iota(I32, (N_HEADS, ncol), 0)
    col_i = lax.broadcasted_iota(I32, (N_HEADS, ncol), 1)
    want = sub_col * N_KV_HEADS + (head_i // GROUP).astype(F32)
    is_new_item = new_kept & (col_i // SUBLANES == n_keep - 1)
    ok = ((col_i % SUBLANES).astype(F32) == want) & jnp.logical_not(is_new_item)

    s = jnp.where(ok, _dot_nt(q8, kbuf[slot].astype(BF16)), NEG)
    s_new = jnp.sum(q8.astype(F32) * kn_ref[0].astype(F32), axis=1, keepdims=True)
    s_new = jnp.where(new_kept, s_new, NEG)
    m = jnp.maximum(jnp.max(s, axis=1, keepdims=True), s_new)
    p = jnp.exp(s - m)
    p_new = jnp.exp(s_new - m)
    l = jnp.sum(p, axis=1, keepdims=True) + p_new
    acc = _dot(p.astype(BF16), vbuf[slot].astype(BF16)) + p_new.astype(BF16).astype(F32) * vn_ref[0].astype(F32)
    o_ref[0] = (acc / l).astype(BF16)


def _decode_attend(page_table, pos, q8, meta, kn8, vn8, ck2, cv2):
    db, n_keep = pos.shape
    per_b = lambda a: pl.BlockSpec((1,) + a.shape[1:], lambda b, pt, ps: (b, 0, 0))
    any_spec = pl.BlockSpec(memory_space=pl.ANY)
    ncol = n_keep * SUBLANES
    grid_spec = pltpu.PrefetchScalarGridSpec(
        num_scalar_prefetch=2,
        grid=(db,),
        in_specs=[per_b(q8), per_b(meta), per_b(kn8), per_b(vn8), any_spec, any_spec],
        out_specs=pl.BlockSpec((1, N_HEADS, HEAD_DIM), lambda b, pt, ps: (b, 0, 0)),
        scratch_shapes=[pltpu.VMEM((2, ncol, HEAD_DIM), F32), pltpu.VMEM((2, ncol, HEAD_DIM), F32),
                        pltpu.VMEM((n_keep, ncol), BF16), pltpu.SemaphoreType.DMA((2, 2))],
    )
    return pl.pallas_call(
        functools.partial(_decode_attend_kernel, n_keep=n_keep),
        grid_spec=grid_spec,
        out_shape=jax.ShapeDtypeStruct((db, N_HEADS, HEAD_DIM), BF16),
        compiler_params=_params(1),
        name="decode_attend",
    )(page_table, pos, q8, meta, kn8, vn8, ck2, cv2)


def _layer_norm(r, g, b):
    mu = jnp.mean(r, axis=-1, keepdims=True)
    d = r - mu
    var = jnp.mean(d * d, axis=-1, keepdims=True)
    return d * lax.rsqrt(var + LN_EPS) * g + b


def _route(logits_t, rbias):
    n_exp, tm = logits_t.shape
    epg = n_exp // N_GROUPS
    s = jax.nn.sigmoid(logits_t)
    sb = s + rbias
    ie = lax.broadcasted_iota(I32, (epg, tm), 0)
    gs_rows = []
    for g in range(N_GROUPS):
        blk = sb[g * epg:(g + 1) * epg, :]
        m1 = jnp.max(blk, axis=0, keepdims=True)
        i1 = jnp.min(jnp.where(blk == m1, ie, epg), axis=0, keepdims=True)
        m2 = jnp.max(jnp.where(ie == i1, -jnp.inf, blk), axis=0, keepdims=True)
        gs_rows.append(m1 + m2)
    picked = [jnp.zeros((1, tm), jnp.bool_) for _ in range(N_GROUPS)]
    cur = list(gs_rows)
    for _ in range(TOPK_GROUPS):
        mx = cur[0]
        for g in range(1, N_GROUPS):
            mx = jnp.maximum(mx, cur[g])
        found = jnp.zeros((1, tm), jnp.bool_)
        for g in range(N_GROUPS):
            hit = (cur[g] == mx) & jnp.logical_not(found)
            found = found | hit
            picked[g] = picked[g] | hit
            cur[g] = jnp.where(hit, -jnp.inf, cur[g])
    masked = jnp.concatenate(
        [jnp.where(picked[g], sb[g * epg:(g + 1) * epg, :], NEG) for g in range(N_GROUPS)], axis=0)
    iall = lax.broadcasted_iota(I32, (n_exp, tm), 0)
    e_rows, w_rows = [], []
    for _ in range(TOP_K):
        mx = jnp.max(masked, axis=0, keepdims=True)
        ix = jnp.min(jnp.where(masked == mx, iall, n_exp), axis=0, keepdims=True)
        hit = iall == ix
        w_rows.append(jnp.sum(jnp.where(hit, s, 0.0), axis=0, keepdims=True))
        e_rows.append(ix)
        masked = jnp.where(hit, -jnp.inf, masked)
    wsum = w_rows[0]
    for w in w_rows[1:]:
        wsum = wsum + w
    gates = [w / wsum * ROUTED_SCALE for w in w_rows]
    return jnp.concatenate(e_rows, axis=0), jnp.concatenate(gates, axis=0)


def _post_attn_kernel(xp, xs, ap, as_, cp, cs, gap, gas, gcp, gcs, woa, woc, wo, g1, b1, rwt, rb,
                      h2_o, e_o, gate_o, *, n_prompt_tiles):
    i = pl.program_id(0)
    is_p = i < n_prompt_tiles
    pick = lambda a, b: jnp.where(is_p, a[...], b[...])
    a = _dot(pick(ap, as_), woa[...])
    c = _dot(pick(cp, cs), woc[...])
    merged = pick(gap, gas).astype(F32) * a + pick(gcp, gcs).astype(F32) * c
    r = ALPHA * pick(xp, xs) + _dot(merged.astype(BF16), wo[...])
    h = _layer_norm(r, g1[...], b1[...])
    tm = h.shape[0]
    for k in range(D_MODEL // LANES):
        h2_o[pl.ds(k, tm, stride=SUBLANES), :] = h[:, k * LANES:(k + 1) * LANES]
    logits_t = lax.dot_general(rwt[...], h, (((1,), (1,)), ((), ())),
                               precision=lax.Precision.HIGHEST, preferred_element_type=F32)
    e_idx, gates = _route(logits_t, rb[...])
    e_o[...] = e_idx
    gate_o[...] = gates


def _post_attn(x_p, x_s, attn_p, attn_s, conv_p, conv_s, ga_p, ga_s, gc_p, gc_s,
               woa, woc, wo, g1, b1, rwt, rb):
    tp, ts = x_p.shape[0], x_s.shape[0]
    tm = TOKEN_TILE
    npt, nst = tp // tm, ts // tm
    n_tok = tp + ts
    p_spec = pl.BlockSpec((tm, D_MODEL), lambda i: (jnp.minimum(i, npt - 1), 0))
    s_spec = pl.BlockSpec((tm, D_MODEL), lambda i: (jnp.maximum(i - npt, 0), 0))
    consts = [woa, woc, wo, g1, b1, rwt, rb]
    return pl.pallas_call(
        functools.partial(_post_attn_kernel, n_prompt_tiles=npt),
        grid=(npt + nst,),
        in_specs=[p_spec, s_spec] * 5 + [_const_spec(c.shape) for c in consts],
        out_specs=[pl.BlockSpec((tm * SUBLANES, LANES), lambda i: (i, 0)),
                   pl.BlockSpec((TOP_K, tm), lambda i: (0, i)),
                   pl.BlockSpec((TOP_K, tm), lambda i: (0, i))],
        out_shape=[jax.ShapeDtypeStruct((n_tok * SUBLANES, LANES), F32),
                   jax.ShapeDtypeStruct((TOP_K, n_tok), I32),
                   jax.ShapeDtypeStruct((TOP_K, n_tok), F32)],
        compiler_params=_params(1),
        name="post_attn",
    )(x_p, x_s, attn_p, attn_s, conv_p, conv_s, ga_p, ga_s, gc_p, gc_s, *consts)


def _rank_kernel(e_ref, rank_o, cnt_o, carry_ref):
    i = pl.program_id(0)

    @pl.when(i == 0)
    def _():
        carry_ref[...] = jnp.zeros_like(carry_ref)

    n_exp = carry_ref.shape[0]
    e = e_ref[...]
    tk = e.shape[1]
    ie = lax.broadcasted_iota(I32, (n_exp, tk), 0)
    onehot = jnp.zeros((n_exp, tk), F32)
    for j in range(TOP_K):
        onehot = onehot + (ie == e[j:j + 1, :]).astype(F32)
    before = (lax.broadcasted_iota(I32, (tk, tk), 0) < lax.broadcasted_iota(I32, (tk, tk), 1)).astype(BF16)
    prefix = _dot(onehot.astype(BF16), before) + carry_ref[:, 0:1]
    rows = [jnp.sum(jnp.where(ie == e[j:j + 1, :], prefix, 0.0), axis=0, keepdims=True) for j in range(TOP_K)]
    rank_o[...] = jnp.concatenate(rows, axis=0).astype(I32)
    carry_ref[...] = carry_ref[...] + jnp.sum(onehot, axis=1, keepdims=True)
    cnt_o[...] = carry_ref[...].astype(I32)


def _rank(e_t, n_exp):
    n_tok = e_t.shape[1]
    tk = TOKEN_TILE
    return pl.pallas_call(
        _rank_kernel,
        grid=(n_tok // tk,),
        in_specs=[pl.BlockSpec((TOP_K, tk), lambda i: (0, i))],
        out_specs=[pl.BlockSpec((TOP_K, tk), lambda i: (0, i)),
                   pl.BlockSpec((n_exp, LANES), lambda i: (0, 0))],
        out_shape=[jax.ShapeDtypeStruct((TOP_K, n_tok), I32), jax.ShapeDtypeStruct((n_exp, LANES), I32)],
        scratch_shapes=[pltpu.VMEM((n_exp, LANES), F32)],
        compiler_params=_params(1),
        name="moe_rank",
    )(e_t)


def _slot_kernel(e_ref, rank_ref, pstart_ref, slot_o):
    e = e_ref[...]
    n_exp = pstart_ref.shape[0]
    tk = e.shape[1]
    ie = lax.broadcasted_iota(I32, (n_exp, tk), 0)
    ps = pstart_ref[:, 0:1]
    rows = [jnp.sum(jnp.where(ie == e[j:j + 1, :], ps, 0.0), axis=0, keepdims=True) for j in range(TOP_K)]
    slot_o[...] = rank_ref[...] + jnp.concatenate(rows, axis=0).astype(I32)


def _slots(e_t, rank_t, pstart_f):
    n_tok = e_t.shape[1]
    tk = TOKEN_TILE
    spec = pl.BlockSpec((TOP_K, tk), lambda i: (0, i))
    return pl.pallas_call(
        _slot_kernel,
        grid=(n_tok // tk,),
        in_specs=[spec, spec, _const_spec(pstart_f.shape)],
        out_specs=spec,
        out_shape=jax.ShapeDtypeStruct((TOP_K, n_tok), I32),
        compiler_params=_params(1),
        name="moe_slots",
    )(e_t, rank_t, pstart_f)


def _dispatch_kernel(pend_ref, padded_ref, slot_ref, h2_ref, xs_hbm, zbuf, sem, zsem, *, n_exp):
    i = pl.program_id(0)
    td = slot_ref.shape[1]
    blk_rows = MOE_BLOCK * SUBLANES

    def zero_copy(e):
        dst = pl.ds(pl.multiple_of((pend_ref[e] - MOE_BLOCK) * SUBLANES, SUBLANES), blk_rows)
        return pltpu.make_async_copy(zbuf, xs_hbm.at[dst], zsem)

    @pl.when(i == 0)
    def _():
        zbuf[...] = jnp.zeros_like(zbuf)

        def start(e, carry):
            @pl.when(padded_ref[e] > 0)
            def _():
                zero_copy(e).start()
            return carry

        def wait(e, carry):
            @pl.when(padded_ref[e] > 0)
            def _():
                zero_copy(e).wait()
            return carry

        lax.fori_loop(0, n_exp, start, 0)
        lax.fori_loop(0, n_exp, wait, 0)

    def row_copy(t, j):
        src = pl.ds(pl.multiple_of(t * SUBLANES, SUBLANES), SUBLANES)
        dst = pl.ds(pl.multiple_of(slot_ref[j, t] * SUBLANES, SUBLANES), SUBLANES)
        return pltpu.make_async_copy(h2_ref.at[src], xs_hbm.at[dst], sem)

    def start_tok(t, carry):
        for j in range(TOP_K):
            row_copy(t, j).start(priority=j % 2)
        return carry

    def wait_tok(t, carry):
        for j in range(TOP_K):
            row_copy(t, j).wait()
        return carry

    lax.fori_loop(0, td, start_tok, 0)
    lax.fori_loop(0, td, wait_tok, 0)


def _dispatch(pend, padded, slot_t, h2, m_pad):
    n_tok = slot_t.shape[1]
    td = TOKEN_TILE
    n_exp = pend.shape[0]
    grid_spec = pltpu.PrefetchScalarGridSpec(
        num_scalar_prefetch=2,
        grid=(n_tok // td,),
        in_specs=[pl.BlockSpec((TOP_K, td), lambda i, *_: (0, i), memory_space=pltpu.SMEM),
                  pl.BlockSpec((td * SUBLANES, LANES), lambda i, *_: (i, 0))],
        out_specs=pl.BlockSpec(memory_space=pl.ANY),
        scratch_shapes=[pltpu.VMEM((MOE_BLOCK * SUBLANES, LANES), F32),
                        pltpu.SemaphoreType.DMA(()), pltpu.SemaphoreType.DMA(())],
    )
    return pl.pallas_call(
        functools.partial(_dispatch_kernel, n_exp=n_exp),
        grid_spec=grid_spec,
        out_shape=jax.ShapeDtypeStruct((m_pad * SUBLANES, LANES), F32),
        compiler_params=_params(1),
        name="moe_dispatch",
    )(pend, padded, slot_t, h2)


def _tile_rows(ref, n):
    return jnp.concatenate([ref[pl.ds(k, n, stride=SUBLANES), :] for k in range(D_MODEL // LANES)], axis=1)


def _expert_kernel(be_ref, nused_ref, x_ref, wg_ref, wu_ref, wd_ref, y_ref, wgu_b, wd_b):
    i = pl.program_id(0)
    d_exp = wd_b.shape[0]

    @pl.when(i < nused_ref[0])
    def _():
        changed = (i == 0) | (be_ref[i] != be_ref[jnp.maximum(i - 1, 0)])

        @pl.when(changed)
        def _():
            wgu_b[:, :d_exp] = wg_ref[0].astype(BF16)
            wgu_b[:, d_exp:] = wu_ref[0].astype(BF16)
            wd_b[...] = wd_ref[0].astype(BF16)

        x = _tile_rows(x_ref, MOE_BLOCK).astype(BF16)
        gu = _dot(x, wgu_b[...])
        hh = jax.nn.silu(gu[:, :d_exp]) * gu[:, d_exp:]
        y = _dot(hh.astype(BF16), wd_b[...])
        for k in range(D_MODEL // LANES):
            y_ref[pl.ds(k, MOE_BLOCK, stride=SUBLANES), :] = y[:, k * LANES:(k + 1) * LANES]


def _experts(block_e, nused, xs, w_gate, w_up, w_down):
    n_blocks = block_e.shape[0]
    d_exp = w_gate.shape[2]
    rows = MOE_BLOCK * SUBLANES
    last = lambda i, nu: jnp.minimum(i, nu[0] - 1)
    grid_spec = pltpu.PrefetchScalarGridSpec(
        num_scalar_prefetch=2,
        grid=(n_blocks,),
        in_specs=[pl.BlockSpec((rows, LANES), lambda i, be, nu: (last(i, nu), 0)),
                  pl.BlockSpec((1, D_MODEL, d_exp), lambda i, be, nu: (be[i], 0, 0)),
                  pl.BlockSpec((1, D_MODEL, d_exp), lambda i, be, nu: (be[i], 0, 0)),
                  pl.BlockSpec((1, d_exp, D_MODEL), lambda i, be, nu: (be[i], 0, 0))],
        out_specs=pl.BlockSpec((rows, LANES), lambda i, be, nu: (last(i, nu), 0)),
        scratch_shapes=[pltpu.VMEM((D_MODEL, 2 * d_exp), BF16), pltpu.VMEM((d_exp, D_MODEL), BF16)],
    )
    return pl.pallas_call(
        _expert_kernel,
        grid_spec=grid_spec,
        out_shape=jax.ShapeDtypeStruct(xs.shape, F32),
        compiler_params=_params(1),
        name="moe_experts",
    )(block_e, nused, xs, w_gate, w_up, w_down)


def _combine_kernel(slot_ref, slot_next_ref, gate_ref, h2_ref, ys_hbm, shg, shu, shd, g2, b2, yp_o, ys_o,
                    buf, sem, *, n_prompt_tiles):
    i = pl.program_id(0)
    n = pl.num_programs(0)
    tc = gate_ref.shape[0]

    def start_tile(s_ref, bslot):
        def body(t, carry):
            dst = pl.ds(pl.multiple_of(t * SUBLANES, SUBLANES), SUBLANES)
            for j in range(TOP_K):
                src = pl.ds(pl.multiple_of(s_ref[j, t] * SUBLANES, SUBLANES), SUBLANES)
                pltpu.make_async_copy(ys_hbm.at[src], buf.at[bslot, j, dst], sem.at[bslot]).start(priority=j % 2)
            return carry
        lax.fori_loop(0, tc, body, 0)

    def wait_tile(bslot):
        for j in range(TOP_K):
            pltpu.make_async_copy(ys_hbm.at[pl.ds(0, tc * SUBLANES)], buf.at[bslot, j], sem.at[bslot]).wait()

    @pl.when(i == 0)
    def _():
        start_tile(slot_ref, 0)

    @pl.when(i + 1 < n)
    def _():
        start_tile(slot_next_ref, (i + 1) % 2)

    cur = i % 2
    h = _tile_rows(h2_ref, tc)
    hb = h.astype(BF16)
    shared = _dot((jax.nn.silu(_dot(hb, shg[...])) * _dot(hb, shu[...])).astype(BF16), shd[...])
    wait_tile(cur)
    routed = jnp.zeros((tc, D_MODEL), F32)
    for j in range(TOP_K):
        routed = routed + _tile_rows(buf.at[cur, j], tc) * gate_ref[:, j:j + 1]
    y = _layer_norm(ALPHA * h + (routed + shared), g2[...], b2[...])

    @pl.when(i < n_prompt_tiles)
    def _():
        yp_o[...] = y

    @pl.when(i >= n_prompt_tiles)
    def _():
        ys_o[...] = y


def _combine(slot_t, gates, h2, ys, shg, shu, shd, g2, b2, n_prompt, n_decode):
    tc = TOKEN_TILE
    npt, nst = n_prompt // tc, n_decode // tc
    consts = [shg, shu, shd, g2, b2]
    return pl.pallas_call(
        functools.partial(_combine_kernel, n_prompt_tiles=npt),
        grid=(npt + nst,),
        in_specs=[pl.BlockSpec((TOP_K, tc), lambda i: (0, i), memory_space=pltpu.SMEM),
                  pl.BlockSpec((TOP_K, tc), lambda i: (0, jnp.minimum(i + 1, npt + nst - 1)),
                               memory_space=pltpu.SMEM),
                  pl.BlockSpec((tc, TOP_K), lambda i: (i, 0)),
                  pl.BlockSpec((tc * SUBLANES, LANES), lambda i: (i, 0)),
                  pl.BlockSpec(memory_space=pl.ANY)] + [_const_spec(c.shape) for c in consts],
        out_specs=[pl.BlockSpec((tc, D_MODEL), lambda i: (jnp.minimum(i, npt - 1), 0)),
                   pl.BlockSpec((tc, D_MODEL), lambda i: (jnp.maximum(i - npt, 0), 0))],
        out_shape=[jax.ShapeDtypeStruct((n_prompt, D_MODEL), F32),
                   jax.ShapeDtypeStruct((n_decode, D_MODEL), F32)],
        scratch_shapes=[pltpu.VMEM((2, TOP_K, tc * SUBLANES, LANES), F32), pltpu.SemaphoreType.DMA((2,))],
        compiler_params=_params(1),
        name="moe_combine",
    )(slot_t, slot_t, gates, h2, ys, *consts)


def _pad_rows(a, n):
    return jnp.pad(a, ((0, n - a.shape[0]),) + ((0, 0),) * (a.ndim - 1))


def kernel(x_prompt, x_sample, cache_k, cache_v, cache_kidx, page_table, state_conv, w_in, conv_w, w_o_attn,
           w_o_conv, w_o, ln1_g, ln1_b, router_w, router_bias, moe_w_gate, moe_w_up, moe_w_down,
           shared_w_gate, shared_w_up, shared_w_down, ln2_g, ln2_b):
    nb, seq, _ = x_prompt.shape
    db = x_sample.shape[0]
    n_pool = cache_k.shape[1]
    n_exp = router_w.shape[-1]
    tp = nb * seq
    ts = TOKEN_TILE
    nkv = N_KV_HEADS * HEAD_DIM

    weights = _prep_in_weights(w_in[0])
    cw = conv_w[0]

    xp2 = x_prompt.reshape(tp, D_MODEL)
    (q_p, k_p, v_p, kvb_p, qi_p, idx_p, kib_p, conv_p, ga_p, gc_p, ulast_p) = _in_proj_prompt(xp2, weights, cw, nb)
    xs2 = _pad_rows(x_sample.reshape(db, D_MODEL), ts)
    prev2 = _pad_rows(state_conv[0, :, 0, :], ts)
    prev1 = _pad_rows(state_conv[0, :, 1, :], ts)
    (q_s, k_s, v_s, _, qi_s, idx_s, _, conv_s, ga_s, gc_s, u_s) = _in_proj_decode(xs2, prev2, prev1, weights, cw)

    attn_p = _attn_prompt(q_p, kvb_p, qi_p, kib_p, idx_p, nb)
    q8 = q_s[:db].reshape(db, N_HEADS, HEAD_DIM)
    qi8 = qi_s[:db].reshape(db, IDX_HEADS, LANES)[:, :, :IDX_DIM]
    wi8 = idx_s[:db, IDX_DIM:IDX_DIM + IDX_HEADS].reshape(db, IDX_HEADS, 1)
    kin = idx_s[:db, :IDX_DIM].astype(BF16).reshape(db, 1, IDX_DIM)
    kn8 = jnp.repeat(k_s[:db].reshape(db, N_KV_HEADS, HEAD_DIM), GROUP, axis=1).astype(BF16)
    vn8 = jnp.repeat(v_s[:db].reshape(db, N_KV_HEADS, HEAD_DIM), GROUP, axis=1).astype(BF16)
    ck2 = cache_k[0].reshape(n_pool * PAGE_SIZE * N_KV_HEADS, HEAD_DIM)
    cv2 = cache_v[0].reshape(n_pool * PAGE_SIZE * N_KV_HEADS, HEAD_DIM)
    meta = _decode_select(page_table, qi8, wi8, kin, jnp.swapaxes(cache_kidx[0], 1, 2))
    attn_s8 = _decode_attend(page_table, meta[:, 0, :], q8, meta, kn8, vn8, ck2, cv2)
    attn_s = _pad_rows(attn_s8.reshape(db, N_HEADS * HEAD_DIM), ts)

    h2, e_t, gate_t = _post_attn(
        xp2, xs2, attn_p, attn_s, conv_p, conv_s, ga_p, ga_s, gc_p, gc_s,
        w_o_attn[0].astype(BF16), w_o_conv[0].astype(BF16), w_o[0].astype(BF16),
        ln1_g[0].reshape(1, D_MODEL), ln1_b[0].reshape(1, D_MODEL),
        router_w[0].T, router_bias[0].reshape(n_exp, 1))

    n_tok = tp + ts
    rank_t, cnt = _rank(e_t, n_exp)
    counts = cnt[:, 0]
    padded = (counts + MOE_BLOCK - 1) // MOE_BLOCK * MOE_BLOCK
    pend = jnp.cumsum(padded)
    pstart = pend - padded
    n_blocks = (n_tok * TOP_K + n_exp * (MOE_BLOCK - 1) + MOE_BLOCK - 1) // MOE_BLOCK
    nused = (pend[-1] // MOE_BLOCK).astype(I32)
    blk = jnp.minimum(jnp.arange(n_blocks, dtype=I32), nused - 1)
    block_e = jnp.minimum(jnp.sum((pend[None, :] <= (blk * MOE_BLOCK)[:, None]).astype(I32), axis=1), n_exp - 1)
    slot_t = _slots(e_t, rank_t, jnp.broadcast_to(pstart.astype(F32)[:, None], (n_exp, LANES)))
    xs = _dispatch(pend.astype(I32), padded.astype(I32), slot_t, h2, n_blocks * MOE_BLOCK)
    ys = _experts(block_e, nused.reshape(1), xs, moe_w_gate[0], moe_w_up[0], moe_w_down[0])
    y_p, y_s = _combine(slot_t, gate_t.T, h2, ys,
                        shared_w_gate[0].astype(BF16), shared_w_up[0].astype(BF16),
                        shared_w_down[0].astype(BF16),
                        ln2_g[0].reshape(1, D_MODEL), ln2_b[0].reshape(1, D_MODEL), tp, ts)

    conv_sample = jnp.stack([state_conv[0, :, 1, :], u_s[:db]], axis=1)[None]
    return (y_p.reshape(nb, seq, D_MODEL),
            y_s[:db].reshape(db, 1, D_MODEL),
            k_p.reshape(1, nb, seq, N_KV_HEADS, HEAD_DIM),
            v_p.reshape(1, nb, seq, N_KV_HEADS, HEAD_DIM),
            idx_p[:, :IDX_DIM].reshape(1, nb, seq, IDX_DIM),
            ulast_p[:, SUBLANES - (CONV_W - 1):, :][None],
            k_s[:db].reshape(1, db, 1, N_KV_HEADS, HEAD_DIM),
            v_s[:db].reshape(1, db, 1, N_KV_HEADS, HEAD_DIM),
            idx_s[:db, :IDX_DIM].reshape(1, db, 1, IDX_DIM),
            conv_sample)
```

```python
import functools

import jax
import jax.numpy as jnp
import numpy as np
from jax import lax
from jax.experimental import pallas as pl
from jax.experimental.pallas import tpu as pltpu

F32 = jnp.float32
BF16 = jnp.bfloat16
I32 = jnp.int32

D_MODEL = 1024
N_HEADS = 8
HEAD_DIM = 128
N_KV_HEADS = 2
GROUP = N_HEADS // N_KV_HEADS
IDX_HEADS = 8
IDX_DIM = 64
TOPK_KEYS = 256
IDX_SCALE = IDX_DIM ** -0.5 * IDX_HEADS ** -0.5
Q_SCALE = HEAD_DIM ** -0.5
PAGE_SIZE = 128
D_CONV = D_MODEL
CONV_W = 3
TOP_K = 8
N_GROUPS = 8
TOPK_GROUPS = 4
ROUTED_SCALE = 2.5
DEPTH = 1
ALPHA = (2 * DEPTH) ** 0.25
LN_EPS = 1e-5
NEG = -1e30
IN_WIDTHS = (N_HEADS * HEAD_DIM, N_KV_HEADS * HEAD_DIM, N_KV_HEADS * HEAD_DIM,
             IDX_HEADS * IDX_DIM, IDX_DIM, IDX_HEADS,
             D_CONV, D_CONV, D_CONV, D_MODEL, D_MODEL)

LANES = 128
SUBLANES = 8
VMEM_LIMIT_BYTES = 56 * 1024 * 1024

PROJ_ROWS = 512
ATT_Q = 256
SEARCH_ROWS = 128
TOKEN_TILE = 256
MOE_BLOCK = 256
CONV_CHUNK = 256
INT_MIN = -2 ** 31


def _np_key(v):
    b = int(np.float32(v).view(np.int32))
    return b ^ ((b >> 31) & 0x7FFFFFFF)


KEY_HALF = _np_key(NEG * 0.5)


def _sort_key(s):
    b = lax.bitcast_convert_type(s + 0.0, I32)
    return b ^ ((b >> 31) & jnp.int32(0x7FFFFFFF))


def _dot(a, b):
    return jnp.dot(a, b, preferred_element_type=F32)


def _dot_nt(a, b):
    return lax.dot_general(a, b, (((1,), (1,)), ((), ())), preferred_element_type=F32)


def _params(n_grid):
    return pltpu.CompilerParams(dimension_semantics=("arbitrary",) * n_grid,
                                vmem_limit_bytes=VMEM_LIMIT_BYTES)


def _const_spec(shape):
    nd = len(shape)
    return pl.BlockSpec(shape, lambda *_: (0,) * nd, pipeline_mode=pl.Buffered(1))


def _proj_common(xb, w1, w2, w3, wga, wgc, q_o, k_o, v_o, kv_o, qi_o, idx_o, kib_o, sga_o, sgc_o):
    nq = N_HEADS * HEAD_DIM
    nkv = N_KV_HEADS * HEAD_DIM
    z1 = _dot(xb, w1[...])
    q_o[...] = (z1[:, :nq] * Q_SCALE).astype(BF16)
    k = z1[:, nq:nq + nkv]
    v = z1[:, nq + nkv:nq + 2 * nkv]
    k_o[...] = k
    v_o[...] = v
    kv_o[:, :nkv] = k.astype(BF16)
    kv_o[:, nkv:] = v.astype(BF16)
    qi_o[...] = _dot(xb, w2[...]).astype(BF16)
    z3 = _dot(xb, w3[...])
    idx_o[...] = z3
    kib_o[...] = z3.astype(BF16)
    sga_o[...] = jax.nn.sigmoid(_dot(xb, wga[...])).astype(BF16)
    sgc_o[...] = jax.nn.sigmoid(_dot(xb, wgc[...])).astype(BF16)


def _in_proj_prompt_kernel(x_ref, w1, w2, w3, wb, wc, wx, wga, wgc, cw_ref,
                           q_o, k_o, v_o, kv_o, qi_o, idx_o, kib_o, conv_o, sga_o, sgc_o, ulast_o,
                           tail_ref):
    j = pl.program_id(1)

    @pl.when(j == 0)
    def _():
        tail_ref[...] = jnp.zeros_like(tail_ref)

    xb = x_ref[...].astype(BF16)
    _proj_common(xb, w1, w2, w3, wga, wgc, q_o, k_o, v_o, kv_o, qi_o, idx_o, kib_o, sga_o, sgc_o)
    tm = xb.shape[0]
    row = lax.broadcasted_iota(I32, (tm, CONV_CHUNK), 0)
    for c in range(D_CONV // CONV_CHUNK):
        cs = slice(c * CONV_CHUNK, (c + 1) * CONV_CHUNK)
        gb = _dot(xb, wb[:, cs])
        u = _dot(xb, wc[:, cs]) * _dot(xb, wx[:, cs])
        p1 = tail_ref[SUBLANES - 1:SUBLANES, cs]
        p2 = tail_ref[SUBLANES - 2:SUBLANES - 1, cs]
        u1 = jnp.where(row == 0, p1, pltpu.roll(u, 1, 0))
        u2 = jnp.where(row == 0, p2, jnp.where(row == 1, p1, pltpu.roll(u, 2, 0)))
        conv = cw_ref[0:1, cs] * u2 + cw_ref[1:2, cs] * u1 + cw_ref[2:3, cs] * u
        conv_o[:, cs] = (gb * conv).astype(BF16)
        tail_ref[:, cs] = u[tm - SUBLANES:, :]
    ulast_o[0] = tail_ref[...]


def _in_proj_decode_kernel(x_ref, p2_ref, p1_ref, w1, w2, w3, wb, wc, wx, wga, wgc, cw_ref,
                           q_o, k_o, v_o, kv_o, qi_o, idx_o, kib_o, conv_o, sga_o, sgc_o, u_o):
    xb = x_ref[...].astype(BF16)
    _proj_common(xb, w1, w2, w3, wga, wgc, q_o, k_o, v_o, kv_o, qi_o, idx_o, kib_o, sga_o, sgc_o)
    for c in range(D_CONV // CONV_CHUNK):
        cs = slice(c * CONV_CHUNK, (c + 1) * CONV_CHUNK)
        gb = _dot(xb, wb[:, cs])
        u = _dot(xb, wc[:, cs]) * _dot(xb, wx[:, cs])
        conv = cw_ref[0:1, cs] * p2_ref[:, cs] + cw_ref[1:2, cs] * p1_ref[:, cs] + cw_ref[2:3, cs] * u
        conv_o[:, cs] = (gb * conv).astype(BF16)
        u_o[:, cs] = u


def _prep_in_weights(w_in):
    w = w_in.astype(BF16)
    offs = np.concatenate([[0], np.cumsum(IN_WIDTHS)])
    part = [w[:, int(offs[i]):int(offs[i + 1])] for i in range(len(IN_WIDTHS))]
    q, k, v, qi, ki, wi, gb, gc, xv, ga, gcv = part
    d = w.shape[0]
    w1 = jnp.concatenate([q, k, v], axis=1)
    w2 = jnp.pad(qi.reshape(d, IDX_HEADS, IDX_DIM),
                 ((0, 0), (0, 0), (0, LANES - IDX_DIM))).reshape(d, IDX_HEADS * LANES)
    w3 = jnp.concatenate([ki, wi, jnp.zeros((d, LANES - IDX_DIM - IDX_HEADS), BF16)], axis=1)
    return (w1, w2, w3, gb, gc, xv, ga, gcv)


def _proj_out_shapes(t):
    nkv = N_KV_HEADS * HEAD_DIM
    sd = jax.ShapeDtypeStruct
    return [sd((t, N_HEADS * HEAD_DIM), BF16), sd((t, nkv), F32), sd((t, nkv), F32), sd((t, 2 * nkv), BF16),
            sd((t, IDX_HEADS * LANES), BF16), sd((t, LANES), F32), sd((t, LANES), BF16),
            sd((t, D_CONV), BF16), sd((t, D_MODEL), BF16), sd((t, D_MODEL), BF16)]


def _in_proj_prompt(x2d, weights, conv_w, nb):
    t = x2d.shape[0]
    s = t // nb
    tm = min(PROJ_ROWS, s)
    nj = s // tm
    row_spec = lambda w: pl.BlockSpec((tm, w), lambda b, j: (b * nj + j, 0))
    out_shapes = _proj_out_shapes(t) + [jax.ShapeDtypeStruct((nb, SUBLANES, D_CONV), F32)]
    out_specs = [row_spec(o.shape[1]) for o in out_shapes[:-1]]
    out_specs.append(pl.BlockSpec((1, SUBLANES, D_CONV), lambda b, j: (b, 0, 0)))
    return pl.pallas_call(
        _in_proj_prompt_kernel,
        grid=(nb, nj),
        in_specs=[row_spec(D_MODEL)] + [_const_spec(w.shape) for w in weights] + [_const_spec(conv_w.shape)],
        out_specs=out_specs,
        out_shape=out_shapes,
        scratch_shapes=[pltpu.VMEM((SUBLANES, D_CONV), F32)],
        compiler_params=_params(2),
        name="in_proj_prompt",
    )(x2d, *weights, conv_w)


def _in_proj_decode(x2d, p2, p1, weights, conv_w):
    t = x2d.shape[0]
    row_spec = lambda w: pl.BlockSpec((t, w), lambda i: (0, 0))
    out_shapes = _proj_out_shapes(t) + [jax.ShapeDtypeStruct((t, D_CONV), F32)]
    return pl.pallas_call(
        _in_proj_decode_kernel,
        grid=(1,),
        in_specs=[row_spec(D_MODEL)] * 3 + [_const_spec(w.shape) for w in weights] + [_const_spec(conv_w.shape)],
        out_specs=[row_spec(o.shape[1]) for o in out_shapes],
        out_shape=out_shapes,
        compiler_params=_params(1),
        name="in_proj_decode",
    )(x2d, p2, p1, *weights, conv_w)


def _attn_prompt_kernel(q_ref, kv_ref, qi_ref, kib_ref, idx_ref, o_ref,
                        key_ref, bias_ref, t_ref, j_ref, need_ref, m_ref, acc_ref,
                        *, n_keep, seq_len):
    tq = q_ref.shape[0]
    i = pl.program_id(1)
    nchunk = i + 1
    nsub = tq // LANES
    lane = lax.broadcasted_iota(I32, (tq, LANES), 1)

    wi_cols = [idx_ref[:, IDX_DIM + h:IDX_DIM + h + 1] * IDX_SCALE for h in range(IDX_HEADS)]
    row_i = lax.broadcasted_iota(I32, (tq, tq), 0)
    col_i = lax.broadcasted_iota(I32, (tq, tq), 1)

    def score_chunk(c, carry):
        kic = kib_ref[pl.ds(pl.multiple_of(c * tq, tq), tq), :]
        acc = jnp.zeros((tq, tq), F32)
        for h in range(IDX_HEADS):
            d = _dot_nt(qi_ref[:, h * LANES:(h + 1) * LANES], kic)
            acc = acc + jnp.maximum(d, 0.0) * wi_cols[h]
        s = jnp.where((c < i) | (col_i <= row_i), acc, NEG)
        key_ref[c] = _sort_key(s)
        return carry

    lax.fori_loop(0, nchunk, score_chunk, 0)

    rb = SEARCH_ROWS

    def count(r0, pred):
        def body(c, cnt):
            for k in range(nsub):
                blk = key_ref[c, r0:r0 + rb, k * LANES:(k + 1) * LANES]
                cnt = cnt + pred(blk, c * tq + k * LANES).astype(I32)
            return cnt
        cnt = lax.fori_loop(0, nchunk, body, jnp.zeros((rb, LANES), I32))
        return jnp.sum(cnt, axis=1, keepdims=True)

    any_excess = jnp.int32(0)
    for r in range(tq // rb):
        r0 = r * rb

        def bit_body(bi, t):
            cand = t + lax.shift_left(jnp.int32(1), 31 - bi)
            cand_b = jnp.broadcast_to(cand, (rb, LANES))
            cnt = count(r0, lambda blk, _: blk >= cand_b)
            return jnp.where(cnt >= n_keep, cand, t)

        t = lax.fori_loop(0, 32, bit_body, jnp.full((rb, 1), INT_MIN, I32))
        t_b = jnp.broadcast_to(t, (rb, LANES))
        cnt_gt = count(r0, lambda blk, _: blk > t_b)
        cnt_eq = count(r0, lambda blk, _: blk == t_b)
        need = n_keep - cnt_gt
        excess = (cnt_eq > need) & (t > KEY_HALF)
        any_excess = any_excess + jnp.max(excess.astype(I32))
        t_ref[r0:r0 + rb, :] = t_b
        need_ref[r0:r0 + rb, :] = jnp.broadcast_to(need, (rb, LANES))

    j_ref[...] = jnp.full(j_ref.shape, seq_len, I32)

    @pl.when(any_excess > 0)
    def _():
        nbits = max(1, (seq_len - 1).bit_length())
        lane_rb = lax.broadcasted_iota(I32, (rb, LANES), 1)
        for r in range(tq // rb):
            r0 = r * rb
            t_b = t_ref[r0:r0 + rb, :]
            need = need_ref[r0:r0 + rb, 0:1]

            def jbit(bi, jj):
                cand = jj | lax.shift_left(jnp.int32(1), nbits - 1 - bi)
                cand_b = jnp.broadcast_to(cand, (rb, LANES))
                g = count(r0, lambda blk, base: (blk == t_b) & ((base + lane_rb) < cand_b))
                return jnp.where(g < need, cand, jj)

            jj = lax.fori_loop(0, nbits, jbit, jnp.zeros((rb, 1), I32))
            j_ref[r0:r0 + rb, :] = jnp.broadcast_to(jj, (rb, LANES))

    t_all = t_ref[...]
    j_all = j_ref[...]

    def bias_chunk(c, carry):
        for k in range(nsub):
            blk = key_ref[c, :, k * LANES:(k + 1) * LANES]
            col = c * tq + k * LANES + lane
            sel = (blk > KEY_HALF) & ((blk > t_all) | ((blk == t_all) & (col <= j_all)))
            bias_ref[c, :, k * LANES:(k + 1) * LANES] = jnp.where(sel, 0.0, NEG)
        return carry

    lax.fori_loop(0, nchunk, bias_chunk, 0)

    nkv = N_KV_HEADS * HEAD_DIM
    gq = GROUP * tq

    def group_logits(c, g):
        rows = pl.ds(pl.multiple_of(c * tq, tq), tq)
        kc = kv_ref[rows, g * HEAD_DIM:(g + 1) * HEAD_DIM]
        qg = jnp.concatenate([q_ref[:, h * HEAD_DIM:(h + 1) * HEAD_DIM]
                              for h in range(g * GROUP, (g + 1) * GROUP)], axis=0)
        s = _dot_nt(qg, kc).reshape(GROUP, tq, tq) + bias_ref[c][None]
        return s.reshape(gq, tq)

    m_ref[...] = jnp.full(m_ref.shape, NEG, F32)

    def max_chunk(c, carry):
        for g in range(N_KV_HEADS):
            s = group_logits(c, g)
            mx = s[:, :LANES]
            for k in range(1, nsub):
                mx = jnp.maximum(mx, s[:, k * LANES:(k + 1) * LANES])
            m_ref[g] = jnp.maximum(m_ref[g], mx)
        return carry

    lax.fori_loop(0, nchunk, max_chunk, 0)
    for g in range(N_KV_HEADS):
        m_ref[g] = jnp.broadcast_to(jnp.max(m_ref[g], axis=1, keepdims=True), (gq, LANES))

    acc_ref[...] = jnp.zeros_like(acc_ref)
    ones = jnp.ones((tq, HEAD_DIM), BF16)

    def pv_chunk(c, carry):
        rows = pl.ds(pl.multiple_of(c * tq, tq), tq)
        for g in range(N_KV_HEADS):
            s = group_logits(c, g)
            m = m_ref[g]
            p = jnp.concatenate([jnp.exp(s[:, k * LANES:(k + 1) * LANES] - m) for k in range(nsub)], axis=1)
            vc = kv_ref[rows, nkv + g * HEAD_DIM:nkv + (g + 1) * HEAD_DIM]
            acc_ref[g] = acc_ref[g] + _dot(p.astype(BF16), jnp.concatenate([vc, ones], axis=1))
        return carry

    lax.fori_loop(0, nchunk, pv_chunk, 0)
    for h in range(N_HEADS):
        a = acc_ref[h // GROUP, (h % GROUP) * tq:(h % GROUP + 1) * tq, :]
        o_ref[:, h * HEAD_DIM:(h + 1) * HEAD_DIM] = (a[:, :HEAD_DIM] / a[:, HEAD_DIM:]).astype(BF16)


def _attn_prompt(q, kvb, qi, kib, idx, nb):
    t = q.shape[0]
    s = t // nb
    tq = min(ATT_Q, s)
    nq = s // tq
    n_keep = min(TOPK_KEYS, s // 4)
    blk = lambda w: pl.BlockSpec((tq, w), lambda b, i: (b * nq + i, 0))
    seq = lambda w: pl.BlockSpec((s, w), lambda b, i: (b, 0))
    return pl.pallas_call(
        functools.partial(_attn_prompt_kernel, n_keep=n_keep, seq_len=s),
        grid=(nb, nq),
        in_specs=[blk(q.shape[1]), seq(kvb.shape[1]), blk(qi.shape[1]), seq(kib.shape[1]), blk(idx.shape[1])],
        out_specs=blk(N_HEADS * HEAD_DIM),
        out_shape=jax.ShapeDtypeStruct((t, N_HEADS * HEAD_DIM), BF16),
        scratch_shapes=[pltpu.VMEM((nq, tq, tq), I32), pltpu.VMEM((nq, tq, tq), F32),
                        pltpu.VMEM((tq, LANES), I32), pltpu.VMEM((tq, LANES), I32), pltpu.VMEM((tq, LANES), I32),
                        pltpu.VMEM((N_KV_HEADS, GROUP * tq, LANES), F32),
                        pltpu.VMEM((N_KV_HEADS, GROUP * tq, 2 * HEAD_DIM), F32)],
        compiler_params=_params(2),
        name="attn_prompt",
    )(q, kvb, qi, kib, idx)


DECODE_PAGES_PER_ROW = 4


def _decode_select_kernel(pt_ref, qi_ref, wi_ref, kin_ref, cx_hbm, meta_o, xbuf, key_ref, sem,
                          *, n_pages, n_keep):
    b = pl.program_id(0)
    nb = pl.num_programs(0)
    ppr = DECODE_PAGES_PER_ROW
    w = ppr * PAGE_SIZE
    nrow = n_pages // ppr
    past = n_pages * PAGE_SIZE

    def page_copy(bb, p, slot):
        return pltpu.make_async_copy(cx_hbm.at[pt_ref[bb, p]], xbuf.at[slot, p], sem.at[slot])

    def start_batch(bb, slot):
        def body(p, carry):
            page_copy(bb, p, slot).start()
            return carry
        lax.fori_loop(0, n_pages, body, 0)

    def wait_batch(bb, slot):
        def body(p, carry):
            page_copy(bb, p, slot).wait()
            return carry
        lax.fori_loop(0, n_pages, body, 0)

    @pl.when(b == 0)
    def _():
        start_batch(0, 0)

    @pl.when(b + 1 < nb)
    def _():
        start_batch(b + 1, (b + 1) % 2)

    slot = b % 2
    wait_batch(b, slot)

    qi8 = qi_ref[0]
    wi8 = wi_ref[0] * IDX_SCALE

    def score_row(r, carry):
        kx = jnp.concatenate([xbuf[slot, r * ppr + k] for k in range(ppr)], axis=1).astype(BF16)
        d = _dot(qi8, kx)
        sc = jnp.sum(jnp.maximum(d, 0.0) * wi8, axis=0, keepdims=True)
        key_ref[pl.ds(r, 1), :] = _sort_key(sc)
        return carry

    lax.fori_loop(0, nrow, score_row, 0, unroll=4)
    dn = jnp.sum(qi8.astype(F32) * kin_ref[0].astype(F32), axis=1, keepdims=True)
    key_new = _sort_key(jnp.sum(jnp.maximum(dn, 0.0) * wi8, axis=0, keepdims=True))

    keys = key_ref[...]
    col = (lax.broadcasted_iota(I32, keys.shape, 0) * w + lax.broadcasted_iota(I32, keys.shape, 1))

    def total(x):
        return jnp.sum(jnp.sum(x.astype(I32), axis=1, keepdims=True), axis=0, keepdims=True)

    def bit_body(bi, t):
        cand = t + lax.shift_left(jnp.int32(1), 31 - bi)
        cnt = total(keys >= cand) + (key_new >= cand).astype(I32)
        return jnp.where(cnt >= n_keep, cand, t)

    t = lax.fori_loop(0, 32, bit_body, jnp.full((1, 1), INT_MIN, I32))
    need = n_keep - (total(keys > t) + (key_new > t).astype(I32))
    nbits = past.bit_length()

    def jbit(bi, jj):
        cand = jj | lax.shift_left(jnp.int32(1), nbits - 1 - bi)
        g = total((keys == t) & (col < cand)) + ((key_new == t) & (past < cand)).astype(I32)
        return jnp.where(g < need, cand, jj)

    jj = lax.fori_loop(0, nbits, jbit, jnp.zeros((1, 1), I32))
    sel = (keys > t) | ((keys == t) & (col <= jj))
    sel_new = (key_new > t) | ((key_new == t) & (past <= jj))

    incl_lane = (lax.broadcasted_iota(I32, (w, w), 0) <= lax.broadcasted_iota(I32, (w, w), 1)).astype(BF16)
    cnt_in_row = _dot(sel.astype(BF16), incl_lane)
    row_tot = cnt_in_row[:, w - 1:w]
    rows_before = (lax.broadcasted_iota(I32, (nrow, nrow), 1) < lax.broadcasted_iota(I32, (nrow, nrow), 0))
    row_off = _dot(rows_before.astype(BF16), jnp.broadcast_to(row_tot, (nrow, LANES)).astype(BF16))[:, 0:1]
    jl = lax.broadcasted_iota(I32, (1, n_keep), 1).astype(F32)
    row_j = jnp.sum(((row_off + row_tot) <= jl).astype(F32), axis=0, keepdims=True)
    onehot = (lax.broadcasted_iota(I32, (nrow, n_keep), 0).astype(F32) == row_j)
    cnt_j = lax.dot_general(cnt_in_row.astype(BF16), onehot.astype(BF16), (((0,), (0,)), ((), ())),
                            preferred_element_type=F32)
    off_j = jnp.sum(jnp.where(onehot, row_off, 0.0), axis=0, keepdims=True)
    lane_j = jnp.sum((cnt_j <= (jl - off_j)).astype(F32), axis=0, keepdims=True)
    pos = jnp.minimum((row_j * w + lane_j).astype(I32), past - 1)
    is_pos_row = lax.broadcasted_iota(I32, (SUBLANES, n_keep), 0) == 0
    meta_o[0] = jnp.where(is_pos_row, pos, sel_new.astype(I32))


def _decode_select(page_table, qi8, wi8, kin, cxt):
    db, n_pages = page_table.shape
    past = n_pages * PAGE_SIZE
    n_keep = min(TOPK_KEYS, (past + 1) // 4)
    per_b = lambda a: pl.BlockSpec((1,) + a.shape[1:], lambda b, pt: (b, 0, 0))
    grid_spec = pltpu.PrefetchScalarGridSpec(
        num_scalar_prefetch=1,
        grid=(db,),
        in_specs=[per_b(qi8), per_b(wi8), per_b(kin), pl.BlockSpec(memory_space=pl.ANY)],
        out_specs=pl.BlockSpec((1, SUBLANES, n_keep), lambda b, pt: (b, 0, 0)),
        scratch_shapes=[pltpu.VMEM((2, n_pages, IDX_DIM, PAGE_SIZE), F32),
                        pltpu.VMEM((n_pages // DECODE_PAGES_PER_ROW, DECODE_PAGES_PER_ROW * PAGE_SIZE), I32),
                        pltpu.SemaphoreType.DMA((2,))],
    )
    return pl.pallas_call(
        functools.partial(_decode_select_kernel, n_pages=n_pages, n_keep=n_keep),
        grid_spec=grid_spec,
        out_shape=jax.ShapeDtypeStruct((db, SUBLANES, n_keep), I32),
        compiler_params=_params(1),
        name="decode_select",
    )(page_table, qi8, wi8, kin, cxt)


def _decode_attend_kernel(pt_ref, pos_ref, q_ref, meta_ref, kn_ref, vn_ref, ck_hbm, cv_hbm, o_ref,
                          kbuf, vbuf, expand_ref, sem, *, n_keep):
    b = pl.program_id(0)
    nb = pl.num_programs(0)
    pos_per_tile = SUBLANES // N_KV_HEADS
    ncol = n_keep * SUBLANES

    def item_copies(bb, j, slot):
        pos = pos_ref[bb, j]
        phys = pt_ref[bb, pos // PAGE_SIZE]
        tile = (phys * PAGE_SIZE + pos % PAGE_SIZE) // pos_per_tile
        src = pl.ds(pl.multiple_of(tile * SUBLANES, SUBLANES), SUBLANES)
        dst = pl.ds(pl.multiple_of(j * SUBLANES, SUBLANES), SUBLANES)
        return (pltpu.make_async_copy(ck_hbm.at[src], kbuf.at[slot, dst], sem.at[0, slot]),
                pltpu.make_async_copy(cv_hbm.at[src], vbuf.at[slot, dst], sem.at[1, slot]))

    def start_batch(bb, slot):
        def body(j, carry):
            for prio, cp in enumerate(item_copies(bb, j, slot)):
                cp.start(priority=prio)
            return carry
        lax.fori_loop(0, n_keep, body, 0)

    def wait_batch(slot):
        pltpu.make_async_copy(ck_hbm.at[pl.ds(0, ncol)], kbuf.at[slot], sem.at[0, slot]).wait()
        pltpu.make_async_copy(cv_hbm.at[pl.ds(0, ncol)], vbuf.at[slot], sem.at[1, slot]).wait()

    @pl.when(b == 0)
    def _():
        start_batch(0, 0)
        item_of_col = lax.broadcasted_iota(I32, (n_keep, ncol), 1) // SUBLANES
        expand_ref[...] = (item_of_col == lax.broadcasted_iota(I32, (n_keep, ncol), 0)).astype(BF16)

    @pl.when(b + 1 < nb)
    def _():
        start_batch(b + 1, (b + 1) % 2)

    slot = b % 2
    wait_batch(slot)

    q8 = q_ref[0]
    meta = meta_ref[0]
    new_kept = meta[1:2, 0:1] > 0
    sub = (meta % pos_per_tile).astype(BF16)
    sub_col = _dot(sub, expand_ref[...])[0:1, :]
    head_i = lax.broadcasted_iota(I32, (N_HEADS, ncol), 0)
    col_i = lax.broadcasted_iota(I32, (N_HEADS, ncol), 1)
    want = sub_col * N_KV_HEADS + (head_i // GROUP).astype(F32)
    is_new_item = new_kept & (col_i // SUBLANES == n_keep - 1)
    ok = ((col_i % SUBLANES).astype(F32) == want) & jnp.logical_not(is_new_item)

    s = jnp.where(ok, _dot_nt(q8, kbuf[slot].astype(BF16)), NEG)
    s_new = jnp.sum(q8.astype(F32) * kn_ref[0].astype(F32), axis=1, keepdims=True)
    s_new = jnp.where(new_kept, s_new, NEG)
    m = jnp.maximum(jnp.max(s, axis=1, keepdims=True), s_new)
    p = jnp.exp(s - m)
    p_new = jnp.exp(s_new - m)
    l = jnp.sum(p, axis=1, keepdims=True) + p_new
    acc = _dot(p.astype(BF16), vbuf[slot].astype(BF16)) + p_new.astype(BF16).astype(F32) * vn_ref[0].astype(F32)
    o_ref[0] = (acc / l).astype(BF16)


def _decode_attend(page_table, pos, q8, meta, kn8, vn8, ck2, cv2):
    db, n_keep = pos.shape
    per_b = lambda a: pl.BlockSpec((1,) + a.shape[1:], lambda b, pt, ps: (b, 0, 0))
    any_spec = pl.BlockSpec(memory_space=pl.ANY)
    ncol = n_keep * SUBLANES
    grid_spec = pltpu.PrefetchScalarGridSpec(
        num_scalar_prefetch=2,
        grid=(db,),
        in_specs=[per_b(q8), per_b(meta), per_b(kn8), per_b(vn8), any_spec, any_spec],
        out_specs=pl.BlockSpec((1, N_HEADS, HEAD_DIM), lambda b, pt, ps: (b, 0, 0)),
        scratch_shapes=[pltpu.VMEM((2, ncol, HEAD_DIM), F32), pltpu.VMEM((2, ncol, HEAD_DIM), F32),
                        pltpu.VMEM((n_keep, ncol), BF16), pltpu.SemaphoreType.DMA((2, 2))],
    )
    return pl.pallas_call(
        functools.partial(_decode_attend_kernel, n_keep=n_keep),
        grid_spec=grid_spec,
        out_shape=jax.ShapeDtypeStruct((db, N_HEADS, HEAD_DIM), BF16),
        compiler_params=_params(1),
        name="decode_attend",
    )(page_table, pos, q8, meta, kn8, vn8, ck2, cv2)


def _layer_norm(r, g, b):
    mu = jnp.mean(r, axis=-1, keepdims=True)
    d = r - mu
    var = jnp.mean(d * d, axis=-1, keepdims=True)
    return d * lax.rsqrt(var + LN_EPS) * g + b


def _route(logits_t, rbias):
    n_exp, tm = logits_t.shape
    epg = n_exp // N_GROUPS
    s = jax.nn.sigmoid(logits_t)
    sb = s + rbias
    ie = lax.broadcasted_iota(I32, (epg, tm), 0)
    gs_rows = []
    for g in range(N_GROUPS):
        blk = sb[g * epg:(g + 1) * epg, :]
        m1 = jnp.max(blk, axis=0, keepdims=True)
        i1 = jnp.min(jnp.where(blk == m1, ie, epg), axis=0, keepdims=True)
        m2 = jnp.max(jnp.where(ie == i1, -jnp.inf, blk), axis=0, keepdims=True)
        gs_rows.append(m1 + m2)
    picked = [jnp.zeros((1, tm), jnp.bool_) for _ in range(N_GROUPS)]
    cur = list(gs_rows)
    for _ in range(TOPK_GROUPS):
        mx = cur[0]
        for g in range(1, N_GROUPS):
            mx = jnp.maximum(mx, cur[g])
        found = jnp.zeros((1, tm), jnp.bool_)
        for g in range(N_GROUPS):
            hit = (cur[g] == mx) & jnp.logical_not(found)
            found = found | hit
            picked[g] = picked[g] | hit
            cur[g] = jnp.where(hit, -jnp.inf, cur[g])
    masked = jnp.concatenate(
        [jnp.where(picked[g], sb[g * epg:(g + 1) * epg, :], NEG) for g in range(N_GROUPS)], axis=0)
    iall = lax.broadcasted_iota(I32, (n_exp, tm), 0)
    e_rows, w_rows = [], []
    for _ in range(TOP_K):
        mx = jnp.max(masked, axis=0, keepdims=True)
        ix = jnp.min(jnp.where(masked == mx, iall, n_exp), axis=0, keepdims=True)
        hit = iall == ix
        w_rows.append(jnp.sum(jnp.where(hit, s, 0.0), axis=0, keepdims=True))
        e_rows.append(ix)
        masked = jnp.where(hit, -jnp.inf, masked)
    wsum = w_rows[0]
    for w in w_rows[1:]:
        wsum = wsum + w
    gates = [w / wsum * ROUTED_SCALE for w in w_rows]
    return jnp.concatenate(e_rows, axis=0), jnp.concatenate(gates, axis=0)


def _post_attn_kernel(xp, xs, ap, as_, cp, cs, gap, gas, gcp, gcs, woa, woc, wo, g1, b1, rwt, rb,
                      h2_o, e_o, gate_o, *, n_prompt_tiles):
    i = pl.program_id(0)
    is_p = i < n_prompt_tiles
    pick = lambda a, b: jnp.where(is_p, a[...], b[...])
    a = _dot(pick(ap, as_), woa[...])
    c = _dot(pick(cp, cs), woc[...])
    merged = pick(gap, gas).astype(F32) * a + pick(gcp, gcs).astype(F32) * c
    r = ALPHA * pick(xp, xs) + _dot(merged.astype(BF16), wo[...])
    h = _layer_norm(r, g1[...], b1[...])
    h2_o[...] = h
    logits_t = lax.dot_general(rwt[...], h, (((1,), (1,)), ((), ())),
                               precision=lax.Precision.HIGHEST, preferred_element_type=F32)
    e_idx, gates = _route(logits_t, rb[...])
    e_o[...] = e_idx
    gate_o[...] = gates


def _post_attn(x_p, x_s, attn_p, attn_s, conv_p, conv_s, ga_p, ga_s, gc_p, gc_s,
               woa, woc, wo, g1, b1, rwt, rb):
    tp, ts = x_p.shape[0], x_s.shape[0]
    tm = TOKEN_TILE
    npt, nst = tp // tm, ts // tm
    n_tok = tp + ts
    p_spec = pl.BlockSpec((tm, D_MODEL), lambda i: (jnp.minimum(i, npt - 1), 0))
    s_spec = pl.BlockSpec((tm, D_MODEL), lambda i: (jnp.maximum(i - npt, 0), 0))
    consts = [woa, woc, wo, g1, b1, rwt, rb]
    return pl.pallas_call(
        functools.partial(_post_attn_kernel, n_prompt_tiles=npt),
        grid=(npt + nst,),
        in_specs=[p_spec, s_spec] * 5 + [_const_spec(c.shape) for c in consts],
        out_specs=[pl.BlockSpec((tm, D_MODEL), lambda i: (i, 0)),
                   pl.BlockSpec((TOP_K, tm), lambda i: (0, i)),
                   pl.BlockSpec((TOP_K, tm), lambda i: (0, i))],
        out_shape=[jax.ShapeDtypeStruct((n_tok, D_MODEL), F32),
                   jax.ShapeDtypeStruct((TOP_K, n_tok), I32),
                   jax.ShapeDtypeStruct((TOP_K, n_tok), F32)],
        compiler_params=_params(1),
        name="post_attn",
    )(x_p, x_s, attn_p, attn_s, conv_p, conv_s, ga_p, ga_s, gc_p, gc_s, *consts)


def _rank_kernel(e_ref, rank_o, cnt_o, carry_ref):
    i = pl.program_id(0)

    @pl.when(i == 0)
    def _():
        carry_ref[...] = jnp.zeros_like(carry_ref)

    n_exp = carry_ref.shape[0]
    e = e_ref[...]
    tk = e.shape[1]
    ie = lax.broadcasted_iota(I32, (n_exp, tk), 0)
    onehot = jnp.zeros((n_exp, tk), F32)
    for j in range(TOP_K):
        onehot = onehot + (ie == e[j:j + 1, :]).astype(F32)
    before = (lax.broadcasted_iota(I32, (tk, tk), 0) < lax.broadcasted_iota(I32, (tk, tk), 1)).astype(BF16)
    prefix = _dot(onehot.astype(BF16), before) + carry_ref[:, 0:1]
    rows = [jnp.sum(jnp.where(ie == e[j:j + 1, :], prefix, 0.0), axis=0, keepdims=True) for j in range(TOP_K)]
    rank_o[...] = jnp.concatenate(rows, axis=0).astype(I32)
    carry_ref[...] = carry_ref[...] + jnp.sum(onehot, axis=1, keepdims=True)
    cnt_o[...] = carry_ref[...].astype(I32)


def _rank(e_t, n_exp):
    n_tok = e_t.shape[1]
    tk = TOKEN_TILE
    return pl.pallas_call(
        _rank_kernel,
        grid=(n_tok // tk,),
        in_specs=[pl.BlockSpec((TOP_K, tk), lambda i: (0, i))],
        out_specs=[pl.BlockSpec((TOP_K, tk), lambda i: (0, i)),
                   pl.BlockSpec((n_exp, LANES), lambda i: (0, 0))],
        out_shape=[jax.ShapeDtypeStruct((TOP_K, n_tok), I32), jax.ShapeDtypeStruct((n_exp, LANES), I32)],
        scratch_shapes=[pltpu.VMEM((n_exp, LANES), F32)],
        compiler_params=_params(1),
        name="moe_rank",
    )(e_t)


def _slot_kernel(e_ref, rank_ref, pstart_ref, slot_o):
    e = e_ref[...]
    n_exp = pstart_ref.shape[0]
    tk = e.shape[1]
    ie = lax.broadcasted_iota(I32, (n_exp, tk), 0)
    ps = pstart_ref[:, 0:1]
    rows = [jnp.sum(jnp.where(ie == e[j:j + 1, :], ps, 0.0), axis=0, keepdims=True) for j in range(TOP_K)]
    slot_o[...] = rank_ref[...] + jnp.concatenate(rows, axis=0).astype(I32)


def _slots(e_t, rank_t, pstart_f):
    n_tok = e_t.shape[1]
    tk = TOKEN_TILE
    spec = pl.BlockSpec((TOP_K, tk), lambda i: (0, i))
    return pl.pallas_call(
        _slot_kernel,
        grid=(n_tok // tk,),
        in_specs=[spec, spec, _const_spec(pstart_f.shape)],
        out_specs=spec,
        out_shape=jax.ShapeDtypeStruct((TOP_K, n_tok), I32),
        compiler_params=_params(1),
        name="moe_slots",
    )(e_t, rank_t, pstart_f)


def _dispatch_kernel(pend_ref, padded_ref, slot_ref, h_ref, xs_hbm, zbuf, sem, zsem, *, n_exp):
    i = pl.program_id(0)
    td = slot_ref.shape[1]

    def zero_copy(e):
        dst = pl.ds(pl.multiple_of(pend_ref[e] - MOE_BLOCK, MOE_BLOCK), MOE_BLOCK)
        return pltpu.make_async_copy(zbuf, xs_hbm.at[dst], zsem)

    @pl.when(i == 0)
    def _():
        zbuf[...] = jnp.zeros_like(zbuf)

        def start(e, carry):
            @pl.when(padded_ref[e] > 0)
            def _():
                zero_copy(e).start()
            return carry

        def wait(e, carry):
            @pl.when(padded_ref[e] > 0)
            def _():
                zero_copy(e).wait()
            return carry

        lax.fori_loop(0, n_exp, start, 0)
        lax.fori_loop(0, n_exp, wait, 0)

    def row_copy(t, j):
        return pltpu.make_async_copy(h_ref.at[pl.ds(t, 1)], xs_hbm.at[pl.ds(slot_ref[j, t], 1)], sem)

    def start_tok(t, carry):
        for j in range(TOP_K):
            row_copy(t, j).start(priority=j % 2)
        return carry

    def wait_tok(t, carry):
        for j in range(TOP_K):
            row_copy(t, j).wait()
        return carry

    lax.fori_loop(0, td, start_tok, 0)
    lax.fori_loop(0, td, wait_tok, 0)


def _dispatch(pend, padded, slot_t, h2, m_pad):
    n_tok = slot_t.shape[1]
    td = TOKEN_TILE
    n_exp = pend.shape[0]
    grid_spec = pltpu.PrefetchScalarGridSpec(
        num_scalar_prefetch=2,
        grid=(n_tok // td,),
        in_specs=[pl.BlockSpec((TOP_K, td), lambda i, *_: (0, i), memory_space=pltpu.SMEM),
                  pl.BlockSpec((td, D_MODEL), lambda i, *_: (i, 0))],
        out_specs=pl.BlockSpec(memory_space=pl.ANY),
        scratch_shapes=[pltpu.VMEM((MOE_BLOCK, D_MODEL), F32),
                        pltpu.SemaphoreType.DMA(()), pltpu.SemaphoreType.DMA(())],
    )
    return pl.pallas_call(
        functools.partial(_dispatch_kernel, n_exp=n_exp),
        grid_spec=grid_spec,
        out_shape=jax.ShapeDtypeStruct((m_pad, D_MODEL), F32),
        compiler_params=_params(1),
        name="moe_dispatch",
    )(pend, padded, slot_t, h2)


def _expert_kernel(be_ref, nused_ref, x_ref, wg_ref, wu_ref, wd_ref, y_ref, wgu_b, wd_b):
    i = pl.program_id(0)
    d_exp = wd_b.shape[0]

    @pl.when(i < nused_ref[0])
    def _():
        changed = (i == 0) | (be_ref[i] != be_ref[jnp.maximum(i - 1, 0)])

        @pl.when(changed)
        def _():
            wgu_b[:, :d_exp] = wg_ref[0].astype(BF16)
            wgu_b[:, d_exp:] = wu_ref[0].astype(BF16)
            wd_b[...] = wd_ref[0].astype(BF16)

        gu = _dot(x_ref[...].astype(BF16), wgu_b[...])
        hh = jax.nn.silu(gu[:, :d_exp]) * gu[:, d_exp:]
        y_ref[...] = _dot(hh.astype(BF16), wd_b[...])


def _experts(block_e, nused, xs, w_gate, w_up, w_down):
    n_blocks = block_e.shape[0]
    d_exp = w_gate.shape[2]
    last = lambda i, nu: jnp.minimum(i, nu[0] - 1)
    grid_spec = pltpu.PrefetchScalarGridSpec(
        num_scalar_prefetch=2,
        grid=(n_blocks,),
        in_specs=[pl.BlockSpec((MOE_BLOCK, D_MODEL), lambda i, be, nu: (last(i, nu), 0)),
                  pl.BlockSpec((1, D_MODEL, d_exp), lambda i, be, nu: (be[i], 0, 0)),
                  pl.BlockSpec((1, D_MODEL, d_exp), lambda i, be, nu: (be[i], 0, 0)),
                  pl.BlockSpec((1, d_exp, D_MODEL), lambda i, be, nu: (be[i], 0, 0))],
        out_specs=pl.BlockSpec((MOE_BLOCK, D_MODEL), lambda i, be, nu: (last(i, nu), 0)),
        scratch_shapes=[pltpu.VMEM((D_MODEL, 2 * d_exp), BF16), pltpu.VMEM((d_exp, D_MODEL), BF16)],
    )
    return pl.pallas_call(
        _expert_kernel,
        grid_spec=grid_spec,
        out_shape=jax.ShapeDtypeStruct(xs.shape, F32),
        compiler_params=_params(1),
        name="moe_experts",
    )(block_e, nused, xs, w_gate, w_up, w_down)


def _combine_kernel(slot_ref, slot_next_ref, gate_ref, h2_ref, ys_hbm, shg, shu, shd, g2, b2, yp_o, ys_o,
                    buf, sem, *, n_prompt_tiles):
    i = pl.program_id(0)
    n = pl.num_programs(0)
    tc = gate_ref.shape[0]

    def start_tile(s_ref, bslot):
        def body(t, carry):
            for j in range(TOP_K):
                pltpu.make_async_copy(ys_hbm.at[pl.ds(s_ref[j, t], 1)], buf.at[bslot, j, pl.ds(t, 1)],
                                      sem.at[bslot]).start(priority=j % 2)
            return carry
        lax.fori_loop(0, tc, body, 0)

    def wait_tile(bslot):
        for j in range(TOP_K):
            pltpu.make_async_copy(ys_hbm.at[pl.ds(0, tc)], buf.at[bslot, j], sem.at[bslot]).wait()

    @pl.when(i == 0)
    def _():
        start_tile(slot_ref, 0)

    @pl.when(i + 1 < n)
    def _():
        start_tile(slot_next_ref, (i + 1) % 2)

    cur = i % 2
    h = h2_ref[...]
    hb = h.astype(BF16)
    shared = _dot((jax.nn.silu(_dot(hb, shg[...])) * _dot(hb, shu[...])).astype(BF16), shd[...])
    wait_tile(cur)
    routed = jnp.zeros((tc, D_MODEL), F32)
    for j in range(TOP_K):
        routed = routed + buf[cur, j] * gate_ref[:, j:j + 1]
    y = _layer_norm(ALPHA * h + (routed + shared), g2[...], b2[...])

    @pl.when(i < n_prompt_tiles)
    def _():
        yp_o[...] = y

    @pl.when(i >= n_prompt_tiles)
    def _():
        ys_o[...] = y


def _combine(slot_t, gates, h2, ys, shg, shu, shd, g2, b2, n_prompt, n_decode):
    tc = TOKEN_TILE
    npt, nst = n_prompt // tc, n_decode // tc
    consts = [shg, shu, shd, g2, b2]
    return pl.pallas_call(
        functools.partial(_combine_kernel, n_prompt_tiles=npt),
        grid=(npt + nst,),
        in_specs=[pl.BlockSpec((TOP_K, tc), lambda i: (0, i), memory_space=pltpu.SMEM),
                  pl.BlockSpec((TOP_K, tc), lambda i: (0, jnp.minimum(i + 1, npt + nst - 1)),
                               memory_space=pltpu.SMEM),
                  pl.BlockSpec((tc, TOP_K), lambda i: (i, 0)),
                  pl.BlockSpec((tc, D_MODEL), lambda i: (i, 0)),
                  pl.BlockSpec(memory_space=pl.ANY)] + [_const_spec(c.shape) for c in consts],
        out_specs=[pl.BlockSpec((tc, D_MODEL), lambda i: (jnp.minimum(i, npt - 1), 0)),
                   pl.BlockSpec((tc, D_MODEL), lambda i: (jnp.maximum(i - npt, 0), 0))],
        out_shape=[jax.ShapeDtypeStruct((n_prompt, D_MODEL), F32),
                   jax.ShapeDtypeStruct((n_decode, D_MODEL), F32)],
        scratch_shapes=[pltpu.VMEM((2, TOP_K, tc, D_MODEL), F32), pltpu.SemaphoreType.DMA((2,))],
        compiler_params=_params(1),
        name="moe_combine",
    )(slot_t, slot_t, gates, h2, ys, *consts)


def _pad_rows(a, n):
    return jnp.pad(a, ((0, n - a.shape[0]),) + ((0, 0),) * (a.ndim - 1))


def kernel(x_prompt, x_sample, cache_k, cache_v, cache_kidx, page_table, state_conv, w_in, conv_w, w_o_attn,
           w_o_conv, w_o, ln1_g, ln1_b, router_w, router_bias, moe_w_gate, moe_w_up, moe_w_down,
           shared_w_gate, shared_w_up, shared_w_down, ln2_g, ln2_b):
    nb, seq, _ = x_prompt.shape
    db = x_sample.shape[0]
    n_pool = cache_k.shape[1]
    n_exp = router_w.shape[-1]
    tp = nb * seq
    ts = TOKEN_TILE
    nkv = N_KV_HEADS * HEAD_DIM

    weights = _prep_in_weights(w_in[0])
    cw = conv_w[0]

    xp2 = x_prompt.reshape(tp, D_MODEL)
    (q_p, k_p, v_p, kvb_p, qi_p, idx_p, kib_p, conv_p, ga_p, gc_p, ulast_p) = _in_proj_prompt(xp2, weights, cw, nb)
    xs2 = _pad_rows(x_sample.reshape(db, D_MODEL), ts)
    prev2 = _pad_rows(state_conv[0, :, 0, :], ts)
    prev1 = _pad_rows(state_conv[0, :, 1, :], ts)
    (q_s, k_s, v_s, _, qi_s, idx_s, _, conv_s, ga_s, gc_s, u_s) = _in_proj_decode(xs2, prev2, prev1, weights, cw)

    attn_p = _attn_prompt(q_p, kvb_p, qi_p, kib_p, idx_p, nb)
    q8 = q_s[:db].reshape(db, N_HEADS, HEAD_DIM)
    qi8 = qi_s[:db].reshape(db, IDX_HEADS, LANES)[:, :, :IDX_DIM]
    wi8 = idx_s[:db, IDX_DIM:IDX_DIM + IDX_HEADS].reshape(db, IDX_HEADS, 1)
    kin = idx_s[:db, :IDX_DIM].astype(BF16).reshape(db, 1, IDX_DIM)
    kn8 = jnp.repeat(k_s[:db].reshape(db, N_KV_HEADS, HEAD_DIM), GROUP, axis=1).astype(BF16)
    vn8 = jnp.repeat(v_s[:db].reshape(db, N_KV_HEADS, HEAD_DIM), GROUP, axis=1).astype(BF16)
    ck2 = cache_k[0].reshape(n_pool * PAGE_SIZE * N_KV_HEADS, HEAD_DIM)
    cv2 = cache_v[0].reshape(n_pool * PAGE_SIZE * N_KV_HEADS, HEAD_DIM)
    meta = _decode_select(page_table, qi8, wi8, kin, jnp.swapaxes(cache_kidx[0], 1, 2))
    attn_s8 = _decode_attend(page_table, meta[:, 0, :], q8, meta, kn8, vn8, ck2, cv2)
    attn_s = _pad_rows(attn_s8.reshape(db, N_HEADS * HEAD_DIM), ts)

    h2, e_t, gate_t = _post_attn(
        xp2, xs2, attn_p, attn_s, conv_p, conv_s, ga_p, ga_s, gc_p, gc_s,
        w_o_attn[0].astype(BF16), w_o_conv[0].astype(BF16), w_o[0].astype(BF16),
        ln1_g[0].reshape(1, D_MODEL), ln1_b[0].reshape(1, D_MODEL),
        router_w[0].T, router_bias[0].reshape(n_exp, 1))

    n_tok = tp + ts
    rank_t, cnt = _rank(e_t, n_exp)
    counts = cnt[:, 0]
    padded = (counts + MOE_BLOCK - 1) // MOE_BLOCK * MOE_BLOCK
    pend = jnp.cumsum(padded)
    pstart = pend - padded
    n_blocks = (n_tok * TOP_K + n_exp * (MOE_BLOCK - 1) + MOE_BLOCK - 1) // MOE_BLOCK
    nused = (pend[-1] // MOE_BLOCK).astype(I32)
    blk = jnp.minimum(jnp.arange(n_blocks, dtype=I32), nused - 1)
    block_e = jnp.minimum(jnp.sum((pend[None, :] <= (blk * MOE_BLOCK)[:, None]).astype(I32), axis=1), n_exp - 1)
    slot_t = _slots(e_t, rank_t, jnp.broadcast_to(pstart.astype(F32)[:, None], (n_exp, LANES)))
    xs = _dispatch(pend.astype(I32), padded.astype(I32), slot_t, h2, n_blocks * MOE_BLOCK)
    ys = _experts(block_e, nused.reshape(1), xs, moe_w_gate[0], moe_w_up[0], moe_w_down[0])
    y_p, y_s = _combine(slot_t, gate_t.T, h2, ys,
                        shared_w_gate[0].astype(BF16), shared_w_up[0].astype(BF16),
                        shared_w_down[0].astype(BF16),
                        ln2_g[0].reshape(1, D_MODEL), ln2_b[0].reshape(1, D_MODEL), tp, ts)

    conv_sample = jnp.stack([state_conv[0, :, 1, :], u_s[:db]], axis=1)[None]
    return (y_p.reshape(nb, seq, D_MODEL),
            y_s[:db].reshape(db, 1, D_MODEL),
            k_p.reshape(1, nb, seq, N_KV_HEADS, HEAD_DIM),
            v_p.reshape(1, nb, seq, N_KV_HEADS, HEAD_DIM),
            idx_p[:, :IDX_DIM].reshape(1, nb, seq, IDX_DIM),
            ulast_p[:, SUBLANES - (CONV_W - 1):, :][None],
            k_s[:db].reshape(1, db, 1, N_KV_HEADS, HEAD_DIM),
            v_s[:db].reshape(1, db, 1, N_KV_HEADS, HEAD_DIM),
            idx_s[:db, :IDX_DIM].reshape(1, db, 1, IDX_DIM),
            conv_sample)
```

```python
import functools

import jax
import jax.numpy as jnp
import numpy as np
from jax import lax
from jax.experimental import pallas as pl
from jax.experimental.pallas import tpu as pltpu

F32 = jnp.float32
BF16 = jnp.bfloat16
I32 = jnp.int32

D_MODEL = 1024
N_HEADS = 8
HEAD_DIM = 128
N_KV_HEADS = 2
GROUP = N_HEADS // N_KV_HEADS
IDX_HEADS = 8
IDX_DIM = 64
TOPK_KEYS = 256
IDX_SCALE = IDX_DIM ** -0.5 * IDX_HEADS ** -0.5
Q_SCALE = HEAD_DIM ** -0.5
PAGE_SIZE = 128
D_CONV = D_MODEL
CONV_W = 3
TOP_K = 8
N_GROUPS = 8
TOPK_GROUPS = 4
ROUTED_SCALE = 2.5
DEPTH = 1
ALPHA = (2 * DEPTH) ** 0.25
LN_EPS = 1e-5
NEG = -1e30
IN_WIDTHS = (N_HEADS * HEAD_DIM, N_KV_HEADS * HEAD_DIM, N_KV_HEADS * HEAD_DIM,
             IDX_HEADS * IDX_DIM, IDX_DIM, IDX_HEADS,
             D_CONV, D_CONV, D_CONV, D_MODEL, D_MODEL)

LANES = 128
SUBLANES = 8
VMEM_LIMIT_BYTES = 56 * 1024 * 1024

PROJ_ROWS = 512
ATT_Q = 256
SEARCH_ROWS = 128
TOKEN_TILE = 256
MOE_BLOCK = 256
CONV_CHUNK = 256
INT_MIN = -2 ** 31


def _np_key(v):
    b = int(np.float32(v).view(np.int32))
    return b ^ ((b >> 31) & 0x7FFFFFFF)


KEY_HALF = _np_key(NEG * 0.5)


def _sort_key(s):
    b = lax.bitcast_convert_type(s + 0.0, I32)
    return b ^ ((b >> 31) & jnp.int32(0x7FFFFFFF))


def _dot(a, b):
    return jnp.dot(a, b, preferred_element_type=F32)


def _dot_nt(a, b):
    return lax.dot_general(a, b, (((1,), (1,)), ((), ())), preferred_element_type=F32)


def _params(n_grid):
    return pltpu.CompilerParams(dimension_semantics=("arbitrary",) * n_grid,
                                vmem_limit_bytes=VMEM_LIMIT_BYTES)


def _const_spec(shape):
    nd = len(shape)
    return pl.BlockSpec(shape, lambda *_: (0,) * nd, pipeline_mode=pl.Buffered(1))


def _proj_common(xb, w1, w2, w3, wga, wgc, q_o, k_o, v_o, kv_o, qi_o, idx_o, kib_o, sga_o, sgc_o):
    nq = N_HEADS * HEAD_DIM
    nkv = N_KV_HEADS * HEAD_DIM
    z1 = _dot(xb, w1[...])
    q_o[...] = (z1[:, :nq] * Q_SCALE).astype(BF16)
    k = z1[:, nq:nq + nkv]
    v = z1[:, nq + nkv:nq + 2 * nkv]
    k_o[...] = k
    v_o[...] = v
    kv_o[:, :nkv] = k.astype(BF16)
    kv_o[:, nkv:] = v.astype(BF16)
    qi_o[...] = _dot(xb, w2[...]).astype(BF16)
    z3 = _dot(xb, w3[...])
    idx_o[...] = z3
    kib_o[...] = z3.astype(BF16)
    sga_o[...] = jax.nn.sigmoid(_dot(xb, wga[...])).astype(BF16)
    sgc_o[...] = jax.nn.sigmoid(_dot(xb, wgc[...])).astype(BF16)


def _in_proj_prompt_kernel(x_ref, w1, w2, w3, wb, wc, wx, wga, wgc, cw_ref,
                           q_o, k_o, v_o, kv_o, qi_o, idx_o, kib_o, conv_o, sga_o, sgc_o, ulast_o,
                           tail_ref):
    j = pl.program_id(1)

    @pl.when(j == 0)
    def _():
        tail_ref[...] = jnp.zeros_like(tail_ref)

    xb = x_ref[...].astype(BF16)
    _proj_common(xb, w1, w2, w3, wga, wgc, q_o, k_o, v_o, kv_o, qi_o, idx_o, kib_o, sga_o, sgc_o)
    tm = xb.shape[0]
    row = lax.broadcasted_iota(I32, (tm, CONV_CHUNK), 0)
    for c in range(D_CONV // CONV_CHUNK):
        cs = slice(c * CONV_CHUNK, (c + 1) * CONV_CHUNK)
        gb = _dot(xb, wb[:, cs])
        u = _dot(xb, wc[:, cs]) * _dot(xb, wx[:, cs])
        p1 = tail_ref[SUBLANES - 1:SUBLANES, cs]
        p2 = tail_ref[SUBLANES - 2:SUBLANES - 1, cs]
        u1 = jnp.where(row == 0, p1, pltpu.roll(u, 1, 0))
        u2 = jnp.where(row == 0, p2, jnp.where(row == 1, p1, pltpu.roll(u, 2, 0)))
        conv = cw_ref[0:1, cs] * u2 + cw_ref[1:2, cs] * u1 + cw_ref[2:3, cs] * u
        conv_o[:, cs] = (gb * conv).astype(BF16)
        tail_ref[:, cs] = u[tm - SUBLANES:, :]
    ulast_o[0] = tail_ref[...]


def _in_proj_decode_kernel(x_ref, p2_ref, p1_ref, w1, w2, w3, wb, wc, wx, wga, wgc, cw_ref,
                           q_o, k_o, v_o, kv_o, qi_o, idx_o, kib_o, conv_o, sga_o, sgc_o, u_o):
    xb = x_ref[...].astype(BF16)
    _proj_common(xb, w1, w2, w3, wga, wgc, q_o, k_o, v_o, kv_o, qi_o, idx_o, kib_o, sga_o, sgc_o)
    for c in range(D_CONV // CONV_CHUNK):
        cs = slice(c * CONV_CHUNK, (c + 1) * CONV_CHUNK)
        gb = _dot(xb, wb[:, cs])
        u = _dot(xb, wc[:, cs]) * _dot(xb, wx[:, cs])
        conv = cw_ref[0:1, cs] * p2_ref[:, cs] + cw_ref[1:2, cs] * p1_ref[:, cs] + cw_ref[2:3, cs] * u
        conv_o[:, cs] = (gb * conv).astype(BF16)
        u_o[:, cs] = u


def _prep_in_weights(w_in):
    w = w_in.astype(BF16)
    offs = np.concatenate([[0], np.cumsum(IN_WIDTHS)])
    part = [w[:, int(offs[i]):int(offs[i + 1])] for i in range(len(IN_WIDTHS))]
    q, k, v, qi, ki, wi, gb, gc, xv, ga, gcv = part
    d = w.shape[0]
    w1 = jnp.concatenate([q, k, v], axis=1)
    w2 = jnp.pad(qi.reshape(d, IDX_HEADS, IDX_DIM),
                 ((0, 0), (0, 0), (0, LANES - IDX_DIM))).reshape(d, IDX_HEADS * LANES)
    w3 = jnp.concatenate([ki, wi, jnp.zeros((d, LANES - IDX_DIM - IDX_HEADS), BF16)], axis=1)
    return (w1, w2, w3, gb, gc, xv, ga, gcv)


def _proj_out_shapes(t):
    nkv = N_KV_HEADS * HEAD_DIM
    sd = jax.ShapeDtypeStruct
    return [sd((t, N_HEADS * HEAD_DIM), BF16), sd((t, nkv), F32), sd((t, nkv), F32), sd((t, 2 * nkv), BF16),
            sd((t, IDX_HEADS * LANES), BF16), sd((t, LANES), F32), sd((t, LANES), BF16),
            sd((t, D_CONV), BF16), sd((t, D_MODEL), BF16), sd((t, D_MODEL), BF16)]


def _in_proj_prompt(x2d, weights, conv_w, nb):
    t = x2d.shape[0]
    s = t // nb
    tm = min(PROJ_ROWS, s)
    nj = s // tm
    row_spec = lambda w: pl.BlockSpec((tm, w), lambda b, j: (b * nj + j, 0))
    out_shapes = _proj_out_shapes(t) + [jax.ShapeDtypeStruct((nb, SUBLANES, D_CONV), F32)]
    out_specs = [row_spec(o.shape[1]) for o in out_shapes[:-1]]
    out_specs.append(pl.BlockSpec((1, SUBLANES, D_CONV), lambda b, j: (b, 0, 0)))
    return pl.pallas_call(
        _in_proj_prompt_kernel,
        grid=(nb, nj),
        in_specs=[row_spec(D_MODEL)] + [_const_spec(w.shape) for w in weights] + [_const_spec(conv_w.shape)],
        out_specs=out_specs,
        out_shape=out_shapes,
        scratch_shapes=[pltpu.VMEM((SUBLANES, D_CONV), F32)],
        compiler_params=_params(2),
        name="in_proj_prompt",
    )(x2d, *weights, conv_w)


def _in_proj_decode(x2d, p2, p1, weights, conv_w):
    t = x2d.shape[0]
    row_spec = lambda w: pl.BlockSpec((t, w), lambda i: (0, 0))
    out_shapes = _proj_out_shapes(t) + [jax.ShapeDtypeStruct((t, D_CONV), F32)]
    return pl.pallas_call(
        _in_proj_decode_kernel,
        grid=(1,),
        in_specs=[row_spec(D_MODEL)] * 3 + [_const_spec(w.shape) for w in weights] + [_const_spec(conv_w.shape)],
        out_specs=[row_spec(o.shape[1]) for o in out_shapes],
        out_shape=out_shapes,
        compiler_params=_params(1),
        name="in_proj_decode",
    )(x2d, p2, p1, *weights, conv_w)


def _attn_prompt_kernel(q_ref, kv_ref, qi_ref, kib_ref, idx_ref, o_ref,
                        key_ref, bias_ref, t_ref, j_ref, need_ref, m_ref, acc_ref,
                        *, n_keep, seq_len):
    tq = q_ref.shape[0]
    i = pl.program_id(1)
    nchunk = i + 1
    nsub = tq // LANES
    lane = lax.broadcasted_iota(I32, (tq, LANES), 1)

    wi_cols = [idx_ref[:, IDX_DIM + h:IDX_DIM + h + 1] * IDX_SCALE for h in range(IDX_HEADS)]
    row_i = lax.broadcasted_iota(I32, (tq, tq), 0)
    col_i = lax.broadcasted_iota(I32, (tq, tq), 1)

    def score_chunk(c, carry):
        kic = kib_ref[pl.ds(pl.multiple_of(c * tq, tq), tq), :]
        acc = jnp.zeros((tq, tq), F32)
        for h in range(IDX_HEADS):
            d = _dot_nt(qi_ref[:, h * LANES:(h + 1) * LANES], kic)
            acc = acc + jnp.maximum(d, 0.0) * wi_cols[h]
        s = jnp.where((c < i) | (col_i <= row_i), acc, NEG)
        key_ref[c] = _sort_key(s)
        return carry

    lax.fori_loop(0, nchunk, score_chunk, 0)

    rb = SEARCH_ROWS

    def count(r0, pred):
        def body(c, cnt):
            for k in range(nsub):
                blk = key_ref[c, r0:r0 + rb, k * LANES:(k + 1) * LANES]
                cnt = cnt + pred(blk, c * tq + k * LANES).astype(I32)
            return cnt
        cnt = lax.fori_loop(0, nchunk, body, jnp.zeros((rb, LANES), I32))
        return jnp.sum(cnt, axis=1, keepdims=True)

    any_excess = jnp.int32(0)
    for r in range(tq // rb):
        r0 = r * rb

        def bit_body(bi, t):
            cand = t + lax.shift_left(jnp.int32(1), 31 - bi)
            cand_b = jnp.broadcast_to(cand, (rb, LANES))
            cnt = count(r0, lambda blk, _: blk >= cand_b)
            return jnp.where(cnt >= n_keep, cand, t)

        t = lax.fori_loop(0, 32, bit_body, jnp.full((rb, 1), INT_MIN, I32))
        t_b = jnp.broadcast_to(t, (rb, LANES))
        cnt_gt = count(r0, lambda blk, _: blk > t_b)
        cnt_eq = count(r0, lambda blk, _: blk == t_b)
        need = n_keep - cnt_gt
        excess = (cnt_eq > need) & (t > KEY_HALF)
        any_excess = any_excess + jnp.max(excess.astype(I32))
        t_ref[r0:r0 + rb, :] = t_b
        need_ref[r0:r0 + rb, :] = jnp.broadcast_to(need, (rb, LANES))

    j_ref[...] = jnp.full(j_ref.shape, seq_len, I32)

    @pl.when(any_excess > 0)
    def _():
        nbits = max(1, (seq_len - 1).bit_length())
        lane_rb = lax.broadcasted_iota(I32, (rb, LANES), 1)
        for r in range(tq // rb):
            r0 = r * rb
            t_b = t_ref[r0:r0 + rb, :]
            need = need_ref[r0:r0 + rb, 0:1]

            def jbit(bi, jj):
                cand = jj | lax.shift_left(jnp.int32(1), nbits - 1 - bi)
                cand_b = jnp.broadcast_to(cand, (rb, LANES))
                g = count(r0, lambda blk, base: (blk == t_b) & ((base + lane_rb) < cand_b))
                return jnp.where(g < need, cand, jj)

            jj = lax.fori_loop(0, nbits, jbit, jnp.zeros((rb, 1), I32))
            j_ref[r0:r0 + rb, :] = jnp.broadcast_to(jj, (rb, LANES))

    t_all = t_ref[...]
    j_all = j_ref[...]

    def bias_chunk(c, carry):
        for k in range(nsub):
            blk = key_ref[c, :, k * LANES:(k + 1) * LANES]
            col = c * tq + k * LANES + lane
            sel = (blk > KEY_HALF) & ((blk > t_all) | ((blk == t_all) & (col <= j_all)))
            bias_ref[c, :, k * LANES:(k + 1) * LANES] = jnp.where(sel, 0.0, NEG)
        return carry

    lax.fori_loop(0, nchunk, bias_chunk, 0)

    nkv = N_KV_HEADS * HEAD_DIM
    gq = GROUP * tq

    def group_logits(c, g):
        rows = pl.ds(pl.multiple_of(c * tq, tq), tq)
        kc = kv_ref[rows, g * HEAD_DIM:(g + 1) * HEAD_DIM]
        qg = jnp.concatenate([q_ref[:, h * HEAD_DIM:(h + 1) * HEAD_DIM]
                              for h in range(g * GROUP, (g + 1) * GROUP)], axis=0)
        s = _dot_nt(qg, kc).reshape(GROUP, tq, tq) + bias_ref[c][None]
        return s.reshape(gq, tq)

    m_ref[...] = jnp.full(m_ref.shape, NEG, F32)

    def max_chunk(c, carry):
        for g in range(N_KV_HEADS):
            s = group_logits(c, g)
            mx = s[:, :LANES]
            for k in range(1, nsub):
                mx = jnp.maximum(mx, s[:, k * LANES:(k + 1) * LANES])
            m_ref[g] = jnp.maximum(m_ref[g], mx)
        return carry

    lax.fori_loop(0, nchunk, max_chunk, 0)
    for g in range(N_KV_HEADS):
        m_ref[g] = jnp.broadcast_to(jnp.max(m_ref[g], axis=1, keepdims=True), (gq, LANES))

    acc_ref[...] = jnp.zeros_like(acc_ref)
    ones = jnp.ones((tq, HEAD_DIM), BF16)

    def pv_chunk(c, carry):
        rows = pl.ds(pl.multiple_of(c * tq, tq), tq)
        for g in range(N_KV_HEADS):
            s = group_logits(c, g)
            m = m_ref[g]
            p = jnp.concatenate([jnp.exp(s[:, k * LANES:(k + 1) * LANES] - m) for k in range(nsub)], axis=1)
            vc = kv_ref[rows, nkv + g * HEAD_DIM:nkv + (g + 1) * HEAD_DIM]
            acc_ref[g] = acc_ref[g] + _dot(p.astype(BF16), jnp.concatenate([vc, ones], axis=1))
        return carry

    lax.fori_loop(0, nchunk, pv_chunk, 0)
    for h in range(N_HEADS):
        a = acc_ref[h // GROUP, (h % GROUP) * tq:(h % GROUP + 1) * tq, :]
        o_ref[:, h * HEAD_DIM:(h + 1) * HEAD_DIM] = (a[:, :HEAD_DIM] / a[:, HEAD_DIM:]).astype(BF16)


def _attn_prompt(q, kvb, qi, kib, idx, nb):
    t = q.shape[0]
    s = t // nb
    tq = min(ATT_Q, s)
    nq = s // tq
    n_keep = min(TOPK_KEYS, s // 4)
    blk = lambda w: pl.BlockSpec((tq, w), lambda b, i: (b * nq + i, 0))
    seq = lambda w: pl.BlockSpec((s, w), lambda b, i: (b, 0))
    return pl.pallas_call(
        functools.partial(_attn_prompt_kernel, n_keep=n_keep, seq_len=s),
        grid=(nb, nq),
        in_specs=[blk(q.shape[1]), seq(kvb.shape[1]), blk(qi.shape[1]), seq(kib.shape[1]), blk(idx.shape[1])],
        out_specs=blk(N_HEADS * HEAD_DIM),
        out_shape=jax.ShapeDtypeStruct((t, N_HEADS * HEAD_DIM), BF16),
        scratch_shapes=[pltpu.VMEM((nq, tq, tq), I32), pltpu.VMEM((nq, tq, tq), F32),
                        pltpu.VMEM((tq, LANES), I32), pltpu.VMEM((tq, LANES), I32), pltpu.VMEM((tq, LANES), I32),
                        pltpu.VMEM((N_KV_HEADS, GROUP * tq, LANES), F32),
                        pltpu.VMEM((N_KV_HEADS, GROUP * tq, 2 * HEAD_DIM), F32)],
        compiler_params=_params(2),
        name="attn_prompt",
    )(q, kvb, qi, kib, idx)


DECODE_PAGES_PER_ROW = 4


def _decode_scores_kernel(pt_ref, qi_ref, wi_ref, kin_ref, cx_hbm, key_o, knew_o, xbuf, sem, *, n_pages):
    b = pl.program_id(0)
    nb = pl.num_programs(0)
    ppr = DECODE_PAGES_PER_ROW
    nrow = n_pages // ppr

    def page_copy(bb, p, slot):
        return pltpu.make_async_copy(cx_hbm.at[pt_ref[bb, p]], xbuf.at[slot, p], sem.at[slot])

    def start_batch(bb, slot):
        def body(p, carry):
            page_copy(bb, p, slot).start()
            return carry
        lax.fori_loop(0, n_pages, body, 0)

    def wait_batch(bb, slot):
        def body(p, carry):
            page_copy(bb, p, slot).wait()
            return carry
        lax.fori_loop(0, n_pages, body, 0)

    @pl.when(b == 0)
    def _():
        start_batch(0, 0)

    @pl.when(b + 1 < nb)
    def _():
        start_batch(b + 1, (b + 1) % 2)

    slot = b % 2
    wait_batch(b, slot)

    qi8 = qi_ref[0]
    wi8 = wi_ref[0] * IDX_SCALE

    def score_row(r, carry):
        kx = jnp.concatenate([xbuf[slot, r * ppr + k] for k in range(ppr)], axis=1).astype(BF16)
        d = _dot(qi8, kx)
        sc = jnp.sum(jnp.maximum(d, 0.0) * wi8, axis=0, keepdims=True)
        key_o[0, pl.ds(r, 1), :] = _sort_key(sc)
        return carry

    lax.fori_loop(0, nrow, score_row, 0, unroll=4)
    dn = jnp.sum(qi8.astype(F32) * kin_ref[0].astype(F32), axis=1, keepdims=True)
    key_new = _sort_key(jnp.sum(jnp.maximum(dn, 0.0) * wi8, axis=0, keepdims=True))
    knew_o[0] = jnp.broadcast_to(key_new, (1, LANES))


def _decode_scores(page_table, qi8, wi8, kin, cxt):
    db, n_pages = page_table.shape
    nrow = n_pages // DECODE_PAGES_PER_ROW
    w = DECODE_PAGES_PER_ROW * PAGE_SIZE
    per_b = lambda a: pl.BlockSpec((1,) + a.shape[1:], lambda b, pt: (b, 0, 0))
    grid_spec = pltpu.PrefetchScalarGridSpec(
        num_scalar_prefetch=1,
        grid=(db,),
        in_specs=[per_b(qi8), per_b(wi8), per_b(kin), pl.BlockSpec(memory_space=pl.ANY)],
        out_specs=[pl.BlockSpec((1, nrow, w), lambda b, pt: (b, 0, 0)),
                   pl.BlockSpec((1, 1, LANES), lambda b, pt: (b, 0, 0))],
        scratch_shapes=[pltpu.VMEM((2, n_pages, IDX_DIM, PAGE_SIZE), F32), pltpu.SemaphoreType.DMA((2,))],
    )
    return pl.pallas_call(
        functools.partial(_decode_scores_kernel, n_pages=n_pages),
        grid_spec=grid_spec,
        out_shape=[jax.ShapeDtypeStruct((db, nrow, w), I32), jax.ShapeDtypeStruct((db, 1, LANES), I32)],
        compiler_params=_params(1),
        name="decode_scores",
    )(page_table, qi8, wi8, kin, cxt)


def _decode_pick_kernel(keys_ref, knew_ref, pt_ref, meta_o, *, n_keep):
    nb, nrow, w = keys_ref.shape
    past = nrow * w
    n_pages = pt_ref.shape[1]
    keys = keys_ref[...]
    key_new = knew_ref[...][:, :, 0:1]
    col = (lax.broadcasted_iota(I32, keys.shape, 1) * w + lax.broadcasted_iota(I32, keys.shape, 2))

    def total(x):
        c = x.astype(I32)
        part = c[:, :, :LANES]
        for k in range(1, w // LANES):
            part = part + c[:, :, k * LANES:(k + 1) * LANES]
        return jnp.sum(jnp.sum(part, axis=1, keepdims=True), axis=2, keepdims=True)

    def bit_body(bi, t):
        cand = t + lax.shift_left(jnp.int32(1), 31 - bi)
        cnt = total(keys >= cand) + (key_new >= cand).astype(I32)
        return jnp.where(cnt >= n_keep, cand, t)

    t = lax.fori_loop(0, 32, bit_body, jnp.full((nb, 1, 1), INT_MIN, I32))
    need = n_keep - (total(keys > t) + (key_new > t).astype(I32))
    nbits = past.bit_length()

    def jbit(bi, jj):
        cand = jj | lax.shift_left(jnp.int32(1), nbits - 1 - bi)
        g = total((keys == t) & (col < cand)) + ((key_new == t) & (past < cand)).astype(I32)
        return jnp.where(g < need, cand, jj)

    jj = lax.fori_loop(0, nbits, jbit, jnp.zeros((nb, 1, 1), I32))
    sel = (keys > t) | ((keys == t) & (col <= jj))
    sel_new = ((key_new > t) | ((key_new == t) & (past <= jj))).astype(I32)

    rows = nb * nrow
    incl_lane = (lax.broadcasted_iota(I32, (w, w), 0) <= lax.broadcasted_iota(I32, (w, w), 1)).astype(BF16)
    cnt_in_row = _dot(sel.astype(BF16).reshape(rows, w), incl_lane)
    row_tot = cnt_in_row[:, w - 1:w]
    ri = lax.broadcasted_iota(I32, (rows, rows), 0)
    rj = lax.broadcasted_iota(I32, (rows, rows), 1)
    earlier_row_same_seq = ((ri // nrow) == (rj // nrow)) & (rj < ri)
    row_off = _dot(earlier_row_same_seq.astype(BF16),
                   jnp.broadcast_to(row_tot, (rows, LANES)).astype(BF16))[:, 0:1]
    cnt3 = cnt_in_row.astype(BF16).reshape(nb, nrow, w)
    off3 = row_off.reshape(nb, nrow, 1)
    incl3 = (row_off + row_tot).reshape(nb, nrow, 1)
    jl = lax.broadcasted_iota(I32, (1, 1, n_keep), 2).astype(F32)
    row_j = jnp.sum((incl3 <= jl).astype(F32), axis=1, keepdims=True)
    onehot = lax.broadcasted_iota(I32, (nb, nrow, n_keep), 1).astype(F32) == row_j
    off_j = jnp.sum(jnp.where(onehot, off3, 0.0), axis=1, keepdims=True)
    local_rank = jl - off_j
    onehot_b = onehot.astype(BF16)
    page_i = lax.broadcasted_iota(I32, (n_pages, n_keep), 0)
    row_i = lax.broadcasted_iota(I32, (SUBLANES, n_keep), 0)
    for b in range(nb):
        cnt_j = lax.dot_general(cnt3[b], onehot_b[b], (((0,), (0,)), ((), ())),
                                preferred_element_type=F32)
        lane_j = jnp.sum((cnt_j <= local_rank[b]).astype(F32), axis=0, keepdims=True)
        pos = jnp.minimum((row_j[b] * w + lane_j).astype(I32), past - 1)
        phys = jnp.sum(jnp.where(page_i == pos // PAGE_SIZE, pt_ref[b], 0.0), axis=0, keepdims=True)
        tile = (phys.astype(I32) * PAGE_SIZE + pos % PAGE_SIZE) // (SUBLANES // N_KV_HEADS)
        meta_o[b] = jnp.where(row_i == 0, tile,
                              jnp.where(row_i == 1, sel_new[b], pos % (SUBLANES // N_KV_HEADS)))


def _decode_pick(keys, knew, pt_f):
    nb, nrow, w = keys.shape
    n_keep = min(TOPK_KEYS, (nrow * w + 1) // 4)
    full = lambda a: pl.BlockSpec(a.shape, lambda i: (0,) * a.ndim)
    return pl.pallas_call(
        functools.partial(_decode_pick_kernel, n_keep=n_keep),
        grid=(1,),
        in_specs=[full(keys), full(knew), full(pt_f)],
        out_specs=pl.BlockSpec((nb, SUBLANES, n_keep), lambda i: (0, 0, 0)),
        out_shape=jax.ShapeDtypeStruct((nb, SUBLANES, n_keep), I32),
        compiler_params=_params(1),
        name="decode_pick",
    )(keys, knew, pt_f)


def _decode_attend_kernel(tile_ref, q_ref, meta_ref, kn_ref, vn_ref, ck_hbm, cv_hbm, o_ref,
                          kbuf, vbuf, expand_ref, sem, *, n_keep):
    b = pl.program_id(0)
    nb = pl.num_programs(0)
    ncol = n_keep * SUBLANES

    def item_copies(bb, j, slot):
        src = pl.ds(pl.multiple_of(tile_ref[bb, j] * SUBLANES, SUBLANES), SUBLANES)
        dst = pl.ds(pl.multiple_of(j * SUBLANES, SUBLANES), SUBLANES)
        return (pltpu.make_async_copy(ck_hbm.at[src], kbuf.at[slot, dst], sem.at[0, slot]),
                pltpu.make_async_copy(cv_hbm.at[src], vbuf.at[slot, dst], sem.at[1, slot]))

    def start_batch(bb, slot):
        def body(j, carry):
            for prio, cp in enumerate(item_copies(bb, j, slot)):
                cp.start(priority=prio)
            return carry
        lax.fori_loop(0, n_keep, body, 0)

    def wait_batch(slot):
        pltpu.make_async_copy(ck_hbm.at[pl.ds(0, ncol)], kbuf.at[slot], sem.at[0, slot]).wait()
        pltpu.make_async_copy(cv_hbm.at[pl.ds(0, ncol)], vbuf.at[slot], sem.at[1, slot]).wait()

    @pl.when(b == 0)
    def _():
        start_batch(0, 0)
        item_of_col = lax.broadcasted_iota(I32, (n_keep, ncol), 1) // SUBLANES
        expand_ref[...] = (item_of_col == lax.broadcasted_iota(I32, (n_keep, ncol), 0)).astype(BF16)

    @pl.when(b + 1 < nb)
    def _():
        start_batch(b + 1, (b + 1) % 2)

    slot = b % 2
    wait_batch(slot)

    q8 = q_ref[0]
    meta = meta_ref[0]
    new_kept = meta[1:2, 0:1] > 0
    sub_col = _dot(meta.astype(BF16), expand_ref[...])[2:3, :]
    head_i = lax.broadcasted_iota(I32, (N_HEADS, ncol), 0)
    col_i = lax.broadcasted_iota(I32, (N_HEADS, ncol), 1)
    want = sub_col * N_KV_HEADS + (head_i // GROUP).astype(F32)
    is_new_item = new_kept & (col_i // SUBLANES == n_keep - 1)
    ok = ((col_i % SUBLANES).astype(F32) == want) & jnp.logical_not(is_new_item)

    s = jnp.where(ok, _dot_nt(q8, kbuf[slot].astype(BF16)), NEG)
    s_new = jnp.sum(q8.astype(F32) * kn_ref[0].astype(F32), axis=1, keepdims=True)
    s_new = jnp.where(new_kept, s_new, NEG)
    m = jnp.maximum(jnp.max(s, axis=1, keepdims=True), s_new)
    p = jnp.exp(s - m)
    p_new = jnp.exp(s_new - m)
    l = jnp.sum(p, axis=1, keepdims=True) + p_new
    acc = _dot(p.astype(BF16), vbuf[slot].astype(BF16)) + p_new.astype(BF16).astype(F32) * vn_ref[0].astype(F32)
    o_ref[0] = (acc / l).astype(BF16)


def _decode_attend(tiles, q8, meta, kn8, vn8, ck2, cv2):
    db, n_keep = tiles.shape
    per_b = lambda a: pl.BlockSpec((1,) + a.shape[1:], lambda b, tl: (b, 0, 0))
    any_spec = pl.BlockSpec(memory_space=pl.ANY)
    ncol = n_keep * SUBLANES
    grid_spec = pltpu.PrefetchScalarGridSpec(
        num_scalar_prefetch=1,
        grid=(db,),
        in_specs=[per_b(q8), per_b(meta), per_b(kn8), per_b(vn8), any_spec, any_spec],
        out_specs=pl.BlockSpec((1, N_HEADS, HEAD_DIM), lambda b, tl: (b, 0, 0)),
        scratch_shapes=[pltpu.VMEM((2, ncol, HEAD_DIM), F32), pltpu.VMEM((2, ncol, HEAD_DIM), F32),
                        pltpu.VMEM((n_keep, ncol), BF16), pltpu.SemaphoreType.DMA((2, 2))],
    )
    return pl.pallas_call(
        functools.partial(_decode_attend_kernel, n_keep=n_keep),
        grid_spec=grid_spec,
        out_shape=jax.ShapeDtypeStruct((db, N_HEADS, HEAD_DIM), BF16),
        compiler_params=_params(1),
        name="decode_attend",
    )(tiles, q8, meta, kn8, vn8, ck2, cv2)


def _layer_norm(r, g, b):
    mu = jnp.mean(r, axis=-1, keepdims=True)
    d = r - mu
    var = jnp.mean(d * d, axis=-1, keepdims=True)
    return d * lax.rsqrt(var + LN_EPS) * g + b


def _route(logits_t, rbias):
    n_exp, tm = logits_t.shape
    epg = n_exp // N_GROUPS
    s = jax.nn.sigmoid(logits_t)
    sb = s + rbias
    ie = lax.broadcasted_iota(I32, (epg, tm), 0)
    gs_rows = []
    for g in range(N_GROUPS):
        blk = sb[g * epg:(g + 1) * epg, :]
        m1 = jnp.max(blk, axis=0, keepdims=True)
        i1 = jnp.min(jnp.where(blk == m1, ie, epg), axis=0, keepdims=True)
        m2 = jnp.max(jnp.where(ie == i1, -jnp.inf, blk), axis=0, keepdims=True)
        gs_rows.append(m1 + m2)
    picked = [jnp.zeros((1, tm), jnp.bool_) for _ in range(N_GROUPS)]
    cur = list(gs_rows)
    for _ in range(TOPK_GROUPS):
        mx = cur[0]
        for g in range(1, N_GROUPS):
            mx = jnp.maximum(mx, cur[g])
        found = jnp.zeros((1, tm), jnp.bool_)
        for g in range(N_GROUPS):
            hit = (cur[g] == mx) & jnp.logical_not(found)
            found = found | hit
            picked[g] = picked[g] | hit
            cur[g] = jnp.where(hit, -jnp.inf, cur[g])
    masked = jnp.concatenate(
        [jnp.where(picked[g], sb[g * epg:(g + 1) * epg, :], NEG) for g in range(N_GROUPS)], axis=0)
    iall = lax.broadcasted_iota(I32, (n_exp, tm), 0)
    e_rows, w_rows = [], []
    for _ in range(TOP_K):
        mx = jnp.max(masked, axis=0, keepdims=True)
        ix = jnp.min(jnp.where(masked == mx, iall, n_exp), axis=0, keepdims=True)
        hit = iall == ix
        w_rows.append(jnp.sum(jnp.where(hit, s, 0.0), axis=0, keepdims=True))
        e_rows.append(ix)
        masked = jnp.where(hit, -jnp.inf, masked)
    wsum = w_rows[0]
    for w in w_rows[1:]:
        wsum = wsum + w
    gates = [w / wsum * ROUTED_SCALE for w in w_rows]
    return jnp.concatenate(e_rows, axis=0), jnp.concatenate(gates, axis=0)


def _post_attn_kernel(xp, xs, ap, as_, cp, cs, gap, gas, gcp, gcs, woa, woc, wo, g1, b1, rwt, rb,
                      h2_o, e_o, gate_o, *, n_prompt_tiles):
    i = pl.program_id(0)
    is_p = i < n_prompt_tiles
    pick = lambda a, b: jnp.where(is_p, a[...], b[...])
    a = _dot(pick(ap, as_), woa[...])
    c = _dot(pick(cp, cs), woc[...])
    merged = pick(gap, gas).astype(F32) * a + pick(gcp, gcs).astype(F32) * c
    r = ALPHA * pick(xp, xs) + _dot(merged.astype(BF16), wo[...])
    h = _layer_norm(r, g1[...], b1[...])
    tm = h.shape[0]
    for k in range(D_MODEL // LANES):
        h2_o[pl.ds(k, tm, stride=SUBLANES), :] = h[:, k * LANES:(k + 1) * LANES]
    logits_t = lax.dot_general(rwt[...], h, (((1,), (1,)), ((), ())),
                               precision=lax.Precision.HIGHEST, preferred_element_type=F32)
    e_idx, gates = _route(logits_t, rb[...])
    e_o[...] = e_idx
    gate_o[...] = gates


def _post_attn(x_p, x_s, attn_p, attn_s, conv_p, conv_s, ga_p, ga_s, gc_p, gc_s,
               woa, woc, wo, g1, b1, rwt, rb):
    tp, ts = x_p.shape[0], x_s.shape[0]
    tm = TOKEN_TILE
    npt, nst = tp // tm, ts // tm
    n_tok = tp + ts
    p_spec = pl.BlockSpec((tm, D_MODEL), lambda i: (jnp.minimum(i, npt - 1), 0))
    s_spec = pl.BlockSpec((tm, D_MODEL), lambda i: (jnp.maximum(i - npt, 0), 0))
    consts = [woa, woc, wo, g1, b1, rwt, rb]
    return pl.pallas_call(
        functools.partial(_post_attn_kernel, n_prompt_tiles=npt),
        grid=(npt + nst,),
        in_specs=[p_spec, s_spec] * 5 + [_const_spec(c.shape) for c in consts],
        out_specs=[pl.BlockSpec((tm * SUBLANES, LANES), lambda i: (i, 0)),
                   pl.BlockSpec((TOP_K, tm), lambda i: (0, i)),
                   pl.BlockSpec((TOP_K, tm), lambda i: (0, i))],
        out_shape=[jax.ShapeDtypeStruct((n_tok * SUBLANES, LANES), F32),
                   jax.ShapeDtypeStruct((TOP_K, n_tok), I32),
                   jax.ShapeDtypeStruct((TOP_K, n_tok), F32)],
        compiler_params=_params(1),
        name="post_attn",
    )(x_p, x_s, attn_p, attn_s, conv_p, conv_s, ga_p, ga_s, gc_p, gc_s, *consts)


def _rank_kernel(e_ref, rank_o, cnt_o, carry_ref):
    i = pl.program_id(0)

    @pl.when(i == 0)
    def _():
        carry_ref[...] = jnp.zeros_like(carry_ref)

    n_exp = carry_ref.shape[0]
    e = e_ref[...]
    tk = e.shape[1]
    ie = lax.broadcasted_iota(I32, (n_exp, tk), 0)
    onehot = jnp.zeros((n_exp, tk), F32)
    for j in range(TOP_K):
        onehot = onehot + (ie == e[j:j + 1, :]).astype(F32)
    before = (lax.broadcasted_iota(I32, (tk, tk), 0) < lax.broadcasted_iota(I32, (tk, tk), 1)).astype(BF16)
    prefix = _dot(onehot.astype(BF16), before) + carry_ref[:, 0:1]
    rows = [jnp.sum(jnp.where(ie == e[j:j + 1, :], prefix, 0.0), axis=0, keepdims=True) for j in range(TOP_K)]
    rank_o[...] = jnp.concatenate(rows, axis=0).astype(I32)
    carry_ref[...] = carry_ref[...] + jnp.sum(onehot, axis=1, keepdims=True)
    cnt_o[...] = carry_ref[...].astype(I32)


def _rank(e_t, n_exp):
    n_tok = e_t.shape[1]
    tk = TOKEN_TILE
    return pl.pallas_call(
        _rank_kernel,
        grid=(n_tok // tk,),
        in_specs=[pl.BlockSpec((TOP_K, tk), lambda i: (0, i))],
        out_specs=[pl.BlockSpec((TOP_K, tk), lambda i: (0, i)),
                   pl.BlockSpec((n_exp, LANES), lambda i: (0, 0))],
        out_shape=[jax.ShapeDtypeStruct((TOP_K, n_tok), I32), jax.ShapeDtypeStruct((n_exp, LANES), I32)],
        scratch_shapes=[pltpu.VMEM((n_exp, LANES), F32)],
        compiler_params=_params(1),
        name="moe_rank",
    )(e_t)


def _slot_kernel(e_ref, rank_ref, pstart_ref, slot_o):
    e = e_ref[...]
    n_exp = pstart_ref.shape[0]
    tk = e.shape[1]
    ie = lax.broadcasted_iota(I32, (n_exp, tk), 0)
    ps = pstart_ref[:, 0:1]
    rows = [jnp.sum(jnp.where(ie == e[j:j + 1, :], ps, 0.0), axis=0, keepdims=True) for j in range(TOP_K)]
    slot_o[...] = rank_ref[...] + jnp.concatenate(rows, axis=0).astype(I32)


def _slots(e_t, rank_t, pstart_f):
    n_tok = e_t.shape[1]
    tk = TOKEN_TILE
    spec = pl.BlockSpec((TOP_K, tk), lambda i: (0, i))
    return pl.pallas_call(
        _slot_kernel,
        grid=(n_tok // tk,),
        in_specs=[spec, spec, _const_spec(pstart_f.shape)],
        out_specs=spec,
        out_shape=jax.ShapeDtypeStruct((TOP_K, n_tok), I32),
        compiler_params=_params(1),
        name="moe_slots",
    )(e_t, rank_t, pstart_f)


def _dispatch_kernel(pend_ref, padded_ref, slot_ref, h2_ref, xs_hbm, zbuf, sem, zsem, *, n_exp):
    i = pl.program_id(0)
    td = slot_ref.shape[1]
    blk_rows = MOE_BLOCK * SUBLANES

    def zero_copy(e):
        dst = pl.ds(pl.multiple_of((pend_ref[e] - MOE_BLOCK) * SUBLANES, SUBLANES), blk_rows)
        return pltpu.make_async_copy(zbuf, xs_hbm.at[dst], zsem)

    @pl.when(i == 0)
    def _():
        zbuf[...] = jnp.zeros_like(zbuf)

        def start(e, carry):
            @pl.when(padded_ref[e] > 0)
            def _():
                zero_copy(e).start()
            return carry

        def wait(e, carry):
            @pl.when(padded_ref[e] > 0)
            def _():
                zero_copy(e).wait()
            return carry

        lax.fori_loop(0, n_exp, start, 0)
        lax.fori_loop(0, n_exp, wait, 0)

    def row_copy(t, j):
        src = pl.ds(pl.multiple_of(t * SUBLANES, SUBLANES), SUBLANES)
        dst = pl.ds(pl.multiple_of(slot_ref[j, t] * SUBLANES, SUBLANES), SUBLANES)
        return pltpu.make_async_copy(h2_ref.at[src], xs_hbm.at[dst], sem)

    def start_tok(t, carry):
        for j in range(TOP_K):
            row_copy(t, j).start(priority=j % 2)
        return carry

    def wait_tok(t, carry):
        for j in range(TOP_K):
            row_copy(t, j).wait()
        return carry

    lax.fori_loop(0, td, start_tok, 0)
    lax.fori_loop(0, td, wait_tok, 0)


def _dispatch(pend, padded, slot_t, h2, m_pad):
    n_tok = slot_t.shape[1]
    td = TOKEN_TILE
    n_exp = pend.shape[0]
    grid_spec = pltpu.PrefetchScalarGridSpec(
        num_scalar_prefetch=2,
        grid=(n_tok // td,),
        in_specs=[pl.BlockSpec((TOP_K, td), lambda i, *_: (0, i), memory_space=pltpu.SMEM),
                  pl.BlockSpec((td * SUBLANES, LANES), lambda i, *_: (i, 0))],
        out_specs=pl.BlockSpec(memory_space=pl.ANY),
        scratch_shapes=[pltpu.VMEM((MOE_BLOCK * SUBLANES, LANES), F32),
                        pltpu.SemaphoreType.DMA(()), pltpu.SemaphoreType.DMA(())],
    )
    return pl.pallas_call(
        functools.partial(_dispatch_kernel, n_exp=n_exp),
        grid_spec=grid_spec,
        out_shape=jax.ShapeDtypeStruct((m_pad * SUBLANES, LANES), F32),
        compiler_params=_params(1),
        name="moe_dispatch",
    )(pend, padded, slot_t, h2)


def _tile_rows(ref, n):
    return jnp.concatenate([ref[pl.ds(k, n, stride=SUBLANES), :] for k in range(D_MODEL // LANES)], axis=1)


def _expert_kernel(be_ref, nused_ref, x_ref, wg_ref, wu_ref, wd_ref, y_ref, wgu_b, wd_b):
    i = pl.program_id(0)
    d_exp = wd_b.shape[0]

    @pl.when(i < nused_ref[0])
    def _():
        changed = (i == 0) | (be_ref[i] != be_ref[jnp.maximum(i - 1, 0)])

        @pl.when(changed)
        def _():
            wgu_b[:, :d_exp] = wg_ref[0].astype(BF16)
            wgu_b[:, d_exp:] = wu_ref[0].astype(BF16)
            wd_b[...] = wd_ref[0].astype(BF16)

        x = _tile_rows(x_ref, MOE_BLOCK).astype(BF16)
        gu = _dot(x, wgu_b[...])
        hh = jax.nn.silu(gu[:, :d_exp]) * gu[:, d_exp:]
        y = _dot(hh.astype(BF16), wd_b[...])
        for k in range(D_MODEL // LANES):
            y_ref[pl.ds(k, MOE_BLOCK, stride=SUBLANES), :] = y[:, k * LANES:(k + 1) * LANES]


def _experts(block_e, nused, xs, w_gate, w_up, w_down):
    n_blocks = block_e.shape[0]
    d_exp = w_gate.shape[2]
    rows = MOE_BLOCK * SUBLANES
    last = lambda i, nu: jnp.minimum(i, nu[0] - 1)
    grid_spec = pltpu.PrefetchScalarGridSpec(
        num_scalar_prefetch=2,
        grid=(n_blocks,),
        in_specs=[pl.BlockSpec((rows, LANES), lambda i, be, nu: (last(i, nu), 0)),
                  pl.BlockSpec((1, D_MODEL, d_exp), lambda i, be, nu: (be[i], 0, 0)),
                  pl.BlockSpec((1, D_MODEL, d_exp), lambda i, be, nu: (be[i], 0, 0)),
                  pl.BlockSpec((1, d_exp, D_MODEL), lambda i, be, nu: (be[i], 0, 0))],
        out_specs=pl.BlockSpec((rows, LANES), lambda i, be, nu: (last(i, nu), 0)),
        scratch_shapes=[pltpu.VMEM((D_MODEL, 2 * d_exp), BF16), pltpu.VMEM((d_exp, D_MODEL), BF16)],
    )
    return pl.pallas_call(
        _expert_kernel,
        grid_spec=grid_spec,
        out_shape=jax.ShapeDtypeStruct(xs.shape, F32),
        compiler_params=_params(1),
        name="moe_experts",
    )(block_e, nused, xs, w_gate, w_up, w_down)


def _combine_kernel(slot_ref, slot_next_ref, gate_ref, h2_ref, ys_hbm, shg, shu, shd, g2, b2, yp_o, ys_o,
                    buf, sem, *, n_prompt_tiles):
    i = pl.program_id(0)
    n = pl.num_programs(0)
    tc = gate_ref.shape[0]

    def start_tile(s_ref, bslot):
        def body(t, carry):
            dst = pl.ds(pl.multiple_of(t * SUBLANES, SUBLANES), SUBLANES)
            for j in range(TOP_K):
                src = pl.ds(pl.multiple_of(s_ref[j, t] * SUBLANES, SUBLANES), SUBLANES)
                pltpu.make_async_copy(ys_hbm.at[src], buf.at[bslot, j, dst], sem.at[bslot]).start(priority=j % 2)
            return carry
        lax.fori_loop(0, tc, body, 0)

    def wait_tile(bslot):
        for j in range(TOP_K):
            pltpu.make_async_copy(ys_hbm.at[pl.ds(0, tc * SUBLANES)], buf.at[bslot, j], sem.at[bslot]).wait()

    @pl.when(i == 0)
    def _():
        start_tile(slot_ref, 0)

    @pl.when(i + 1 < n)
    def _():
        start_tile(slot_next_ref, (i + 1) % 2)

    cur = i % 2
    h = _tile_rows(h2_ref, tc)
    hb = h.astype(BF16)
    shared = _dot((jax.nn.silu(_dot(hb, shg[...])) * _dot(hb, shu[...])).astype(BF16), shd[...])
    wait_tile(cur)
    routed = jnp.zeros((tc, D_MODEL), F32)
    for j in range(TOP_K):
        routed = routed + _tile_rows(buf.at[cur, j], tc) * gate_ref[:, j:j + 1]
    y = _layer_norm(ALPHA * h + (routed + shared), g2[...], b2[...])

    @pl.when(i < n_prompt_tiles)
    def _():
        yp_o[...] = y

    @pl.when(i >= n_prompt_tiles)
    def _():
        ys_o[...] = y


def _combine(slot_t, gates, h2, ys, shg, shu, shd, g2, b2, n_prompt, n_decode):
    tc = TOKEN_TILE
    npt, nst = n_prompt // tc, n_decode // tc
    consts = [shg, shu, shd, g2, b2]
    return pl.pallas_call(
        functools.partial(_combine_kernel, n_prompt_tiles=npt),
        grid=(npt + nst,),
        in_specs=[pl.BlockSpec((TOP_K, tc), lambda i: (0, i), memory_space=pltpu.SMEM),
                  pl.BlockSpec((TOP_K, tc), lambda i: (0, jnp.minimum(i + 1, npt + nst - 1)),
                               memory_space=pltpu.SMEM),
                  pl.BlockSpec((tc, TOP_K), lambda i: (i, 0)),
                  pl.BlockSpec((tc * SUBLANES, LANES), lambda i: (i, 0)),
                  pl.BlockSpec(memory_space=pl.ANY)] + [_const_spec(c.shape) for c in consts],
        out_specs=[pl.BlockSpec((tc, D_MODEL), lambda i: (jnp.minimum(i, npt - 1), 0)),
                   pl.BlockSpec((tc, D_MODEL), lambda i: (jnp.maximum(i - npt, 0), 0))],
        out_shape=[jax.ShapeDtypeStruct((n_prompt, D_MODEL), F32),
                   jax.ShapeDtypeStruct((n_decode, D_MODEL), F32)],
        scratch_shapes=[pltpu.VMEM((2, TOP_K, tc * SUBLANES, LANES), F32), pltpu.SemaphoreType.DMA((2,))],
        compiler_params=_params(1),
        name="moe_combine",
    )(slot_t, slot_t, gates, h2, ys, *consts)


def _pad_rows(a, n):
    return jnp.pad(a, ((0, n - a.shape[0]),) + ((0, 0),) * (a.ndim - 1))


def kernel(x_prompt, x_sample, cache_k, cache_v, cache_kidx, page_table, state_conv, w_in, conv_w, w_o_attn,
           w_o_conv, w_o, ln1_g, ln1_b, router_w, router_bias, moe_w_gate, moe_w_up, moe_w_down,
           shared_w_gate, shared_w_up, shared_w_down, ln2_g, ln2_b):
    nb, seq, _ = x_prompt.shape
    db = x_sample.shape[0]
    n_pool = cache_k.shape[1]
    n_exp = router_w.shape[-1]
    tp = nb * seq
    ts = TOKEN_TILE
    nkv = N_KV_HEADS * HEAD_DIM

    weights = _prep_in_weights(w_in[0])
    cw = conv_w[0]

    xp2 = x_prompt.reshape(tp, D_MODEL)
    (q_p, k_p, v_p, kvb_p, qi_p, idx_p, kib_p, conv_p, ga_p, gc_p, ulast_p) = _in_proj_prompt(xp2, weights, cw, nb)
    xs2 = _pad_rows(x_sample.reshape(db, D_MODEL), ts)
    prev2 = _pad_rows(state_conv[0, :, 0, :], ts)
    prev1 = _pad_rows(state_conv[0, :, 1, :], ts)
    (q_s, k_s, v_s, _, qi_s, idx_s, _, conv_s, ga_s, gc_s, u_s) = _in_proj_decode(xs2, prev2, prev1, weights, cw)

    attn_p = _attn_prompt(q_p, kvb_p, qi_p, kib_p, idx_p, nb)
    q8 = q_s[:db].reshape(db, N_HEADS, HEAD_DIM)
    qi8 = qi_s[:db].reshape(db, IDX_HEADS, LANES)[:, :, :IDX_DIM]
    wi8 = idx_s[:db, IDX_DIM:IDX_DIM + IDX_HEADS].reshape(db, IDX_HEADS, 1)
    kin = idx_s[:db, :IDX_DIM].astype(BF16).reshape(db, 1, IDX_DIM)
    kn8 = jnp.repeat(k_s[:db].reshape(db, N_KV_HEADS, HEAD_DIM), GROUP, axis=1).astype(BF16)
    vn8 = jnp.repeat(v_s[:db].reshape(db, N_KV_HEADS, HEAD_DIM), GROUP, axis=1).astype(BF16)
    ck2 = cache_k[0].reshape(n_pool * PAGE_SIZE * N_KV_HEADS, HEAD_DIM)
    cv2 = cache_v[0].reshape(n_pool * PAGE_SIZE * N_KV_HEADS, HEAD_DIM)
    keys_s, knew_s = _decode_scores(page_table, qi8, wi8, kin, jnp.swapaxes(cache_kidx[0], 1, 2))
    meta = _decode_pick(keys_s, knew_s, page_table.astype(F32)[:, :, None])
    attn_s8 = _decode_attend(meta[:, 0, :], q8, meta, kn8, vn8, ck2, cv2)
    attn_s = _pad_rows(attn_s8.reshape(db, N_HEADS * HEAD_DIM), ts)

    h2, e_t, gate_t = _post_attn(
        xp2, xs2, attn_p, attn_s, conv_p, conv_s, ga_p, ga_s, gc_p, gc_s,
        w_o_attn[0].astype(BF16), w_o_conv[0].astype(BF16), w_o[0].astype(BF16),
        ln1_g[0].reshape(1, D_MODEL), ln1_b[0].reshape(1, D_MODEL),
        router_w[0].T, router_bias[0].reshape(n_exp, 1))

    n_tok = tp + ts
    rank_t, cnt = _rank(e_t, n_exp)
    counts = cnt[:, 0]
    padded = (counts + MOE_BLOCK - 1) // MOE_BLOCK * MOE_BLOCK
    pend = jnp.cumsum(padded)
    pstart = pend - padded
    n_blocks = (n_tok * TOP_K + n_exp * (MOE_BLOCK - 1) + MOE_BLOCK - 1) // MOE_BLOCK
    nused = (pend[-1] // MOE_BLOCK).astype(I32)
    blk = jnp.minimum(jnp.arange(n_blocks, dtype=I32), nused - 1)
    block_e = jnp.minimum(jnp.sum((pend[None, :] <= (blk * MOE_BLOCK)[:, None]).astype(I32), axis=1), n_exp - 1)
    slot_t = _slots(e_t, rank_t, jnp.broadcast_to(pstart.astype(F32)[:, None], (n_exp, LANES)))
    xs = _dispatch(pend.astype(I32), padded.astype(I32), slot_t, h2, n_blocks * MOE_BLOCK)
    ys = _experts(block_e, nused.reshape(1), xs, moe_w_gate[0], moe_w_up[0], moe_w_down[0])
    y_p, y_s = _combine(slot_t, gate_t.T, h2, ys,
                        shared_w_gate[0].astype(BF16), shared_w_up[0].astype(BF16),
                        shared_w_down[0].astype(BF16),
                        ln2_g[0].reshape(1, D_MODEL), ln2_b[0].reshape(1, D_MODEL), tp, ts)

    conv_sample = jnp.stack([state_conv[0, :, 1, :], u_s[:db]], axis=1)[None]
    return (y_p.reshape(nb, seq, D_MODEL),
            y_s[:db].reshape(db, 1, D_MODEL),
            k_p.reshape(1, nb, seq, N_KV_HEADS, HEAD_DIM),
            v_p.reshape(1, nb, seq, N_KV_HEADS, HEAD_DIM),
            idx_p[:, :IDX_DIM].reshape(1, nb, seq, IDX_DIM),
            ulast_p[:, SUBLANES - (CONV_W - 1):, :][None],
            k_s[:db].reshape(1, db, 1, N_KV_HEADS, HEAD_DIM),
            v_s[:db].reshape(1, db, 1, N_KV_HEADS, HEAD_DIM),
            idx_s[:db, :IDX_DIM].reshape(1, db, 1, IDX_DIM),
            conv_sample)
```

```python
import functools

import jax
import jax.numpy as jnp
import numpy as np
from jax import lax
from jax.experimental import pallas as pl
from jax.experimental.pallas import tpu as pltpu

F32 = jnp.float32
BF16 = jnp.bfloat16
I32 = jnp.int32

D_MODEL = 1024
N_HEADS = 8
HEAD_DIM = 128
N_KV_HEADS = 2
GROUP = N_HEADS // N_KV_HEADS
IDX_HEADS = 8
IDX_DIM = 64
TOPK_KEYS = 256
IDX_SCALE = IDX_DIM ** -0.5 * IDX_HEADS ** -0.5
Q_SCALE = HEAD_DIM ** -0.5
PAGE_SIZE = 128
D_CONV = D_MODEL
CONV_W = 3
TOP_K = 8
N_GROUPS = 8
TOPK_GROUPS = 4
ROUTED_SCALE = 2.5
DEPTH = 1
ALPHA = (2 * DEPTH) ** 0.25
LN_EPS = 1e-5
NEG = -1e30
IN_WIDTHS = (N_HEADS * HEAD_DIM, N_KV_HEADS * HEAD_DIM, N_KV_HEADS * HEAD_DIM,
             IDX_HEADS * IDX_DIM, IDX_DIM, IDX_HEADS,
             D_CONV, D_CONV, D_CONV, D_MODEL, D_MODEL)

LANES = 128
SUBLANES = 8
VMEM_LIMIT_BYTES = 56 * 1024 * 1024

PROJ_ROWS = 512
ATT_Q = 256
SEARCH_ROWS = 128
TOKEN_TILE = 256
MOE_BLOCK = 256
CONV_CHUNK = 256
INT_MIN = -2 ** 31
SOFTMAX_DENOM_MIN = 1e-25
F32_MIN_NORMAL = float(np.finfo(np.float32).tiny)
BF16_MIN_NORMAL_BITS = 0x0080


def _np_key(v):
    b = int(np.float32(v).view(np.int32))
    return b ^ ((b >> 31) & 0x7FFFFFFF)


KEY_HALF = _np_key(NEG * 0.5)


def _sort_key(s):
    b = lax.bitcast_convert_type(s + 0.0, I32)
    return b ^ ((b >> 31) & jnp.int32(0x7FFFFFFF))


def _dot(a, b):
    return jnp.dot(a, b, preferred_element_type=F32)


def _dot_nt(a, b):
    return lax.dot_general(a, b, (((1,), (1,)), ((), ())), preferred_element_type=F32)


def _params(n_grid):
    return pltpu.CompilerParams(dimension_semantics=("arbitrary",) * n_grid,
                                vmem_limit_bytes=VMEM_LIMIT_BYTES)


def _const_spec(shape):
    nd = len(shape)
    return pl.BlockSpec(shape, lambda *_: (0,) * nd, pipeline_mode=pl.Buffered(1))


def _proj_common(xb, w1, w2, w3, wga, wgc, q_o, k_o, v_o, kv_o, qi_o, idx_o, kib_o, sga_o, sgc_o):
    nq = N_HEADS * HEAD_DIM
    nkv = N_KV_HEADS * HEAD_DIM
    z1 = _dot(xb, w1[...])
    q_o[...] = (z1[:, :nq] * Q_SCALE).astype(BF16)
    k = z1[:, nq:nq + nkv]
    v = z1[:, nq + nkv:nq + 2 * nkv]
    k_o[...] = k
    v_o[...] = v
    kv_o[:, :nkv] = k.astype(BF16)
    kv_o[:, nkv:] = v.astype(BF16)
    qi_o[...] = _dot(xb, w2[...]).astype(BF16)
    z3 = _dot(xb, w3[...])
    idx_o[...] = z3
    kib_o[...] = z3.astype(BF16)
    sga_o[...] = jax.nn.sigmoid(_dot(xb, wga[...])).astype(BF16)
    sgc_o[...] = jax.nn.sigmoid(_dot(xb, wgc[...])).astype(BF16)


def _in_proj_prompt_kernel(x_ref, w1, w2, w3, wb, wc, wx, wga, wgc, cw_ref,
                           q_o, k_o, v_o, kv_o, qi_o, idx_o, kib_o, conv_o, sga_o, sgc_o, ulast_o,
                           tail_ref):
    j = pl.program_id(1)

    @pl.when(j == 0)
    def _():
        tail_ref[...] = jnp.zeros_like(tail_ref)

    xb = x_ref[...].astype(BF16)
    _proj_common(xb, w1, w2, w3, wga, wgc, q_o, k_o, v_o, kv_o, qi_o, idx_o, kib_o, sga_o, sgc_o)
    tm = xb.shape[0]
    row = lax.broadcasted_iota(I32, (tm, CONV_CHUNK), 0)
    for c in range(D_CONV // CONV_CHUNK):
        cs = slice(c * CONV_CHUNK, (c + 1) * CONV_CHUNK)
        gb = _dot(xb, wb[:, cs])
        u = _dot(xb, wc[:, cs]) * _dot(xb, wx[:, cs])
        p1 = tail_ref[SUBLANES - 1:SUBLANES, cs]
        p2 = tail_ref[SUBLANES - 2:SUBLANES - 1, cs]
        u1 = jnp.where(row == 0, p1, pltpu.roll(u, 1, 0))
        u2 = jnp.where(row == 0, p2, jnp.where(row == 1, p1, pltpu.roll(u, 2, 0)))
        conv = cw_ref[0:1, cs] * u2 + cw_ref[1:2, cs] * u1 + cw_ref[2:3, cs] * u
        conv_o[:, cs] = (gb * conv).astype(BF16)
        tail_ref[:, cs] = u[tm - SUBLANES:, :]
    ulast_o[0] = tail_ref[...]


def _in_proj_decode_kernel(x_ref, p2_ref, p1_ref, w1, w2, w3, wb, wc, wx, wga, wgc, cw_ref,
                           q_o, k_o, v_o, kv_o, qi_o, idx_o, kib_o, conv_o, sga_o, sgc_o, u_o):
    xb = x_ref[...].astype(BF16)
    _proj_common(xb, w1, w2, w3, wga, wgc, q_o, k_o, v_o, kv_o, qi_o, idx_o, kib_o, sga_o, sgc_o)
    for c in range(D_CONV // CONV_CHUNK):
        cs = slice(c * CONV_CHUNK, (c + 1) * CONV_CHUNK)
        gb = _dot(xb, wb[:, cs])
        u = _dot(xb, wc[:, cs]) * _dot(xb, wx[:, cs])
        conv = cw_ref[0:1, cs] * p2_ref[:, cs] + cw_ref[1:2, cs] * p1_ref[:, cs] + cw_ref[2:3, cs] * u
        conv_o[:, cs] = (gb * conv).astype(BF16)
        u_o[:, cs] = u


def _prep_in_weights(w_in):
    w = w_in.astype(BF16)
    offs = np.concatenate([[0], np.cumsum(IN_WIDTHS)])
    part = [w[:, int(offs[i]):int(offs[i + 1])] for i in range(len(IN_WIDTHS))]
    q, k, v, qi, ki, wi, gb, gc, xv, ga, gcv = part
    d = w.shape[0]
    w1 = jnp.concatenate([q, k, v], axis=1)
    w2 = jnp.pad(qi.reshape(d, IDX_HEADS, IDX_DIM),
                 ((0, 0), (0, 0), (0, LANES - IDX_DIM))).reshape(d, IDX_HEADS * LANES)
    w3 = jnp.concatenate([ki, wi, jnp.zeros((d, LANES - IDX_DIM - IDX_HEADS), BF16)], axis=1)
    return (w1, w2, w3, gb, gc, xv, ga, gcv)


def _proj_out_shapes(t):
    nkv = N_KV_HEADS * HEAD_DIM
    sd = jax.ShapeDtypeStruct
    return [sd((t, N_HEADS * HEAD_DIM), BF16), sd((t, nkv), F32), sd((t, nkv), F32), sd((t, 2 * nkv), BF16),
            sd((t, IDX_HEADS * LANES), BF16), sd((t, LANES), F32), sd((t, LANES), BF16),
            sd((t, D_CONV), BF16), sd((t, D_MODEL), BF16), sd((t, D_MODEL), BF16)]


def _in_proj_prompt(x2d, weights, conv_w, nb):
    t = x2d.shape[0]
    s = t // nb
    tm = min(PROJ_ROWS, s)
    nj = s // tm
    row_spec = lambda w: pl.BlockSpec((tm, w), lambda b, j: (b * nj + j, 0))
    out_shapes = _proj_out_shapes(t) + [jax.ShapeDtypeStruct((nb, SUBLANES, D_CONV), F32)]
    out_specs = [row_spec(o.shape[1]) for o in out_shapes[:-1]]
    out_specs.append(pl.BlockSpec((1, SUBLANES, D_CONV), lambda b, j: (b, 0, 0)))
    return pl.pallas_call(
        _in_proj_prompt_kernel,
        grid=(nb, nj),
        in_specs=[row_spec(D_MODEL)] + [_const_spec(w.shape) for w in weights] + [_const_spec(conv_w.shape)],
        out_specs=out_specs,
        out_shape=out_shapes,
        scratch_shapes=[pltpu.VMEM((SUBLANES, D_CONV), F32)],
        compiler_params=_params(2),
        name="in_proj_prompt",
    )(x2d, *weights, conv_w)


def _in_proj_decode(x2d, p2, p1, weights, conv_w):
    t = x2d.shape[0]
    row_spec = lambda w: pl.BlockSpec((t, w), lambda i: (0, 0))
    out_shapes = _proj_out_shapes(t) + [jax.ShapeDtypeStruct((t, D_CONV), F32)]
    return pl.pallas_call(
        _in_proj_decode_kernel,
        grid=(1,),
        in_specs=[row_spec(D_MODEL)] * 3 + [_const_spec(w.shape) for w in weights] + [_const_spec(conv_w.shape)],
        out_specs=[row_spec(o.shape[1]) for o in out_shapes],
        out_shape=out_shapes,
        compiler_params=_params(1),
        name="in_proj_decode",
    )(x2d, p2, p1, *weights, conv_w)


def _attn_prompt_kernel(q_ref, kv_ref, qi_ref, kib_ref, idx_ref, o_ref,
                        key_ref, hi_ref, bias_ref, t_ref, j_ref, need_ref, m_ref, acc_ref, knorm_ref,
                        *, n_keep, seq_len):
    tq = q_ref.shape[0]
    i = pl.program_id(1)
    nchunk = i + 1
    nsub = tq // LANES
    lane = lax.broadcasted_iota(I32, (tq, LANES), 1)

    wi_cols = [idx_ref[:, IDX_DIM + h:IDX_DIM + h + 1] * IDX_SCALE for h in range(IDX_HEADS)]
    row_i = lax.broadcasted_iota(I32, (tq, tq), 0)
    col_i = lax.broadcasted_iota(I32, (tq, tq), 1)

    def score_chunk(c, carry):
        kic = kib_ref[pl.ds(pl.multiple_of(c * tq, tq), tq), :]
        acc = jnp.zeros((tq, tq), F32)
        for h in range(IDX_HEADS):
            d = _dot_nt(qi_ref[:, h * LANES:(h + 1) * LANES], kic)
            acc = acc + jnp.maximum(d, 0.0) * wi_cols[h]
        acc = jnp.where(jnp.abs(acc) < F32_MIN_NORMAL, 0.0, acc)
        s = jnp.where((c < i) | (col_i <= row_i), acc, NEG)
        bits = lax.bitcast_convert_type(s, I32)
        key_ref[c] = bits ^ ((bits >> 31) & jnp.int32(0x7FFFFFFF))
        hi_ref[c] = lax.bitcast_convert_type(bits & jnp.int32(-65536), F32).astype(BF16)
        return carry

    lax.fori_loop(0, nchunk, score_chunk, 0)

    rb = SEARCH_ROWS

    def count(r0, pred):
        def body(c, cnt):
            for k in range(nsub):
                blk = key_ref[c, r0:r0 + rb, k * LANES:(k + 1) * LANES]
                cnt = cnt + pred(blk, c * tq + k * LANES).astype(I32)
            return cnt
        cnt = lax.fori_loop(0, nchunk, body, jnp.zeros((rb, LANES), I32))
        return jnp.sum(cnt, axis=1, keepdims=True)

    def count_hi(r0, cand_b):
        def body(c, cnt):
            for k in range(nsub):
                blk = hi_ref[c, r0:r0 + rb, k * LANES:(k + 1) * LANES]
                cnt = cnt + jnp.where(blk >= cand_b, jnp.ones_like(cnt), jnp.zeros_like(cnt))
            return cnt
        cnt = lax.fori_loop(0, nchunk, body, jnp.zeros((rb, LANES), BF16))
        return jnp.sum(cnt.astype(F32), axis=1, keepdims=True)

    any_excess = jnp.int32(0)
    for r in range(tq // rb):
        r0 = r * rb

        def hi_bit_body(bi, p16):
            cand = p16 + lax.shift_left(jnp.int32(1), 15 - bi)
            raw = cand ^ ((cand >> 31) & jnp.int32(0x7FFF))
            raw = jnp.where((raw > 0) & (raw < BF16_MIN_NORMAL_BITS), BF16_MIN_NORMAL_BITS, raw)
            cand_f = lax.bitcast_convert_type(lax.shift_left(raw, 16), F32)
            cand_b = jnp.broadcast_to(cand_f, (rb, LANES)).astype(BF16)
            return jnp.where(count_hi(r0, cand_b) >= n_keep, cand, p16)

        p16 = lax.fori_loop(0, 16, hi_bit_body, jnp.full((rb, 1), -2 ** 15, I32))

        def bit_body(bi, t):
            cand = t + lax.shift_left(jnp.int32(1), 15 - bi)
            cand_b = jnp.broadcast_to(cand, (rb, LANES))
            cnt = count(r0, lambda blk, _: blk >= cand_b)
            return jnp.where(cnt >= n_keep, cand, t)

        t = lax.fori_loop(0, 16, bit_body, lax.shift_left(p16, 16))
        t_b = jnp.broadcast_to(t, (rb, LANES))
        cnt_gt = count(r0, lambda blk, _: blk > t_b)
        cnt_eq = count(r0, lambda blk, _: blk == t_b)
        need = n_keep - cnt_gt
        excess = (cnt_eq > need) & (t > KEY_HALF)
        any_excess = any_excess + jnp.max(excess.astype(I32))
        t_ref[r0:r0 + rb, :] = t_b
        need_ref[r0:r0 + rb, :] = jnp.broadcast_to(need, (rb, LANES))

    j_ref[...] = jnp.full(j_ref.shape, seq_len, I32)

    @pl.when(any_excess > 0)
    def _():
        nbits = max(1, (seq_len - 1).bit_length())
        lane_rb = lax.broadcasted_iota(I32, (rb, LANES), 1)
        for r in range(tq // rb):
            r0 = r * rb
            t_b = t_ref[r0:r0 + rb, :]
            need = need_ref[r0:r0 + rb, 0:1]

            def jbit(bi, jj):
                cand = jj | lax.shift_left(jnp.int32(1), nbits - 1 - bi)
                cand_b = jnp.broadcast_to(cand, (rb, LANES))
                g = count(r0, lambda blk, base: (blk == t_b) & ((base + lane_rb) < cand_b))
                return jnp.where(g < need, cand, jj)

            jj = lax.fori_loop(0, nbits, jbit, jnp.zeros((rb, 1), I32))
            j_ref[r0:r0 + rb, :] = jnp.broadcast_to(jj, (rb, LANES))

    t_all = t_ref[...]
    j_all = j_ref[...]

    def bias_chunk(c, carry):
        for k in range(nsub):
            blk = key_ref[c, :, k * LANES:(k + 1) * LANES]
            col = c * tq + k * LANES + lane
            sel = (blk > KEY_HALF) & ((blk > t_all) | ((blk == t_all) & (col <= j_all)))
            bias_ref[c, :, k * LANES:(k + 1) * LANES] = jnp.where(sel, 0.0, NEG)
        return carry

    lax.fori_loop(0, nchunk, bias_chunk, 0)

    nkv = N_KV_HEADS * HEAD_DIM
    gq = GROUP * tq

    def group_logits(c, g):
        rows = pl.ds(pl.multiple_of(c * tq, tq), tq)
        kc = kv_ref[rows, g * HEAD_DIM:(g + 1) * HEAD_DIM]
        qg = jnp.concatenate([q_ref[:, h * HEAD_DIM:(h + 1) * HEAD_DIM]
                              for h in range(g * GROUP, (g + 1) * GROUP)], axis=0)
        s = _dot_nt(qg, kc).reshape(GROUP, tq, tq) + bias_ref[c][None]
        return s.reshape(gq, tq)

    def exact_row_max():
        m_ref[...] = jnp.full(m_ref.shape, NEG, F32)

        def max_chunk(c, carry):
            for g in range(N_KV_HEADS):
                s = group_logits(c, g)
                mx = s[:, :LANES]
                for k in range(1, nsub):
                    mx = jnp.maximum(mx, s[:, k * LANES:(k + 1) * LANES])
                m_ref[g] = jnp.maximum(m_ref[g], mx)
            return carry

        lax.fori_loop(0, nchunk, max_chunk, 0)
        for g in range(N_KV_HEADS):
            m_ref[g] = jnp.broadcast_to(jnp.max(m_ref[g], axis=1, keepdims=True), (gq, LANES))

    ones = jnp.ones((tq, HEAD_DIM), BF16)

    def accumulate():
        acc_ref[...] = jnp.zeros_like(acc_ref)

        def pv_chunk(c, carry):
            rows = pl.ds(pl.multiple_of(c * tq, tq), tq)
            for g in range(N_KV_HEADS):
                s = group_logits(c, g)
                m = m_ref[g]
                p = jnp.concatenate([jnp.exp(s[:, k * LANES:(k + 1) * LANES] - m) for k in range(nsub)], axis=1)
                vc = kv_ref[rows, nkv + g * HEAD_DIM:nkv + (g + 1) * HEAD_DIM]
                acc_ref[g] = acc_ref[g] + _dot(p.astype(BF16), jnp.concatenate([vc, ones], axis=1))
            return carry

        lax.fori_loop(0, nchunk, pv_chunk, 0)

    @pl.when(i == 0)
    def _():
        for g in range(N_KV_HEADS):
            def norm_chunk(c, mx):
                kc = kv_ref[pl.ds(pl.multiple_of(c * tq, tq), tq), g * HEAD_DIM:(g + 1) * HEAD_DIM].astype(F32)
                return jnp.maximum(mx, jnp.sum(kc * kc, axis=1, keepdims=True))
            k2 = lax.fori_loop(0, seq_len // tq, norm_chunk, jnp.zeros((tq, 1), F32))
            knorm_ref[g] = jnp.broadcast_to(jnp.sqrt(jnp.max(k2, axis=0, keepdims=True)), (SUBLANES, LANES))

    for h in range(N_HEADS):
        g, r = h // GROUP, h % GROUP
        qh = q_ref[:, h * HEAD_DIM:(h + 1) * HEAD_DIM].astype(F32)
        qn = jnp.sqrt(jnp.sum(qh * qh, axis=1, keepdims=True))
        m_ref[g, r * tq:(r + 1) * tq, :] = qn * knorm_ref[g][0:1, :]
    accumulate()
    denom_min = jnp.min(acc_ref[:, :, HEAD_DIM:HEAD_DIM + 1])

    @pl.when(denom_min < SOFTMAX_DENOM_MIN)
    def _():
        exact_row_max()
        accumulate()

    for h in range(N_HEADS):
        a = acc_ref[h // GROUP, (h % GROUP) * tq:(h % GROUP + 1) * tq, :]
        o_ref[:, h * HEAD_DIM:(h + 1) * HEAD_DIM] = (a[:, :HEAD_DIM] / a[:, HEAD_DIM:]).astype(BF16)


def _attn_prompt(q, kvb, qi, kib, idx, nb):
    t = q.shape[0]
    s = t // nb
    tq = min(ATT_Q, s)
    nq = s // tq
    n_keep = min(TOPK_KEYS, s // 4)
    blk = lambda w: pl.BlockSpec((tq, w), lambda b, i: (b * nq + i, 0))
    seq = lambda w: pl.BlockSpec((s, w), lambda b, i: (b, 0))
    return pl.pallas_call(
        functools.partial(_attn_prompt_kernel, n_keep=n_keep, seq_len=s),
        grid=(nb, nq),
        in_specs=[blk(q.shape[1]), seq(kvb.shape[1]), blk(qi.shape[1]), seq(kib.shape[1]), blk(idx.shape[1])],
        out_specs=blk(N_HEADS * HEAD_DIM),
        out_shape=jax.ShapeDtypeStruct((t, N_HEADS * HEAD_DIM), BF16),
        scratch_shapes=[pltpu.VMEM((nq, tq, tq), I32), pltpu.VMEM((nq, tq, tq), BF16),
                        pltpu.VMEM((nq, tq, tq), F32),
                        pltpu.VMEM((tq, LANES), I32), pltpu.VMEM((tq, LANES), I32), pltpu.VMEM((tq, LANES), I32),
                        pltpu.VMEM((N_KV_HEADS, GROUP * tq, LANES), F32),
                        pltpu.VMEM((N_KV_HEADS, GROUP * tq, 2 * HEAD_DIM), F32),
                        pltpu.VMEM((N_KV_HEADS, SUBLANES, LANES), F32)],
        compiler_params=_params(2),
        name="attn_prompt",
    )(q, kvb, qi, kib, idx)


DECODE_PAGES_PER_ROW = 4


def _decode_scores_kernel(pt_ref, qi_ref, wi_ref, kin_ref, cx_hbm, key_o, knew_o, xbuf, sem, *, n_pages):
    b = pl.program_id(0)
    nb = pl.num_programs(0)
    ppr = DECODE_PAGES_PER_ROW
    nrow = n_pages // ppr

    def page_copy(bb, p, slot):
        return pltpu.make_async_copy(cx_hbm.at[pt_ref[bb, p]], xbuf.at[slot, p], sem.at[slot])

    def start_batch(bb, slot):
        def body(p, carry):
            page_copy(bb, p, slot).start()
            return carry
        lax.fori_loop(0, n_pages, body, 0)

    def wait_batch(bb, slot):
        def body(p, carry):
            page_copy(bb, p, slot).wait()
            return carry
        lax.fori_loop(0, n_pages, body, 0)

    @pl.when(b == 0)
    def _():
        start_batch(0, 0)

    @pl.when(b + 1 < nb)
    def _():
        start_batch(b + 1, (b + 1) % 2)

    slot = b % 2
    wait_batch(b, slot)

    qi8 = qi_ref[0]
    wi8 = wi_ref[0] * IDX_SCALE

    def score_row(r, carry):
        kx = jnp.concatenate([xbuf[slot, r * ppr + k] for k in range(ppr)], axis=1).astype(BF16)
        d = _dot(qi8, kx)
        sc = jnp.sum(jnp.maximum(d, 0.0) * wi8, axis=0, keepdims=True)
        key_o[0, pl.ds(r, 1), :] = _sort_key(sc)
        return carry

    lax.fori_loop(0, nrow, score_row, 0, unroll=4)
    dn = jnp.sum(qi8.astype(F32) * kin_ref[0].astype(F32), axis=1, keepdims=True)
    key_new = _sort_key(jnp.sum(jnp.maximum(dn, 0.0) * wi8, axis=0, keepdims=True))
    knew_o[0] = jnp.broadcast_to(key_new, (1, LANES))


def _decode_scores(page_table, qi8, wi8, kin, cxt):
    db, n_pages = page_table.shape
    nrow = n_pages // DECODE_PAGES_PER_ROW
    w = DECODE_PAGES_PER_ROW * PAGE_SIZE
    per_b = lambda a: pl.BlockSpec((1,) + a.shape[1:], lambda b, pt: (b, 0, 0))
    grid_spec = pltpu.PrefetchScalarGridSpec(
        num_scalar_prefetch=1,
        grid=(db,),
        in_specs=[per_b(qi8), per_b(wi8), per_b(kin), pl.BlockSpec(memory_space=pl.ANY)],
        out_specs=[pl.BlockSpec((1, nrow, w), lambda b, pt: (b, 0, 0)),
                   pl.BlockSpec((1, 1, LANES), lambda b, pt: (b, 0, 0))],
        scratch_shapes=[pltpu.VMEM((2, n_pages, IDX_DIM, PAGE_SIZE), F32), pltpu.SemaphoreType.DMA((2,))],
    )
    return pl.pallas_call(
        functools.partial(_decode_scores_kernel, n_pages=n_pages),
        grid_spec=grid_spec,
        out_shape=[jax.ShapeDtypeStruct((db, nrow, w), I32), jax.ShapeDtypeStruct((db, 1, LANES), I32)],
        compiler_params=_params(1),
        name="decode_scores",
    )(page_table, qi8, wi8, kin, cxt)


def _decode_pick_kernel(keys_ref, knew_ref, pt_ref, meta_o, *, n_keep):
    nb, nrow, w = keys_ref.shape
    past = nrow * w
    n_pages = pt_ref.shape[1]
    keys = keys_ref[...]
    key_new = knew_ref[...][:, :, 0:1]
    col = (lax.broadcasted_iota(I32, keys.shape, 1) * w + lax.broadcasted_iota(I32, keys.shape, 2))

    def total(x):
        c = x.astype(I32)
        part = c[:, :, :LANES]
        for k in range(1, w // LANES):
            part = part + c[:, :, k * LANES:(k + 1) * LANES]
        return jnp.sum(jnp.sum(part, axis=1, keepdims=True), axis=2, keepdims=True)

    def bit_body(bi, t):
        cand = t + lax.shift_left(jnp.int32(1), 31 - bi)
        cnt = total(keys >= cand) + (key_new >= cand).astype(I32)
        return jnp.where(cnt >= n_keep, cand, t)

    t = lax.fori_loop(0, 32, bit_body, jnp.full((nb, 1, 1), INT_MIN, I32))
    need = n_keep - (total(keys > t) + (key_new > t).astype(I32))
    nbits = past.bit_length()

    def jbit(bi, jj):
        cand = jj | lax.shift_left(jnp.int32(1), nbits - 1 - bi)
        g = total((keys == t) & (col < cand)) + ((key_new == t) & (past < cand)).astype(I32)
        return jnp.where(g < need, cand, jj)

    jj = lax.fori_loop(0, nbits, jbit, jnp.zeros((nb, 1, 1), I32))
    sel = (keys > t) | ((keys == t) & (col <= jj))
    sel_new = ((key_new > t) | ((key_new == t) & (past <= jj))).astype(I32)

    rows = nb * nrow
    incl_lane = (lax.broadcasted_iota(I32, (w, w), 0) <= lax.broadcasted_iota(I32, (w, w), 1)).astype(BF16)
    cnt_in_row = _dot(sel.astype(BF16).reshape(rows, w), incl_lane)
    row_tot = cnt_in_row[:, w - 1:w]
    ri = lax.broadcasted_iota(I32, (rows, rows), 0)
    rj = lax.broadcasted_iota(I32, (rows, rows), 1)
    earlier_row_same_seq = ((ri // nrow) == (rj // nrow)) & (rj < ri)
    row_off = _dot(earlier_row_same_seq.astype(BF16),
                   jnp.broadcast_to(row_tot, (rows, LANES)).astype(BF16))[:, 0:1]
    cnt3 = cnt_in_row.astype(BF16).reshape(nb, nrow, w)
    off3 = row_off.reshape(nb, nrow, 1)
    incl3 = (row_off + row_tot).reshape(nb, nrow, 1)
    jl = lax.broadcasted_iota(I32, (1, 1, n_keep), 2).astype(F32)
    row_j = jnp.sum((incl3 <= jl).astype(F32), axis=1, keepdims=True)
    onehot = lax.broadcasted_iota(I32, (nb, nrow, n_keep), 1).astype(F32) == row_j
    off_j = jnp.sum(jnp.where(onehot, off3, 0.0), axis=1, keepdims=True)
    local_rank = jl - off_j
    onehot_b = onehot.astype(BF16)
    page_i = lax.broadcasted_iota(I32, (n_pages, n_keep), 0)
    row_i = lax.broadcasted_iota(I32, (SUBLANES, n_keep), 0)
    for b in range(nb):
        cnt_j = lax.dot_general(cnt3[b], onehot_b[b], (((0,), (0,)), ((), ())),
                                preferred_element_type=F32)
        lane_j = jnp.sum((cnt_j <= local_rank[b]).astype(F32), axis=0, keepdims=True)
        pos = jnp.minimum((row_j[b] * w + lane_j).astype(I32), past - 1)
        phys = jnp.sum(jnp.where(page_i == pos // PAGE_SIZE, pt_ref[b], 0.0), axis=0, keepdims=True)
        tile = (phys.astype(I32) * PAGE_SIZE + pos % PAGE_SIZE) // (SUBLANES // N_KV_HEADS)
        meta_o[b] = jnp.where(row_i == 0, tile,
                              jnp.where(row_i == 1, sel_new[b], pos % (SUBLANES // N_KV_HEADS)))


def _decode_pick(keys, knew, pt_f):
    nb, nrow, w = keys.shape
    n_keep = min(TOPK_KEYS, (nrow * w + 1) // 4)
    full = lambda a: pl.BlockSpec(a.shape, lambda i: (0,) * a.ndim)
    return pl.pallas_call(
        functools.partial(_decode_pick_kernel, n_keep=n_keep),
        grid=(1,),
        in_specs=[full(keys), full(knew), full(pt_f)],
        out_specs=pl.BlockSpec((nb, SUBLANES, n_keep), lambda i: (0, 0, 0)),
        out_shape=jax.ShapeDtypeStruct((nb, SUBLANES, n_keep), I32),
        compiler_params=_params(1),
        name="decode_pick",
    )(keys, knew, pt_f)


def _decode_attend_kernel(tile_ref, q_ref, meta_ref, kn_ref, vn_ref, ck_hbm, cv_hbm, o_ref,
                          kbuf, vbuf, expand_ref, sem, *, n_keep):
    b = pl.program_id(0)
    nb = pl.num_programs(0)
    ncol = n_keep * SUBLANES

    def item_copies(bb, j, slot):
        src = pl.ds(pl.multiple_of(tile_ref[bb, j] * SUBLANES, SUBLANES), SUBLANES)
        dst = pl.ds(pl.multiple_of(j * SUBLANES, SUBLANES), SUBLANES)
        return (pltpu.make_async_copy(ck_hbm.at[src], kbuf.at[slot, dst], sem.at[0, slot]),
                pltpu.make_async_copy(cv_hbm.at[src], vbuf.at[slot, dst], sem.at[1, slot]))

    def start_batch(bb, slot):
        def body(j, carry):
            for prio, cp in enumerate(item_copies(bb, j, slot)):
                cp.start(priority=prio)
            return carry
        lax.fori_loop(0, n_keep, body, 0)

    def wait_batch(slot):
        pltpu.make_async_copy(ck_hbm.at[pl.ds(0, ncol)], kbuf.at[slot], sem.at[0, slot]).wait()
        pltpu.make_async_copy(cv_hbm.at[pl.ds(0, ncol)], vbuf.at[slot], sem.at[1, slot]).wait()

    @pl.when(b == 0)
    def _():
        start_batch(0, 0)
        item_of_col = lax.broadcasted_iota(I32, (n_keep, ncol), 1) // SUBLANES
        expand_ref[...] = (item_of_col == lax.broadcasted_iota(I32, (n_keep, ncol), 0)).astype(BF16)

    @pl.when(b + 1 < nb)
    def _():
        start_batch(b + 1, (b + 1) % 2)

    slot = b % 2
    wait_batch(slot)

    q8 = q_ref[0]
    meta = meta_ref[0]
    new_kept = meta[1:2, 0:1] > 0
    sub_col = _dot(meta.astype(BF16), expand_ref[...])[2:3, :]
    head_i = lax.broadcasted_iota(I32, (N_HEADS, ncol), 0)
    col_i = lax.broadcasted_iota(I32, (N_HEADS, ncol), 1)
    want = sub_col * N_KV_HEADS + (head_i // GROUP).astype(F32)
    is_new_item = new_kept & (col_i // SUBLANES == n_keep - 1)
    ok = ((col_i % SUBLANES).astype(F32) == want) & jnp.logical_not(is_new_item)

    s = jnp.where(ok, _dot_nt(q8, kbuf[slot].astype(BF16)), NEG)
    s_new = jnp.sum(q8.astype(F32) * kn_ref[0].astype(F32), axis=1, keepdims=True)
    s_new = jnp.where(new_kept, s_new, NEG)
    m = jnp.maximum(jnp.max(s, axis=1, keepdims=True), s_new)
    p = jnp.exp(s - m)
    p_new = jnp.exp(s_new - m)
    l = jnp.sum(p, axis=1, keepdims=True) + p_new
    acc = _dot(p.astype(BF16), vbuf[slot].astype(BF16)) + p_new.astype(BF16).astype(F32) * vn_ref[0].astype(F32)
    o_ref[0] = (acc / l).astype(BF16)


def _decode_attend(tiles, q8, meta, kn8, vn8, ck2, cv2):
    db, n_keep = tiles.shape
    per_b = lambda a: pl.BlockSpec((1,) + a.shape[1:], lambda b, tl: (b, 0, 0))
    any_spec = pl.BlockSpec(memory_space=pl.ANY)
    ncol = n_keep * SUBLANES
    grid_spec = pltpu.PrefetchScalarGridSpec(
        num_scalar_prefetch=1,
        grid=(db,),
        in_specs=[per_b(q8), per_b(meta), per_b(kn8), per_b(vn8), any_spec, any_spec],
        out_specs=pl.BlockSpec((1, N_HEADS, HEAD_DIM), lambda b, tl: (b, 0, 0)),
        scratch_shapes=[pltpu.VMEM((2, ncol, HEAD_DIM), F32), pltpu.VMEM((2, ncol, HEAD_DIM), F32),
                        pltpu.VMEM((n_keep, ncol), BF16), pltpu.SemaphoreType.DMA((2, 2))],
    )
    return pl.pallas_call(
        functools.partial(_decode_attend_kernel, n_keep=n_keep),
        grid_spec=grid_spec,
        out_shape=jax.ShapeDtypeStruct((db, N_HEADS, HEAD_DIM), BF16),
        compiler_params=_params(1),
        name="decode_attend",
    )(tiles, q8, meta, kn8, vn8, ck2, cv2)


def _layer_norm(r, g, b):
    mu = jnp.mean(r, axis=-1, keepdims=True)
    d = r - mu
    var = jnp.mean(d * d, axis=-1, keepdims=True)
    return d * lax.rsqrt(var + LN_EPS) * g + b


def _route(logits_t, rbias):
    n_exp, tm = logits_t.shape
    epg = n_exp // N_GROUPS
    s = jax.nn.sigmoid(logits_t)
    sb = s + rbias
    ie = lax.broadcasted_iota(I32, (epg, tm), 0)
    gs_rows = []
    for g in range(N_GROUPS):
        blk = sb[g * epg:(g + 1) * epg, :]
        m1 = jnp.max(blk, axis=0, keepdims=True)
        i1 = jnp.min(jnp.where(blk == m1, ie, epg), axis=0, keepdims=True)
        m2 = jnp.max(jnp.where(ie == i1, -jnp.inf, blk), axis=0, keepdims=True)
        gs_rows.append(m1 + m2)
    picked = [jnp.zeros((1, tm), jnp.bool_) for _ in range(N_GROUPS)]
    cur = list(gs_rows)
    for _ in range(TOPK_GROUPS):
        mx = cur[0]
        for g in range(1, N_GROUPS):
            mx = jnp.maximum(mx, cur[g])
        found = jnp.zeros((1, tm), jnp.bool_)
        for g in range(N_GROUPS):
            hit = (cur[g] == mx) & jnp.logical_not(found)
            found = found | hit
            picked[g] = picked[g] | hit
            cur[g] = jnp.where(hit, -jnp.inf, cur[g])
    masked = jnp.concatenate(
        [jnp.where(picked[g], sb[g * epg:(g + 1) * epg, :], NEG) for g in range(N_GROUPS)], axis=0)
    iall = lax.broadcasted_iota(I32, (n_exp, tm), 0)
    e_rows, w_rows = [], []
    for _ in range(TOP_K):
        mx = jnp.max(masked, axis=0, keepdims=True)
        ix = jnp.min(jnp.where(masked == mx, iall, n_exp), axis=0, keepdims=True)
        hit = iall == ix
        w_rows.append(jnp.sum(jnp.where(hit, s, 0.0), axis=0, keepdims=True))
        e_rows.append(ix)
        masked = jnp.where(hit, -jnp.inf, masked)
    wsum = w_rows[0]
    for w in w_rows[1:]:
        wsum = wsum + w
    gates = [w / wsum * ROUTED_SCALE for w in w_rows]
    return jnp.concatenate(e_rows, axis=0), jnp.concatenate(gates, axis=0)


def _post_attn_kernel(xp, xs, ap, as_, cp, cs, gap, gas, gcp, gcs, woa, woc, wo, g1, b1, rwt, rb,
                      h2_o, e_o, gate_o, *, n_prompt_tiles):
    i = pl.program_id(0)
    is_p = i < n_prompt_tiles
    pick = lambda a, b: jnp.where(is_p, a[...], b[...])
    a = _dot(pick(ap, as_), woa[...])
    c = _dot(pick(cp, cs), woc[...])
    merged = pick(gap, gas).astype(F32) * a + pick(gcp, gcs).astype(F32) * c
    r = ALPHA * pick(xp, xs) + _dot(merged.astype(BF16), wo[...])
    h = _layer_norm(r, g1[...], b1[...])
    tm = h.shape[0]
    for k in range(D_MODEL // LANES):
        h2_o[pl.ds(k, tm, stride=SUBLANES), :] = h[:, k * LANES:(k + 1) * LANES]
    logits_t = lax.dot_general(rwt[...], h, (((1,), (1,)), ((), ())),
                               precision=lax.Precision.HIGHEST, preferred_element_type=F32)
    e_idx, gates = _route(logits_t, rb[...])
    e_o[...] = e_idx
    gate_o[...] = gates


def _post_attn(x_p, x_s, attn_p, attn_s, conv_p, conv_s, ga_p, ga_s, gc_p, gc_s,
               woa, woc, wo, g1, b1, rwt, rb):
    tp, ts = x_p.shape[0], x_s.shape[0]
    tm = TOKEN_TILE
    npt, nst = tp // tm, ts // tm
    n_tok = tp + ts
    p_spec = pl.BlockSpec((tm, D_MODEL), lambda i: (jnp.minimum(i, npt - 1), 0))
    s_spec = pl.BlockSpec((tm, D_MODEL), lambda i: (jnp.maximum(i - npt, 0), 0))
    consts = [woa, woc, wo, g1, b1, rwt, rb]
    return pl.pallas_call(
        functools.partial(_post_attn_kernel, n_prompt_tiles=npt),
        grid=(npt + nst,),
        in_specs=[p_spec, s_spec] * 5 + [_const_spec(c.shape) for c in consts],
        out_specs=[pl.BlockSpec((tm * SUBLANES, LANES), lambda i: (i, 0)),
                   pl.BlockSpec((TOP_K, tm), lambda i: (0, i)),
                   pl.BlockSpec((TOP_K, tm), lambda i: (0, i))],
        out_shape=[jax.ShapeDtypeStruct((n_tok * SUBLANES, LANES), F32),
                   jax.ShapeDtypeStruct((TOP_K, n_tok), I32),
                   jax.ShapeDtypeStruct((TOP_K, n_tok), F32)],
        compiler_params=_params(1),
        name="post_attn",
    )(x_p, x_s, attn_p, attn_s, conv_p, conv_s, ga_p, ga_s, gc_p, gc_s, *consts)


def _rank_kernel(e_ref, rank_o, cnt_o, carry_ref):
    i = pl.program_id(0)

    @pl.when(i == 0)
    def _():
        carry_ref[...] = jnp.zeros_like(carry_ref)

    n_exp = carry_ref.shape[0]
    e = e_ref[...]
    tk = e.shape[1]
    ie = lax.broadcasted_iota(I32, (n_exp, tk), 0)
    onehot = jnp.zeros((n_exp, tk), F32)
    for j in range(TOP_K):
        onehot = onehot + (ie == e[j:j + 1, :]).astype(F32)
    before = (lax.broadcasted_iota(I32, (tk, tk), 0) < lax.broadcasted_iota(I32, (tk, tk), 1)).astype(BF16)
    prefix = _dot(onehot.astype(BF16), before) + carry_ref[:, 0:1]
    rows = [jnp.sum(jnp.where(ie == e[j:j + 1, :], prefix, 0.0), axis=0, keepdims=True) for j in range(TOP_K)]
    rank_o[...] = jnp.concatenate(rows, axis=0).astype(I32)
    carry_ref[...] = carry_ref[...] + jnp.sum(onehot, axis=1, keepdims=True)
    cnt_o[...] = carry_ref[...].astype(I32)


def _rank(e_t, n_exp):
    n_tok = e_t.shape[1]
    tk = TOKEN_TILE
    return pl.pallas_call(
        _rank_kernel,
        grid=(n_tok // tk,),
        in_specs=[pl.BlockSpec((TOP_K, tk), lambda i: (0, i))],
        out_specs=[pl.BlockSpec((TOP_K, tk), lambda i: (0, i)),
                   pl.BlockSpec((n_exp, LANES), lambda i: (0, 0))],
        out_shape=[jax.ShapeDtypeStruct((TOP_K, n_tok), I32), jax.ShapeDtypeStruct((n_exp, LANES), I32)],
        scratch_shapes=[pltpu.VMEM((n_exp, LANES), F32)],
        compiler_params=_params(1),
        name="moe_rank",
    )(e_t)


def _slot_kernel(e_ref, rank_ref, pstart_ref, slot_o):
    e = e_ref[...]
    n_exp = pstart_ref.shape[0]
    tk = e.shape[1]
    ie = lax.broadcasted_iota(I32, (n_exp, tk), 0)
    ps = pstart_ref[:, 0:1]
    rows = [jnp.sum(jnp.where(ie == e[j:j + 1, :], ps, 0.0), axis=0, keepdims=True) for j in range(TOP_K)]
    slot_o[...] = rank_ref[...] + jnp.concatenate(rows, axis=0).astype(I32)


def _slots(e_t, rank_t, pstart_f):
    n_tok = e_t.shape[1]
    tk = TOKEN_TILE
    spec = pl.BlockSpec((TOP_K, tk), lambda i: (0, i))
    return pl.pallas_call(
        _slot_kernel,
        grid=(n_tok // tk,),
        in_specs=[spec, spec, _const_spec(pstart_f.shape)],
        out_specs=spec,
        out_shape=jax.ShapeDtypeStruct((TOP_K, n_tok), I32),
        compiler_params=_params(1),
        name="moe_slots",
    )(e_t, rank_t, pstart_f)


def _dispatch_kernel(pend_ref, padded_ref, slot_ref, h2_ref, xs_hbm, zbuf, sem, zsem, *, n_exp):
    i = pl.program_id(0)
    td = slot_ref.shape[1]
    blk_rows = MOE_BLOCK * SUBLANES

    def zero_copy(e):
        dst = pl.ds(pl.multiple_of((pend_ref[e] - MOE_BLOCK) * SUBLANES, SUBLANES), blk_rows)
        return pltpu.make_async_copy(zbuf, xs_hbm.at[dst], zsem)

    @pl.when(i == 0)
    def _():
        zbuf[...] = jnp.zeros_like(zbuf)

        def start(e, carry):
            @pl.when(padded_ref[e] > 0)
            def _():
                zero_copy(e).start()
            return carry

        def wait(e, carry):
            @pl.when(padded_ref[e] > 0)
            def _():
                zero_copy(e).wait()
            return carry

        lax.fori_loop(0, n_exp, start, 0)
        lax.fori_loop(0, n_exp, wait, 0)

    def row_copy(t, j):
        src = pl.ds(pl.multiple_of(t * SUBLANES, SUBLANES), SUBLANES)
        dst = pl.ds(pl.multiple_of(slot_ref[j, t] * SUBLANES, SUBLANES), SUBLANES)
        return pltpu.make_async_copy(h2_ref.at[src], xs_hbm.at[dst], sem)

    def start_tok(t, carry):
        for j in range(TOP_K):
            row_copy(t, j).start(priority=j % 2)
        return carry

    def wait_tok(t, carry):
        for j in range(TOP_K):
            row_copy(t, j).wait()
        return carry

    lax.fori_loop(0, td, start_tok, 0)
    lax.fori_loop(0, td, wait_tok, 0)


def _dispatch(pend, padded, slot_t, h2, m_pad):
    n_tok = slot_t.shape[1]
    td = TOKEN_TILE
    n_exp = pend.shape[0]
    grid_spec = pltpu.PrefetchScalarGridSpec(
        num_scalar_prefetch=2,
        grid=(n_tok // td,),
        in_specs=[pl.BlockSpec((TOP_K, td), lambda i, *_: (0, i), memory_space=pltpu.SMEM),
                  pl.BlockSpec((td * SUBLANES, LANES), lambda i, *_: (i, 0))],
        out_specs=pl.BlockSpec(memory_space=pl.ANY),
        scratch_shapes=[pltpu.VMEM((MOE_BLOCK * SUBLANES, LANES), F32),
                        pltpu.SemaphoreType.DMA(()), pltpu.SemaphoreType.DMA(())],
    )
    return pl.pallas_call(
        functools.partial(_dispatch_kernel, n_exp=n_exp),
        grid_spec=grid_spec,
        out_shape=jax.ShapeDtypeStruct((m_pad * SUBLANES, LANES), F32),
        compiler_params=_params(1),
        name="moe_dispatch",
    )(pend, padded, slot_t, h2)


def _tile_rows(ref, n):
    return jnp.concatenate([ref[pl.ds(k, n, stride=SUBLANES), :] for k in range(D_MODEL // LANES)], axis=1)


def _expert_kernel(be_ref, nused_ref, x_ref, wg_ref, wu_ref, wd_ref, y_ref, wgu_b, wd_b):
    i = pl.program_id(0)
    d_exp = wd_b.shape[0]

    @pl.when(i < nused_ref[0])
    def _():
        changed = (i == 0) | (be_ref[i] != be_ref[jnp.maximum(i - 1, 0)])

        @pl.when(changed)
        def _():
            wgu_b[:, :d_exp] = wg_ref[0].astype(BF16)
            wgu_b[:, d_exp:] = wu_ref[0].astype(BF16)
            wd_b[...] = wd_ref[0].astype(BF16)

        x = _tile_rows(x_ref, MOE_BLOCK).astype(BF16)
        gu = _dot(x, wgu_b[...])
        hh = jax.nn.silu(gu[:, :d_exp]) * gu[:, d_exp:]
        y = _dot(hh.astype(BF16), wd_b[...])
        for k in range(D_MODEL // LANES):
            y_ref[pl.ds(k, MOE_BLOCK, stride=SUBLANES), :] = y[:, k * LANES:(k + 1) * LANES]


def _experts(block_e, nused, xs, w_gate, w_up, w_down):
    n_blocks = block_e.shape[0]
    d_exp = w_gate.shape[2]
    rows = MOE_BLOCK * SUBLANES
    last = lambda i, nu: jnp.minimum(i, nu[0] - 1)
    grid_spec = pltpu.PrefetchScalarGridSpec(
        num_scalar_prefetch=2,
        grid=(n_blocks,),
        in_specs=[pl.BlockSpec((rows, LANES), lambda i, be, nu: (last(i, nu), 0)),
                  pl.BlockSpec((1, D_MODEL, d_exp), lambda i, be, nu: (be[i], 0, 0)),
                  pl.BlockSpec((1, D_MODEL, d_exp), lambda i, be, nu: (be[i], 0, 0)),
                  pl.BlockSpec((1, d_exp, D_MODEL), lambda i, be, nu: (be[i], 0, 0))],
        out_specs=pl.BlockSpec((rows, LANES), lambda i, be, nu: (last(i, nu), 0)),
        scratch_shapes=[pltpu.VMEM((D_MODEL, 2 * d_exp), BF16), pltpu.VMEM((d_exp, D_MODEL), BF16)],
    )
    return pl.pallas_call(
        _expert_kernel,
        grid_spec=grid_spec,
        out_shape=jax.ShapeDtypeStruct(xs.shape, F32),
        compiler_params=_params(1),
        name="moe_experts",
    )(block_e, nused, xs, w_gate, w_up, w_down)


def _combine_kernel(slot_ref, slot_next_ref, gate_ref, h2_ref, ys_hbm, shg, shu, shd, g2, b2, yp_o, ys_o,
                    buf, sem, *, n_prompt_tiles):
    i = pl.program_id(0)
    n = pl.num_programs(0)
    tc = gate_ref.shape[0]

    def start_tile(s_ref, bslot):
        def body(t, carry):
            dst = pl.ds(pl.multiple_of(t * SUBLANES, SUBLANES), SUBLANES)
            for j in range(TOP_K):
                src = pl.ds(pl.multiple_of(s_ref[j, t] * SUBLANES, SUBLANES), SUBLANES)
                pltpu.make_async_copy(ys_hbm.at[src], buf.at[bslot, j, dst], sem.at[bslot]).start(priority=j % 2)
            return carry
        lax.fori_loop(0, tc, body, 0)

    def wait_tile(bslot):
        for j in range(TOP_K):
            pltpu.make_async_copy(ys_hbm.at[pl.ds(0, tc * SUBLANES)], buf.at[bslot, j], sem.at[bslot]).wait()

    @pl.when(i == 0)
    def _():
        start_tile(slot_ref, 0)

    @pl.when(i + 1 < n)
    def _():
        start_tile(slot_next_ref, (i + 1) % 2)

    cur = i % 2
    h = _tile_rows(h2_ref, tc)
    hb = h.astype(BF16)
    shared = _dot((jax.nn.silu(_dot(hb, shg[...])) * _dot(hb, shu[...])).astype(BF16), shd[...])
    wait_tile(cur)
    routed = jnp.zeros((tc, D_MODEL), F32)
    for j in range(TOP_K):
        routed = routed + _tile_rows(buf.at[cur, j], tc) * gate_ref[:, j:j + 1]
    y = _layer_norm(ALPHA * h + (routed + shared), g2[...], b2[...])

    @pl.when(i < n_prompt_tiles)
    def _():
        yp_o[...] = y

    @pl.when(i >= n_prompt_tiles)
    def _():
        ys_o[...] = y


def _combine(slot_t, gates, h2, ys, shg, shu, shd, g2, b2, n_prompt, n_decode):
    tc = TOKEN_TILE
    npt, nst = n_prompt // tc, n_decode // tc
    consts = [shg, shu, shd, g2, b2]
    return pl.pallas_call(
        functools.partial(_combine_kernel, n_prompt_tiles=npt),
        grid=(npt + nst,),
        in_specs=[pl.BlockSpec((TOP_K, tc), lambda i: (0, i), memory_space=pltpu.SMEM),
                  pl.BlockSpec((TOP_K, tc), lambda i: (0, jnp.minimum(i + 1, npt + nst - 1)),
                               memory_space=pltpu.SMEM),
                  pl.BlockSpec((tc, TOP_K), lambda i: (i, 0)),
                  pl.BlockSpec((tc * SUBLANES, LANES), lambda i: (i, 0)),
                  pl.BlockSpec(memory_space=pl.ANY)] + [_const_spec(c.shape) for c in consts],
        out_specs=[pl.BlockSpec((tc, D_MODEL), lambda i: (jnp.minimum(i, npt - 1), 0)),
                   pl.BlockSpec((tc, D_MODEL), lambda i: (jnp.maximum(i - npt, 0), 0))],
        out_shape=[jax.ShapeDtypeStruct((n_prompt, D_MODEL), F32),
                   jax.ShapeDtypeStruct((n_decode, D_MODEL), F32)],
        scratch_shapes=[pltpu.VMEM((2, TOP_K, tc * SUBLANES, LANES), F32), pltpu.SemaphoreType.DMA((2,))],
        compiler_params=_params(1),
        name="moe_combine",
    )(slot_t, slot_t, gates, h2, ys, *consts)


def _pad_rows(a, n):
    return jnp.pad(a, ((0, n - a.shape[0]),) + ((0, 0),) * (a.ndim - 1))


def kernel(x_prompt, x_sample, cache_k, cache_v, cache_kidx, page_table, state_conv, w_in, conv_w, w_o_attn,
           w_o_conv, w_o, ln1_g, ln1_b, router_w, router_bias, moe_w_gate, moe_w_up, moe_w_down,
           shared_w_gate, shared_w_up, shared_w_down, ln2_g, ln2_b):
    nb, seq, _ = x_prompt.shape
    db = x_sample.shape[0]
    n_pool = cache_k.shape[1]
    n_exp = router_w.shape[-1]
    tp = nb * seq
    ts = TOKEN_TILE
    nkv = N_KV_HEADS * HEAD_DIM

    weights = _prep_in_weights(w_in[0])
    cw = conv_w[0]

    xp2 = x_prompt.reshape(tp, D_MODEL)
    (q_p, k_p, v_p, kvb_p, qi_p, idx_p, kib_p, conv_p, ga_p, gc_p, ulast_p) = _in_proj_prompt(xp2, weights, cw, nb)
    xs2 = _pad_rows(x_sample.reshape(db, D_MODEL), ts)
    prev2 = _pad_rows(state_conv[0, :, 0, :], ts)
    prev1 = _pad_rows(state_conv[0, :, 1, :], ts)
    (q_s, k_s, v_s, _, qi_s, idx_s, _, conv_s, ga_s, gc_s, u_s) = _in_proj_decode(xs2, prev2, prev1, weights, cw)

    attn_p = _attn_prompt(q_p, kvb_p, qi_p, kib_p, idx_p, nb)
    q8 = q_s[:db].reshape(db, N_HEADS, HEAD_DIM)
    qi8 = qi_s[:db].reshape(db, IDX_HEADS, LANES)[:, :, :IDX_DIM]
    wi8 = idx_s[:db, IDX_DIM:IDX_DIM + IDX_HEADS].reshape(db, IDX_HEADS, 1)
    kin = idx_s[:db, :IDX_DIM].astype(BF16).reshape(db, 1, IDX_DIM)
    kn8 = jnp.repeat(k_s[:db].reshape(db, N_KV_HEADS, HEAD_DIM), GROUP, axis=1).astype(BF16)
    vn8 = jnp.repeat(v_s[:db].reshape(db, N_KV_HEADS, HEAD_DIM), GROUP, axis=1).astype(BF16)
    ck2 = cache_k[0].reshape(n_pool * PAGE_SIZE * N_KV_HEADS, HEAD_DIM)
    cv2 = cache_v[0].reshape(n_pool * PAGE_SIZE * N_KV_HEADS, HEAD_DIM)
    keys_s, knew_s = _decode_scores(page_table, qi8, wi8, kin, jnp.swapaxes(cache_kidx[0], 1, 2))
    meta = _decode_pick(keys_s, knew_s, page_table.astype(F32)[:, :, None])
    attn_s8 = _decode_attend(meta[:, 0, :], q8, meta, kn8, vn8, ck2, cv2)
    attn_s = _pad_rows(attn_s8.reshape(db, N_HEADS * HEAD_DIM), ts)

    h2, e_t, gate_t = _post_attn(
        xp2, xs2, attn_p, attn_s, conv_p, conv_s, ga_p, ga_s, gc_p, gc_s,
        w_o_attn[0].astype(BF16), w_o_conv[0].astype(BF16), w_o[0].astype(BF16),
        ln1_g[0].reshape(1, D_MODEL), ln1_b[0].reshape(1, D_MODEL),
        router_w[0].T, router_bias[0].reshape(n_exp, 1))

    n_tok = tp + ts
    rank_t, cnt = _rank(e_t, n_exp)
    counts = cnt[:, 0]
    padded = (counts + MOE_BLOCK - 1) // MOE_BLOCK * MOE_BLOCK
    pend = jnp.cumsum(padded)
    pstart = pend - padded
    n_blocks = (n_tok * TOP_K + n_exp * (MOE_BLOCK - 1) + MOE_BLOCK - 1) // MOE_BLOCK
    nused = (pend[-1] // MOE_BLOCK).astype(I32)
    blk = jnp.minimum(jnp.arange(n_blocks, dtype=I32), nused - 1)
    block_e = jnp.minimum(jnp.sum((pend[None, :] <= (blk * MOE_BLOCK)[:, None]).astype(I32), axis=1), n_exp - 1)
    slot_t = _slots(e_t, rank_t, jnp.broadcast_to(pstart.astype(F32)[:, None], (n_exp, LANES)))
    xs = _dispatch(pend.astype(I32), padded.astype(I32), slot_t, h2, n_blocks * MOE_BLOCK)
    ys = _experts(block_e, nused.reshape(1), xs, moe_w_gate[0], moe_w_up[0], moe_w_down[0])
    y_p, y_s = _combine(slot_t, gate_t.T, h2, ys,
                        shared_w_gate[0].astype(BF16), shared_w_up[0].astype(BF16),
                        shared_w_down[0].astype(BF16),
                        ln2_g[0].reshape(1, D_MODEL), ln2_b[0].reshape(1, D_MODEL), tp, ts)

    conv_sample = jnp.stack([state_conv[0, :, 1, :], u_s[:db]], axis=1)[None]
    return (y_p.reshape(nb, seq, D_MODEL),
            y_s[:db].reshape(db, 1, D_MODEL),
            k_p.reshape(1, nb, seq, N_KV_HEADS, HEAD_DIM),
            v_p.reshape(1, nb, seq, N_KV_HEADS, HEAD_DIM),
            idx_p[:, :IDX_DIM].reshape(1, nb, seq, IDX_DIM),
            ulast_p[:, SUBLANES - (CONV_W - 1):, :][None],
            k_s[:db].reshape(1, db, 1, N_KV_HEADS, HEAD_DIM),
            v_s[:db].reshape(1, db, 1, N_KV_HEADS, HEAD_DIM),
            idx_s[:db, :IDX_DIM].reshape(1, db, 1, IDX_DIM),
            conv_sample)
```

```python
import functools

import jax
import jax.numpy as jnp
import numpy as np
from jax import lax
from jax.experimental import pallas as pl
from jax.experimental.pallas import tpu as pltpu

F32 = jnp.float32
BF16 = jnp.bfloat16
I32 = jnp.int32

D_MODEL = 1024
N_HEADS = 8
HEAD_DIM = 128
N_KV_HEADS = 2
GROUP = N_HEADS // N_KV_HEADS
IDX_HEADS = 8
IDX_DIM = 64
TOPK_KEYS = 256
IDX_SCALE = IDX_DIM ** -0.5 * IDX_HEADS ** -0.5
Q_SCALE = HEAD_DIM ** -0.5
PAGE_SIZE = 128
D_CONV = D_MODEL
CONV_W = 3
TOP_K = 8
N_GROUPS = 8
TOPK_GROUPS = 4
ROUTED_SCALE = 2.5
DEPTH = 1
ALPHA = (2 * DEPTH) ** 0.25
LN_EPS = 1e-5
NEG = -1e30
IN_WIDTHS = (N_HEADS * HEAD_DIM, N_KV_HEADS * HEAD_DIM, N_KV_HEADS * HEAD_DIM,
             IDX_HEADS * IDX_DIM, IDX_DIM, IDX_HEADS,
             D_CONV, D_CONV, D_CONV, D_MODEL, D_MODEL)

LANES = 128
SUBLANES = 8
VMEM_LIMIT_BYTES = 56 * 1024 * 1024

PROJ_ROWS = 512
ATT_Q = 256
SEARCH_ROWS = 128
TOKEN_TILE = 256
MOE_BLOCK = 256
CONV_CHUNK = 256
INT_MIN = -2 ** 31
SOFTMAX_DENOM_MIN = 1e-25
F32_MIN_NORMAL = float(np.finfo(np.float32).tiny)
BF16_MIN_NORMAL_BITS = 0x0080


def _np_key(v):
    b = int(np.float32(v).view(np.int32))
    return b ^ ((b >> 31) & 0x7FFFFFFF)


KEY_HALF = _np_key(NEG * 0.5)


def _sort_key(s):
    b = lax.bitcast_convert_type(s + 0.0, I32)
    return b ^ ((b >> 31) & jnp.int32(0x7FFFFFFF))


def _dot(a, b):
    return jnp.dot(a, b, preferred_element_type=F32)


def _dot_nt(a, b):
    return lax.dot_general(a, b, (((1,), (1,)), ((), ())), preferred_element_type=F32)


def _params(n_grid):
    return pltpu.CompilerParams(dimension_semantics=("arbitrary",) * n_grid,
                                vmem_limit_bytes=VMEM_LIMIT_BYTES)


def _const_spec(shape):
    nd = len(shape)
    return pl.BlockSpec(shape, lambda *_: (0,) * nd, pipeline_mode=pl.Buffered(1))


def _proj_common(xb, w1, w2, w3, wga, wgc, q_o, k_o, v_o, kv_o, qi_o, idx_o, kib_o, sga_o, sgc_o):
    nq = N_HEADS * HEAD_DIM
    nkv = N_KV_HEADS * HEAD_DIM
    z1 = _dot(xb, w1[...])
    q_o[...] = (z1[:, :nq] * Q_SCALE).astype(BF16)
    k = z1[:, nq:nq + nkv]
    v = z1[:, nq + nkv:nq + 2 * nkv]
    k_o[...] = k
    v_o[...] = v
    kv_o[:, :nkv] = k.astype(BF16)
    kv_o[:, nkv:] = v.astype(BF16)
    qi_o[...] = _dot(xb, w2[...]).astype(BF16)
    z3 = _dot(xb, w3[...])
    idx_o[...] = z3
    kib_o[...] = z3.astype(BF16)
    sga_o[...] = jax.nn.sigmoid(_dot(xb, wga[...])).astype(BF16)
    sgc_o[...] = jax.nn.sigmoid(_dot(xb, wgc[...])).astype(BF16)


def _in_proj_prompt_kernel(x_ref, w1, w2, w3, wb, wc, wx, wga, wgc, cw_ref,
                           q_o, k_o, v_o, kv_o, qi_o, idx_o, kib_o, conv_o, sga_o, sgc_o, ulast_o,
                           tail_ref):
    j = pl.program_id(1)

    @pl.when(j == 0)
    def _():
        tail_ref[...] = jnp.zeros_like(tail_ref)

    xb = x_ref[...].astype(BF16)
    _proj_common(xb, w1, w2, w3, wga, wgc, q_o, k_o, v_o, kv_o, qi_o, idx_o, kib_o, sga_o, sgc_o)
    tm = xb.shape[0]
    row = lax.broadcasted_iota(I32, (tm, CONV_CHUNK), 0)
    for c in range(D_CONV // CONV_CHUNK):
        cs = slice(c * CONV_CHUNK, (c + 1) * CONV_CHUNK)
        gb = _dot(xb, wb[:, cs])
        u = _dot(xb, wc[:, cs]) * _dot(xb, wx[:, cs])
        p1 = tail_ref[SUBLANES - 1:SUBLANES, cs]
        p2 = tail_ref[SUBLANES - 2:SUBLANES - 1, cs]
        u1 = jnp.where(row == 0, p1, pltpu.roll(u, 1, 0))
        u2 = jnp.where(row == 0, p2, jnp.where(row == 1, p1, pltpu.roll(u, 2, 0)))
        conv = cw_ref[0:1, cs] * u2 + cw_ref[1:2, cs] * u1 + cw_ref[2:3, cs] * u
        conv_o[:, cs] = (gb * conv).astype(BF16)
        tail_ref[:, cs] = u[tm - SUBLANES:, :]
    ulast_o[0] = tail_ref[...]


def _in_proj_decode_kernel(x_ref, p2_ref, p1_ref, w1, w2, w3, wb, wc, wx, wga, wgc, cw_ref,
                           q_o, k_o, v_o, kv_o, qi_o, idx_o, kib_o, conv_o, sga_o, sgc_o, u_o):
    xb = x_ref[...].astype(BF16)
    _proj_common(xb, w1, w2, w3, wga, wgc, q_o, k_o, v_o, kv_o, qi_o, idx_o, kib_o, sga_o, sgc_o)
    for c in range(D_CONV // CONV_CHUNK):
        cs = slice(c * CONV_CHUNK, (c + 1) * CONV_CHUNK)
        gb = _dot(xb, wb[:, cs])
        u = _dot(xb, wc[:, cs]) * _dot(xb, wx[:, cs])
        conv = cw_ref[0:1, cs] * p2_ref[:, cs] + cw_ref[1:2, cs] * p1_ref[:, cs] + cw_ref[2:3, cs] * u
        conv_o[:, cs] = (gb * conv).astype(BF16)
        u_o[:, cs] = u


def _prep_in_weights(w_in):
    w = w_in.astype(BF16)
    offs = np.concatenate([[0], np.cumsum(IN_WIDTHS)])
    part = [w[:, int(offs[i]):int(offs[i + 1])] for i in range(len(IN_WIDTHS))]
    q, k, v, qi, ki, wi, gb, gc, xv, ga, gcv = part
    d = w.shape[0]
    w1 = jnp.concatenate([q, k, v], axis=1)
    w2 = jnp.pad(qi.reshape(d, IDX_HEADS, IDX_DIM),
                 ((0, 0), (0, 0), (0, LANES - IDX_DIM))).reshape(d, IDX_HEADS * LANES)
    w3 = jnp.concatenate([ki, wi, jnp.zeros((d, LANES - IDX_DIM - IDX_HEADS), BF16)], axis=1)
    return (w1, w2, w3, gb, gc, xv, ga, gcv)


def _proj_out_shapes(t):
    nkv = N_KV_HEADS * HEAD_DIM
    sd = jax.ShapeDtypeStruct
    return [sd((t, N_HEADS * HEAD_DIM), BF16), sd((t, nkv), F32), sd((t, nkv), F32), sd((t, 2 * nkv), BF16),
            sd((t, IDX_HEADS * LANES), BF16), sd((t, LANES), F32), sd((t, LANES), BF16),
            sd((t, D_CONV), BF16), sd((t, D_MODEL), BF16), sd((t, D_MODEL), BF16)]


def _in_proj_prompt(x2d, weights, conv_w, nb):
    t = x2d.shape[0]
    s = t // nb
    tm = min(PROJ_ROWS, s)
    nj = s // tm
    row_spec = lambda w: pl.BlockSpec((tm, w), lambda b, j: (b * nj + j, 0))
    out_shapes = _proj_out_shapes(t) + [jax.ShapeDtypeStruct((nb, SUBLANES, D_CONV), F32)]
    out_specs = [row_spec(o.shape[1]) for o in out_shapes[:-1]]
    out_specs.append(pl.BlockSpec((1, SUBLANES, D_CONV), lambda b, j: (b, 0, 0)))
    return pl.pallas_call(
        _in_proj_prompt_kernel,
        grid=(nb, nj),
        in_specs=[row_spec(D_MODEL)] + [_const_spec(w.shape) for w in weights] + [_const_spec(conv_w.shape)],
        out_specs=out_specs,
        out_shape=out_shapes,
        scratch_shapes=[pltpu.VMEM((SUBLANES, D_CONV), F32)],
        compiler_params=_params(2),
        name="in_proj_prompt",
    )(x2d, *weights, conv_w)


def _in_proj_decode(x2d, p2, p1, weights, conv_w):
    t = x2d.shape[0]
    row_spec = lambda w: pl.BlockSpec((t, w), lambda i: (0, 0))
    out_shapes = _proj_out_shapes(t) + [jax.ShapeDtypeStruct((t, D_CONV), F32)]
    return pl.pallas_call(
        _in_proj_decode_kernel,
        grid=(1,),
        in_specs=[row_spec(D_MODEL)] * 3 + [_const_spec(w.shape) for w in weights] + [_const_spec(conv_w.shape)],
        out_specs=[row_spec(o.shape[1]) for o in out_shapes],
        out_shape=out_shapes,
        compiler_params=_params(1),
        name="in_proj_decode",
    )(x2d, p2, p1, *weights, conv_w)


def _attn_prompt_kernel(q_ref, kv_ref, qi_ref, kib_ref, idx_ref, o_ref,
                        key_ref, hi_ref, bias_ref, t_ref, j_ref, need_ref, m_ref, acc_ref, knorm_ref,
                        *, n_keep, seq_len):
    tq = q_ref.shape[0]
    i = pl.program_id(1)
    nchunk = i + 1
    nsub = tq // LANES
    lane = lax.broadcasted_iota(I32, (tq, LANES), 1)

    wi_cols = [idx_ref[:, IDX_DIM + h:IDX_DIM + h + 1] * IDX_SCALE for h in range(IDX_HEADS)]
    row_i = lax.broadcasted_iota(I32, (tq, tq), 0)
    col_i = lax.broadcasted_iota(I32, (tq, tq), 1)

    def score_chunk(c, carry):
        kic = kib_ref[pl.ds(pl.multiple_of(c * tq, tq), tq), :]
        acc = jnp.zeros((tq, tq), F32)
        for h in range(IDX_HEADS):
            d = _dot_nt(qi_ref[:, h * LANES:(h + 1) * LANES], kic)
            acc = acc + jnp.maximum(d, 0.0) * wi_cols[h]
        acc = jnp.where(jnp.abs(acc) < F32_MIN_NORMAL, 0.0, acc)
        s = jnp.where((c < i) | (col_i <= row_i), acc, NEG)
        bits = lax.bitcast_convert_type(s, I32)
        key_ref[c] = bits ^ ((bits >> 31) & jnp.int32(0x7FFFFFFF))
        hi_ref[c] = lax.bitcast_convert_type(bits & jnp.int32(-65536), F32).astype(BF16)
        return carry

    lax.fori_loop(0, nchunk, score_chunk, 0)

    rb = SEARCH_ROWS

    nrb = tq // rb

    def lane_counts(r0, preds):
        def body(c, cnts):
            for k in range(nsub):
                blk = key_ref[c, r0:r0 + rb, k * LANES:(k + 1) * LANES]
                cnts = tuple(cnt + pred(blk, c * tq + k * LANES).astype(I32) for cnt, pred in zip(cnts, preds))
            return cnts
        return lax.fori_loop(0, nchunk, body, tuple(jnp.zeros((rb, LANES), I32) for _ in preds))

    def count(r0, pred):
        return jnp.sum(lane_counts(r0, (pred,))[0], axis=1, keepdims=True)

    def lane_counts_hi(r0, cand_b):
        def body(c, cnt):
            for k in range(nsub):
                blk = hi_ref[c, r0:r0 + rb, k * LANES:(k + 1) * LANES]
                cnt = cnt + jnp.where(blk >= cand_b, jnp.ones_like(cnt), jnp.zeros_like(cnt))
            return cnt
        return lax.fori_loop(0, nchunk, body, jnp.zeros((rb, LANES), BF16))

    def hi_bit_body(bi, p16s):
        cands, lanes = [], []
        for r in range(nrb):
            cand = p16s[r] + lax.shift_left(jnp.int32(1), 15 - bi)
            raw = cand ^ ((cand >> 31) & jnp.int32(0x7FFF))
            raw = jnp.where((raw > 0) & (raw < BF16_MIN_NORMAL_BITS), BF16_MIN_NORMAL_BITS, raw)
            cand_f = lax.bitcast_convert_type(lax.shift_left(raw, 16), F32)
            cands.append(cand)
            lanes.append(lane_counts_hi(r * rb, jnp.broadcast_to(cand_f, (rb, LANES)).astype(BF16)))
        return tuple(jnp.where(jnp.sum(lanes[r].astype(F32), axis=1, keepdims=True) >= n_keep, cands[r], p16s[r])
                     for r in range(nrb))

    p16s = lax.fori_loop(0, 16, hi_bit_body, tuple(jnp.full((rb, 1), -2 ** 15, I32) for _ in range(nrb)))

    def bit_body(bi, ts):
        cands, lanes = [], []
        for r in range(nrb):
            cand = ts[r] + lax.shift_left(jnp.int32(1), 15 - bi)
            cand_b = jnp.broadcast_to(cand, (rb, LANES))
            cands.append(cand)
            lanes.append(lane_counts(r * rb, (lambda blk, _, cand_b=cand_b: blk >= cand_b,))[0])
        return tuple(jnp.where(jnp.sum(lanes[r], axis=1, keepdims=True) >= n_keep, cands[r], ts[r])
                     for r in range(nrb))

    ts = lax.fori_loop(0, 16, bit_body, tuple(lax.shift_left(p, 16) for p in p16s))
    gt_eq = []
    for r in range(nrb):
        t_b = jnp.broadcast_to(ts[r], (rb, LANES))
        gt_eq.append(lane_counts(r * rb, (lambda blk, _, t_b=t_b: blk > t_b, lambda blk, _, t_b=t_b: blk == t_b)))
    any_excess = jnp.int32(0)
    for r in range(nrb):
        r0 = r * rb
        need = n_keep - jnp.sum(gt_eq[r][0], axis=1, keepdims=True)
        excess = (jnp.sum(gt_eq[r][1], axis=1, keepdims=True) > need) & (ts[r] > KEY_HALF)
        any_excess = any_excess + jnp.max(excess.astype(I32))
        t_ref[r0:r0 + rb, :] = jnp.broadcast_to(ts[r], (rb, LANES))
        need_ref[r0:r0 + rb, :] = jnp.broadcast_to(need, (rb, LANES))

    j_ref[...] = jnp.full(j_ref.shape, seq_len, I32)

    @pl.when(any_excess > 0)
    def _():
        nbits = max(1, (seq_len - 1).bit_length())
        lane_rb = lax.broadcasted_iota(I32, (rb, LANES), 1)
        for r in range(tq // rb):
            r0 = r * rb
            t_b = t_ref[r0:r0 + rb, :]
            need = need_ref[r0:r0 + rb, 0:1]

            def jbit(bi, jj):
                cand = jj | lax.shift_left(jnp.int32(1), nbits - 1 - bi)
                cand_b = jnp.broadcast_to(cand, (rb, LANES))
                g = count(r0, lambda blk, base: (blk == t_b) & ((base + lane_rb) < cand_b))
                return jnp.where(g < need, cand, jj)

            jj = lax.fori_loop(0, nbits, jbit, jnp.zeros((rb, 1), I32))
            j_ref[r0:r0 + rb, :] = jnp.broadcast_to(jj, (rb, LANES))

    t_all = t_ref[...]
    j_all = j_ref[...]

    def bias_chunk(c, carry):
        for k in range(nsub):
            blk = key_ref[c, :, k * LANES:(k + 1) * LANES]
            col = c * tq + k * LANES + lane
            sel = (blk > KEY_HALF) & ((blk > t_all) | ((blk == t_all) & (col <= j_all)))
            bias_ref[c, :, k * LANES:(k + 1) * LANES] = jnp.where(sel, 0.0, NEG)
        return carry

    lax.fori_loop(0, nchunk, bias_chunk, 0)

    nkv = N_KV_HEADS * HEAD_DIM
    gq = GROUP * tq

    def group_logits(c, g):
        rows = pl.ds(pl.multiple_of(c * tq, tq), tq)
        kc = kv_ref[rows, g * HEAD_DIM:(g + 1) * HEAD_DIM]
        qg = jnp.concatenate([q_ref[:, h * HEAD_DIM:(h + 1) * HEAD_DIM]
                              for h in range(g * GROUP, (g + 1) * GROUP)], axis=0)
        s = _dot_nt(qg, kc).reshape(GROUP, tq, tq) + bias_ref[c][None]
        return s.reshape(gq, tq)

    def exact_row_max():
        m_ref[...] = jnp.full(m_ref.shape, NEG, F32)

        def max_chunk(c, carry):
            for g in range(N_KV_HEADS):
                s = group_logits(c, g)
                mx = s[:, :LANES]
                for k in range(1, nsub):
                    mx = jnp.maximum(mx, s[:, k * LANES:(k + 1) * LANES])
                m_ref[g] = jnp.maximum(m_ref[g], mx)
            return carry

        lax.fori_loop(0, nchunk, max_chunk, 0)
        for g in range(N_KV_HEADS):
            m_ref[g] = jnp.broadcast_to(jnp.max(m_ref[g], axis=1, keepdims=True), (gq, LANES))

    ones = jnp.ones((tq, HEAD_DIM), BF16)

    def accumulate():
        acc_ref[...] = jnp.zeros_like(acc_ref)

        def pv_chunk(c, carry):
            rows = pl.ds(pl.multiple_of(c * tq, tq), tq)
            for g in range(N_KV_HEADS):
                s = group_logits(c, g)
                m = m_ref[g]
                p = jnp.concatenate([jnp.exp(s[:, k * LANES:(k + 1) * LANES] - m) for k in range(nsub)], axis=1)
                vc = kv_ref[rows, nkv + g * HEAD_DIM:nkv + (g + 1) * HEAD_DIM]
                acc_ref[g] = acc_ref[g] + _dot(p.astype(BF16), jnp.concatenate([vc, ones], axis=1))
            return carry

        lax.fori_loop(0, nchunk, pv_chunk, 0)

    @pl.when(i == 0)
    def _():
        for g in range(N_KV_HEADS):
            def norm_chunk(c, mx):
                kc = kv_ref[pl.ds(pl.multiple_of(c * tq, tq), tq), g * HEAD_DIM:(g + 1) * HEAD_DIM].astype(F32)
                return jnp.maximum(mx, jnp.sum(kc * kc, axis=1, keepdims=True))
            k2 = lax.fori_loop(0, seq_len // tq, norm_chunk, jnp.zeros((tq, 1), F32))
            knorm_ref[g] = jnp.broadcast_to(jnp.sqrt(jnp.max(k2, axis=0, keepdims=True)), (SUBLANES, LANES))

    for h in range(N_HEADS):
        g, r = h // GROUP, h % GROUP
        qh = q_ref[:, h * HEAD_DIM:(h + 1) * HEAD_DIM].astype(F32)
        qn = jnp.sqrt(jnp.sum(qh * qh, axis=1, keepdims=True))
        m_ref[g, r * tq:(r + 1) * tq, :] = qn * knorm_ref[g][0:1, :]
    accumulate()
    denom_min = jnp.min(acc_ref[:, :, HEAD_DIM:HEAD_DIM + 1])

    @pl.when(denom_min < SOFTMAX_DENOM_MIN)
    def _():
        exact_row_max()
        accumulate()

    for h in range(N_HEADS):
        a = acc_ref[h // GROUP, (h % GROUP) * tq:(h % GROUP + 1) * tq, :]
        o_ref[:, h * HEAD_DIM:(h + 1) * HEAD_DIM] = (a[:, :HEAD_DIM] / a[:, HEAD_DIM:]).astype(BF16)


def _attn_prompt(q, kvb, qi, kib, idx, nb):
    t = q.shape[0]
    s = t // nb
    tq = min(ATT_Q, s)
    nq = s // tq
    n_keep = min(TOPK_KEYS, s // 4)
    blk = lambda w: pl.BlockSpec((tq, w), lambda b, i: (b * nq + i, 0))
    seq = lambda w: pl.BlockSpec((s, w), lambda b, i: (b, 0))
    return pl.pallas_call(
        functools.partial(_attn_prompt_kernel, n_keep=n_keep, seq_len=s),
        grid=(nb, nq),
        in_specs=[blk(q.shape[1]), seq(kvb.shape[1]), blk(qi.shape[1]), seq(kib.shape[1]), blk(idx.shape[1])],
        out_specs=blk(N_HEADS * HEAD_DIM),
        out_shape=jax.ShapeDtypeStruct((t, N_HEADS * HEAD_DIM), BF16),
        scratch_shapes=[pltpu.VMEM((nq, tq, tq), I32), pltpu.VMEM((nq, tq, tq), BF16),
                        pltpu.VMEM((nq, tq, tq), F32),
                        pltpu.VMEM((tq, LANES), I32), pltpu.VMEM((tq, LANES), I32), pltpu.VMEM((tq, LANES), I32),
                        pltpu.VMEM((N_KV_HEADS, GROUP * tq, LANES), F32),
                        pltpu.VMEM((N_KV_HEADS, GROUP * tq, 2 * HEAD_DIM), F32),
                        pltpu.VMEM((N_KV_HEADS, SUBLANES, LANES), F32)],
        compiler_params=_params(2),
        name="attn_prompt",
    )(q, kvb, qi, kib, idx)


DECODE_PAGES_PER_ROW = 4


def _decode_scores_kernel(pt_ref, qi_ref, wi_ref, kin_ref, cx_hbm, key_o, knew_o, xbuf, sem, *, n_pages):
    b = pl.program_id(0)
    nb = pl.num_programs(0)
    ppr = DECODE_PAGES_PER_ROW
    nrow = n_pages // ppr

    def page_copy(bb, p, slot):
        return pltpu.make_async_copy(cx_hbm.at[pt_ref[bb, p]], xbuf.at[slot, p], sem.at[slot])

    def start_batch(bb, slot):
        def body(p, carry):
            page_copy(bb, p, slot).start()
            return carry
        lax.fori_loop(0, n_pages, body, 0)

    def wait_batch(bb, slot):
        def body(p, carry):
            page_copy(bb, p, slot).wait()
            return carry
        lax.fori_loop(0, n_pages, body, 0)

    @pl.when(b == 0)
    def _():
        start_batch(0, 0)

    @pl.when(b + 1 < nb)
    def _():
        start_batch(b + 1, (b + 1) % 2)

    slot = b % 2
    wait_batch(b, slot)

    qi8 = qi_ref[0]
    wi8 = wi_ref[0] * IDX_SCALE

    def score_row(r, carry):
        kx = jnp.concatenate([xbuf[slot, r * ppr + k] for k in range(ppr)], axis=1).astype(BF16)
        d = _dot(qi8, kx)
        sc = jnp.sum(jnp.maximum(d, 0.0) * wi8, axis=0, keepdims=True)
        key_o[0, pl.ds(r, 1), :] = _sort_key(sc)
        return carry

    lax.fori_loop(0, nrow, score_row, 0, unroll=4)
    dn = jnp.sum(qi8.astype(F32) * kin_ref[0].astype(F32), axis=1, keepdims=True)
    key_new = _sort_key(jnp.sum(jnp.maximum(dn, 0.0) * wi8, axis=0, keepdims=True))
    knew_o[0] = jnp.broadcast_to(key_new, (1, LANES))


def _decode_scores(page_table, qi8, wi8, kin, cxt):
    db, n_pages = page_table.shape
    nrow = n_pages // DECODE_PAGES_PER_ROW
    w = DECODE_PAGES_PER_ROW * PAGE_SIZE
    per_b = lambda a: pl.BlockSpec((1,) + a.shape[1:], lambda b, pt: (b, 0, 0))
    grid_spec = pltpu.PrefetchScalarGridSpec(
        num_scalar_prefetch=1,
        grid=(db,),
        in_specs=[per_b(qi8), per_b(wi8), per_b(kin), pl.BlockSpec(memory_space=pl.ANY)],
        out_specs=[pl.BlockSpec((1, nrow, w), lambda b, pt: (b, 0, 0)),
                   pl.BlockSpec((1, 1, LANES), lambda b, pt: (b, 0, 0))],
        scratch_shapes=[pltpu.VMEM((2, n_pages, IDX_DIM, PAGE_SIZE), F32), pltpu.SemaphoreType.DMA((2,))],
    )
    return pl.pallas_call(
        functools.partial(_decode_scores_kernel, n_pages=n_pages),
        grid_spec=grid_spec,
        out_shape=[jax.ShapeDtypeStruct((db, nrow, w), I32), jax.ShapeDtypeStruct((db, 1, LANES), I32)],
        compiler_params=_params(1),
        name="decode_scores",
    )(page_table, qi8, wi8, kin, cxt)


def _decode_pick_kernel(keys_ref, knew_ref, pt_ref, meta_o, *, n_keep):
    nb, nrow, w = keys_ref.shape
    past = nrow * w
    n_pages = pt_ref.shape[1]
    keys = keys_ref[...]
    key_new = knew_ref[...][:, :, 0:1]
    col = (lax.broadcasted_iota(I32, keys.shape, 1) * w + lax.broadcasted_iota(I32, keys.shape, 2))

    def total(x):
        c = x.astype(I32)
        part = c[:, :, :LANES]
        for k in range(1, w // LANES):
            part = part + c[:, :, k * LANES:(k + 1) * LANES]
        return jnp.sum(jnp.sum(part, axis=1, keepdims=True), axis=2, keepdims=True)

    def bit_body(bi, t):
        cand = t + lax.shift_left(jnp.int32(1), 31 - bi)
        cnt = total(keys >= cand) + (key_new >= cand).astype(I32)
        return jnp.where(cnt >= n_keep, cand, t)

    t = lax.fori_loop(0, 32, bit_body, jnp.full((nb, 1, 1), INT_MIN, I32))
    need = n_keep - (total(keys > t) + (key_new > t).astype(I32))
    nbits = past.bit_length()

    def jbit(bi, jj):
        cand = jj | lax.shift_left(jnp.int32(1), nbits - 1 - bi)
        g = total((keys == t) & (col < cand)) + ((key_new == t) & (past < cand)).astype(I32)
        return jnp.where(g < need, cand, jj)

    jj = lax.fori_loop(0, nbits, jbit, jnp.zeros((nb, 1, 1), I32))
    sel = (keys > t) | ((keys == t) & (col <= jj))
    sel_new = ((key_new > t) | ((key_new == t) & (past <= jj))).astype(I32)

    rows = nb * nrow
    incl_lane = (lax.broadcasted_iota(I32, (w, w), 0) <= lax.broadcasted_iota(I32, (w, w), 1)).astype(BF16)
    cnt_in_row = _dot(sel.astype(BF16).reshape(rows, w), incl_lane)
    row_tot = cnt_in_row[:, w - 1:w]
    ri = lax.broadcasted_iota(I32, (rows, rows), 0)
    rj = lax.broadcasted_iota(I32, (rows, rows), 1)
    earlier_row_same_seq = ((ri // nrow) == (rj // nrow)) & (rj < ri)
    row_off = _dot(earlier_row_same_seq.astype(BF16),
                   jnp.broadcast_to(row_tot, (rows, LANES)).astype(BF16))[:, 0:1]
    cnt3 = cnt_in_row.astype(BF16).reshape(nb, nrow, w)
    off3 = row_off.reshape(nb, nrow, 1)
    incl3 = (row_off + row_tot).reshape(nb, nrow, 1)
    jl = lax.broadcasted_iota(I32, (1, 1, n_keep), 2).astype(F32)
    row_j = jnp.sum((incl3 <= jl).astype(F32), axis=1, keepdims=True)
    onehot = lax.broadcasted_iota(I32, (nb, nrow, n_keep), 1).astype(F32) == row_j
    off_j = jnp.sum(jnp.where(onehot, off3, 0.0), axis=1, keepdims=True)
    local_rank = jl - off_j
    onehot_b = onehot.astype(BF16)
    page_i = lax.broadcasted_iota(I32, (n_pages, n_keep), 0)
    row_i = lax.broadcasted_iota(I32, (SUBLANES, n_keep), 0)
    for b in range(nb):
        cnt_j = lax.dot_general(cnt3[b], onehot_b[b], (((0,), (0,)), ((), ())),
                                preferred_element_type=F32)
        lane_j = jnp.sum((cnt_j <= local_rank[b]).astype(F32), axis=0, keepdims=True)
        pos = jnp.minimum((row_j[b] * w + lane_j).astype(I32), past - 1)
        phys = jnp.sum(jnp.where(page_i == pos // PAGE_SIZE, pt_ref[b], 0.0), axis=0, keepdims=True)
        tile = (phys.astype(I32) * PAGE_SIZE + pos % PAGE_SIZE) // (SUBLANES // N_KV_HEADS)
        meta_o[b] = jnp.where(row_i == 0, tile,
                              jnp.where(row_i == 1, sel_new[b], pos % (SUBLANES // N_KV_HEADS)))


def _decode_pick(keys, knew, pt_f):
    nb, nrow, w = keys.shape
    n_keep = min(TOPK_KEYS, (nrow * w + 1) // 4)
    full = lambda a: pl.BlockSpec(a.shape, lambda i: (0,) * a.ndim)
    return pl.pallas_call(
        functools.partial(_decode_pick_kernel, n_keep=n_keep),
        grid=(1,),
        in_specs=[full(keys), full(knew), full(pt_f)],
        out_specs=pl.BlockSpec((nb, SUBLANES, n_keep), lambda i: (0, 0, 0)),
        out_shape=jax.ShapeDtypeStruct((nb, SUBLANES, n_keep), I32),
        compiler_params=_params(1),
        name="decode_pick",
    )(keys, knew, pt_f)


def _decode_attend_kernel(tile_ref, q_ref, meta_ref, kn_ref, vn_ref, ck_hbm, cv_hbm, o_ref,
                          kbuf, vbuf, expand_ref, sem, *, n_keep):
    b = pl.program_id(0)
    nb = pl.num_programs(0)
    ncol = n_keep * SUBLANES

    def item_copies(bb, j, slot):
        src = pl.ds(pl.multiple_of(tile_ref[bb, j] * SUBLANES, SUBLANES), SUBLANES)
        dst = pl.ds(pl.multiple_of(j * SUBLANES, SUBLANES), SUBLANES)
        return (pltpu.make_async_copy(ck_hbm.at[src], kbuf.at[slot, dst], sem.at[0, slot]),
                pltpu.make_async_copy(cv_hbm.at[src], vbuf.at[slot, dst], sem.at[1, slot]))

    def start_batch(bb, slot):
        def body(j, carry):
            for prio, cp in enumerate(item_copies(bb, j, slot)):
                cp.start(priority=prio)
            return carry
        lax.fori_loop(0, n_keep, body, 0)

    def wait_batch(slot):
        pltpu.make_async_copy(ck_hbm.at[pl.ds(0, ncol)], kbuf.at[slot], sem.at[0, slot]).wait()
        pltpu.make_async_copy(cv_hbm.at[pl.ds(0, ncol)], vbuf.at[slot], sem.at[1, slot]).wait()

    @pl.when(b == 0)
    def _():
        start_batch(0, 0)
        item_of_col = lax.broadcasted_iota(I32, (n_keep, ncol), 1) // SUBLANES
        expand_ref[...] = (item_of_col == lax.broadcasted_iota(I32, (n_keep, ncol), 0)).astype(BF16)

    @pl.when(b + 1 < nb)
    def _():
        start_batch(b + 1, (b + 1) % 2)

    slot = b % 2
    wait_batch(slot)

    q8 = q_ref[0]
    meta = meta_ref[0]
    new_kept = meta[1:2, 0:1] > 0
    sub_col = _dot(meta.astype(BF16), expand_ref[...])[2:3, :]
    head_i = lax.broadcasted_iota(I32, (N_HEADS, ncol), 0)
    col_i = lax.broadcasted_iota(I32, (N_HEADS, ncol), 1)
    want = sub_col * N_KV_HEADS + (head_i // GROUP).astype(F32)
    is_new_item = new_kept & (col_i // SUBLANES == n_keep - 1)
    ok = ((col_i % SUBLANES).astype(F32) == want) & jnp.logical_not(is_new_item)

    s = jnp.where(ok, _dot_nt(q8, kbuf[slot].astype(BF16)), NEG)
    s_new = jnp.sum(q8.astype(F32) * kn_ref[0].astype(F32), axis=1, keepdims=True)
    s_new = jnp.where(new_kept, s_new, NEG)
    m = jnp.maximum(jnp.max(s, axis=1, keepdims=True), s_new)
    p = jnp.exp(s - m)
    p_new = jnp.exp(s_new - m)
    l = jnp.sum(p, axis=1, keepdims=True) + p_new
    acc = _dot(p.astype(BF16), vbuf[slot].astype(BF16)) + p_new.astype(BF16).astype(F32) * vn_ref[0].astype(F32)
    o_ref[0] = (acc / l).astype(BF16)


def _decode_attend(tiles, q8, meta, kn8, vn8, ck2, cv2):
    db, n_keep = tiles.shape
    per_b = lambda a: pl.BlockSpec((1,) + a.shape[1:], lambda b, tl: (b, 0, 0))
    any_spec = pl.BlockSpec(memory_space=pl.ANY)
    ncol = n_keep * SUBLANES
    grid_spec = pltpu.PrefetchScalarGridSpec(
        num_scalar_prefetch=1,
        grid=(db,),
        in_specs=[per_b(q8), per_b(meta), per_b(kn8), per_b(vn8), any_spec, any_spec],
        out_specs=pl.BlockSpec((1, N_HEADS, HEAD_DIM), lambda b, tl: (b, 0, 0)),
        scratch_shapes=[pltpu.VMEM((2, ncol, HEAD_DIM), F32), pltpu.VMEM((2, ncol, HEAD_DIM), F32),
                        pltpu.VMEM((n_keep, ncol), BF16), pltpu.SemaphoreType.DMA((2, 2))],
    )
    return pl.pallas_call(
        functools.partial(_decode_attend_kernel, n_keep=n_keep),
        grid_spec=grid_spec,
        out_shape=jax.ShapeDtypeStruct((db, N_HEADS, HEAD_DIM), BF16),
        compiler_params=_params(1),
        name="decode_attend",
    )(tiles, q8, meta, kn8, vn8, ck2, cv2)


def _layer_norm(r, g, b):
    mu = jnp.mean(r, axis=-1, keepdims=True)
    d = r - mu
    var = jnp.mean(d * d, axis=-1, keepdims=True)
    return d * lax.rsqrt(var + LN_EPS) * g + b


def _route(logits_t, rbias):
    n_exp, tm = logits_t.shape
    epg = n_exp // N_GROUPS
    s = jax.nn.sigmoid(logits_t)
    sb = s + rbias
    ie = lax.broadcasted_iota(I32, (epg, tm), 0)
    gs_rows = []
    for g in range(N_GROUPS):
        blk = sb[g * epg:(g + 1) * epg, :]
        m1 = jnp.max(blk, axis=0, keepdims=True)
        i1 = jnp.min(jnp.where(blk == m1, ie, epg), axis=0, keepdims=True)
        m2 = jnp.max(jnp.where(ie == i1, -jnp.inf, blk), axis=0, keepdims=True)
        gs_rows.append(m1 + m2)
    picked = [jnp.zeros((1, tm), jnp.bool_) for _ in range(N_GROUPS)]
    cur = list(gs_rows)
    for _ in range(TOPK_GROUPS):
        mx = cur[0]
        for g in range(1, N_GROUPS):
            mx = jnp.maximum(mx, cur[g])
        found = jnp.zeros((1, tm), jnp.bool_)
        for g in range(N_GROUPS):
            hit = (cur[g] == mx) & jnp.logical_not(found)
            found = found | hit
            picked[g] = picked[g] | hit
            cur[g] = jnp.where(hit, -jnp.inf, cur[g])
    masked = jnp.concatenate(
        [jnp.where(picked[g], sb[g * epg:(g + 1) * epg, :], NEG) for g in range(N_GROUPS)], axis=0)
    iall = lax.broadcasted_iota(I32, (n_exp, tm), 0)
    e_rows, w_rows = [], []
    for _ in range(TOP_K):
        mx = jnp.max(masked, axis=0, keepdims=True)
        ix = jnp.min(jnp.where(masked == mx, iall, n_exp), axis=0, keepdims=True)
        hit = iall == ix
        w_rows.append(jnp.sum(jnp.where(hit, s, 0.0), axis=0, keepdims=True))
        e_rows.append(ix)
        masked = jnp.where(hit, -jnp.inf, masked)
    wsum = w_rows[0]
    for w in w_rows[1:]:
        wsum = wsum + w
    gates = [w / wsum * ROUTED_SCALE for w in w_rows]
    return jnp.concatenate(e_rows, axis=0), jnp.concatenate(gates, axis=0)


def _post_attn_kernel(xp, xs, ap, as_, cp, cs, gap, gas, gcp, gcs, woa, woc, wo, g1, b1, rwt, rb,
                      h2_o, e_o, gate_o, *, n_prompt_tiles):
    i = pl.program_id(0)
    is_p = i < n_prompt_tiles
    pick = lambda a, b: jnp.where(is_p, a[...], b[...])
    a = _dot(pick(ap, as_), woa[...])
    c = _dot(pick(cp, cs), woc[...])
    merged = pick(gap, gas).astype(F32) * a + pick(gcp, gcs).astype(F32) * c
    r = ALPHA * pick(xp, xs) + _dot(merged.astype(BF16), wo[...])
    h = _layer_norm(r, g1[...], b1[...])
    tm = h.shape[0]
    for k in range(D_MODEL // LANES):
        h2_o[pl.ds(k, tm, stride=SUBLANES), :] = h[:, k * LANES:(k + 1) * LANES]
    logits_t = lax.dot_general(rwt[...], h, (((1,), (1,)), ((), ())),
                               precision=lax.Precision.HIGHEST, preferred_element_type=F32)
    e_idx, gates = _route(logits_t, rb[...])
    e_o[...] = e_idx
    gate_o[...] = gates


def _post_attn(x_p, x_s, attn_p, attn_s, conv_p, conv_s, ga_p, ga_s, gc_p, gc_s,
               woa, woc, wo, g1, b1, rwt, rb):
    tp, ts = x_p.shape[0], x_s.shape[0]
    tm = TOKEN_TILE
    npt, nst = tp // tm, ts // tm
    n_tok = tp + ts
    p_spec = pl.BlockSpec((tm, D_MODEL), lambda i: (jnp.minimum(i, npt - 1), 0))
    s_spec = pl.BlockSpec((tm, D_MODEL), lambda i: (jnp.maximum(i - npt, 0), 0))
    consts = [woa, woc, wo, g1, b1, rwt, rb]
    return pl.pallas_call(
        functools.partial(_post_attn_kernel, n_prompt_tiles=npt),
        grid=(npt + nst,),
        in_specs=[p_spec, s_spec] * 5 + [_const_spec(c.shape) for c in consts],
        out_specs=[pl.BlockSpec((tm * SUBLANES, LANES), lambda i: (i, 0)),
                   pl.BlockSpec((TOP_K, tm), lambda i: (0, i)),
                   pl.BlockSpec((TOP_K, tm), lambda i: (0, i))],
        out_shape=[jax.ShapeDtypeStruct((n_tok * SUBLANES, LANES), F32),
                   jax.ShapeDtypeStruct((TOP_K, n_tok), I32),
                   jax.ShapeDtypeStruct((TOP_K, n_tok), F32)],
        compiler_params=_params(1),
        name="post_attn",
    )(x_p, x_s, attn_p, attn_s, conv_p, conv_s, ga_p, ga_s, gc_p, gc_s, *consts)


def _rank_kernel(e_ref, rank_o, cnt_o, carry_ref):
    i = pl.program_id(0)

    @pl.when(i == 0)
    def _():
        carry_ref[...] = jnp.zeros_like(carry_ref)

    n_exp = carry_ref.shape[0]
    e = e_ref[...]
    tk = e.shape[1]
    ie = lax.broadcasted_iota(I32, (n_exp, tk), 0)
    onehot = jnp.zeros((n_exp, tk), F32)
    for j in range(TOP_K):
        onehot = onehot + (ie == e[j:j + 1, :]).astype(F32)
    before = (lax.broadcasted_iota(I32, (tk, tk), 0) < lax.broadcasted_iota(I32, (tk, tk), 1)).astype(BF16)
    prefix = _dot(onehot.astype(BF16), before) + carry_ref[:, 0:1]
    rows = [jnp.sum(jnp.where(ie == e[j:j + 1, :], prefix, 0.0), axis=0, keepdims=True) for j in range(TOP_K)]
    rank_o[...] = jnp.concatenate(rows, axis=0).astype(I32)
    carry_ref[...] = carry_ref[...] + jnp.sum(onehot, axis=1, keepdims=True)
    cnt_o[...] = carry_ref[...].astype(I32)


def _rank(e_t, n_exp):
    n_tok = e_t.shape[1]
    tk = TOKEN_TILE
    return pl.pallas_call(
        _rank_kernel,
        grid=(n_tok // tk,),
        in_specs=[pl.BlockSpec((TOP_K, tk), lambda i: (0, i))],
        out_specs=[pl.BlockSpec((TOP_K, tk), lambda i: (0, i)),
                   pl.BlockSpec((n_exp, LANES), lambda i: (0, 0))],
        out_shape=[jax.ShapeDtypeStruct((TOP_K, n_tok), I32), jax.ShapeDtypeStruct((n_exp, LANES), I32)],
        scratch_shapes=[pltpu.VMEM((n_exp, LANES), F32)],
        compiler_params=_params(1),
        name="moe_rank",
    )(e_t)


def _slot_kernel(e_ref, rank_ref, pstart_ref, slot_o):
    e = e_ref[...]
    n_exp = pstart_ref.shape[0]
    tk = e.shape[1]
    ie = lax.broadcasted_iota(I32, (n_exp, tk), 0)
    ps = pstart_ref[:, 0:1]
    rows = [jnp.sum(jnp.where(ie == e[j:j + 1, :], ps, 0.0), axis=0, keepdims=True) for j in range(TOP_K)]
    slot_o[...] = rank_ref[...] + jnp.concatenate(rows, axis=0).astype(I32)


def _slots(e_t, rank_t, pstart_f):
    n_tok = e_t.shape[1]
    tk = TOKEN_TILE
    spec = pl.BlockSpec((TOP_K, tk), lambda i: (0, i))
    return pl.pallas_call(
        _slot_kernel,
        grid=(n_tok // tk,),
        in_specs=[spec, spec, _const_spec(pstart_f.shape)],
        out_specs=spec,
        out_shape=jax.ShapeDtypeStruct((TOP_K, n_tok), I32),
        compiler_params=_params(1),
        name="moe_slots",
    )(e_t, rank_t, pstart_f)


def _dispatch_kernel(pend_ref, padded_ref, slot_ref, h2_ref, xs_hbm, zbuf, sem, zsem, *, n_exp):
    i = pl.program_id(0)
    td = slot_ref.shape[1]
    blk_rows = MOE_BLOCK * SUBLANES

    def zero_copy(e):
        dst = pl.ds(pl.multiple_of((pend_ref[e] - MOE_BLOCK) * SUBLANES, SUBLANES), blk_rows)
        return pltpu.make_async_copy(zbuf, xs_hbm.at[dst], zsem)

    @pl.when(i == 0)
    def _():
        zbuf[...] = jnp.zeros_like(zbuf)

        def start(e, carry):
            @pl.when(padded_ref[e] > 0)
            def _():
                zero_copy(e).start()
            return carry

        def wait(e, carry):
            @pl.when(padded_ref[e] > 0)
            def _():
                zero_copy(e).wait()
            return carry

        lax.fori_loop(0, n_exp, start, 0)
        lax.fori_loop(0, n_exp, wait, 0)

    def row_copy(t, j):
        src = pl.ds(pl.multiple_of(t * SUBLANES, SUBLANES), SUBLANES)
        dst = pl.ds(pl.multiple_of(slot_ref[j, t] * SUBLANES, SUBLANES), SUBLANES)
        return pltpu.make_async_copy(h2_ref.at[src], xs_hbm.at[dst], sem)

    def start_tok(t, carry):
        for j in range(TOP_K):
            row_copy(t, j).start(priority=j % 2)
        return carry

    def wait_tok(t, carry):
        for j in range(TOP_K):
            row_copy(t, j).wait()
        return carry

    lax.fori_loop(0, td, start_tok, 0)
    lax.fori_loop(0, td, wait_tok, 0)


def _dispatch(pend, padded, slot_t, h2, m_pad):
    n_tok = slot_t.shape[1]
    td = TOKEN_TILE
    n_exp = pend.shape[0]
    grid_spec = pltpu.PrefetchScalarGridSpec(
        num_scalar_prefetch=2,
        grid=(n_tok // td,),
        in_specs=[pl.BlockSpec((TOP_K, td), lambda i, *_: (0, i), memory_space=pltpu.SMEM),
                  pl.BlockSpec((td * SUBLANES, LANES), lambda i, *_: (i, 0))],
        out_specs=pl.BlockSpec(memory_space=pl.ANY),
        scratch_shapes=[pltpu.VMEM((MOE_BLOCK * SUBLANES, LANES), F32),
                        pltpu.SemaphoreType.DMA(()), pltpu.SemaphoreType.DMA(())],
    )
    return pl.pallas_call(
        functools.partial(_dispatch_kernel, n_exp=n_exp),
        grid_spec=grid_spec,
        out_shape=jax.ShapeDtypeStruct((m_pad * SUBLANES, LANES), F32),
        compiler_params=_params(1),
        name="moe_dispatch",
    )(pend, padded, slot_t, h2)


def _tile_rows(ref, n):
    return jnp.concatenate([ref[pl.ds(k, n, stride=SUBLANES), :] for k in range(D_MODEL // LANES)], axis=1)


PLAN_EXPERT, PLAN_SLOT, PLAN_NEXT, PLAN_HAS_NEXT = range(4)


def _expert_kernel(plan_ref, nused_ref, x_ref, wg_hbm, wu_hbm, wd_hbm, y_ref, wg_f, wu_f, wd_f, wgu_b, wd_b, sem):
    i = pl.program_id(0)
    d_exp = wd_b.shape[0]

    def weight_copies(e, s):
        return (pltpu.make_async_copy(wg_hbm.at[e], wg_f.at[s], sem.at[s]),
                pltpu.make_async_copy(wu_hbm.at[e], wu_f.at[s], sem.at[s]),
                pltpu.make_async_copy(wd_hbm.at[e], wd_f.at[s], sem.at[s]))

    @pl.when(i < nused_ref[0])
    def _():
        e = plan_ref[PLAN_EXPERT, i]
        changed = (i == 0) | (e != plan_ref[PLAN_EXPERT, jnp.maximum(i - 1, 0)])

        @pl.when(changed)
        def _():
            s = plan_ref[PLAN_SLOT, i]

            @pl.when(i == 0)
            def _():
                for cp in weight_copies(e, s):
                    cp.start()

            for cp in weight_copies(e, s):
                cp.wait()

            @pl.when(plan_ref[PLAN_HAS_NEXT, i] > 0)
            def _():
                for cp in weight_copies(plan_ref[PLAN_NEXT, i], 1 - s):
                    cp.start()

            wgu_b[:, :d_exp] = wg_f[s].astype(BF16)
            wgu_b[:, d_exp:] = wu_f[s].astype(BF16)
            wd_b[...] = wd_f[s].astype(BF16)

        x = _tile_rows(x_ref, MOE_BLOCK).astype(BF16)
        gu = _dot(x, wgu_b[...])
        hh = jax.nn.silu(gu[:, :d_exp]) * gu[:, d_exp:]
        y = _dot(hh.astype(BF16), wd_b[...])
        for k in range(D_MODEL // LANES):
            y_ref[pl.ds(k, MOE_BLOCK, stride=SUBLANES), :] = y[:, k * LANES:(k + 1) * LANES]


def _expert_plan(block_e, nused):
    n_blocks = block_e.shape[0]
    idx = jnp.arange(n_blocks, dtype=I32)
    changed = (idx == 0) | (block_e != jnp.roll(block_e, 1))
    slot = (jnp.cumsum(changed.astype(I32)) - 1) % 2
    change_pos = jnp.where(changed & (idx < nused), idx, n_blocks)
    next_pos = jnp.concatenate([lax.cummin(change_pos[::-1])[::-1][1:], jnp.full((1,), n_blocks, I32)])
    has_next = (next_pos < n_blocks).astype(I32)
    nxt = block_e[jnp.minimum(next_pos, n_blocks - 1)]
    return jnp.stack([block_e, slot, nxt, has_next]).astype(I32)


def _experts(block_e, nused, xs, w_gate, w_up, w_down):
    n_blocks = block_e.shape[0]
    d_exp = w_gate.shape[2]
    rows = MOE_BLOCK * SUBLANES
    last = lambda i, nu: jnp.minimum(i, nu[0] - 1)
    any_spec = pl.BlockSpec(memory_space=pl.ANY)
    grid_spec = pltpu.PrefetchScalarGridSpec(
        num_scalar_prefetch=2,
        grid=(n_blocks,),
        in_specs=[pl.BlockSpec((rows, LANES), lambda i, plan, nu: (last(i, nu), 0)), any_spec, any_spec, any_spec],
        out_specs=pl.BlockSpec((rows, LANES), lambda i, plan, nu: (last(i, nu), 0)),
        scratch_shapes=[pltpu.VMEM((2, D_MODEL, d_exp), F32), pltpu.VMEM((2, D_MODEL, d_exp), F32),
                        pltpu.VMEM((2, d_exp, D_MODEL), F32),
                        pltpu.VMEM((D_MODEL, 2 * d_exp), BF16), pltpu.VMEM((d_exp, D_MODEL), BF16),
                        pltpu.SemaphoreType.DMA((2,))],
    )
    return pl.pallas_call(
        _expert_kernel,
        grid_spec=grid_spec,
        out_shape=jax.ShapeDtypeStruct(xs.shape, F32),
        compiler_params=_params(1),
        name="moe_experts",
    )(_expert_plan(block_e, nused[0]), nused, xs, w_gate, w_up, w_down)


def _combine_kernel(slot_ref, slot_next_ref, gate_ref, h2_ref, ys_hbm, shg, shu, shd, g2, b2, yp_o, ys_o,
                    buf, sem, *, n_prompt_tiles):
    i = pl.program_id(0)
    n = pl.num_programs(0)
    tc = gate_ref.shape[0]

    def start_tile(s_ref, bslot):
        def body(t, carry):
            dst = pl.ds(pl.multiple_of(t * SUBLANES, SUBLANES), SUBLANES)
            for j in range(TOP_K):
                src = pl.ds(pl.multiple_of(s_ref[j, t] * SUBLANES, SUBLANES), SUBLANES)
                pltpu.make_async_copy(ys_hbm.at[src], buf.at[bslot, j, dst], sem.at[bslot]).start(priority=j % 2)
            return carry
        lax.fori_loop(0, tc, body, 0)

    def wait_tile(bslot):
        for j in range(TOP_K):
            pltpu.make_async_copy(ys_hbm.at[pl.ds(0, tc * SUBLANES)], buf.at[bslot, j], sem.at[bslot]).wait()

    @pl.when(i == 0)
    def _():
        start_tile(slot_ref, 0)

    @pl.when(i + 1 < n)
    def _():
        start_tile(slot_next_ref, (i + 1) % 2)

    cur = i % 2
    h = _tile_rows(h2_ref, tc)
    hb = h.astype(BF16)
    shared = _dot((jax.nn.silu(_dot(hb, shg[...])) * _dot(hb, shu[...])).astype(BF16), shd[...])
    wait_tile(cur)
    routed = jnp.zeros((tc, D_MODEL), F32)
    for j in range(TOP_K):
        routed = routed + _tile_rows(buf.at[cur, j], tc) * gate_ref[:, j:j + 1]
    y = _layer_norm(ALPHA * h + (routed + shared), g2[...], b2[...])

    @pl.when(i < n_prompt_tiles)
    def _():
        yp_o[...] = y

    @pl.when(i >= n_prompt_tiles)
    def _():
        ys_o[...] = y


def _combine(slot_t, gates, h2, ys, shg, shu, shd, g2, b2, n_prompt, n_decode):
    tc = TOKEN_TILE
    npt, nst = n_prompt // tc, n_decode // tc
    consts = [shg, shu, shd, g2, b2]
    return pl.pallas_call(
        functools.partial(_combine_kernel, n_prompt_tiles=npt),
        grid=(npt + nst,),
        in_specs=[pl.BlockSpec((TOP_K, tc), lambda i: (0, i), memory_space=pltpu.SMEM),
                  pl.BlockSpec((TOP_K, tc), lambda i: (0, jnp.minimum(i + 1, npt + nst - 1)),
                               memory_space=pltpu.SMEM),
                  pl.BlockSpec((tc, TOP_K), lambda i: (i, 0)),
                  pl.BlockSpec((tc * SUBLANES, LANES), lambda i: (i, 0)),
                  pl.BlockSpec(memory_space=pl.ANY)] + [_const_spec(c.shape) for c in consts],
        out_specs=[pl.BlockSpec((tc, D_MODEL), lambda i: (jnp.minimum(i, npt - 1), 0)),
                   pl.BlockSpec((tc, D_MODEL), lambda i: (jnp.maximum(i - npt, 0), 0))],
        out_shape=[jax.ShapeDtypeStruct((n_prompt, D_MODEL), F32),
                   jax.ShapeDtypeStruct((n_decode, D_MODEL), F32)],
        scratch_shapes=[pltpu.VMEM((2, TOP_K, tc * SUBLANES, LANES), F32), pltpu.SemaphoreType.DMA((2,))],
        compiler_params=_params(1),
        name="moe_combine",
    )(slot_t, slot_t, gates, h2, ys, *consts)


def _pad_rows(a, n):
    return jnp.pad(a, ((0, n - a.shape[0]),) + ((0, 0),) * (a.ndim - 1))


def kernel(x_prompt, x_sample, cache_k, cache_v, cache_kidx, page_table, state_conv, w_in, conv_w, w_o_attn,
           w_o_conv, w_o, ln1_g, ln1_b, router_w, router_bias, moe_w_gate, moe_w_up, moe_w_down,
           shared_w_gate, shared_w_up, shared_w_down, ln2_g, ln2_b):
    nb, seq, _ = x_prompt.shape
    db = x_sample.shape[0]
    n_pool = cache_k.shape[1]
    n_exp = router_w.shape[-1]
    tp = nb * seq
    ts = TOKEN_TILE
    nkv = N_KV_HEADS * HEAD_DIM

    weights = _prep_in_weights(w_in[0])
    cw = conv_w[0]

    xp2 = x_prompt.reshape(tp, D_MODEL)
    (q_p, k_p, v_p, kvb_p, qi_p, idx_p, kib_p, conv_p, ga_p, gc_p, ulast_p) = _in_proj_prompt(xp2, weights, cw, nb)
    xs2 = _pad_rows(x_sample.reshape(db, D_MODEL), ts)
    prev2 = _pad_rows(state_conv[0, :, 0, :], ts)
    prev1 = _pad_rows(state_conv[0, :, 1, :], ts)
    (q_s, k_s, v_s, _, qi_s, idx_s, _, conv_s, ga_s, gc_s, u_s) = _in_proj_decode(xs2, prev2, prev1, weights, cw)

    attn_p = _attn_prompt(q_p, kvb_p, qi_p, kib_p, idx_p, nb)
    q8 = q_s[:db].reshape(db, N_HEADS, HEAD_DIM)
    qi8 = qi_s[:db].reshape(db, IDX_HEADS, LANES)[:, :, :IDX_DIM]
    wi8 = idx_s[:db, IDX_DIM:IDX_DIM + IDX_HEADS].reshape(db, IDX_HEADS, 1)
    kin = idx_s[:db, :IDX_DIM].astype(BF16).reshape(db, 1, IDX_DIM)
    kn8 = jnp.repeat(k_s[:db].reshape(db, N_KV_HEADS, HEAD_DIM), GROUP, axis=1).astype(BF16)
    vn8 = jnp.repeat(v_s[:db].reshape(db, N_KV_HEADS, HEAD_DIM), GROUP, axis=1).astype(BF16)
    ck2 = cache_k[0].reshape(n_pool * PAGE_SIZE * N_KV_HEADS, HEAD_DIM)
    cv2 = cache_v[0].reshape(n_pool * PAGE_SIZE * N_KV_HEADS, HEAD_DIM)
    keys_s, knew_s = _decode_scores(page_table, qi8, wi8, kin, jnp.swapaxes(cache_kidx[0], 1, 2))
    meta = _decode_pick(keys_s, knew_s, page_table.astype(F32)[:, :, None])
    attn_s8 = _decode_attend(meta[:, 0, :], q8, meta, kn8, vn8, ck2, cv2)
    attn_s = _pad_rows(attn_s8.reshape(db, N_HEADS * HEAD_DIM), ts)

    h2, e_t, gate_t = _post_attn(
        xp2, xs2, attn_p, attn_s, conv_p, conv_s, ga_p, ga_s, gc_p, gc_s,
        w_o_attn[0].astype(BF16), w_o_conv[0].astype(BF16), w_o[0].astype(BF16),
        ln1_g[0].reshape(1, D_MODEL), ln1_b[0].reshape(1, D_MODEL),
        router_w[0].T, router_bias[0].reshape(n_exp, 1))

    n_tok = tp + ts
    rank_t, cnt = _rank(e_t, n_exp)
    counts = cnt[:, 0]
    padded = (counts + MOE_BLOCK - 1) // MOE_BLOCK * MOE_BLOCK
    pend = jnp.cumsum(padded)
    pstart = pend - padded
    n_blocks = (n_tok * TOP_K + n_exp * (MOE_BLOCK - 1) + MOE_BLOCK - 1) // MOE_BLOCK
    nused = (pend[-1] // MOE_BLOCK).astype(I32)
    blk = jnp.minimum(jnp.arange(n_blocks, dtype=I32), nused - 1)
    block_e = jnp.minimum(jnp.sum((pend[None, :] <= (blk * MOE_BLOCK)[:, None]).astype(I32), axis=1), n_exp - 1)
    slot_t = _slots(e_t, rank_t, jnp.broadcast_to(pstart.astype(F32)[:, None], (n_exp, LANES)))
    xs = _dispatch(pend.astype(I32), padded.astype(I32), slot_t, h2, n_blocks * MOE_BLOCK)
    ys = _experts(block_e, nused.reshape(1), xs, moe_w_gate[0], moe_w_up[0], moe_w_down[0])
    y_p, y_s = _combine(slot_t, gate_t.T, h2, ys,
                        shared_w_gate[0].astype(BF16), shared_w_up[0].astype(BF16),
                        shared_w_down[0].astype(BF16),
                        ln2_g[0].reshape(1, D_MODEL), ln2_b[0].reshape(1, D_MODEL), tp, ts)

    conv_sample = jnp.stack([state_conv[0, :, 1, :], u_s[:db]], axis=1)[None]
    return (y_p.reshape(nb, seq, D_MODEL),
            y_s[:db].reshape(db, 1, D_MODEL),
            k_p.reshape(1, nb, seq, N_KV_HEADS, HEAD_DIM),
            v_p.reshape(1, nb, seq, N_KV_HEADS, HEAD_DIM),
            idx_p[:, :IDX_DIM].reshape(1, nb, seq, IDX_DIM),
            ulast_p[:, SUBLANES - (CONV_W - 1):, :][None],
            k_s[:db].reshape(1, db, 1, N_KV_HEADS, HEAD_DIM),
            v_s[:db].reshape(1, db, 1, N_KV_HEADS, HEAD_DIM),
            idx_s[:db, :IDX_DIM].reshape(1, db, 1, IDX_DIM),
            conv_sample)
```

```python
import functools

import jax
import jax.numpy as jnp
import numpy as np
from jax import lax
from jax.experimental import pallas as pl
from jax.experimental.pallas import tpu as pltpu

F32 = jnp.float32
BF16 = jnp.bfloat16
I32 = jnp.int32

D_MODEL = 1024
N_HEADS = 8
HEAD_DIM = 128
N_KV_HEADS = 2
GROUP = N_HEADS // N_KV_HEADS
IDX_HEADS = 8
IDX_DIM = 64
TOPK_KEYS = 256
IDX_SCALE = IDX_DIM ** -0.5 * IDX_HEADS ** -0.5
Q_SCALE = HEAD_DIM ** -0.5
PAGE_SIZE = 128
D_CONV = D_MODEL
CONV_W = 3
TOP_K = 8
N_GROUPS = 8
TOPK_GROUPS = 4
ROUTED_SCALE = 2.5
DEPTH = 1
ALPHA = (2 * DEPTH) ** 0.25
LN_EPS = 1e-5
NEG = -1e30
IN_WIDTHS = (N_HEADS * HEAD_DIM, N_KV_HEADS * HEAD_DIM, N_KV_HEADS * HEAD_DIM,
             IDX_HEADS * IDX_DIM, IDX_DIM, IDX_HEADS,
             D_CONV, D_CONV, D_CONV, D_MODEL, D_MODEL)

LANES = 128
SUBLANES = 8
VMEM_LIMIT_BYTES = 56 * 1024 * 1024

PROJ_ROWS = 512
ATT_Q = 256
TOKEN_TILE = 256
MOE_BLOCK = 256
CONV_CHUNK = 256
INT_MIN = -2 ** 31
SOFTMAX_DENOM_MIN = 1e-25
F32_MIN_NORMAL = float(np.finfo(np.float32).tiny)
BF16_MIN_NORMAL_BITS = 0x0080


def _np_key(v):
    b = int(np.float32(v).view(np.int32))
    return b ^ ((b >> 31) & 0x7FFFFFFF)


KEY_HALF = _np_key(NEG * 0.5)


def _sort_key(s):
    b = lax.bitcast_convert_type(s + 0.0, I32)
    return b ^ ((b >> 31) & jnp.int32(0x7FFFFFFF))


def _dot(a, b):
    return jnp.dot(a, b, preferred_element_type=F32)


def _dot_nt(a, b):
    return lax.dot_general(a, b, (((1,), (1,)), ((), ())), preferred_element_type=F32)


def _params(n_grid):
    return pltpu.CompilerParams(dimension_semantics=("arbitrary",) * n_grid,
                                vmem_limit_bytes=VMEM_LIMIT_BYTES)


def _const_spec(shape):
    nd = len(shape)
    return pl.BlockSpec(shape, lambda *_: (0,) * nd, pipeline_mode=pl.Buffered(1))


def _proj_common(xb, w1, w2, w3, wga, wgc, q_o, k_o, v_o, kv_o, qi_o, idx_o, kib_o, sga_o, sgc_o):
    nq = N_HEADS * HEAD_DIM
    nkv = N_KV_HEADS * HEAD_DIM
    z1 = _dot(xb, w1[...])
    q_o[...] = (z1[:, :nq] * Q_SCALE).astype(BF16)
    k = z1[:, nq:nq + nkv]
    v = z1[:, nq + nkv:nq + 2 * nkv]
    k_o[...] = k
    v_o[...] = v
    kv_o[:, :nkv] = k.astype(BF16)
    kv_o[:, nkv:] = v.astype(BF16)
    qi_o[...] = _dot(xb, w2[...]).astype(BF16)
    z3 = _dot(xb, w3[...])
    idx_o[...] = z3
    kib_o[...] = z3.astype(BF16)
    sga_o[...] = jax.nn.sigmoid(_dot(xb, wga[...])).astype(BF16)
    sgc_o[...] = jax.nn.sigmoid(_dot(xb, wgc[...])).astype(BF16)


def _in_proj_prompt_kernel(x_ref, w1, w2, w3, wb, wc, wx, wga, wgc, cw_ref,
                           q_o, k_o, v_o, kv_o, qi_o, idx_o, kib_o, conv_o, sga_o, sgc_o, ulast_o,
                           tail_ref):
    j = pl.program_id(1)

    @pl.when(j == 0)
    def _():
        tail_ref[...] = jnp.zeros_like(tail_ref)

    xb = x_ref[...].astype(BF16)
    _proj_common(xb, w1, w2, w3, wga, wgc, q_o, k_o, v_o, kv_o, qi_o, idx_o, kib_o, sga_o, sgc_o)
    tm = xb.shape[0]
    row = lax.broadcasted_iota(I32, (tm, CONV_CHUNK), 0)
    for c in range(D_CONV // CONV_CHUNK):
        cs = slice(c * CONV_CHUNK, (c + 1) * CONV_CHUNK)
        gb = _dot(xb, wb[:, cs])
        u = _dot(xb, wc[:, cs]) * _dot(xb, wx[:, cs])
        p1 = tail_ref[SUBLANES - 1:SUBLANES, cs]
        p2 = tail_ref[SUBLANES - 2:SUBLANES - 1, cs]
        u1 = jnp.where(row == 0, p1, pltpu.roll(u, 1, 0))
        u2 = jnp.where(row == 0, p2, jnp.where(row == 1, p1, pltpu.roll(u, 2, 0)))
        conv = cw_ref[0:1, cs] * u2 + cw_ref[1:2, cs] * u1 + cw_ref[2:3, cs] * u
        conv_o[:, cs] = (gb * conv).astype(BF16)
        tail_ref[:, cs] = u[tm - SUBLANES:, :]
    ulast_o[0] = tail_ref[...]


def _in_proj_decode_kernel(x_ref, p2_ref, p1_ref, w1, w2, w3, wb, wc, wx, wga, wgc, cw_ref,
                           q_o, k_o, v_o, kv_o, qi_o, idx_o, kib_o, conv_o, sga_o, sgc_o, u_o):
    xb = x_ref[...].astype(BF16)
    _proj_common(xb, w1, w2, w3, wga, wgc, q_o, k_o, v_o, kv_o, qi_o, idx_o, kib_o, sga_o, sgc_o)
    for c in range(D_CONV // CONV_CHUNK):
        cs = slice(c * CONV_CHUNK, (c + 1) * CONV_CHUNK)
        gb = _dot(xb, wb[:, cs])
        u = _dot(xb, wc[:, cs]) * _dot(xb, wx[:, cs])
        conv = cw_ref[0:1, cs] * p2_ref[:, cs] + cw_ref[1:2, cs] * p1_ref[:, cs] + cw_ref[2:3, cs] * u
        conv_o[:, cs] = (gb * conv).astype(BF16)
        u_o[:, cs] = u


def _prep_in_weights(w_in):
    w = w_in.astype(BF16)
    offs = np.concatenate([[0], np.cumsum(IN_WIDTHS)])
    part = [w[:, int(offs[i]):int(offs[i + 1])] for i in range(len(IN_WIDTHS))]
    q, k, v, qi, ki, wi, gb, gc, xv, ga, gcv = part
    d = w.shape[0]
    w1 = jnp.concatenate([q, k, v], axis=1)
    w2 = jnp.pad(qi.reshape(d, IDX_HEADS, IDX_DIM),
                 ((0, 0), (0, 0), (0, LANES - IDX_DIM))).reshape(d, IDX_HEADS * LANES)
    w3 = jnp.concatenate([ki, wi, jnp.zeros((d, LANES - IDX_DIM - IDX_HEADS), BF16)], axis=1)
    return (w1, w2, w3, gb, gc, xv, ga, gcv)


def _proj_out_shapes(t):
    nkv = N_KV_HEADS * HEAD_DIM
    sd = jax.ShapeDtypeStruct
    return [sd((t, N_HEADS * HEAD_DIM), BF16), sd((t, nkv), F32), sd((t, nkv), F32), sd((t, 2 * nkv), BF16),
            sd((t, IDX_HEADS * LANES), BF16), sd((t, LANES), F32), sd((t, LANES), BF16),
            sd((t, D_CONV), BF16), sd((t, D_MODEL), BF16), sd((t, D_MODEL), BF16)]


def _in_proj_prompt(x2d, weights, conv_w, nb):
    t = x2d.shape[0]
    s = t // nb
    tm = min(PROJ_ROWS, s)
    nj = s // tm
    row_spec = lambda w: pl.BlockSpec((tm, w), lambda b, j: (b * nj + j, 0))
    out_shapes = _proj_out_shapes(t) + [jax.ShapeDtypeStruct((nb, SUBLANES, D_CONV), F32)]
    out_specs = [row_spec(o.shape[1]) for o in out_shapes[:-1]]
    out_specs.append(pl.BlockSpec((1, SUBLANES, D_CONV), lambda b, j: (b, 0, 0)))
    return pl.pallas_call(
        _in_proj_prompt_kernel,
        grid=(nb, nj),
        in_specs=[row_spec(D_MODEL)] + [_const_spec(w.shape) for w in weights] + [_const_spec(conv_w.shape)],
        out_specs=out_specs,
        out_shape=out_shapes,
        scratch_shapes=[pltpu.VMEM((SUBLANES, D_CONV), F32)],
        compiler_params=_params(2),
        name="in_proj_prompt",
    )(x2d, *weights, conv_w)


def _in_proj_decode(x2d, p2, p1, weights, conv_w):
    t = x2d.shape[0]
    row_spec = lambda w: pl.BlockSpec((t, w), lambda i: (0, 0))
    out_shapes = _proj_out_shapes(t) + [jax.ShapeDtypeStruct((t, D_CONV), F32)]
    return pl.pallas_call(
        _in_proj_decode_kernel,
        grid=(1,),
        in_specs=[row_spec(D_MODEL)] * 3 + [_const_spec(w.shape) for w in weights] + [_const_spec(conv_w.shape)],
        out_specs=[row_spec(o.shape[1]) for o in out_shapes],
        out_shape=out_shapes,
        compiler_params=_params(1),
        name="in_proj_decode",
    )(x2d, p2, p1, *weights, conv_w)


def _attn_prompt_kernel(q_ref, kv_ref, qi_ref, kib_ref, idx_ref, o_ref,
                        key_ref, hi_ref, bias_ref, t_ref, j_ref, need_ref, m_ref, acc_ref, knorm_ref,
                        *, n_keep, seq_len):
    tq = q_ref.shape[0]
    i = pl.program_id(1)
    nchunk = i + 1
    nsub = tq // LANES

    idx_t = jnp.transpose(idx_ref[...])
    wi_rows = [idx_t[IDX_DIM + h:IDX_DIM + h + 1, :] * IDX_SCALE for h in range(IDX_HEADS)]
    key_i = lax.broadcasted_iota(I32, (tq, tq), 0)
    qry_i = lax.broadcasted_iota(I32, (tq, tq), 1)

    def score_chunk(c, carry):
        kic = kib_ref[pl.ds(pl.multiple_of(c * tq, tq), tq), :]
        acc = jnp.zeros((tq, tq), F32)
        for h in range(IDX_HEADS):
            d = _dot_nt(kic, qi_ref[:, h * LANES:(h + 1) * LANES])
            acc = acc + jnp.maximum(d, 0.0) * wi_rows[h]
        acc = jnp.where(jnp.abs(acc) < F32_MIN_NORMAL, 0.0, acc)
        s = jnp.where((c < i) | (key_i <= qry_i), acc, NEG)
        bits = lax.bitcast_convert_type(s, I32)
        key_ref[c] = bits ^ ((bits >> 31) & jnp.int32(0x7FFFFFFF))
        hi_ref[c] = lax.bitcast_convert_type(bits & jnp.int32(-65536), F32).astype(BF16)
        return carry

    lax.fori_loop(0, nchunk, score_chunk, 0)

    sub_i = lax.broadcasted_iota(I32, (SUBLANES, tq), 0)

    def counts(preds, with_pos=False):
        def body(c, cnts):
            for kk in range(tq // SUBLANES):
                blk = key_ref[c, kk * SUBLANES:(kk + 1) * SUBLANES, :]
                pos = (c * tq + kk * SUBLANES + sub_i) if with_pos else None
                cnts = tuple(cnt + pred(blk, pos).astype(I32) for cnt, pred in zip(cnts, preds))
            return cnts
        cnts = lax.fori_loop(0, nchunk, body, tuple(jnp.zeros((SUBLANES, tq), I32) for _ in preds))
        return tuple(jnp.sum(cnt, axis=0, keepdims=True) for cnt in cnts)

    def count_hi(cand):
        rows16 = 2 * SUBLANES
        def body(c, cnt):
            for kk in range(tq // rows16):
                blk = hi_ref[c, kk * rows16:(kk + 1) * rows16, :]
                cnt = cnt + jnp.where(blk >= cand, jnp.ones_like(cnt), jnp.zeros_like(cnt))
            return cnt
        cnt = lax.fori_loop(0, nchunk, body, jnp.zeros((rows16, tq), BF16))
        return jnp.sum(cnt.astype(F32), axis=0, keepdims=True)

    def hi_bit_body(bi, p16):
        cand = p16 + lax.shift_left(jnp.int32(1), 15 - bi)
        raw = cand ^ ((cand >> 31) & jnp.int32(0x7FFF))
        raw = jnp.where((raw > 0) & (raw < BF16_MIN_NORMAL_BITS), BF16_MIN_NORMAL_BITS, raw)
        cand_f = lax.bitcast_convert_type(lax.shift_left(raw, 16), F32)
        total = count_hi(jnp.broadcast_to(cand_f, (2 * SUBLANES, tq)).astype(BF16))
        return jnp.where(total >= n_keep, cand, p16)

    p16 = lax.fori_loop(0, 16, hi_bit_body, jnp.full((1, tq), -2 ** 15, I32))

    def bit_body(bi, t):
        cand = t + lax.shift_left(jnp.int32(1), 15 - bi)
        cand_b = jnp.broadcast_to(cand, (SUBLANES, tq))
        total, = counts((lambda blk, _: blk >= cand_b,))
        return jnp.where(total >= n_keep, cand, t)

    t = lax.fori_loop(0, 16, bit_body, lax.shift_left(p16, 16))
    t_b = jnp.broadcast_to(t, (SUBLANES, tq))
    cnt_gt, cnt_eq = counts((lambda blk, _: blk > t_b, lambda blk, _: blk == t_b))
    need = n_keep - cnt_gt
    any_excess = jnp.max(((cnt_eq > need) & (t > KEY_HALF)).astype(I32))
    t_ref[...] = t_b
    need_ref[...] = jnp.broadcast_to(need, (SUBLANES, tq))
    j_ref[...] = jnp.full(j_ref.shape, seq_len, I32)

    @pl.when(any_excess > 0)
    def _():
        nbits = max(1, (seq_len - 1).bit_length())
        t_b = t_ref[...]
        need = need_ref[0:1, :]

        def jbit(bi, jj):
            cand = jj | lax.shift_left(jnp.int32(1), nbits - 1 - bi)
            cand_b = jnp.broadcast_to(cand, (SUBLANES, tq))
            g, = counts((lambda blk, pos: (blk == t_b) & (pos < cand_b),), with_pos=True)
            return jnp.where(g < need, cand, jj)

        jj = lax.fori_loop(0, nbits, jbit, jnp.zeros((1, tq), I32))
        j_ref[...] = jnp.broadcast_to(jj, (SUBLANES, tq))

    t_row = t_ref[0:1, :]
    j_row = j_ref[0:1, :]

    def bias_chunk(c, carry):
        blk = key_ref[c]
        sel = (blk > KEY_HALF) & ((blk > t_row) | ((blk == t_row) & ((c * tq + key_i) <= j_row)))
        bias_ref[c] = jnp.transpose(jnp.where(sel, 0.0, NEG))
        return carry

    lax.fori_loop(0, nchunk, bias_chunk, 0)

    nkv = N_KV_HEADS * HEAD_DIM
    gq = GROUP * tq

    def group_logits(c, g):
        rows = pl.ds(pl.multiple_of(c * tq, tq), tq)
        kc = kv_ref[rows, g * HEAD_DIM:(g + 1) * HEAD_DIM]
        qg = jnp.concatenate([q_ref[:, h * HEAD_DIM:(h + 1) * HEAD_DIM]
                              for h in range(g * GROUP, (g + 1) * GROUP)], axis=0)
        s = _dot_nt(qg, kc).reshape(GROUP, tq, tq) + bias_ref[c][None]
        return s.reshape(gq, tq)

    def exact_row_max():
        m_ref[...] = jnp.full(m_ref.shape, NEG, F32)

        def max_chunk(c, carry):
            for g in range(N_KV_HEADS):
                s = group_logits(c, g)
                mx = s[:, :LANES]
                for k in range(1, nsub):
                    mx = jnp.maximum(mx, s[:, k * LANES:(k + 1) * LANES])
                m_ref[g] = jnp.maximum(m_ref[g], mx)
            return carry

        lax.fori_loop(0, nchunk, max_chunk, 0)
        for g in range(N_KV_HEADS):
            m_ref[g] = jnp.broadcast_to(jnp.max(m_ref[g], axis=1, keepdims=True), (gq, LANES))

    ones = jnp.ones((tq, HEAD_DIM), BF16)

    def accumulate():
        acc_ref[...] = jnp.zeros_like(acc_ref)

        def pv_chunk(c, carry):
            rows = pl.ds(pl.multiple_of(c * tq, tq), tq)
            for g in range(N_KV_HEADS):
                s = group_logits(c, g)
                m = m_ref[g]
                p = jnp.concatenate([jnp.exp(s[:, k * LANES:(k + 1) * LANES] - m) for k in range(nsub)], axis=1)
                vc = kv_ref[rows, nkv + g * HEAD_DIM:nkv + (g + 1) * HEAD_DIM]
                acc_ref[g] = acc_ref[g] + _dot(p.astype(BF16), jnp.concatenate([vc, ones], axis=1))
            return carry

        lax.fori_loop(0, nchunk, pv_chunk, 0)

    @pl.when(i == 0)
    def _():
        for g in range(N_KV_HEADS):
            def norm_chunk(c, mx):
                kc = kv_ref[pl.ds(pl.multiple_of(c * tq, tq), tq), g * HEAD_DIM:(g + 1) * HEAD_DIM].astype(F32)
                return jnp.maximum(mx, jnp.sum(kc * kc, axis=1, keepdims=True))
            k2 = lax.fori_loop(0, seq_len // tq, norm_chunk, jnp.zeros((tq, 1), F32))
            knorm_ref[g] = jnp.broadcast_to(jnp.sqrt(jnp.max(k2, axis=0, keepdims=True)), (SUBLANES, LANES))

    for h in range(N_HEADS):
        g, r = h // GROUP, h % GROUP
        qh = q_ref[:, h * HEAD_DIM:(h + 1) * HEAD_DIM].astype(F32)
        qn = jnp.sqrt(jnp.sum(qh * qh, axis=1, keepdims=True))
        m_ref[g, r * tq:(r + 1) * tq, :] = qn * knorm_ref[g][0:1, :]
    accumulate()
    denom_min = jnp.min(acc_ref[:, :, HEAD_DIM:HEAD_DIM + 1])

    @pl.when(denom_min < SOFTMAX_DENOM_MIN)
    def _():
        exact_row_max()
        accumulate()

    for h in range(N_HEADS):
        a = acc_ref[h // GROUP, (h % GROUP) * tq:(h % GROUP + 1) * tq, :]
        o_ref[:, h * HEAD_DIM:(h + 1) * HEAD_DIM] = (a[:, :HEAD_DIM] / a[:, HEAD_DIM:]).astype(BF16)


def _attn_prompt(q, kvb, qi, kib, idx, nb):
    t = q.shape[0]
    s = t // nb
    tq = min(ATT_Q, s)
    nq = s // tq
    n_keep = min(TOPK_KEYS, s // 4)
    blk = lambda w: pl.BlockSpec((tq, w), lambda b, i: (b * nq + i, 0))
    seq = lambda w: pl.BlockSpec((s, w), lambda b, i: (b, 0))
    return pl.pallas_call(
        functools.partial(_attn_prompt_kernel, n_keep=n_keep, seq_len=s),
        grid=(nb, nq),
        in_specs=[blk(q.shape[1]), seq(kvb.shape[1]), blk(qi.shape[1]), seq(kib.shape[1]), blk(idx.shape[1])],
        out_specs=blk(N_HEADS * HEAD_DIM),
        out_shape=jax.ShapeDtypeStruct((t, N_HEADS * HEAD_DIM), BF16),
        scratch_shapes=[pltpu.VMEM((nq, tq, tq), I32), pltpu.VMEM((nq, tq, tq), BF16),
                        pltpu.VMEM((nq, tq, tq), F32),
                        pltpu.VMEM((SUBLANES, tq), I32), pltpu.VMEM((SUBLANES, tq), I32),
                        pltpu.VMEM((SUBLANES, tq), I32),
                        pltpu.VMEM((N_KV_HEADS, GROUP * tq, LANES), F32),
                        pltpu.VMEM((N_KV_HEADS, GROUP * tq, 2 * HEAD_DIM), F32),
                        pltpu.VMEM((N_KV_HEADS, SUBLANES, LANES), F32)],
        compiler_params=_params(2),
        name="attn_prompt",
    )(q, kvb, qi, kib, idx)


DECODE_PAGES_PER_ROW = 4


def _decode_scores_kernel(pt_ref, qi_ref, wi_ref, kin_ref, cx_hbm, key_o, knew_o, xbuf, sem, *, n_pages):
    b = pl.program_id(0)
    nb = pl.num_programs(0)
    ppr = DECODE_PAGES_PER_ROW
    nrow = n_pages // ppr

    def page_copy(bb, p, slot):
        return pltpu.make_async_copy(cx_hbm.at[pt_ref[bb, p]], xbuf.at[slot, p], sem.at[slot])

    def start_batch(bb, slot):
        def body(p, carry):
            page_copy(bb, p, slot).start()
            return carry
        lax.fori_loop(0, n_pages, body, 0)

    def wait_batch(bb, slot):
        def body(p, carry):
            page_copy(bb, p, slot).wait()
            return carry
        lax.fori_loop(0, n_pages, body, 0)

    @pl.when(b == 0)
    def _():
        start_batch(0, 0)

    @pl.when(b + 1 < nb)
    def _():
        start_batch(b + 1, (b + 1) % 2)

    slot = b % 2
    wait_batch(b, slot)

    qi8 = qi_ref[0]
    wi8 = wi_ref[0] * IDX_SCALE

    def score_row(r, carry):
        kx = jnp.concatenate([xbuf[slot, r * ppr + k] for k in range(ppr)], axis=1).astype(BF16)
        d = _dot(qi8, kx)
        sc = jnp.sum(jnp.maximum(d, 0.0) * wi8, axis=0, keepdims=True)
        key_o[0, pl.ds(r, 1), :] = _sort_key(sc)
        return carry

    lax.fori_loop(0, nrow, score_row, 0, unroll=4)
    dn = jnp.sum(qi8.astype(F32) * kin_ref[0].astype(F32), axis=1, keepdims=True)
    key_new = _sort_key(jnp.sum(jnp.maximum(dn, 0.0) * wi8, axis=0, keepdims=True))
    knew_o[0] = jnp.broadcast_to(key_new, (1, LANES))


def _decode_scores(page_table, qi8, wi8, kin, cxt):
    db, n_pages = page_table.shape
    nrow = n_pages // DECODE_PAGES_PER_ROW
    w = DECODE_PAGES_PER_ROW * PAGE_SIZE
    per_b = lambda a: pl.BlockSpec((1,) + a.shape[1:], lambda b, pt: (b, 0, 0))
    grid_spec = pltpu.PrefetchScalarGridSpec(
        num_scalar_prefetch=1,
        grid=(db,),
        in_specs=[per_b(qi8), per_b(wi8), per_b(kin), pl.BlockSpec(memory_space=pl.ANY)],
        out_specs=[pl.BlockSpec((1, nrow, w), lambda b, pt: (b, 0, 0)),
                   pl.BlockSpec((1, 1, LANES), lambda b, pt: (b, 0, 0))],
        scratch_shapes=[pltpu.VMEM((2, n_pages, IDX_DIM, PAGE_SIZE), F32), pltpu.SemaphoreType.DMA((2,))],
    )
    return pl.pallas_call(
        functools.partial(_decode_scores_kernel, n_pages=n_pages),
        grid_spec=grid_spec,
        out_shape=[jax.ShapeDtypeStruct((db, nrow, w), I32), jax.ShapeDtypeStruct((db, 1, LANES), I32)],
        compiler_params=_params(1),
        name="decode_scores",
    )(page_table, qi8, wi8, kin, cxt)


def _decode_pick_kernel(keys_ref, knew_ref, pt_ref, meta_o, *, n_keep):
    nb, nrow, w = keys_ref.shape
    past = nrow * w
    n_pages = pt_ref.shape[1]
    keys = keys_ref[...]
    key_new = knew_ref[...][:, :, 0:1]
    col = (lax.broadcasted_iota(I32, keys.shape, 1) * w + lax.broadcasted_iota(I32, keys.shape, 2))

    def total(x):
        c = x.astype(I32)
        part = c[:, :, :LANES]
        for k in range(1, w // LANES):
            part = part + c[:, :, k * LANES:(k + 1) * LANES]
        return jnp.sum(jnp.sum(part, axis=1, keepdims=True), axis=2, keepdims=True)

    def bit_body(bi, t):
        cand = t + lax.shift_left(jnp.int32(1), 31 - bi)
        cnt = total(keys >= cand) + (key_new >= cand).astype(I32)
        return jnp.where(cnt >= n_keep, cand, t)

    t = lax.fori_loop(0, 32, bit_body, jnp.full((nb, 1, 1), INT_MIN, I32))
    need = n_keep - (total(keys > t) + (key_new > t).astype(I32))
    nbits = past.bit_length()

    def jbit(bi, jj):
        cand = jj | lax.shift_left(jnp.int32(1), nbits - 1 - bi)
        g = total((keys == t) & (col < cand)) + ((key_new == t) & (past < cand)).astype(I32)
        return jnp.where(g < need, cand, jj)

    jj = lax.fori_loop(0, nbits, jbit, jnp.zeros((nb, 1, 1), I32))
    sel = (keys > t) | ((keys == t) & (col <= jj))
    sel_new = ((key_new > t) | ((key_new == t) & (past <= jj))).astype(I32)

    rows = nb * nrow
    incl_lane = (lax.broadcasted_iota(I32, (w, w), 0) <= lax.broadcasted_iota(I32, (w, w), 1)).astype(BF16)
    cnt_in_row = _dot(sel.astype(BF16).reshape(rows, w), incl_lane)
    row_tot = cnt_in_row[:, w - 1:w]
    ri = lax.broadcasted_iota(I32, (rows, rows), 0)
    rj = lax.broadcasted_iota(I32, (rows, rows), 1)
    earlier_row_same_seq = ((ri // nrow) == (rj // nrow)) & (rj < ri)
    row_off = _dot(earlier_row_same_seq.astype(BF16),
                   jnp.broadcast_to(row_tot, (rows, LANES)).astype(BF16))[:, 0:1]
    cnt3 = cnt_in_row.astype(BF16).reshape(nb, nrow, w)
    off3 = row_off.reshape(nb, nrow, 1)
    incl3 = (row_off + row_tot).reshape(nb, nrow, 1)
    jl = lax.broadcasted_iota(I32, (1, 1, n_keep), 2).astype(F32)
    row_j = jnp.sum((incl3 <= jl).astype(F32), axis=1, keepdims=True)
    onehot = lax.broadcasted_iota(I32, (nb, nrow, n_keep), 1).astype(F32) == row_j
    off_j = jnp.sum(jnp.where(onehot, off3, 0.0), axis=1, keepdims=True)
    local_rank = jl - off_j
    onehot_b = onehot.astype(BF16)
    page_i = lax.broadcasted_iota(I32, (n_pages, n_keep), 0)
    row_i = lax.broadcasted_iota(I32, (SUBLANES, n_keep), 0)
    for b in range(nb):
        cnt_j = lax.dot_general(cnt3[b], onehot_b[b], (((0,), (0,)), ((), ())),
                                preferred_element_type=F32)
        lane_j = jnp.sum((cnt_j <= local_rank[b]).astype(F32), axis=0, keepdims=True)
        pos = jnp.minimum((row_j[b] * w + lane_j).astype(I32), past - 1)
        phys = jnp.sum(jnp.where(page_i == pos // PAGE_SIZE, pt_ref[b], 0.0), axis=0, keepdims=True)
        tile = (phys.astype(I32) * PAGE_SIZE + pos % PAGE_SIZE) // (SUBLANES // N_KV_HEADS)
        meta_o[b] = jnp.where(row_i == 0, tile,
                              jnp.where(row_i == 1, sel_new[b], pos % (SUBLANES // N_KV_HEADS)))


def _decode_pick(keys, knew, pt_f):
    nb, nrow, w = keys.shape
    n_keep = min(TOPK_KEYS, (nrow * w + 1) // 4)
    full = lambda a: pl.BlockSpec(a.shape, lambda i: (0,) * a.ndim)
    return pl.pallas_call(
        functools.partial(_decode_pick_kernel, n_keep=n_keep),
        grid=(1,),
        in_specs=[full(keys), full(knew), full(pt_f)],
        out_specs=pl.BlockSpec((nb, SUBLANES, n_keep), lambda i: (0, 0, 0)),
        out_shape=jax.ShapeDtypeStruct((nb, SUBLANES, n_keep), I32),
        compiler_params=_params(1),
        name="decode_pick",
    )(keys, knew, pt_f)


def _decode_attend_kernel(tile_ref, q_ref, meta_ref, kn_ref, vn_ref, ck_hbm, cv_hbm, o_ref,
                          kbuf, vbuf, expand_ref, sem, *, n_keep):
    b = pl.program_id(0)
    nb = pl.num_programs(0)
    ncol = n_keep * SUBLANES

    def item_copies(bb, j, slot):
        src = pl.ds(pl.multiple_of(tile_ref[bb, j] * SUBLANES, SUBLANES), SUBLANES)
        dst = pl.ds(pl.multiple_of(j * SUBLANES, SUBLANES), SUBLANES)
        return (pltpu.make_async_copy(ck_hbm.at[src], kbuf.at[slot, dst], sem.at[0, slot]),
                pltpu.make_async_copy(cv_hbm.at[src], vbuf.at[slot, dst], sem.at[1, slot]))

    def start_batch(bb, slot):
        def body(j, carry):
            for prio, cp in enumerate(item_copies(bb, j, slot)):
                cp.start(priority=prio)
            return carry
        lax.fori_loop(0, n_keep, body, 0)

    def wait_batch(slot):
        pltpu.make_async_copy(ck_hbm.at[pl.ds(0, ncol)], kbuf.at[slot], sem.at[0, slot]).wait()
        pltpu.make_async_copy(cv_hbm.at[pl.ds(0, ncol)], vbuf.at[slot], sem.at[1, slot]).wait()

    @pl.when(b == 0)
    def _():
        start_batch(0, 0)
        item_of_col = lax.broadcasted_iota(I32, (n_keep, ncol), 1) // SUBLANES
        expand_ref[...] = (item_of_col == lax.broadcasted_iota(I32, (n_keep, ncol), 0)).astype(BF16)

    @pl.when(b + 1 < nb)
    def _():
        start_batch(b + 1, (b + 1) % 2)

    slot = b % 2
    wait_batch(slot)

    q8 = q_ref[0]
    meta = meta_ref[0]
    new_kept = meta[1:2, 0:1] > 0
    sub_col = _dot(meta.astype(BF16), expand_ref[...])[2:3, :]
    head_i = lax.broadcasted_iota(I32, (N_HEADS, ncol), 0)
    col_i = lax.broadcasted_iota(I32, (N_HEADS, ncol), 1)
    want = sub_col * N_KV_HEADS + (head_i // GROUP).astype(F32)
    is_new_item = new_kept & (col_i // SUBLANES == n_keep - 1)
    ok = ((col_i % SUBLANES).astype(F32) == want) & jnp.logical_not(is_new_item)

    s = jnp.where(ok, _dot_nt(q8, kbuf[slot].astype(BF16)), NEG)
    s_new = jnp.sum(q8.astype(F32) * kn_ref[0].astype(F32), axis=1, keepdims=True)
    s_new = jnp.where(new_kept, s_new, NEG)
    m = jnp.maximum(jnp.max(s, axis=1, keepdims=True), s_new)
    p = jnp.exp(s - m)
    p_new = jnp.exp(s_new - m)
    l = jnp.sum(p, axis=1, keepdims=True) + p_new
    acc = _dot(p.astype(BF16), vbuf[slot].astype(BF16)) + p_new.astype(BF16).astype(F32) * vn_ref[0].astype(F32)
    o_ref[0] = (acc / l).astype(BF16)


def _decode_attend(tiles, q8, meta, kn8, vn8, ck2, cv2):
    db, n_keep = tiles.shape
    per_b = lambda a: pl.BlockSpec((1,) + a.shape[1:], lambda b, tl: (b, 0, 0))
    any_spec = pl.BlockSpec(memory_space=pl.ANY)
    ncol = n_keep * SUBLANES
    grid_spec = pltpu.PrefetchScalarGridSpec(
        num_scalar_prefetch=1,
        grid=(db,),
        in_specs=[per_b(q8), per_b(meta), per_b(kn8), per_b(vn8), any_spec, any_spec],
        out_specs=pl.BlockSpec((1, N_HEADS, HEAD_DIM), lambda b, tl: (b, 0, 0)),
        scratch_shapes=[pltpu.VMEM((2, ncol, HEAD_DIM), F32), pltpu.VMEM((2, ncol, HEAD_DIM), F32),
                        pltpu.VMEM((n_keep, ncol), BF16), pltpu.SemaphoreType.DMA((2, 2))],
    )
    return pl.pallas_call(
        functools.partial(_decode_attend_kernel, n_keep=n_keep),
        grid_spec=grid_spec,
        out_shape=jax.ShapeDtypeStruct((db, N_HEADS, HEAD_DIM), BF16),
        compiler_params=_params(1),
        name="decode_attend",
    )(tiles, q8, meta, kn8, vn8, ck2, cv2)


def _layer_norm(r, g, b):
    mu = jnp.mean(r, axis=-1, keepdims=True)
    d = r - mu
    var = jnp.mean(d * d, axis=-1, keepdims=True)
    return d * lax.rsqrt(var + LN_EPS) * g + b


def _route(logits_t, rbias):
    n_exp, tm = logits_t.shape
    epg = n_exp // N_GROUPS
    s = jax.nn.sigmoid(logits_t)
    sb = s + rbias
    ie = lax.broadcasted_iota(I32, (epg, tm), 0)
    gs_rows = []
    for g in range(N_GROUPS):
        blk = sb[g * epg:(g + 1) * epg, :]
        m1 = jnp.max(blk, axis=0, keepdims=True)
        i1 = jnp.min(jnp.where(blk == m1, ie, epg), axis=0, keepdims=True)
        m2 = jnp.max(jnp.where(ie == i1, -jnp.inf, blk), axis=0, keepdims=True)
        gs_rows.append(m1 + m2)
    picked = [jnp.zeros((1, tm), jnp.bool_) for _ in range(N_GROUPS)]
    cur = list(gs_rows)
    for _ in range(TOPK_GROUPS):
        mx = cur[0]
        for g in range(1, N_GROUPS):
            mx = jnp.maximum(mx, cur[g])
        found = jnp.zeros((1, tm), jnp.bool_)
        for g in range(N_GROUPS):
            hit = (cur[g] == mx) & jnp.logical_not(found)
            found = found | hit
            picked[g] = picked[g] | hit
            cur[g] = jnp.where(hit, -jnp.inf, cur[g])
    masked = jnp.concatenate(
        [jnp.where(picked[g], sb[g * epg:(g + 1) * epg, :], NEG) for g in range(N_GROUPS)], axis=0)
    iall = lax.broadcasted_iota(I32, (n_exp, tm), 0)
    e_rows, w_rows = [], []
    for _ in range(TOP_K):
        mx = jnp.max(masked, axis=0, keepdims=True)
        ix = jnp.min(jnp.where(masked == mx, iall, n_exp), axis=0, keepdims=True)
        hit = iall == ix
        w_rows.append(jnp.sum(jnp.where(hit, s, 0.0), axis=0, keepdims=True))
        e_rows.append(ix)
        masked = jnp.where(hit, -jnp.inf, masked)
    wsum = w_rows[0]
    for w in w_rows[1:]:
        wsum = wsum + w
    gates = [w / wsum * ROUTED_SCALE for w in w_rows]
    return jnp.concatenate(e_rows, axis=0), jnp.concatenate(gates, axis=0)


def _post_attn_kernel(xp, xs, ap, as_, cp, cs, gap, gas, gcp, gcs, woa, woc, wo, g1, b1, rwt, rb,
                      h2_o, e_o, gate_o, *, n_prompt_tiles):
    i = pl.program_id(0)
    is_p = i < n_prompt_tiles
    pick = lambda a, b: jnp.where(is_p, a[...], b[...])
    a = _dot(pick(ap, as_), woa[...])
    c = _dot(pick(cp, cs), woc[...])
    merged = pick(gap, gas).astype(F32) * a + pick(gcp, gcs).astype(F32) * c
    r = ALPHA * pick(xp, xs) + _dot(merged.astype(BF16), wo[...])
    h = _layer_norm(r, g1[...], b1[...])
    tm = h.shape[0]
    for k in range(D_MODEL // LANES):
        h2_o[pl.ds(k, tm, stride=SUBLANES), :] = h[:, k * LANES:(k + 1) * LANES]
    logits_t = lax.dot_general(rwt[...], h, (((1,), (1,)), ((), ())),
                               precision=lax.Precision.HIGHEST, preferred_element_type=F32)
    e_idx, gates = _route(logits_t, rb[...])
    e_o[...] = e_idx
    gate_o[...] = gates


def _post_attn(x_p, x_s, attn_p, attn_s, conv_p, conv_s, ga_p, ga_s, gc_p, gc_s,
               woa, woc, wo, g1, b1, rwt, rb):
    tp, ts = x_p.shape[0], x_s.shape[0]
    tm = TOKEN_TILE
    npt, nst = tp // tm, ts // tm
    n_tok = tp + ts
    p_spec = pl.BlockSpec((tm, D_MODEL), lambda i: (jnp.minimum(i, npt - 1), 0))
    s_spec = pl.BlockSpec((tm, D_MODEL), lambda i: (jnp.maximum(i - npt, 0), 0))
    consts = [woa, woc, wo, g1, b1, rwt, rb]
    return pl.pallas_call(
        functools.partial(_post_attn_kernel, n_prompt_tiles=npt),
        grid=(npt + nst,),
        in_specs=[p_spec, s_spec] * 5 + [_const_spec(c.shape) for c in consts],
        out_specs=[pl.BlockSpec((tm * SUBLANES, LANES), lambda i: (i, 0)),
                   pl.BlockSpec((TOP_K, tm), lambda i: (0, i)),
                   pl.BlockSpec((TOP_K, tm), lambda i: (0, i))],
        out_shape=[jax.ShapeDtypeStruct((n_tok * SUBLANES, LANES), F32),
                   jax.ShapeDtypeStruct((TOP_K, n_tok), I32),
                   jax.ShapeDtypeStruct((TOP_K, n_tok), F32)],
        compiler_params=_params(1),
        name="post_attn",
    )(x_p, x_s, attn_p, attn_s, conv_p, conv_s, ga_p, ga_s, gc_p, gc_s, *consts)


def _rank_kernel(e_ref, rank_o, cnt_o, carry_ref):
    i = pl.program_id(0)

    @pl.when(i == 0)
    def _():
        carry_ref[...] = jnp.zeros_like(carry_ref)

    n_exp = carry_ref.shape[0]
    e = e_ref[...]
    tk = e.shape[1]
    ie = lax.broadcasted_iota(I32, (n_exp, tk), 0)
    onehot = jnp.zeros((n_exp, tk), F32)
    for j in range(TOP_K):
        onehot = onehot + (ie == e[j:j + 1, :]).astype(F32)
    before = (lax.broadcasted_iota(I32, (tk, tk), 0) < lax.broadcasted_iota(I32, (tk, tk), 1)).astype(BF16)
    prefix = _dot(onehot.astype(BF16), before) + carry_ref[:, 0:1]
    rows = [jnp.sum(jnp.where(ie == e[j:j + 1, :], prefix, 0.0), axis=0, keepdims=True) for j in range(TOP_K)]
    rank_o[...] = jnp.concatenate(rows, axis=0).astype(I32)
    carry_ref[...] = carry_ref[...] + jnp.sum(onehot, axis=1, keepdims=True)
    cnt_o[...] = carry_ref[...].astype(I32)


def _rank(e_t, n_exp):
    n_tok = e_t.shape[1]
    tk = TOKEN_TILE
    return pl.pallas_call(
        _rank_kernel,
        grid=(n_tok // tk,),
        in_specs=[pl.BlockSpec((TOP_K, tk), lambda i: (0, i))],
        out_specs=[pl.BlockSpec((TOP_K, tk), lambda i: (0, i)),
                   pl.BlockSpec((n_exp, LANES), lambda i: (0, 0))],
        out_shape=[jax.ShapeDtypeStruct((TOP_K, n_tok), I32), jax.ShapeDtypeStruct((n_exp, LANES), I32)],
        scratch_shapes=[pltpu.VMEM((n_exp, LANES), F32)],
        compiler_params=_params(1),
        name="moe_rank",
    )(e_t)


def _slot_kernel(e_ref, rank_ref, pstart_ref, slot_o):
    e = e_ref[...]
    n_exp = pstart_ref.shape[0]
    tk = e.shape[1]
    ie = lax.broadcasted_iota(I32, (n_exp, tk), 0)
    ps = pstart_ref[:, 0:1]
    rows = [jnp.sum(jnp.where(ie == e[j:j + 1, :], ps, 0.0), axis=0, keepdims=True) for j in range(TOP_K)]
    slot_o[...] = rank_ref[...] + jnp.concatenate(rows, axis=0).astype(I32)


def _slots(e_t, rank_t, pstart_f):
    n_tok = e_t.shape[1]
    tk = TOKEN_TILE
    spec = pl.BlockSpec((TOP_K, tk), lambda i: (0, i))
    return pl.pallas_call(
        _slot_kernel,
        grid=(n_tok // tk,),
        in_specs=[spec, spec, _const_spec(pstart_f.shape)],
        out_specs=spec,
        out_shape=jax.ShapeDtypeStruct((TOP_K, n_tok), I32),
        compiler_params=_params(1),
        name="moe_slots",
    )(e_t, rank_t, pstart_f)


def _dispatch_kernel(pend_ref, padded_ref, slot_ref, h2_ref, xs_hbm, zbuf, sem, zsem, *, n_exp):
    i = pl.program_id(0)
    td = slot_ref.shape[1]
    blk_rows = MOE_BLOCK * SUBLANES

    def zero_copy(e):
        dst = pl.ds(pl.multiple_of((pend_ref[e] - MOE_BLOCK) * SUBLANES, SUBLANES), blk_rows)
        return pltpu.make_async_copy(zbuf, xs_hbm.at[dst], zsem)

    @pl.when(i == 0)
    def _():
        zbuf[...] = jnp.zeros_like(zbuf)

        def start(e, carry):
            @pl.when(padded_ref[e] > 0)
            def _():
                zero_copy(e).start()
            return carry

        def wait(e, carry):
            @pl.when(padded_ref[e] > 0)
            def _():
                zero_copy(e).wait()
            return carry

        lax.fori_loop(0, n_exp, start, 0)
        lax.fori_loop(0, n_exp, wait, 0)

    def row_copy(t, j):
        src = pl.ds(pl.multiple_of(t * SUBLANES, SUBLANES), SUBLANES)
        dst = pl.ds(pl.multiple_of(slot_ref[j, t] * SUBLANES, SUBLANES), SUBLANES)
        return pltpu.make_async_copy(h2_ref.at[src], xs_hbm.at[dst], sem)

    def start_tok(t, carry):
        for j in range(TOP_K):
            row_copy(t, j).start(priority=j % 2)
        return carry

    def wait_tok(t, carry):
        for j in range(TOP_K):
            row_copy(t, j).wait()
        return carry

    lax.fori_loop(0, td, start_tok, 0)
    lax.fori_loop(0, td, wait_tok, 0)


def _dispatch(pend, padded, slot_t, h2, m_pad):
    n_tok = slot_t.shape[1]
    td = TOKEN_TILE
    n_exp = pend.shape[0]
    grid_spec = pltpu.PrefetchScalarGridSpec(
        num_scalar_prefetch=2,
        grid=(n_tok // td,),
        in_specs=[pl.BlockSpec((TOP_K, td), lambda i, *_: (0, i), memory_space=pltpu.SMEM),
                  pl.BlockSpec((td * SUBLANES, LANES), lambda i, *_: (i, 0))],
        out_specs=pl.BlockSpec(memory_space=pl.ANY),
        scratch_shapes=[pltpu.VMEM((MOE_BLOCK * SUBLANES, LANES), F32),
                        pltpu.SemaphoreType.DMA(()), pltpu.SemaphoreType.DMA(())],
    )
    return pl.pallas_call(
        functools.partial(_dispatch_kernel, n_exp=n_exp),
        grid_spec=grid_spec,
        out_shape=jax.ShapeDtypeStruct((m_pad * SUBLANES, LANES), F32),
        compiler_params=_params(1),
        name="moe_dispatch",
    )(pend, padded, slot_t, h2)


def _tile_rows(ref, n):
    return jnp.concatenate([ref[pl.ds(k, n, stride=SUBLANES), :] for k in range(D_MODEL // LANES)], axis=1)


PLAN_EXPERT, PLAN_SLOT, PLAN_NEXT, PLAN_HAS_NEXT = range(4)


def _expert_kernel(plan_ref, nused_ref, x_ref, wg_hbm, wu_hbm, wd_hbm, y_ref, wg_f, wu_f, wd_f, wgu_b, wd_b, sem):
    i = pl.program_id(0)
    d_exp = wd_b.shape[0]

    def weight_copies(e, s):
        return (pltpu.make_async_copy(wg_hbm.at[e], wg_f.at[s], sem.at[s]),
                pltpu.make_async_copy(wu_hbm.at[e], wu_f.at[s], sem.at[s]),
                pltpu.make_async_copy(wd_hbm.at[e], wd_f.at[s], sem.at[s]))

    @pl.when(i < nused_ref[0])
    def _():
        e = plan_ref[PLAN_EXPERT, i]
        changed = (i == 0) | (e != plan_ref[PLAN_EXPERT, jnp.maximum(i - 1, 0)])

        @pl.when(changed)
        def _():
            s = plan_ref[PLAN_SLOT, i]

            @pl.when(i == 0)
            def _():
                for cp in weight_copies(e, s):
                    cp.start()

            for cp in weight_copies(e, s):
                cp.wait()

            @pl.when(plan_ref[PLAN_HAS_NEXT, i] > 0)
            def _():
                for cp in weight_copies(plan_ref[PLAN_NEXT, i], 1 - s):
                    cp.start()

            wgu_b[:, :d_exp] = wg_f[s].astype(BF16)
            wgu_b[:, d_exp:] = wu_f[s].astype(BF16)
            wd_b[...] = wd_f[s].astype(BF16)

        x = _tile_rows(x_ref, MOE_BLOCK).astype(BF16)
        gu = _dot(x, wgu_b[...])
        hh = jax.nn.silu(gu[:, :d_exp]) * gu[:, d_exp:]
        y = _dot(hh.astype(BF16), wd_b[...])
        for k in range(D_MODEL // LANES):
            y_ref[pl.ds(k, MOE_BLOCK, stride=SUBLANES), :] = y[:, k * LANES:(k + 1) * LANES]


def _expert_plan(block_e, nused):
    n_blocks = block_e.shape[0]
    idx = jnp.arange(n_blocks, dtype=I32)
    changed = (idx == 0) | (block_e != jnp.roll(block_e, 1))
    slot = (jnp.cumsum(changed.astype(I32)) - 1) % 2
    change_pos = jnp.where(changed & (idx < nused), idx, n_blocks)
    next_pos = jnp.concatenate([lax.cummin(change_pos[::-1])[::-1][1:], jnp.full((1,), n_blocks, I32)])
    has_next = (next_pos < n_blocks).astype(I32)
    nxt = block_e[jnp.minimum(next_pos, n_blocks - 1)]
    return jnp.stack([block_e, slot, nxt, has_next]).astype(I32)


def _experts(block_e, nused, xs, w_gate, w_up, w_down):
    n_blocks = block_e.shape[0]
    d_exp = w_gate.shape[2]
    rows = MOE_BLOCK * SUBLANES
    last = lambda i, nu: jnp.minimum(i, nu[0] - 1)
    any_spec = pl.BlockSpec(memory_space=pl.ANY)
    grid_spec = pltpu.PrefetchScalarGridSpec(
        num_scalar_prefetch=2,
        grid=(n_blocks,),
        in_specs=[pl.BlockSpec((rows, LANES), lambda i, plan, nu: (last(i, nu), 0)), any_spec, any_spec, any_spec],
        out_specs=pl.BlockSpec((rows, LANES), lambda i, plan, nu: (last(i, nu), 0)),
        scratch_shapes=[pltpu.VMEM((2, D_MODEL, d_exp), F32), pltpu.VMEM((2, D_MODEL, d_exp), F32),
                        pltpu.VMEM((2, d_exp, D_MODEL), F32),
                        pltpu.VMEM((D_MODEL, 2 * d_exp), BF16), pltpu.VMEM((d_exp, D_MODEL), BF16),
                        pltpu.SemaphoreType.DMA((2,))],
    )
    return pl.pallas_call(
        _expert_kernel,
        grid_spec=grid_spec,
        out_shape=jax.ShapeDtypeStruct(xs.shape, F32),
        compiler_params=_params(1),
        name="moe_experts",
    )(_expert_plan(block_e, nused[0]), nused, xs, w_gate, w_up, w_down)


def _combine_kernel(slot_ref, slot_next_ref, gate_ref, h2_ref, ys_hbm, shg, shu, shd, g2, b2, yp_o, ys_o,
                    buf, sem, *, n_prompt_tiles):
    i = pl.program_id(0)
    n = pl.num_programs(0)
    tc = gate_ref.shape[0]

    def start_tile(s_ref, bslot):
        def body(t, carry):
            dst = pl.ds(pl.multiple_of(t * SUBLANES, SUBLANES), SUBLANES)
            for j in range(TOP_K):
                src = pl.ds(pl.multiple_of(s_ref[j, t] * SUBLANES, SUBLANES), SUBLANES)
                pltpu.make_async_copy(ys_hbm.at[src], buf.at[bslot, j, dst], sem.at[bslot]).start(priority=j % 2)
            return carry
        lax.fori_loop(0, tc, body, 0)

    def wait_tile(bslot):
        for j in range(TOP_K):
            pltpu.make_async_copy(ys_hbm.at[pl.ds(0, tc * SUBLANES)], buf.at[bslot, j], sem.at[bslot]).wait()

    @pl.when(i == 0)
    def _():
        start_tile(slot_ref, 0)

    @pl.when(i + 1 < n)
    def _():
        start_tile(slot_next_ref, (i + 1) % 2)

    cur = i % 2
    h = _tile_rows(h2_ref, tc)
    hb = h.astype(BF16)
    shared = _dot((jax.nn.silu(_dot(hb, shg[...])) * _dot(hb, shu[...])).astype(BF16), shd[...])
    wait_tile(cur)
    routed = jnp.zeros((tc, D_MODEL), F32)
    for j in range(TOP_K):
        routed = routed + _tile_rows(buf.at[cur, j], tc) * gate_ref[:, j:j + 1]
    y = _layer_norm(ALPHA * h + (routed + shared), g2[...], b2[...])

    @pl.when(i < n_prompt_tiles)
    def _():
        yp_o[...] = y

    @pl.when(i >= n_prompt_tiles)
    def _():
        ys_o[...] = y


def _combine(slot_t, gates, h2, ys, shg, shu, shd, g2, b2, n_prompt, n_decode):
    tc = TOKEN_TILE
    npt, nst = n_prompt // tc, n_decode // tc
    consts = [shg, shu, shd, g2, b2]
    return pl.pallas_call(
        functools.partial(_combine_kernel, n_prompt_tiles=npt),
        grid=(npt + nst,),
        in_specs=[pl.BlockSpec((TOP_K, tc), lambda i: (0, i), memory_space=pltpu.SMEM),
                  pl.BlockSpec((TOP_K, tc), lambda i: (0, jnp.minimum(i + 1, npt + nst - 1)),
                               memory_space=pltpu.SMEM),
                  pl.BlockSpec((tc, TOP_K), lambda i: (i, 0)),
                  pl.BlockSpec((tc * SUBLANES, LANES), lambda i: (i, 0)),
                  pl.BlockSpec(memory_space=pl.ANY)] + [_const_spec(c.shape) for c in consts],
        out_specs=[pl.BlockSpec((tc, D_MODEL), lambda i: (jnp.minimum(i, npt - 1), 0)),
                   pl.BlockSpec((tc, D_MODEL), lambda i: (jnp.maximum(i - npt, 0), 0))],
        out_shape=[jax.ShapeDtypeStruct((n_prompt, D_MODEL), F32),
                   jax.ShapeDtypeStruct((n_decode, D_MODEL), F32)],
        scratch_shapes=[pltpu.VMEM((2, TOP_K, tc * SUBLANES, LANES), F32), pltpu.SemaphoreType.DMA((2,))],
        compiler_params=_params(1),
        name="moe_combine",
    )(slot_t, slot_t, gates, h2, ys, *consts)


def _pad_rows(a, n):
    return jnp.pad(a, ((0, n - a.shape[0]),) + ((0, 0),) * (a.ndim - 1))


def kernel(x_prompt, x_sample, cache_k, cache_v, cache_kidx, page_table, state_conv, w_in, conv_w, w_o_attn,
           w_o_conv, w_o, ln1_g, ln1_b, router_w, router_bias, moe_w_gate, moe_w_up, moe_w_down,
           shared_w_gate, shared_w_up, shared_w_down, ln2_g, ln2_b):
    nb, seq, _ = x_prompt.shape
    db = x_sample.shape[0]
    n_pool = cache_k.shape[1]
    n_exp = router_w.shape[-1]
    tp = nb * seq
    ts = TOKEN_TILE
    nkv = N_KV_HEADS * HEAD_DIM

    weights = _prep_in_weights(w_in[0])
    cw = conv_w[0]

    xp2 = x_prompt.reshape(tp, D_MODEL)
    (q_p, k_p, v_p, kvb_p, qi_p, idx_p, kib_p, conv_p, ga_p, gc_p, ulast_p) = _in_proj_prompt(xp2, weights, cw, nb)
    xs2 = _pad_rows(x_sample.reshape(db, D_MODEL), ts)
    prev2 = _pad_rows(state_conv[0, :, 0, :], ts)
    prev1 = _pad_rows(state_conv[0, :, 1, :], ts)
    (q_s, k_s, v_s, _, qi_s, idx_s, _, conv_s, ga_s, gc_s, u_s) = _in_proj_decode(xs2, prev2, prev1, weights, cw)

    attn_p = _attn_prompt(q_p, kvb_p, qi_p, kib_p, idx_p, nb)
    q8 = q_s[:db].reshape(db, N_HEADS, HEAD_DIM)
    qi8 = qi_s[:db].reshape(db, IDX_HEADS, LANES)[:, :, :IDX_DIM]
    wi8 = idx_s[:db, IDX_DIM:IDX_DIM + IDX_HEADS].reshape(db, IDX_HEADS, 1)
    kin = idx_s[:db, :IDX_DIM].astype(BF16).reshape(db, 1, IDX_DIM)
    kn8 = jnp.repeat(k_s[:db].reshape(db, N_KV_HEADS, HEAD_DIM), GROUP, axis=1).astype(BF16)
    vn8 = jnp.repeat(v_s[:db].reshape(db, N_KV_HEADS, HEAD_DIM), GROUP, axis=1).astype(BF16)
    ck2 = cache_k[0].reshape(n_pool * PAGE_SIZE * N_KV_HEADS, HEAD_DIM)
    cv2 = cache_v[0].reshape(n_pool * PAGE_SIZE * N_KV_HEADS, HEAD_DIM)
    keys_s, knew_s = _decode_scores(page_table, qi8, wi8, kin, jnp.swapaxes(cache_kidx[0], 1, 2))
    meta = _decode_pick(keys_s, knew_s, page_table.astype(F32)[:, :, None])
    attn_s8 = _decode_attend(meta[:, 0, :], q8, meta, kn8, vn8, ck2, cv2)
    attn_s = _pad_rows(attn_s8.reshape(db, N_HEADS * HEAD_DIM), ts)

    h2, e_t, gate_t = _post_attn(
        xp2, xs2, attn_p, attn_s, conv_p, conv_s, ga_p, ga_s, gc_p, gc_s,
        w_o_attn[0].astype(BF16), w_o_conv[0].astype(BF16), w_o[0].astype(BF16),
        ln1_g[0].reshape(1, D_MODEL), ln1_b[0].reshape(1, D_MODEL),
        router_w[0].T, router_bias[0].reshape(n_exp, 1))

    n_tok = tp + ts
    rank_t, cnt = _rank(e_t, n_exp)
    counts = cnt[:, 0]
    padded = (counts + MOE_BLOCK - 1) // MOE_BLOCK * MOE_BLOCK
    pend = jnp.cumsum(padded)
    pstart = pend - padded
    n_blocks = (n_tok * TOP_K + n_exp * (MOE_BLOCK - 1) + MOE_BLOCK - 1) // MOE_BLOCK
    nused = (pend[-1] // MOE_BLOCK).astype(I32)
    blk = jnp.minimum(jnp.arange(n_blocks, dtype=I32), nused - 1)
    block_e = jnp.minimum(jnp.sum((pend[None, :] <= (blk * MOE_BLOCK)[:, None]).astype(I32), axis=1), n_exp - 1)
    slot_t = _slots(e_t, rank_t, jnp.broadcast_to(pstart.astype(F32)[:, None], (n_exp, LANES)))
    xs = _dispatch(pend.astype(I32), padded.astype(I32), slot_t, h2, n_blocks * MOE_BLOCK)
    ys = _experts(block_e, nused.reshape(1), xs, moe_w_gate[0], moe_w_up[0], moe_w_down[0])
    y_p, y_s = _combine(slot_t, gate_t.T, h2, ys,
                        shared_w_gate[0].astype(BF16), shared_w_up[0].astype(BF16),
                        shared_w_down[0].astype(BF16),
                        ln2_g[0].reshape(1, D_MODEL), ln2_b[0].reshape(1, D_MODEL), tp, ts)

    conv_sample = jnp.stack([state_conv[0, :, 1, :], u_s[:db]], axis=1)[None]
    return (y_p.reshape(nb, seq, D_MODEL),
            y_s[:db].reshape(db, 1, D_MODEL),
            k_p.reshape(1, nb, seq, N_KV_HEADS, HEAD_DIM),
            v_p.reshape(1, nb, seq, N_KV_HEADS, HEAD_DIM),
            idx_p[:, :IDX_DIM].reshape(1, nb, seq, IDX_DIM),
            ulast_p[:, SUBLANES - (CONV_W - 1):, :][None],
            k_s[:db].reshape(1, db, 1, N_KV_HEADS, HEAD_DIM),
            v_s[:db].reshape(1, db, 1, N_KV_HEADS, HEAD_DIM),
            idx_s[:db, :IDX_DIM].reshape(1, db, 1, IDX_DIM),
            conv_sample)
```

```python
import functools

import jax
import jax.numpy as jnp
import numpy as np
from jax import lax
from jax.experimental import pallas as pl
from jax.experimental.pallas import tpu as pltpu

F32 = jnp.float32
BF16 = jnp.bfloat16
I32 = jnp.int32

D_MODEL = 1024
N_HEADS = 8
HEAD_DIM = 128
N_KV_HEADS = 2
GROUP = N_HEADS // N_KV_HEADS
IDX_HEADS = 8
IDX_DIM = 64
TOPK_KEYS = 256
IDX_SCALE = IDX_DIM ** -0.5 * IDX_HEADS ** -0.5
Q_SCALE = HEAD_DIM ** -0.5
PAGE_SIZE = 128
D_CONV = D_MODEL
CONV_W = 3
TOP_K = 8
N_GROUPS = 8
TOPK_GROUPS = 4
ROUTED_SCALE = 2.5
DEPTH = 1
ALPHA = (2 * DEPTH) ** 0.25
LN_EPS = 1e-5
NEG = -1e30
IN_WIDTHS = (N_HEADS * HEAD_DIM, N_KV_HEADS * HEAD_DIM, N_KV_HEADS * HEAD_DIM,
             IDX_HEADS * IDX_DIM, IDX_DIM, IDX_HEADS,
             D_CONV, D_CONV, D_CONV, D_MODEL, D_MODEL)

LANES = 128
SUBLANES = 8
VMEM_LIMIT_BYTES = 56 * 1024 * 1024

PROJ_ROWS = 512
ATT_Q = 256
TOKEN_TILE = 256
MOE_BLOCK = 256
CONV_CHUNK = 256
INT_MIN = -2 ** 31
SOFTMAX_DENOM_MIN = 1e-25
F32_MIN_NORMAL = float(np.finfo(np.float32).tiny)
BF16_MIN_NORMAL_BITS = 0x0080


def _np_key(v):
    b = int(np.float32(v).view(np.int32))
    return b ^ ((b >> 31) & 0x7FFFFFFF)


KEY_HALF = _np_key(NEG * 0.5)


def _sort_key(s):
    b = lax.bitcast_convert_type(s + 0.0, I32)
    return b ^ ((b >> 31) & jnp.int32(0x7FFFFFFF))


def _dot(a, b):
    return jnp.dot(a, b, preferred_element_type=F32)


def _dot_nt(a, b):
    return lax.dot_general(a, b, (((1,), (1,)), ((), ())), preferred_element_type=F32)


def _params(n_grid):
    return pltpu.CompilerParams(dimension_semantics=("arbitrary",) * n_grid,
                                vmem_limit_bytes=VMEM_LIMIT_BYTES)


def _const_spec(shape):
    nd = len(shape)
    return pl.BlockSpec(shape, lambda *_: (0,) * nd, pipeline_mode=pl.Buffered(1))


def _proj_common(xb, w1, w2, w3, wga, wgc, q_o, k_o, v_o, kv_o, qi_o, idx_o, kib_o, sga_o, sgc_o):
    nq = N_HEADS * HEAD_DIM
    nkv = N_KV_HEADS * HEAD_DIM
    z1 = _dot(xb, w1[...])
    q_o[...] = (z1[:, :nq] * Q_SCALE).astype(BF16)
    k = z1[:, nq:nq + nkv]
    v = z1[:, nq + nkv:nq + 2 * nkv]
    for g in range(N_KV_HEADS):
        k_o[pl.ds(g, k.shape[0], stride=N_KV_HEADS), :] = k[:, g * HEAD_DIM:(g + 1) * HEAD_DIM]
        v_o[pl.ds(g, v.shape[0], stride=N_KV_HEADS), :] = v[:, g * HEAD_DIM:(g + 1) * HEAD_DIM]
    kv_o[:, :nkv] = k.astype(BF16)
    kv_o[:, nkv:] = v.astype(BF16)
    qi_o[...] = _dot(xb, w2[...]).astype(BF16)
    z3 = _dot(xb, w3[...])
    idx_o[...] = z3
    kib_o[...] = z3.astype(BF16)
    sga_o[...] = jax.nn.sigmoid(_dot(xb, wga[...])).astype(BF16)
    sgc_o[...] = jax.nn.sigmoid(_dot(xb, wgc[...])).astype(BF16)


def _in_proj_prompt_kernel(x_ref, w1, w2, w3, wb, wc, wx, wga, wgc, cw_ref,
                           q_o, k_o, v_o, kv_o, qi_o, idx_o, kib_o, conv_o, sga_o, sgc_o, ulast_o,
                           tail_ref):
    j = pl.program_id(1)

    @pl.when(j == 0)
    def _():
        tail_ref[...] = jnp.zeros_like(tail_ref)

    xb = x_ref[...].astype(BF16)
    _proj_common(xb, w1, w2, w3, wga, wgc, q_o, k_o, v_o, kv_o, qi_o, idx_o, kib_o, sga_o, sgc_o)
    tm = xb.shape[0]
    row = lax.broadcasted_iota(I32, (tm, CONV_CHUNK), 0)
    for c in range(D_CONV // CONV_CHUNK):
        cs = slice(c * CONV_CHUNK, (c + 1) * CONV_CHUNK)
        gb = _dot(xb, wb[:, cs])
        u = _dot(xb, wc[:, cs]) * _dot(xb, wx[:, cs])
        p1 = tail_ref[SUBLANES - 1:SUBLANES, cs]
        p2 = tail_ref[SUBLANES - 2:SUBLANES - 1, cs]
        u1 = jnp.where(row == 0, p1, pltpu.roll(u, 1, 0))
        u2 = jnp.where(row == 0, p2, jnp.where(row == 1, p1, pltpu.roll(u, 2, 0)))
        conv = cw_ref[0:1, cs] * u2 + cw_ref[1:2, cs] * u1 + cw_ref[2:3, cs] * u
        conv_o[:, cs] = (gb * conv).astype(BF16)
        tail_ref[:, cs] = u[tm - SUBLANES:, :]
    ulast_o[0] = tail_ref[...]


def _in_proj_decode_kernel(x_ref, p2_ref, p1_ref, w1, w2, w3, wb, wc, wx, wga, wgc, cw_ref,
                           q_o, k_o, v_o, kv_o, qi_o, idx_o, kib_o, conv_o, sga_o, sgc_o, u_o):
    xb = x_ref[...].astype(BF16)
    _proj_common(xb, w1, w2, w3, wga, wgc, q_o, k_o, v_o, kv_o, qi_o, idx_o, kib_o, sga_o, sgc_o)
    for c in range(D_CONV // CONV_CHUNK):
        cs = slice(c * CONV_CHUNK, (c + 1) * CONV_CHUNK)
        gb = _dot(xb, wb[:, cs])
        u = _dot(xb, wc[:, cs]) * _dot(xb, wx[:, cs])
        conv = cw_ref[0:1, cs] * p2_ref[:, cs] + cw_ref[1:2, cs] * p1_ref[:, cs] + cw_ref[2:3, cs] * u
        conv_o[:, cs] = (gb * conv).astype(BF16)
        u_o[:, cs] = u


def _prep_in_weights(w_in):
    w = w_in.astype(BF16)
    offs = np.concatenate([[0], np.cumsum(IN_WIDTHS)])
    part = [w[:, int(offs[i]):int(offs[i + 1])] for i in range(len(IN_WIDTHS))]
    q, k, v, qi, ki, wi, gb, gc, xv, ga, gcv = part
    d = w.shape[0]
    w1 = jnp.concatenate([q, k, v], axis=1)
    w2 = jnp.pad(qi.reshape(d, IDX_HEADS, IDX_DIM),
                 ((0, 0), (0, 0), (0, LANES - IDX_DIM))).reshape(d, IDX_HEADS * LANES)
    w3 = jnp.concatenate([ki, wi, jnp.zeros((d, LANES - IDX_DIM - IDX_HEADS), BF16)], axis=1)
    return (w1, w2, w3, gb, gc, xv, ga, gcv)


def _proj_out_shapes(t):
    nkv = N_KV_HEADS * HEAD_DIM
    sd = jax.ShapeDtypeStruct
    return [sd((t, N_HEADS * HEAD_DIM), BF16), sd((t * N_KV_HEADS, HEAD_DIM), F32),
            sd((t * N_KV_HEADS, HEAD_DIM), F32), sd((t, 2 * nkv), BF16),
            sd((t, IDX_HEADS * LANES), BF16), sd((t, LANES), F32), sd((t, LANES), BF16),
            sd((t, D_CONV), BF16), sd((t, D_MODEL), BF16), sd((t, D_MODEL), BF16)]


def _in_proj_prompt(x2d, weights, conv_w, nb):
    t = x2d.shape[0]
    s = t // nb
    tm = min(PROJ_ROWS, s)
    nj = s // tm
    row_spec = lambda w, rep=1: pl.BlockSpec((tm * rep, w), lambda b, j: (b * nj + j, 0))
    out_shapes = _proj_out_shapes(t) + [jax.ShapeDtypeStruct((nb, SUBLANES, D_CONV), F32)]
    out_specs = [row_spec(o.shape[1], o.shape[0] // t) for o in out_shapes[:-1]]
    out_specs.append(pl.BlockSpec((1, SUBLANES, D_CONV), lambda b, j: (b, 0, 0)))
    return pl.pallas_call(
        _in_proj_prompt_kernel,
        grid=(nb, nj),
        in_specs=[row_spec(D_MODEL)] + [_const_spec(w.shape) for w in weights] + [_const_spec(conv_w.shape)],
        out_specs=out_specs,
        out_shape=out_shapes,
        scratch_shapes=[pltpu.VMEM((SUBLANES, D_CONV), F32)],
        compiler_params=_params(2),
        name="in_proj_prompt",
    )(x2d, *weights, conv_w)


def _in_proj_decode(x2d, p2, p1, weights, conv_w):
    t = x2d.shape[0]
    row_spec = lambda w, rep=1: pl.BlockSpec((t * rep, w), lambda i: (0, 0))
    out_shapes = _proj_out_shapes(t) + [jax.ShapeDtypeStruct((t, D_CONV), F32)]
    return pl.pallas_call(
        _in_proj_decode_kernel,
        grid=(1,),
        in_specs=[row_spec(D_MODEL)] * 3 + [_const_spec(w.shape) for w in weights] + [_const_spec(conv_w.shape)],
        out_specs=[row_spec(o.shape[1], o.shape[0] // t) for o in out_shapes],
        out_shape=out_shapes,
        compiler_params=_params(1),
        name="in_proj_decode",
    )(x2d, p2, p1, *weights, conv_w)


def _attn_prompt_kernel(q_ref, kv_ref, qi_ref, kib_ref, idx_ref, o_ref,
                        key_ref, hi_ref, bias_ref, t_ref, j_ref, need_ref, m_ref, acc_ref, knorm_ref,
                        *, n_keep, seq_len):
    tq = q_ref.shape[0]
    i = pl.program_id(1)
    nchunk = i + 1
    nsub = tq // LANES

    idx_t = jnp.transpose(idx_ref[...])
    wi_rows = [idx_t[IDX_DIM + h:IDX_DIM + h + 1, :] * IDX_SCALE for h in range(IDX_HEADS)]
    key_i = lax.broadcasted_iota(I32, (tq, tq), 0)
    qry_i = lax.broadcasted_iota(I32, (tq, tq), 1)

    def score_chunk(c, carry):
        kic = kib_ref[pl.ds(pl.multiple_of(c * tq, tq), tq), :]
        acc = jnp.zeros((tq, tq), F32)
        for h in range(IDX_HEADS):
            d = _dot_nt(kic, qi_ref[:, h * LANES:(h + 1) * LANES])
            acc = acc + jnp.maximum(d, 0.0) * wi_rows[h]
        acc = jnp.where(jnp.abs(acc) < F32_MIN_NORMAL, 0.0, acc)
        s = jnp.where((c < i) | (key_i <= qry_i), acc, NEG)
        bits = lax.bitcast_convert_type(s, I32)
        key_ref[c] = bits ^ ((bits >> 31) & jnp.int32(0x7FFFFFFF))
        hi_ref[c] = lax.bitcast_convert_type(bits & jnp.int32(-65536), F32).astype(BF16)
        return carry

    lax.fori_loop(0, nchunk, score_chunk, 0)

    sub_i = lax.broadcasted_iota(I32, (SUBLANES, tq), 0)

    def counts(preds, with_pos=False):
        def body(c, cnts):
            for kk in range(tq // SUBLANES):
                blk = key_ref[c, kk * SUBLANES:(kk + 1) * SUBLANES, :]
                pos = (c * tq + kk * SUBLANES + sub_i) if with_pos else None
                cnts = tuple(cnt + pred(blk, pos).astype(I32) for cnt, pred in zip(cnts, preds))
            return cnts
        cnts = lax.fori_loop(0, nchunk, body, tuple(jnp.zeros((SUBLANES, tq), I32) for _ in preds))
        return tuple(jnp.sum(cnt, axis=0, keepdims=True) for cnt in cnts)

    def count_hi(cand):
        rows16 = 2 * SUBLANES
        def body(c, cnt):
            for kk in range(tq // rows16):
                blk = hi_ref[c, kk * rows16:(kk + 1) * rows16, :]
                cnt = cnt + jnp.where(blk >= cand, jnp.ones_like(cnt), jnp.zeros_like(cnt))
            return cnt
        cnt = lax.fori_loop(0, nchunk, body, jnp.zeros((rows16, tq), BF16))
        return jnp.sum(cnt.astype(F32), axis=0, keepdims=True)

    def hi_bit_body(bi, p16):
        cand = p16 + lax.shift_left(jnp.int32(1), 15 - bi)
        raw = cand ^ ((cand >> 31) & jnp.int32(0x7FFF))
        raw = jnp.where((raw > 0) & (raw < BF16_MIN_NORMAL_BITS), BF16_MIN_NORMAL_BITS, raw)
        cand_f = lax.bitcast_convert_type(lax.shift_left(raw, 16), F32)
        total = count_hi(jnp.broadcast_to(cand_f, (2 * SUBLANES, tq)).astype(BF16))
        return jnp.where(total >= n_keep, cand, p16)

    p16 = lax.fori_loop(0, 16, hi_bit_body, jnp.full((1, tq), -2 ** 15, I32))

    def bit_body(bi, t):
        cand = t + lax.shift_left(jnp.int32(1), 15 - bi)
        cand_b = jnp.broadcast_to(cand, (SUBLANES, tq))
        total, = counts((lambda blk, _: blk >= cand_b,))
        return jnp.where(total >= n_keep, cand, t)

    t = lax.fori_loop(0, 16, bit_body, lax.shift_left(p16, 16))
    t_b = jnp.broadcast_to(t, (SUBLANES, tq))
    cnt_gt, cnt_eq = counts((lambda blk, _: blk > t_b, lambda blk, _: blk == t_b))
    need = n_keep - cnt_gt
    any_excess = jnp.max(((cnt_eq > need) & (t > KEY_HALF)).astype(I32))
    t_ref[...] = t_b
    need_ref[...] = jnp.broadcast_to(need, (SUBLANES, tq))
    j_ref[...] = jnp.full(j_ref.shape, seq_len, I32)

    @pl.when(any_excess > 0)
    def _():
        nbits = max(1, (seq_len - 1).bit_length())
        t_b = t_ref[...]
        need = need_ref[0:1, :]

        def jbit(bi, jj):
            cand = jj | lax.shift_left(jnp.int32(1), nbits - 1 - bi)
            cand_b = jnp.broadcast_to(cand, (SUBLANES, tq))
            g, = counts((lambda blk, pos: (blk == t_b) & (pos < cand_b),), with_pos=True)
            return jnp.where(g < need, cand, jj)

        jj = lax.fori_loop(0, nbits, jbit, jnp.zeros((1, tq), I32))
        j_ref[...] = jnp.broadcast_to(jj, (SUBLANES, tq))

    t_row = t_ref[0:1, :]
    j_row = j_ref[0:1, :]

    def bias_chunk(c, carry):
        blk = key_ref[c]
        sel = (blk > KEY_HALF) & ((blk > t_row) | ((blk == t_row) & ((c * tq + key_i) <= j_row)))
        bias_ref[c] = jnp.transpose(jnp.where(sel, 0.0, NEG))
        return carry

    lax.fori_loop(0, nchunk, bias_chunk, 0)

    nkv = N_KV_HEADS * HEAD_DIM
    gq = GROUP * tq

    def group_logits(c, g):
        rows = pl.ds(pl.multiple_of(c * tq, tq), tq)
        kc = kv_ref[rows, g * HEAD_DIM:(g + 1) * HEAD_DIM]
        qg = jnp.concatenate([q_ref[:, h * HEAD_DIM:(h + 1) * HEAD_DIM]
                              for h in range(g * GROUP, (g + 1) * GROUP)], axis=0)
        s = _dot_nt(qg, kc).reshape(GROUP, tq, tq) + bias_ref[c][None]
        return s.reshape(gq, tq)

    def exact_row_max():
        m_ref[...] = jnp.full(m_ref.shape, NEG, F32)

        def max_chunk(c, carry):
            for g in range(N_KV_HEADS):
                s = group_logits(c, g)
                mx = s[:, :LANES]
                for k in range(1, nsub):
                    mx = jnp.maximum(mx, s[:, k * LANES:(k + 1) * LANES])
                m_ref[g] = jnp.maximum(m_ref[g], mx)
            return carry

        lax.fori_loop(0, nchunk, max_chunk, 0)
        for g in range(N_KV_HEADS):
            m_ref[g] = jnp.broadcast_to(jnp.max(m_ref[g], axis=1, keepdims=True), (gq, LANES))

    ones = jnp.ones((tq, HEAD_DIM), BF16)

    def accumulate():
        acc_ref[...] = jnp.zeros_like(acc_ref)

        def pv_chunk(c, carry):
            rows = pl.ds(pl.multiple_of(c * tq, tq), tq)
            for g in range(N_KV_HEADS):
                s = group_logits(c, g)
                m = m_ref[g]
                p = jnp.concatenate([jnp.exp(s[:, k * LANES:(k + 1) * LANES] - m) for k in range(nsub)], axis=1)
                vc = kv_ref[rows, nkv + g * HEAD_DIM:nkv + (g + 1) * HEAD_DIM]
                acc_ref[g] = acc_ref[g] + _dot(p.astype(BF16), jnp.concatenate([vc, ones], axis=1))
            return carry

        lax.fori_loop(0, nchunk, pv_chunk, 0)

    @pl.when(i == 0)
    def _():
        for g in range(N_KV_HEADS):
            def norm_chunk(c, mx):
                kc = kv_ref[pl.ds(pl.multiple_of(c * tq, tq), tq), g * HEAD_DIM:(g + 1) * HEAD_DIM].astype(F32)
                return jnp.maximum(mx, jnp.sum(kc * kc, axis=1, keepdims=True))
            k2 = lax.fori_loop(0, seq_len // tq, norm_chunk, jnp.zeros((tq, 1), F32))
            knorm_ref[g] = jnp.broadcast_to(jnp.sqrt(jnp.max(k2, axis=0, keepdims=True)), (SUBLANES, LANES))

    for h in range(N_HEADS):
        g, r = h // GROUP, h % GROUP
        qh = q_ref[:, h * HEAD_DIM:(h + 1) * HEAD_DIM].astype(F32)
        qn = jnp.sqrt(jnp.sum(qh * qh, axis=1, keepdims=True))
        m_ref[g, r * tq:(r + 1) * tq, :] = qn * knorm_ref[g][0:1, :]
    accumulate()
    denom_min = jnp.min(acc_ref[:, :, HEAD_DIM:HEAD_DIM + 1])

    @pl.when(denom_min < SOFTMAX_DENOM_MIN)
    def _():
        exact_row_max()
        accumulate()

    for h in range(N_HEADS):
        a = acc_ref[h // GROUP, (h % GROUP) * tq:(h % GROUP + 1) * tq, :]
        o_ref[:, h * HEAD_DIM:(h + 1) * HEAD_DIM] = (a[:, :HEAD_DIM] / a[:, HEAD_DIM:]).astype(BF16)


def _attn_prompt(q, kvb, qi, kib, idx, nb):
    t = q.shape[0]
    s = t // nb
    tq = min(ATT_Q, s)
    nq = s // tq
    n_keep = min(TOPK_KEYS, s // 4)
    blk = lambda w: pl.BlockSpec((tq, w), lambda b, i: (b * nq + i, 0))
    seq = lambda w: pl.BlockSpec((s, w), lambda b, i: (b, 0))
    return pl.pallas_call(
        functools.partial(_attn_prompt_kernel, n_keep=n_keep, seq_len=s),
        grid=(nb, nq),
        in_specs=[blk(q.shape[1]), seq(kvb.shape[1]), blk(qi.shape[1]), seq(kib.shape[1]), blk(idx.shape[1])],
        out_specs=blk(N_HEADS * HEAD_DIM),
        out_shape=jax.ShapeDtypeStruct((t, N_HEADS * HEAD_DIM), BF16),
        scratch_shapes=[pltpu.VMEM((nq, tq, tq), I32), pltpu.VMEM((nq, tq, tq), BF16),
                        pltpu.VMEM((nq, tq, tq), F32),
                        pltpu.VMEM((SUBLANES, tq), I32), pltpu.VMEM((SUBLANES, tq), I32),
                        pltpu.VMEM((SUBLANES, tq), I32),
                        pltpu.VMEM((N_KV_HEADS, GROUP * tq, LANES), F32),
                        pltpu.VMEM((N_KV_HEADS, GROUP * tq, 2 * HEAD_DIM), F32),
                        pltpu.VMEM((N_KV_HEADS, SUBLANES, LANES), F32)],
        compiler_params=_params(2),
        name="attn_prompt",
    )(q, kvb, qi, kib, idx)


DECODE_PAGES_PER_ROW = 4


def _decode_scores_kernel(pt_ref, qi_ref, wi_ref, kin_ref, cx_hbm, key_o, knew_o, xbuf, sem, *, n_pages):
    b = pl.program_id(0)
    nb = pl.num_programs(0)
    ppr = DECODE_PAGES_PER_ROW
    nrow = n_pages // ppr

    def page_copy(bb, p, slot):
        return pltpu.make_async_copy(cx_hbm.at[pt_ref[bb, p]], xbuf.at[slot, p], sem.at[slot])

    def start_batch(bb, slot):
        def body(p, carry):
            page_copy(bb, p, slot).start()
            return carry
        lax.fori_loop(0, n_pages, body, 0)

    def wait_batch(bb, slot):
        def body(p, carry):
            page_copy(bb, p, slot).wait()
            return carry
        lax.fori_loop(0, n_pages, body, 0)

    @pl.when(b == 0)
    def _():
        start_batch(0, 0)

    @pl.when(b + 1 < nb)
    def _():
        start_batch(b + 1, (b + 1) % 2)

    slot = b % 2
    wait_batch(b, slot)

    qi8 = qi_ref[0]
    wi8 = wi_ref[0] * IDX_SCALE

    def score_row(r, carry):
        kx = jnp.concatenate([xbuf[slot, r * ppr + k] for k in range(ppr)], axis=1).astype(BF16)
        d = _dot(qi8, kx)
        sc = jnp.sum(jnp.maximum(d, 0.0) * wi8, axis=0, keepdims=True)
        key_o[0, pl.ds(r, 1), :] = _sort_key(sc)
        return carry

    lax.fori_loop(0, nrow, score_row, 0, unroll=4)
    dn = jnp.sum(qi8.astype(F32) * kin_ref[0].astype(F32), axis=1, keepdims=True)
    key_new = _sort_key(jnp.sum(jnp.maximum(dn, 0.0) * wi8, axis=0, keepdims=True))
    knew_o[0] = jnp.broadcast_to(key_new, (1, LANES))


def _decode_scores(page_table, qi8, wi8, kin, cxt):
    db, n_pages = page_table.shape
    nrow = n_pages // DECODE_PAGES_PER_ROW
    w = DECODE_PAGES_PER_ROW * PAGE_SIZE
    per_b = lambda a: pl.BlockSpec((1,) + a.shape[1:], lambda b, pt: (b, 0, 0))
    grid_spec = pltpu.PrefetchScalarGridSpec(
        num_scalar_prefetch=1,
        grid=(db,),
        in_specs=[per_b(qi8), per_b(wi8), per_b(kin), pl.BlockSpec(memory_space=pl.ANY)],
        out_specs=[pl.BlockSpec((1, nrow, w), lambda b, pt: (b, 0, 0)),
                   pl.BlockSpec((1, 1, LANES), lambda b, pt: (b, 0, 0))],
        scratch_shapes=[pltpu.VMEM((2, n_pages, IDX_DIM, PAGE_SIZE), F32), pltpu.SemaphoreType.DMA((2,))],
    )
    return pl.pallas_call(
        functools.partial(_decode_scores_kernel, n_pages=n_pages),
        grid_spec=grid_spec,
        out_shape=[jax.ShapeDtypeStruct((db, nrow, w), I32), jax.ShapeDtypeStruct((db, 1, LANES), I32)],
        compiler_params=_params(1),
        name="decode_scores",
    )(page_table, qi8, wi8, kin, cxt)


def _decode_pick_kernel(keys_ref, knew_ref, pt_ref, meta_o, *, n_keep):
    nb, nrow, w = keys_ref.shape
    past = nrow * w
    n_pages = pt_ref.shape[1]
    keys = keys_ref[...]
    key_new = knew_ref[...][:, :, 0:1]
    col = (lax.broadcasted_iota(I32, keys.shape, 1) * w + lax.broadcasted_iota(I32, keys.shape, 2))

    def total(x):
        c = x.astype(I32)
        part = c[:, :, :LANES]
        for k in range(1, w // LANES):
            part = part + c[:, :, k * LANES:(k + 1) * LANES]
        return jnp.sum(jnp.sum(part, axis=1, keepdims=True), axis=2, keepdims=True)

    def bit_body(bi, t):
        cand = t + lax.shift_left(jnp.int32(1), 31 - bi)
        cnt = total(keys >= cand) + (key_new >= cand).astype(I32)
        return jnp.where(cnt >= n_keep, cand, t)

    t = lax.fori_loop(0, 32, bit_body, jnp.full((nb, 1, 1), INT_MIN, I32))
    need = n_keep - (total(keys > t) + (key_new > t).astype(I32))
    nbits = past.bit_length()

    def jbit(bi, jj):
        cand = jj | lax.shift_left(jnp.int32(1), nbits - 1 - bi)
        g = total((keys == t) & (col < cand)) + ((key_new == t) & (past < cand)).astype(I32)
        return jnp.where(g < need, cand, jj)

    jj = lax.fori_loop(0, nbits, jbit, jnp.zeros((nb, 1, 1), I32))
    sel = (keys > t) | ((keys == t) & (col <= jj))
    sel_new = ((key_new > t) | ((key_new == t) & (past <= jj))).astype(I32)

    rows = nb * nrow
    incl_lane = (lax.broadcasted_iota(I32, (w, w), 0) <= lax.broadcasted_iota(I32, (w, w), 1)).astype(BF16)
    cnt_in_row = _dot(sel.astype(BF16).reshape(rows, w), incl_lane)
    row_tot = cnt_in_row[:, w - 1:w]
    ri = lax.broadcasted_iota(I32, (rows, rows), 0)
    rj = lax.broadcasted_iota(I32, (rows, rows), 1)
    earlier_row_same_seq = ((ri // nrow) == (rj // nrow)) & (rj < ri)
    row_off = _dot(earlier_row_same_seq.astype(BF16),
                   jnp.broadcast_to(row_tot, (rows, LANES)).astype(BF16))[:, 0:1]
    cnt3 = cnt_in_row.astype(BF16).reshape(nb, nrow, w)
    off3 = row_off.reshape(nb, nrow, 1)
    incl3 = (row_off + row_tot).reshape(nb, nrow, 1)
    jl = lax.broadcasted_iota(I32, (1, 1, n_keep), 2).astype(F32)
    row_j = jnp.sum((incl3 <= jl).astype(F32), axis=1, keepdims=True)
    onehot = lax.broadcasted_iota(I32, (nb, nrow, n_keep), 1).astype(F32) == row_j
    off_j = jnp.sum(jnp.where(onehot, off3, 0.0), axis=1, keepdims=True)
    local_rank = jl - off_j
    onehot_b = onehot.astype(BF16)
    page_i = lax.broadcasted_iota(I32, (n_pages, n_keep), 0)
    row_i = lax.broadcasted_iota(I32, (SUBLANES, n_keep), 0)
    for b in range(nb):
        cnt_j = lax.dot_general(cnt3[b], onehot_b[b], (((0,), (0,)), ((), ())),
                                preferred_element_type=F32)
        lane_j = jnp.sum((cnt_j <= local_rank[b]).astype(F32), axis=0, keepdims=True)
        pos = jnp.minimum((row_j[b] * w + lane_j).astype(I32), past - 1)
        phys = jnp.sum(jnp.where(page_i == pos // PAGE_SIZE, pt_ref[b], 0.0), axis=0, keepdims=True)
        tile = (phys.astype(I32) * PAGE_SIZE + pos % PAGE_SIZE) // (SUBLANES // N_KV_HEADS)
        meta_o[b] = jnp.where(row_i == 0, tile,
                              jnp.where(row_i == 1, sel_new[b], pos % (SUBLANES // N_KV_HEADS)))


def _decode_pick(keys, knew, pt_f):
    nb, nrow, w = keys.shape
    n_keep = min(TOPK_KEYS, (nrow * w + 1) // 4)
    full = lambda a: pl.BlockSpec(a.shape, lambda i: (0,) * a.ndim)
    return pl.pallas_call(
        functools.partial(_decode_pick_kernel, n_keep=n_keep),
        grid=(1,),
        in_specs=[full(keys), full(knew), full(pt_f)],
        out_specs=pl.BlockSpec((nb, SUBLANES, n_keep), lambda i: (0, 0, 0)),
        out_shape=jax.ShapeDtypeStruct((nb, SUBLANES, n_keep), I32),
        compiler_params=_params(1),
        name="decode_pick",
    )(keys, knew, pt_f)


def _decode_attend_kernel(tile_ref, q_ref, meta_ref, kn_ref, vn_ref, ck_hbm, cv_hbm, o_ref,
                          kbuf, vbuf, expand_ref, sem, *, n_keep):
    b = pl.program_id(0)
    nb = pl.num_programs(0)
    ncol = n_keep * SUBLANES

    def item_copies(bb, j, slot):
        src = pl.ds(pl.multiple_of(tile_ref[bb, j] * SUBLANES, SUBLANES), SUBLANES)
        dst = pl.ds(pl.multiple_of(j * SUBLANES, SUBLANES), SUBLANES)
        return (pltpu.make_async_copy(ck_hbm.at[src], kbuf.at[slot, dst], sem.at[0, slot]),
                pltpu.make_async_copy(cv_hbm.at[src], vbuf.at[slot, dst], sem.at[1, slot]))

    def start_batch(bb, slot):
        def body(j, carry):
            for prio, cp in enumerate(item_copies(bb, j, slot)):
                cp.start(priority=prio)
            return carry
        lax.fori_loop(0, n_keep, body, 0)

    def wait_batch(slot):
        pltpu.make_async_copy(ck_hbm.at[pl.ds(0, ncol)], kbuf.at[slot], sem.at[0, slot]).wait()
        pltpu.make_async_copy(cv_hbm.at[pl.ds(0, ncol)], vbuf.at[slot], sem.at[1, slot]).wait()

    @pl.when(b == 0)
    def _():
        start_batch(0, 0)
        item_of_col = lax.broadcasted_iota(I32, (n_keep, ncol), 1) // SUBLANES
        expand_ref[...] = (item_of_col == lax.broadcasted_iota(I32, (n_keep, ncol), 0)).astype(BF16)

    @pl.when(b + 1 < nb)
    def _():
        start_batch(b + 1, (b + 1) % 2)

    slot = b % 2
    wait_batch(slot)

    q8 = q_ref[0]
    meta = meta_ref[0]
    new_kept = meta[1:2, 0:1] > 0
    sub_col = _dot(meta.astype(BF16), expand_ref[...])[2:3, :]
    head_i = lax.broadcasted_iota(I32, (N_HEADS, ncol), 0)
    col_i = lax.broadcasted_iota(I32, (N_HEADS, ncol), 1)
    want = sub_col * N_KV_HEADS + (head_i // GROUP).astype(F32)
    is_new_item = new_kept & (col_i // SUBLANES == n_keep - 1)
    ok = ((col_i % SUBLANES).astype(F32) == want) & jnp.logical_not(is_new_item)

    s = jnp.where(ok, _dot_nt(q8, kbuf[slot].astype(BF16)), NEG)
    s_new = jnp.sum(q8.astype(F32) * kn_ref[0].astype(F32), axis=1, keepdims=True)
    s_new = jnp.where(new_kept, s_new, NEG)
    m = jnp.maximum(jnp.max(s, axis=1, keepdims=True), s_new)
    p = jnp.exp(s - m)
    p_new = jnp.exp(s_new - m)
    l = jnp.sum(p, axis=1, keepdims=True) + p_new
    acc = _dot(p.astype(BF16), vbuf[slot].astype(BF16)) + p_new.astype(BF16).astype(F32) * vn_ref[0].astype(F32)
    o_ref[0] = (acc / l).astype(BF16)


def _decode_attend(tiles, q8, meta, kn8, vn8, ck2, cv2):
    db, n_keep = tiles.shape
    per_b = lambda a: pl.BlockSpec((1,) + a.shape[1:], lambda b, tl: (b, 0, 0))
    any_spec = pl.BlockSpec(memory_space=pl.ANY)
    ncol = n_keep * SUBLANES
    grid_spec = pltpu.PrefetchScalarGridSpec(
        num_scalar_prefetch=1,
        grid=(db,),
        in_specs=[per_b(q8), per_b(meta), per_b(kn8), per_b(vn8), any_spec, any_spec],
        out_specs=pl.BlockSpec((1, N_HEADS, HEAD_DIM), lambda b, tl: (b, 0, 0)),
        scratch_shapes=[pltpu.VMEM((2, ncol, HEAD_DIM), F32), pltpu.VMEM((2, ncol, HEAD_DIM), F32),
                        pltpu.VMEM((n_keep, ncol), BF16), pltpu.SemaphoreType.DMA((2, 2))],
    )
    return pl.pallas_call(
        functools.partial(_decode_attend_kernel, n_keep=n_keep),
        grid_spec=grid_spec,
        out_shape=jax.ShapeDtypeStruct((db, N_HEADS, HEAD_DIM), BF16),
        compiler_params=_params(1),
        name="decode_attend",
    )(tiles, q8, meta, kn8, vn8, ck2, cv2)


def _layer_norm(r, g, b):
    mu = jnp.mean(r, axis=-1, keepdims=True)
    d = r - mu
    var = jnp.mean(d * d, axis=-1, keepdims=True)
    return d * lax.rsqrt(var + LN_EPS) * g + b


def _route(logits_t, rbias):
    n_exp, tm = logits_t.shape
    epg = n_exp // N_GROUPS
    s = jax.nn.sigmoid(logits_t)
    sb = s + rbias
    ie = lax.broadcasted_iota(I32, (epg, tm), 0)
    gs_rows = []
    for g in range(N_GROUPS):
        blk = sb[g * epg:(g + 1) * epg, :]
        m1 = jnp.max(blk, axis=0, keepdims=True)
        i1 = jnp.min(jnp.where(blk == m1, ie, epg), axis=0, keepdims=True)
        m2 = jnp.max(jnp.where(ie == i1, -jnp.inf, blk), axis=0, keepdims=True)
        gs_rows.append(m1 + m2)
    picked = [jnp.zeros((1, tm), jnp.bool_) for _ in range(N_GROUPS)]
    cur = list(gs_rows)
    for _ in range(TOPK_GROUPS):
        mx = cur[0]
        for g in range(1, N_GROUPS):
            mx = jnp.maximum(mx, cur[g])
        found = jnp.zeros((1, tm), jnp.bool_)
        for g in range(N_GROUPS):
            hit = (cur[g] == mx) & jnp.logical_not(found)
            found = found | hit
            picked[g] = picked[g] | hit
            cur[g] = jnp.where(hit, -jnp.inf, cur[g])
    masked = jnp.concatenate(
        [jnp.where(picked[g], sb[g * epg:(g + 1) * epg, :], NEG) for g in range(N_GROUPS)], axis=0)
    iall = lax.broadcasted_iota(I32, (n_exp, tm), 0)
    e_rows, w_rows = [], []
    for _ in range(TOP_K):
        mx = jnp.max(masked, axis=0, keepdims=True)
        ix = jnp.min(jnp.where(masked == mx, iall, n_exp), axis=0, keepdims=True)
        hit = iall == ix
        w_rows.append(jnp.sum(jnp.where(hit, s, 0.0), axis=0, keepdims=True))
        e_rows.append(ix)
        masked = jnp.where(hit, -jnp.inf, masked)
    wsum = w_rows[0]
    for w in w_rows[1:]:
        wsum = wsum + w
    gates = [w / wsum * ROUTED_SCALE for w in w_rows]
    return jnp.concatenate(e_rows, axis=0), jnp.concatenate(gates, axis=0)


def _post_attn_kernel(xp, xs, ap, as_, cp, cs, gap, gas, gcp, gcs, woa, woc, wo, g1, b1, rwt_hi, rwt_lo, rb,
                      h2_o, e_o, gate_o, *, n_prompt_tiles):
    i = pl.program_id(0)
    is_p = i < n_prompt_tiles
    pick = lambda a, b: jnp.where(is_p, a[...], b[...])
    a = _dot(pick(ap, as_), woa[...])
    c = _dot(pick(cp, cs), woc[...])
    merged = pick(gap, gas).astype(F32) * a + pick(gcp, gcs).astype(F32) * c
    r = ALPHA * pick(xp, xs) + _dot(merged.astype(BF16), wo[...])
    h = _layer_norm(r, g1[...], b1[...])
    tm = h.shape[0]
    for k in range(D_MODEL // LANES):
        h2_o[pl.ds(k, tm, stride=SUBLANES), :] = h[:, k * LANES:(k + 1) * LANES]
    h_hi = h.astype(BF16)
    h_lo = (h - h_hi.astype(F32)).astype(BF16)
    logits_t = _dot_nt(rwt_hi[...], h_hi) + (_dot_nt(rwt_hi[...], h_lo) + _dot_nt(rwt_lo[...], h_hi))
    e_idx, gates = _route(logits_t, rb[...])
    e_o[...] = e_idx
    gate_o[...] = gates


def _post_attn(x_p, x_s, attn_p, attn_s, conv_p, conv_s, ga_p, ga_s, gc_p, gc_s,
               woa, woc, wo, g1, b1, rwt, rb):
    rwt_hi = rwt.astype(BF16)
    rwt_lo = (rwt - rwt_hi.astype(F32)).astype(BF16)
    tp, ts = x_p.shape[0], x_s.shape[0]
    tm = TOKEN_TILE
    npt, nst = tp // tm, ts // tm
    n_tok = tp + ts
    p_spec = pl.BlockSpec((tm, D_MODEL), lambda i: (jnp.minimum(i, npt - 1), 0))
    s_spec = pl.BlockSpec((tm, D_MODEL), lambda i: (jnp.maximum(i - npt, 0), 0))
    consts = [woa, woc, wo, g1, b1, rwt_hi, rwt_lo, rb]
    return pl.pallas_call(
        functools.partial(_post_attn_kernel, n_prompt_tiles=npt),
        grid=(npt + nst,),
        in_specs=[p_spec, s_spec] * 5 + [_const_spec(c.shape) for c in consts],
        out_specs=[pl.BlockSpec((tm * SUBLANES, LANES), lambda i: (i, 0)),
                   pl.BlockSpec((TOP_K, tm), lambda i: (0, i)),
                   pl.BlockSpec((TOP_K, tm), lambda i: (0, i))],
        out_shape=[jax.ShapeDtypeStruct((n_tok * SUBLANES, LANES), F32),
                   jax.ShapeDtypeStruct((TOP_K, n_tok), I32),
                   jax.ShapeDtypeStruct((TOP_K, n_tok), F32)],
        compiler_params=_params(1),
        name="post_attn",
    )(x_p, x_s, attn_p, attn_s, conv_p, conv_s, ga_p, ga_s, gc_p, gc_s, *consts)


def _rank_kernel(e_ref, rank_o, cnt_o, carry_ref):
    i = pl.program_id(0)

    @pl.when(i == 0)
    def _():
        carry_ref[...] = jnp.zeros_like(carry_ref)

    n_exp = carry_ref.shape[0]
    e = e_ref[...]
    tk = e.shape[1]
    ie = lax.broadcasted_iota(I32, (n_exp, tk), 0)
    onehot = jnp.zeros((n_exp, tk), F32)
    for j in range(TOP_K):
        onehot = onehot + (ie == e[j:j + 1, :]).astype(F32)
    before = (lax.broadcasted_iota(I32, (tk, tk), 0) < lax.broadcasted_iota(I32, (tk, tk), 1)).astype(BF16)
    prefix = _dot(onehot.astype(BF16), before) + carry_ref[:, 0:1]
    rows = [jnp.sum(jnp.where(ie == e[j:j + 1, :], prefix, 0.0), axis=0, keepdims=True) for j in range(TOP_K)]
    rank_o[...] = jnp.concatenate(rows, axis=0).astype(I32)
    carry_ref[...] = carry_ref[...] + jnp.sum(onehot, axis=1, keepdims=True)
    cnt_o[...] = carry_ref[...].astype(I32)


def _rank(e_t, n_exp):
    n_tok = e_t.shape[1]
    tk = TOKEN_TILE
    return pl.pallas_call(
        _rank_kernel,
        grid=(n_tok // tk,),
        in_specs=[pl.BlockSpec((TOP_K, tk), lambda i: (0, i))],
        out_specs=[pl.BlockSpec((TOP_K, tk), lambda i: (0, i)),
                   pl.BlockSpec((n_exp, LANES), lambda i: (0, 0))],
        out_shape=[jax.ShapeDtypeStruct((TOP_K, n_tok), I32), jax.ShapeDtypeStruct((n_exp, LANES), I32)],
        scratch_shapes=[pltpu.VMEM((n_exp, LANES), F32)],
        compiler_params=_params(1),
        name="moe_rank",
    )(e_t)


def _slot_kernel(e_ref, rank_ref, pstart_ref, slot_o):
    e = e_ref[...]
    n_exp = pstart_ref.shape[0]
    tk = e.shape[1]
    ie = lax.broadcasted_iota(I32, (n_exp, tk), 0)
    ps = pstart_ref[:, 0:1]
    rows = [jnp.sum(jnp.where(ie == e[j:j + 1, :], ps, 0.0), axis=0, keepdims=True) for j in range(TOP_K)]
    slot_o[...] = rank_ref[...] + jnp.concatenate(rows, axis=0).astype(I32)


def _slots(e_t, rank_t, pstart_f):
    n_tok = e_t.shape[1]
    tk = TOKEN_TILE
    spec = pl.BlockSpec((TOP_K, tk), lambda i: (0, i))
    return pl.pallas_call(
        _slot_kernel,
        grid=(n_tok // tk,),
        in_specs=[spec, spec, _const_spec(pstart_f.shape)],
        out_specs=spec,
        out_shape=jax.ShapeDtypeStruct((TOP_K, n_tok), I32),
        compiler_params=_params(1),
        name="moe_slots",
    )(e_t, rank_t, pstart_f)


def _dispatch_kernel(pend_ref, padded_ref, slot_ref, h2_ref, xs_hbm, zbuf, sem, zsem, *, n_exp):
    i = pl.program_id(0)
    td = slot_ref.shape[1]
    blk_rows = MOE_BLOCK * SUBLANES

    def zero_copy(e):
        dst = pl.ds(pl.multiple_of((pend_ref[e] - MOE_BLOCK) * SUBLANES, SUBLANES), blk_rows)
        return pltpu.make_async_copy(zbuf, xs_hbm.at[dst], zsem)

    @pl.when(i == 0)
    def _():
        zbuf[...] = jnp.zeros_like(zbuf)

        def start(e, carry):
            @pl.when(padded_ref[e] > 0)
            def _():
                zero_copy(e).start()
            return carry

        def wait(e, carry):
            @pl.when(padded_ref[e] > 0)
            def _():
                zero_copy(e).wait()
            return carry

        lax.fori_loop(0, n_exp, start, 0)
        lax.fori_loop(0, n_exp, wait, 0)

    def row_copy(t, j):
        src = pl.ds(pl.multiple_of(t * SUBLANES, SUBLANES), SUBLANES)
        dst = pl.ds(pl.multiple_of(slot_ref[j, t] * SUBLANES, SUBLANES), SUBLANES)
        return pltpu.make_async_copy(h2_ref.at[src], xs_hbm.at[dst], sem)

    def start_tok(t, carry):
        for j in range(TOP_K):
            row_copy(t, j).start(priority=j % 2)
        return carry

    def wait_tok(t, carry):
        for j in range(TOP_K):
            row_copy(t, j).wait()
        return carry

    lax.fori_loop(0, td, start_tok, 0)
    lax.fori_loop(0, td, wait_tok, 0)


def _dispatch(pend, padded, slot_t, h2, m_pad):
    n_tok = slot_t.shape[1]
    td = TOKEN_TILE
    n_exp = pend.shape[0]
    grid_spec = pltpu.PrefetchScalarGridSpec(
        num_scalar_prefetch=2,
        grid=(n_tok // td,),
        in_specs=[pl.BlockSpec((TOP_K, td), lambda i, *_: (0, i), memory_space=pltpu.SMEM),
                  pl.BlockSpec((td * SUBLANES, LANES), lambda i, *_: (i, 0))],
        out_specs=pl.BlockSpec(memory_space=pl.ANY),
        scratch_shapes=[pltpu.VMEM((MOE_BLOCK * SUBLANES, LANES), F32),
                        pltpu.SemaphoreType.DMA(()), pltpu.SemaphoreType.DMA(())],
    )
    return pl.pallas_call(
        functools.partial(_dispatch_kernel, n_exp=n_exp),
        grid_spec=grid_spec,
        out_shape=jax.ShapeDtypeStruct((m_pad * SUBLANES, LANES), F32),
        compiler_params=_params(1),
        name="moe_dispatch",
    )(pend, padded, slot_t, h2)


def _tile_rows(ref, n):
    return jnp.concatenate([ref[pl.ds(k, n, stride=SUBLANES), :] for k in range(D_MODEL // LANES)], axis=1)


PLAN_EXPERT, PLAN_SLOT, PLAN_NEXT, PLAN_HAS_NEXT = range(4)


def _expert_kernel(plan_ref, nused_ref, x_ref, wg_hbm, wu_hbm, wd_hbm, y_ref, wg_f, wu_f, wd_f, wgu_b, wd_b, sem):
    i = pl.program_id(0)
    d_exp = wd_b.shape[0]

    def weight_copies(e, s):
        return (pltpu.make_async_copy(wg_hbm.at[e], wg_f.at[s], sem.at[s]),
                pltpu.make_async_copy(wu_hbm.at[e], wu_f.at[s], sem.at[s]),
                pltpu.make_async_copy(wd_hbm.at[e], wd_f.at[s], sem.at[s]))

    @pl.when(i < nused_ref[0])
    def _():
        e = plan_ref[PLAN_EXPERT, i]
        changed = (i == 0) | (e != plan_ref[PLAN_EXPERT, jnp.maximum(i - 1, 0)])

        @pl.when(changed)
        def _():
            s = plan_ref[PLAN_SLOT, i]

            @pl.when(i == 0)
            def _():
                for cp in weight_copies(e, s):
                    cp.start()

            for cp in weight_copies(e, s):
                cp.wait()

            @pl.when(plan_ref[PLAN_HAS_NEXT, i] > 0)
            def _():
                for cp in weight_copies(plan_ref[PLAN_NEXT, i], 1 - s):
                    cp.start()

            wgu_b[:, :d_exp] = wg_f[s].astype(BF16)
            wgu_b[:, d_exp:] = wu_f[s].astype(BF16)
            wd_b[...] = wd_f[s].astype(BF16)

        x = _tile_rows(x_ref, MOE_BLOCK).astype(BF16)
        gu = _dot(x, wgu_b[...])
        hh = jax.nn.silu(gu[:, :d_exp]) * gu[:, d_exp:]
        y = _dot(hh.astype(BF16), wd_b[...])
        for k in range(D_MODEL // LANES):
            y_ref[pl.ds(k, MOE_BLOCK, stride=SUBLANES), :] = y[:, k * LANES:(k + 1) * LANES]


def _expert_plan(block_e, nused):
    n_blocks = block_e.shape[0]
    idx = jnp.arange(n_blocks, dtype=I32)
    changed = (idx == 0) | (block_e != jnp.roll(block_e, 1))
    slot = (jnp.cumsum(changed.astype(I32)) - 1) % 2
    change_pos = jnp.where(changed & (idx < nused), idx, n_blocks)
    next_pos = jnp.concatenate([lax.cummin(change_pos[::-1])[::-1][1:], jnp.full((1,), n_blocks, I32)])
    has_next = (next_pos < n_blocks).astype(I32)
    nxt = block_e[jnp.minimum(next_pos, n_blocks - 1)]
    return jnp.stack([block_e, slot, nxt, has_next]).astype(I32)


def _experts(block_e, nused, xs, w_gate, w_up, w_down):
    n_blocks = block_e.shape[0]
    d_exp = w_gate.shape[2]
    rows = MOE_BLOCK * SUBLANES
    last = lambda i, nu: jnp.minimum(i, nu[0] - 1)
    any_spec = pl.BlockSpec(memory_space=pl.ANY)
    grid_spec = pltpu.PrefetchScalarGridSpec(
        num_scalar_prefetch=2,
        grid=(n_blocks,),
        in_specs=[pl.BlockSpec((rows, LANES), lambda i, plan, nu: (last(i, nu), 0)), any_spec, any_spec, any_spec],
        out_specs=pl.BlockSpec((rows, LANES), lambda i, plan, nu: (last(i, nu), 0)),
        scratch_shapes=[pltpu.VMEM((2, D_MODEL, d_exp), F32), pltpu.VMEM((2, D_MODEL, d_exp), F32),
                        pltpu.VMEM((2, d_exp, D_MODEL), F32),
                        pltpu.VMEM((D_MODEL, 2 * d_exp), BF16), pltpu.VMEM((d_exp, D_MODEL), BF16),
                        pltpu.SemaphoreType.DMA((2,))],
    )
    return pl.pallas_call(
        _expert_kernel,
        grid_spec=grid_spec,
        out_shape=jax.ShapeDtypeStruct(xs.shape, F32),
        compiler_params=_params(1),
        name="moe_experts",
    )(_expert_plan(block_e, nused[0]), nused, xs, w_gate, w_up, w_down)


def _combine_kernel(slot_ref, slot_next_ref, gate_ref, h2_ref, ys_hbm, shg, shu, shd, g2, b2, yp_o, ys_o,
                    buf, base_ref, y_ref, sem, *, n_prompt_tiles):
    i = pl.program_id(0)
    n = pl.num_programs(0)
    tc = gate_ref.shape[0]

    def start_tile(s_ref, bslot):
        def body(t, carry):
            dst = pl.ds(pl.multiple_of(t * SUBLANES, SUBLANES), SUBLANES)
            for j in range(TOP_K):
                src = pl.ds(pl.multiple_of(s_ref[j, t] * SUBLANES, SUBLANES), SUBLANES)
                pltpu.make_async_copy(ys_hbm.at[src], buf.at[bslot, j, dst], sem.at[bslot]).start(priority=j % 2)
            return carry
        lax.fori_loop(0, tc, body, 0)

    def wait_tile(bslot):
        for j in range(TOP_K):
            pltpu.make_async_copy(ys_hbm.at[pl.ds(0, tc * SUBLANES)], buf.at[bslot, j], sem.at[bslot]).wait()

    @pl.when(i == 0)
    def _():
        start_tile(slot_ref, 0)

    cur = i % 2
    h = _tile_rows(h2_ref, tc)
    hb = h.astype(BF16)
    shared = _dot((jax.nn.silu(_dot(hb, shg[...])) * _dot(hb, shu[...])).astype(BF16), shd[...])
    base_ref[...] = ALPHA * h + shared
    wait_tile(cur)
    gain, bias = g2[...], b2[...]
    grp = SUBLANES

    def finish_group(g):
        r0 = pl.multiple_of(g * grp, grp)
        t0 = pl.multiple_of(g * grp * SUBLANES, grp * SUBLANES)
        routed = jnp.zeros((grp, D_MODEL), F32)
        for j in range(TOP_K):
            rows = jnp.concatenate([buf[cur, j, pl.ds(t0 + k, grp, stride=SUBLANES), :]
                                    for k in range(D_MODEL // LANES)], axis=1)
            routed = routed + rows * gate_ref[pl.ds(r0, grp), j:j + 1]
        y_ref[pl.ds(r0, grp), :] = base_ref[pl.ds(r0, grp), :] + routed

    @pl.when(i + 1 < n)
    def _():
        nxt = (i + 1) % 2

        def body(g, carry):
            for tt in range(grp):
                t = g * grp + tt
                dst = pl.ds(pl.multiple_of(t * SUBLANES, SUBLANES), SUBLANES)
                for j in range(TOP_K):
                    src = pl.ds(pl.multiple_of(slot_next_ref[j, t] * SUBLANES, SUBLANES), SUBLANES)
                    pltpu.make_async_copy(ys_hbm.at[src], buf.at[nxt, j, dst], sem.at[nxt]).start(priority=j % 2)
            finish_group(g)
            return carry

        lax.fori_loop(0, tc // grp, body, 0)

    @pl.when(i + 1 >= n)
    def _():
        def body(g, carry):
            finish_group(g)
            return carry

        lax.fori_loop(0, tc // grp, body, 0)

    y = _layer_norm(y_ref[...], gain, bias)

    @pl.when(i < n_prompt_tiles)
    def _():
        yp_o[...] = y

    @pl.when(i >= n_prompt_tiles)
    def _():
        ys_o[...] = y


def _combine(slot_t, gates, h2, ys, shg, shu, shd, g2, b2, n_prompt, n_decode):
    tc = TOKEN_TILE
    npt, nst = n_prompt // tc, n_decode // tc
    consts = [shg, shu, shd, g2, b2]
    return pl.pallas_call(
        functools.partial(_combine_kernel, n_prompt_tiles=npt),
        grid=(npt + nst,),
        in_specs=[pl.BlockSpec((TOP_K, tc), lambda i: (0, i), memory_space=pltpu.SMEM),
                  pl.BlockSpec((TOP_K, tc), lambda i: (0, jnp.minimum(i + 1, npt + nst - 1)),
                               memory_space=pltpu.SMEM),
                  pl.BlockSpec((tc, TOP_K), lambda i: (i, 0)),
                  pl.BlockSpec((tc * SUBLANES, LANES), lambda i: (i, 0)),
                  pl.BlockSpec(memory_space=pl.ANY)] + [_const_spec(c.shape) for c in consts],
        out_specs=[pl.BlockSpec((tc, D_MODEL), lambda i: (jnp.minimum(i, npt - 1), 0)),
                   pl.BlockSpec((tc, D_MODEL), lambda i: (jnp.maximum(i - npt, 0), 0))],
        out_shape=[jax.ShapeDtypeStruct((n_prompt, D_MODEL), F32),
                   jax.ShapeDtypeStruct((n_decode, D_MODEL), F32)],
        scratch_shapes=[pltpu.VMEM((2, TOP_K, tc * SUBLANES, LANES), F32), pltpu.VMEM((tc, D_MODEL), F32),
                        pltpu.VMEM((tc, D_MODEL), F32), pltpu.SemaphoreType.DMA((2,))],
        compiler_params=_params(1),
        name="moe_combine",
    )(slot_t, slot_t, gates, h2, ys, *consts)


def _pad_rows(a, n):
    return jnp.pad(a, ((0, n - a.shape[0]),) + ((0, 0),) * (a.ndim - 1))


def kernel(x_prompt, x_sample, cache_k, cache_v, cache_kidx, page_table, state_conv, w_in, conv_w, w_o_attn,
           w_o_conv, w_o, ln1_g, ln1_b, router_w, router_bias, moe_w_gate, moe_w_up, moe_w_down,
           shared_w_gate, shared_w_up, shared_w_down, ln2_g, ln2_b):
    nb, seq, _ = x_prompt.shape
    db = x_sample.shape[0]
    n_pool = cache_k.shape[1]
    n_exp = router_w.shape[-1]
    tp = nb * seq
    ts = TOKEN_TILE
    nkv = N_KV_HEADS * HEAD_DIM

    weights = _prep_in_weights(w_in[0])
    cw = conv_w[0]

    xp2 = x_prompt.reshape(tp, D_MODEL)
    (q_p, k_p, v_p, kvb_p, qi_p, idx_p, kib_p, conv_p, ga_p, gc_p, ulast_p) = _in_proj_prompt(xp2, weights, cw, nb)
    xs2 = _pad_rows(x_sample.reshape(db, D_MODEL), ts)
    prev2 = _pad_rows(state_conv[0, :, 0, :], ts)
    prev1 = _pad_rows(state_conv[0, :, 1, :], ts)
    (q_s, k_s, v_s, _, qi_s, idx_s, _, conv_s, ga_s, gc_s, u_s) = _in_proj_decode(xs2, prev2, prev1, weights, cw)

    attn_p = _attn_prompt(q_p, kvb_p, qi_p, kib_p, idx_p, nb)
    q8 = q_s[:db].reshape(db, N_HEADS, HEAD_DIM)
    qi8 = qi_s[:db].reshape(db, IDX_HEADS, LANES)[:, :, :IDX_DIM]
    wi8 = idx_s[:db, IDX_DIM:IDX_DIM + IDX_HEADS].reshape(db, IDX_HEADS, 1)
    kin = idx_s[:db, :IDX_DIM].astype(BF16).reshape(db, 1, IDX_DIM)
    k_new = k_s[:db * N_KV_HEADS].reshape(db, N_KV_HEADS, HEAD_DIM)
    v_new = v_s[:db * N_KV_HEADS].reshape(db, N_KV_HEADS, HEAD_DIM)
    kn8 = jnp.repeat(k_new, GROUP, axis=1).astype(BF16)
    vn8 = jnp.repeat(v_new, GROUP, axis=1).astype(BF16)
    ck2 = cache_k[0].reshape(n_pool * PAGE_SIZE * N_KV_HEADS, HEAD_DIM)
    cv2 = cache_v[0].reshape(n_pool * PAGE_SIZE * N_KV_HEADS, HEAD_DIM)
    keys_s, knew_s = _decode_scores(page_table, qi8, wi8, kin, jnp.swapaxes(cache_kidx[0], 1, 2))
    meta = _decode_pick(keys_s, knew_s, page_table.astype(F32)[:, :, None])
    attn_s8 = _decode_attend(meta[:, 0, :], q8, meta, kn8, vn8, ck2, cv2)
    attn_s = _pad_rows(attn_s8.reshape(db, N_HEADS * HEAD_DIM), ts)

    h2, e_t, gate_t = _post_attn(
        xp2, xs2, attn_p, attn_s, conv_p, conv_s, ga_p, ga_s, gc_p, gc_s,
        w_o_attn[0].astype(BF16), w_o_conv[0].astype(BF16), w_o[0].astype(BF16),
        ln1_g[0].reshape(1, D_MODEL), ln1_b[0].reshape(1, D_MODEL),
        router_w[0].T, router_bias[0].reshape(n_exp, 1))

    n_tok = tp + ts
    rank_t, cnt = _rank(e_t, n_exp)
    counts = cnt[:, 0]
    padded = (counts + MOE_BLOCK - 1) // MOE_BLOCK * MOE_BLOCK
    pend = jnp.cumsum(padded)
    pstart = pend - padded
    n_blocks = (n_tok * TOP_K + n_exp * (MOE_BLOCK - 1) + MOE_BLOCK - 1) // MOE_BLOCK
    nused = (pend[-1] // MOE_BLOCK).astype(I32)
    blk = jnp.minimum(jnp.arange(n_blocks, dtype=I32), nused - 1)
    block_e = jnp.minimum(jnp.sum((pend[None, :] <= (blk * MOE_BLOCK)[:, None]).astype(I32), axis=1), n_exp - 1)
    slot_t = _slots(e_t, rank_t, jnp.broadcast_to(pstart.astype(F32)[:, None], (n_exp, LANES)))
    xs = _dispatch(pend.astype(I32), padded.astype(I32), slot_t, h2, n_blocks * MOE_BLOCK)
    ys = _experts(block_e, nused.reshape(1), xs, moe_w_gate[0], moe_w_up[0], moe_w_down[0])
    y_p, y_s = _combine(slot_t, gate_t.T, h2, ys,
                        shared_w_gate[0].astype(BF16), shared_w_up[0].astype(BF16),
                        shared_w_down[0].astype(BF16),
                        ln2_g[0].reshape(1, D_MODEL), ln2_b[0].reshape(1, D_MODEL), tp, ts)

    conv_sample = jnp.stack([state_conv[0, :, 1, :], u_s[:db]], axis=1)[None]
    return (y_p.reshape(nb, seq, D_MODEL),
            y_s[:db].reshape(db, 1, D_MODEL),
            k_p.reshape(1, nb, seq, N_KV_HEADS, HEAD_DIM),
            v_p.reshape(1, nb, seq, N_KV_HEADS, HEAD_DIM),
            idx_p[:, :IDX_DIM].reshape(1, nb, seq, IDX_DIM),
            ulast_p[:, SUBLANES - (CONV_W - 1):, :][None],
            k_new.reshape(1, db, 1, N_KV_HEADS, HEAD_DIM),
            v_new.reshape(1, db, 1, N_KV_HEADS, HEAD_DIM),
            idx_s[:db, :IDX_DIM].reshape(1, db, 1, IDX_DIM),
            conv_sample)
```

```python
import functools

import jax
import jax.numpy as jnp
import numpy as np
from jax import lax
from jax.experimental import pallas as pl
from jax.experimental.pallas import tpu as pltpu

F32 = jnp.float32
BF16 = jnp.bfloat16
I32 = jnp.int32

D_MODEL = 1024
N_HEADS = 8
HEAD_DIM = 128
N_KV_HEADS = 2
GROUP = N_HEADS // N_KV_HEADS
IDX_HEADS = 8
IDX_DIM = 64
TOPK_KEYS = 256
IDX_SCALE = IDX_DIM ** -0.5 * IDX_HEADS ** -0.5
Q_SCALE = HEAD_DIM ** -0.5
PAGE_SIZE = 128
D_CONV = D_MODEL
CONV_W = 3
TOP_K = 8
N_GROUPS = 8
TOPK_GROUPS = 4
ROUTED_SCALE = 2.5
DEPTH = 1
ALPHA = (2 * DEPTH) ** 0.25
LN_EPS = 1e-5
NEG = -1e30
IN_WIDTHS = (N_HEADS * HEAD_DIM, N_KV_HEADS * HEAD_DIM, N_KV_HEADS * HEAD_DIM,
             IDX_HEADS * IDX_DIM, IDX_DIM, IDX_HEADS,
             D_CONV, D_CONV, D_CONV, D_MODEL, D_MODEL)

LANES = 128
SUBLANES = 8
VMEM_LIMIT_BYTES = 56 * 1024 * 1024

PROJ_ROWS = 512
ATT_Q = 256
TOKEN_TILE = 256
MOE_BLOCK = 256
CONV_CHUNK = 256
INT_MIN = -2 ** 31
SOFTMAX_DENOM_MIN = 1e-25
F32_MIN_NORMAL = float(np.finfo(np.float32).tiny)
BF16_MIN_NORMAL_BITS = 0x0080


def _np_key(v):
    b = int(np.float32(v).view(np.int32))
    return b ^ ((b >> 31) & 0x7FFFFFFF)


KEY_HALF = _np_key(NEG * 0.5)


def _sort_key(s):
    b = lax.bitcast_convert_type(s + 0.0, I32)
    return b ^ ((b >> 31) & jnp.int32(0x7FFFFFFF))


def _dot(a, b):
    return jnp.dot(a, b, preferred_element_type=F32)


def _dot_nt(a, b):
    return lax.dot_general(a, b, (((1,), (1,)), ((), ())), preferred_element_type=F32)


def _params(n_grid):
    return pltpu.CompilerParams(dimension_semantics=("arbitrary",) * n_grid,
                                vmem_limit_bytes=VMEM_LIMIT_BYTES)


def _const_spec(shape):
    nd = len(shape)
    return pl.BlockSpec(shape, lambda *_: (0,) * nd, pipeline_mode=pl.Buffered(1))


def _proj_common(xb, w1, w2, w3, wga, wgc, q_o, k_o, v_o, kv_o, qi_o, idx_o, kib_o, sga_o, sgc_o):
    nq = N_HEADS * HEAD_DIM
    nkv = N_KV_HEADS * HEAD_DIM
    z1 = _dot(xb, w1[...])
    q_o[...] = (z1[:, :nq] * Q_SCALE).astype(BF16)
    k = z1[:, nq:nq + nkv]
    v = z1[:, nq + nkv:nq + 2 * nkv]
    for g in range(N_KV_HEADS):
        k_o[pl.ds(g, k.shape[0], stride=N_KV_HEADS), :] = k[:, g * HEAD_DIM:(g + 1) * HEAD_DIM]
        v_o[pl.ds(g, v.shape[0], stride=N_KV_HEADS), :] = v[:, g * HEAD_DIM:(g + 1) * HEAD_DIM]
    kv_o[:, :nkv] = k.astype(BF16)
    kv_o[:, nkv:] = v.astype(BF16)
    qi_o[...] = _dot(xb, w2[...]).astype(BF16)
    z3 = _dot(xb, w3[...])
    idx_o[...] = z3
    kib_o[...] = z3.astype(BF16)
    sga_o[...] = jax.nn.sigmoid(_dot(xb, wga[...])).astype(BF16)
    sgc_o[...] = jax.nn.sigmoid(_dot(xb, wgc[...])).astype(BF16)


def _in_proj_prompt_kernel(x_ref, w1, w2, w3, wb, wc, wx, wga, wgc, cw_ref,
                           q_o, k_o, v_o, kv_o, qi_o, idx_o, kib_o, conv_o, sga_o, sgc_o, ulast_o,
                           tail_ref):
    j = pl.program_id(1)

    @pl.when(j == 0)
    def _():
        tail_ref[...] = jnp.zeros_like(tail_ref)

    xb = x_ref[...].astype(BF16)
    _proj_common(xb, w1, w2, w3, wga, wgc, q_o, k_o, v_o, kv_o, qi_o, idx_o, kib_o, sga_o, sgc_o)
    tm = xb.shape[0]
    row = lax.broadcasted_iota(I32, (tm, CONV_CHUNK), 0)
    for c in range(D_CONV // CONV_CHUNK):
        cs = slice(c * CONV_CHUNK, (c + 1) * CONV_CHUNK)
        gb = _dot(xb, wb[:, cs])
        u = _dot(xb, wc[:, cs]) * _dot(xb, wx[:, cs])
        p1 = tail_ref[SUBLANES - 1:SUBLANES, cs]
        p2 = tail_ref[SUBLANES - 2:SUBLANES - 1, cs]
        u1 = jnp.where(row == 0, p1, pltpu.roll(u, 1, 0))
        u2 = jnp.where(row == 0, p2, jnp.where(row == 1, p1, pltpu.roll(u, 2, 0)))
        conv = cw_ref[0:1, cs] * u2 + cw_ref[1:2, cs] * u1 + cw_ref[2:3, cs] * u
        conv_o[:, cs] = (gb * conv).astype(BF16)
        tail_ref[:, cs] = u[tm - SUBLANES:, :]
    ulast_o[0] = tail_ref[...]


def _in_proj_decode_kernel(x_ref, p2_ref, p1_ref, w1, w2, w3, wb, wc, wx, wga, wgc, cw_ref,
                           q_o, k_o, v_o, kv_o, qi_o, idx_o, kib_o, conv_o, sga_o, sgc_o, u_o):
    xb = x_ref[...].astype(BF16)
    _proj_common(xb, w1, w2, w3, wga, wgc, q_o, k_o, v_o, kv_o, qi_o, idx_o, kib_o, sga_o, sgc_o)
    for c in range(D_CONV // CONV_CHUNK):
        cs = slice(c * CONV_CHUNK, (c + 1) * CONV_CHUNK)
        gb = _dot(xb, wb[:, cs])
        u = _dot(xb, wc[:, cs]) * _dot(xb, wx[:, cs])
        conv = cw_ref[0:1, cs] * p2_ref[:, cs] + cw_ref[1:2, cs] * p1_ref[:, cs] + cw_ref[2:3, cs] * u
        conv_o[:, cs] = (gb * conv).astype(BF16)
        u_o[:, cs] = u


def _prep_in_weights(w_in):
    w = w_in.astype(BF16)
    offs = np.concatenate([[0], np.cumsum(IN_WIDTHS)])
    part = [w[:, int(offs[i]):int(offs[i + 1])] for i in range(len(IN_WIDTHS))]
    q, k, v, qi, ki, wi, gb, gc, xv, ga, gcv = part
    d = w.shape[0]
    w1 = jnp.concatenate([q, k, v], axis=1)
    w2 = jnp.pad(qi.reshape(d, IDX_HEADS, IDX_DIM),
                 ((0, 0), (0, 0), (0, LANES - IDX_DIM))).reshape(d, IDX_HEADS * LANES)
    w3 = jnp.concatenate([ki, wi, jnp.zeros((d, LANES - IDX_DIM - IDX_HEADS), BF16)], axis=1)
    return (w1, w2, w3, gb, gc, xv, ga, gcv)


def _proj_out_shapes(t):
    nkv = N_KV_HEADS * HEAD_DIM
    sd = jax.ShapeDtypeStruct
    return [sd((t, N_HEADS * HEAD_DIM), BF16), sd((t * N_KV_HEADS, HEAD_DIM), F32),
            sd((t * N_KV_HEADS, HEAD_DIM), F32), sd((t, 2 * nkv), BF16),
            sd((t, IDX_HEADS * LANES), BF16), sd((t, LANES), F32), sd((t, LANES), BF16),
            sd((t, D_CONV), BF16), sd((t, D_MODEL), BF16), sd((t, D_MODEL), BF16)]


def _in_proj_prompt(x2d, weights, conv_w, nb):
    t = x2d.shape[0]
    s = t // nb
    tm = min(PROJ_ROWS, s)
    nj = s // tm
    row_spec = lambda w, rep=1: pl.BlockSpec((tm * rep, w), lambda b, j: (b * nj + j, 0))
    out_shapes = _proj_out_shapes(t) + [jax.ShapeDtypeStruct((nb, SUBLANES, D_CONV), F32)]
    out_specs = [row_spec(o.shape[1], o.shape[0] // t) for o in out_shapes[:-1]]
    out_specs.append(pl.BlockSpec((1, SUBLANES, D_CONV), lambda b, j: (b, 0, 0)))
    return pl.pallas_call(
        _in_proj_prompt_kernel,
        grid=(nb, nj),
        in_specs=[row_spec(D_MODEL)] + [_const_spec(w.shape) for w in weights] + [_const_spec(conv_w.shape)],
        out_specs=out_specs,
        out_shape=out_shapes,
        scratch_shapes=[pltpu.VMEM((SUBLANES, D_CONV), F32)],
        compiler_params=_params(2),
        name="in_proj_prompt",
    )(x2d, *weights, conv_w)


def _in_proj_decode(x2d, p2, p1, weights, conv_w):
    t = x2d.shape[0]
    row_spec = lambda w, rep=1: pl.BlockSpec((t * rep, w), lambda i: (0, 0))
    out_shapes = _proj_out_shapes(t) + [jax.ShapeDtypeStruct((t, D_CONV), F32)]
    return pl.pallas_call(
        _in_proj_decode_kernel,
        grid=(1,),
        in_specs=[row_spec(D_MODEL)] * 3 + [_const_spec(w.shape) for w in weights] + [_const_spec(conv_w.shape)],
        out_specs=[row_spec(o.shape[1], o.shape[0] // t) for o in out_shapes],
        out_shape=out_shapes,
        compiler_params=_params(1),
        name="in_proj_decode",
    )(x2d, p2, p1, *weights, conv_w)


def _attn_prompt_kernel(q_ref, kv_ref, qi_ref, kib_ref, idx_ref, o_ref,
                        key_ref, hi_ref, bias_ref, t_ref, j_ref, need_ref, m_ref, acc_ref, knorm_ref,
                        *, n_keep, seq_len):
    tq = q_ref.shape[0]
    i = pl.program_id(1)
    nchunk = i + 1
    nsub = tq // LANES

    idx_t = jnp.transpose(idx_ref[...])
    wi_rows = [idx_t[IDX_DIM + h:IDX_DIM + h + 1, :] * IDX_SCALE for h in range(IDX_HEADS)]
    key_i = lax.broadcasted_iota(I32, (tq, tq), 0)
    qry_i = lax.broadcasted_iota(I32, (tq, tq), 1)

    def score_chunk(c, carry):
        kic = kib_ref[pl.ds(pl.multiple_of(c * tq, tq), tq), :]
        acc = jnp.zeros((tq, tq), F32)
        for h in range(IDX_HEADS):
            d = _dot_nt(kic, qi_ref[:, h * LANES:(h + 1) * LANES])
            acc = acc + jnp.maximum(d, 0.0) * wi_rows[h]
        acc = jnp.where(jnp.abs(acc) < F32_MIN_NORMAL, 0.0, acc)
        s = jnp.where((c < i) | (key_i <= qry_i), acc, NEG)
        bits = lax.bitcast_convert_type(s, I32)
        key_ref[c] = bits ^ ((bits >> 31) & jnp.int32(0x7FFFFFFF))
        hi_ref[c] = lax.bitcast_convert_type(bits & jnp.int32(-65536), F32).astype(BF16)
        return carry

    lax.fori_loop(0, nchunk, score_chunk, 0)

    sub_i = lax.broadcasted_iota(I32, (SUBLANES, tq), 0)

    def counts(preds, with_pos=False):
        def body(c, cnts):
            for kk in range(tq // SUBLANES):
                blk = key_ref[c, kk * SUBLANES:(kk + 1) * SUBLANES, :]
                pos = (c * tq + kk * SUBLANES + sub_i) if with_pos else None
                cnts = tuple(cnt + pred(blk, pos).astype(I32) for cnt, pred in zip(cnts, preds))
            return cnts
        cnts = lax.fori_loop(0, nchunk, body, tuple(jnp.zeros((SUBLANES, tq), I32) for _ in preds))
        return tuple(jnp.sum(cnt, axis=0, keepdims=True) for cnt in cnts)

    def count_hi(cand):
        rows16 = 2 * SUBLANES
        def body(c, cnt):
            for kk in range(tq // rows16):
                blk = hi_ref[c, kk * rows16:(kk + 1) * rows16, :]
                cnt = cnt + jnp.where(blk >= cand, jnp.ones_like(cnt), jnp.zeros_like(cnt))
            return cnt
        cnt = lax.fori_loop(0, nchunk, body, jnp.zeros((rows16, tq), BF16))
        return jnp.sum(cnt.astype(F32), axis=0, keepdims=True)

    def hi_bit_body(bi, p16):
        cand = p16 + lax.shift_left(jnp.int32(1), 15 - bi)
        raw = cand ^ ((cand >> 31) & jnp.int32(0x7FFF))
        raw = jnp.where((raw > 0) & (raw < BF16_MIN_NORMAL_BITS), BF16_MIN_NORMAL_BITS, raw)
        cand_f = lax.bitcast_convert_type(lax.shift_left(raw, 16), F32)
        total = count_hi(jnp.broadcast_to(cand_f, (2 * SUBLANES, tq)).astype(BF16))
        return jnp.where(total >= n_keep, cand, p16)

    p16 = lax.fori_loop(0, 16, hi_bit_body, jnp.full((1, tq), -2 ** 15, I32))

    def bit_body(bi, t):
        cand = t + lax.shift_left(jnp.int32(1), 15 - bi)
        cand_b = jnp.broadcast_to(cand, (SUBLANES, tq))
        total, = counts((lambda blk, _: blk >= cand_b,))
        return jnp.where(total >= n_keep, cand, t)

    t = lax.fori_loop(0, 16, bit_body, lax.shift_left(p16, 16))
    t_b = jnp.broadcast_to(t, (SUBLANES, tq))
    cnt_gt, cnt_eq = counts((lambda blk, _: blk > t_b, lambda blk, _: blk == t_b))
    need = n_keep - cnt_gt
    any_excess = jnp.max(((cnt_eq > need) & (t > KEY_HALF)).astype(I32))
    t_ref[...] = t_b
    need_ref[...] = jnp.broadcast_to(need, (SUBLANES, tq))
    j_ref[...] = jnp.full(j_ref.shape, seq_len, I32)

    @pl.when(any_excess > 0)
    def _():
        nbits = max(1, (seq_len - 1).bit_length())
        t_b = t_ref[...]
        need = need_ref[0:1, :]

        def jbit(bi, jj):
            cand = jj | lax.shift_left(jnp.int32(1), nbits - 1 - bi)
            cand_b = jnp.broadcast_to(cand, (SUBLANES, tq))
            g, = counts((lambda blk, pos: (blk == t_b) & (pos < cand_b),), with_pos=True)
            return jnp.where(g < need, cand, jj)

        jj = lax.fori_loop(0, nbits, jbit, jnp.zeros((1, tq), I32))
        j_ref[...] = jnp.broadcast_to(jj, (SUBLANES, tq))

    t_row = t_ref[0:1, :]
    j_row = j_ref[0:1, :]

    def bias_chunk(c, carry):
        blk = key_ref[c]
        sel = (blk > KEY_HALF) & ((blk > t_row) | ((blk == t_row) & ((c * tq + key_i) <= j_row)))
        bias_ref[c] = jnp.transpose(jnp.where(sel, 0.0, NEG))
        return carry

    lax.fori_loop(0, nchunk, bias_chunk, 0)

    nkv = N_KV_HEADS * HEAD_DIM
    gq = GROUP * tq

    def group_logits(c, g):
        rows = pl.ds(pl.multiple_of(c * tq, tq), tq)
        kc = kv_ref[rows, g * HEAD_DIM:(g + 1) * HEAD_DIM]
        qg = jnp.concatenate([q_ref[:, h * HEAD_DIM:(h + 1) * HEAD_DIM]
                              for h in range(g * GROUP, (g + 1) * GROUP)], axis=0)
        s = _dot_nt(qg, kc).reshape(GROUP, tq, tq) + bias_ref[c][None]
        return s.reshape(gq, tq)

    def exact_row_max():
        m_ref[...] = jnp.full(m_ref.shape, NEG, F32)

        def max_chunk(c, carry):
            for g in range(N_KV_HEADS):
                s = group_logits(c, g)
                mx = s[:, :LANES]
                for k in range(1, nsub):
                    mx = jnp.maximum(mx, s[:, k * LANES:(k + 1) * LANES])
                m_ref[g] = jnp.maximum(m_ref[g], mx)
            return carry

        lax.fori_loop(0, nchunk, max_chunk, 0)
        for g in range(N_KV_HEADS):
            m_ref[g] = jnp.broadcast_to(jnp.max(m_ref[g], axis=1, keepdims=True), (gq, LANES))

    ones = jnp.ones((tq, HEAD_DIM), BF16)

    def accumulate():
        acc_ref[...] = jnp.zeros_like(acc_ref)

        def pv_chunk(c, carry):
            rows = pl.ds(pl.multiple_of(c * tq, tq), tq)
            for g in range(N_KV_HEADS):
                s = group_logits(c, g)
                m = m_ref[g]
                p = jnp.concatenate([jnp.exp(s[:, k * LANES:(k + 1) * LANES] - m) for k in range(nsub)], axis=1)
                vc = kv_ref[rows, nkv + g * HEAD_DIM:nkv + (g + 1) * HEAD_DIM]
                acc_ref[g] = acc_ref[g] + _dot(p.astype(BF16), jnp.concatenate([vc, ones], axis=1))
            return carry

        lax.fori_loop(0, nchunk, pv_chunk, 0)

    @pl.when(i == 0)
    def _():
        for g in range(N_KV_HEADS):
            def norm_chunk(c, mx):
                kc = kv_ref[pl.ds(pl.multiple_of(c * tq, tq), tq), g * HEAD_DIM:(g + 1) * HEAD_DIM].astype(F32)
                return jnp.maximum(mx, jnp.sum(kc * kc, axis=1, keepdims=True))
            k2 = lax.fori_loop(0, seq_len // tq, norm_chunk, jnp.zeros((tq, 1), F32))
            knorm_ref[g] = jnp.broadcast_to(jnp.sqrt(jnp.max(k2, axis=0, keepdims=True)), (SUBLANES, LANES))

    for h in range(N_HEADS):
        g, r = h // GROUP, h % GROUP
        qh = q_ref[:, h * HEAD_DIM:(h + 1) * HEAD_DIM].astype(F32)
        qn = jnp.sqrt(jnp.sum(qh * qh, axis=1, keepdims=True))
        m_ref[g, r * tq:(r + 1) * tq, :] = qn * knorm_ref[g][0:1, :]
    accumulate()
    denom_min = jnp.min(acc_ref[:, :, HEAD_DIM:HEAD_DIM + 1])

    @pl.when(denom_min < SOFTMAX_DENOM_MIN)
    def _():
        exact_row_max()
        accumulate()

    for h in range(N_HEADS):
        a = acc_ref[h // GROUP, (h % GROUP) * tq:(h % GROUP + 1) * tq, :]
        o_ref[:, h * HEAD_DIM:(h + 1) * HEAD_DIM] = (a[:, :HEAD_DIM] / a[:, HEAD_DIM:]).astype(BF16)


def _attn_prompt(q, kvb, qi, kib, idx, nb):
    t = q.shape[0]
    s = t // nb
    tq = min(ATT_Q, s)
    nq = s // tq
    n_keep = min(TOPK_KEYS, s // 4)
    blk = lambda w: pl.BlockSpec((tq, w), lambda b, i: (b * nq + i, 0))
    seq = lambda w: pl.BlockSpec((s, w), lambda b, i: (b, 0))
    return pl.pallas_call(
        functools.partial(_attn_prompt_kernel, n_keep=n_keep, seq_len=s),
        grid=(nb, nq),
        in_specs=[blk(q.shape[1]), seq(kvb.shape[1]), blk(qi.shape[1]), seq(kib.shape[1]), blk(idx.shape[1])],
        out_specs=blk(N_HEADS * HEAD_DIM),
        out_shape=jax.ShapeDtypeStruct((t, N_HEADS * HEAD_DIM), BF16),
        scratch_shapes=[pltpu.VMEM((nq, tq, tq), I32), pltpu.VMEM((nq, tq, tq), BF16),
                        pltpu.VMEM((nq, tq, tq), F32),
                        pltpu.VMEM((SUBLANES, tq), I32), pltpu.VMEM((SUBLANES, tq), I32),
                        pltpu.VMEM((SUBLANES, tq), I32),
                        pltpu.VMEM((N_KV_HEADS, GROUP * tq, LANES), F32),
                        pltpu.VMEM((N_KV_HEADS, GROUP * tq, 2 * HEAD_DIM), F32),
                        pltpu.VMEM((N_KV_HEADS, SUBLANES, LANES), F32)],
        compiler_params=_params(2),
        name="attn_prompt",
    )(q, kvb, qi, kib, idx)


DECODE_PAGES_PER_ROW = 4


def _decode_scores_kernel(pt_ref, qi_ref, wi_ref, kin_ref, cx_hbm, key_o, knew_o, xbuf, sem, *, n_pages):
    b = pl.program_id(0)
    nb = pl.num_programs(0)
    ppr = DECODE_PAGES_PER_ROW
    nrow = n_pages // ppr

    def page_copy(bb, p, slot):
        return pltpu.make_async_copy(cx_hbm.at[pt_ref[bb, p]], xbuf.at[slot, p], sem.at[slot])

    def start_batch(bb, slot):
        def body(p, carry):
            page_copy(bb, p, slot).start()
            return carry
        lax.fori_loop(0, n_pages, body, 0)

    def wait_batch(bb, slot):
        def body(p, carry):
            page_copy(bb, p, slot).wait()
            return carry
        lax.fori_loop(0, n_pages, body, 0)

    @pl.when(b == 0)
    def _():
        start_batch(0, 0)

    @pl.when(b + 1 < nb)
    def _():
        start_batch(b + 1, (b + 1) % 2)

    slot = b % 2
    wait_batch(b, slot)

    qi8 = qi_ref[0]
    wi8 = wi_ref[0] * IDX_SCALE

    def score_row(r, carry):
        kx = jnp.concatenate([xbuf[slot, r * ppr + k] for k in range(ppr)], axis=1).astype(BF16)
        d = _dot(qi8, kx)
        sc = jnp.sum(jnp.maximum(d, 0.0) * wi8, axis=0, keepdims=True)
        key_o[0, pl.ds(r, 1), :] = _sort_key(sc)
        return carry

    lax.fori_loop(0, nrow, score_row, 0, unroll=4)
    dn = jnp.sum(qi8.astype(F32) * kin_ref[0].astype(F32), axis=1, keepdims=True)
    key_new = _sort_key(jnp.sum(jnp.maximum(dn, 0.0) * wi8, axis=0, keepdims=True))
    knew_o[0] = jnp.broadcast_to(key_new, (1, LANES))


def _decode_scores(page_table, qi8, wi8, kin, cxt):
    db, n_pages = page_table.shape
    nrow = n_pages // DECODE_PAGES_PER_ROW
    w = DECODE_PAGES_PER_ROW * PAGE_SIZE
    per_b = lambda a: pl.BlockSpec((1,) + a.shape[1:], lambda b, pt: (b, 0, 0))
    grid_spec = pltpu.PrefetchScalarGridSpec(
        num_scalar_prefetch=1,
        grid=(db,),
        in_specs=[per_b(qi8), per_b(wi8), per_b(kin), pl.BlockSpec(memory_space=pl.ANY)],
        out_specs=[pl.BlockSpec((1, nrow, w), lambda b, pt: (b, 0, 0)),
                   pl.BlockSpec((1, 1, LANES), lambda b, pt: (b, 0, 0))],
        scratch_shapes=[pltpu.VMEM((2, n_pages, IDX_DIM, PAGE_SIZE), F32), pltpu.SemaphoreType.DMA((2,))],
    )
    return pl.pallas_call(
        functools.partial(_decode_scores_kernel, n_pages=n_pages),
        grid_spec=grid_spec,
        out_shape=[jax.ShapeDtypeStruct((db, nrow, w), I32), jax.ShapeDtypeStruct((db, 1, LANES), I32)],
        compiler_params=_params(1),
        name="decode_scores",
    )(page_table, qi8, wi8, kin, cxt)


def _decode_pick_kernel(keys_ref, knew_ref, pt_ref, meta_o, *, n_keep):
    nb, nrow, w = keys_ref.shape
    past = nrow * w
    n_pages = pt_ref.shape[1]
    keys = keys_ref[...]
    key_new = knew_ref[...][:, :, 0:1]
    col = (lax.broadcasted_iota(I32, keys.shape, 1) * w + lax.broadcasted_iota(I32, keys.shape, 2))

    def total(x):
        c = x.astype(I32)
        part = c[:, :, :LANES]
        for k in range(1, w // LANES):
            part = part + c[:, :, k * LANES:(k + 1) * LANES]
        return jnp.sum(jnp.sum(part, axis=1, keepdims=True), axis=2, keepdims=True)

    def bit_body(bi, t):
        cand = t + lax.shift_left(jnp.int32(1), 31 - bi)
        cnt = total(keys >= cand) + (key_new >= cand).astype(I32)
        return jnp.where(cnt >= n_keep, cand, t)

    t = lax.fori_loop(0, 32, bit_body, jnp.full((nb, 1, 1), INT_MIN, I32))
    need = n_keep - (total(keys > t) + (key_new > t).astype(I32))
    nbits = past.bit_length()

    def jbit(bi, jj):
        cand = jj | lax.shift_left(jnp.int32(1), nbits - 1 - bi)
        g = total((keys == t) & (col < cand)) + ((key_new == t) & (past < cand)).astype(I32)
        return jnp.where(g < need, cand, jj)

    jj = lax.fori_loop(0, nbits, jbit, jnp.zeros((nb, 1, 1), I32))
    sel = (keys > t) | ((keys == t) & (col <= jj))
    sel_new = ((key_new > t) | ((key_new == t) & (past <= jj))).astype(I32)

    rows = nb * nrow
    incl_lane = (lax.broadcasted_iota(I32, (w, w), 0) <= lax.broadcasted_iota(I32, (w, w), 1)).astype(BF16)
    cnt_in_row = _dot(sel.astype(BF16).reshape(rows, w), incl_lane)
    row_tot = cnt_in_row[:, w - 1:w]
    ri = lax.broadcasted_iota(I32, (rows, rows), 0)
    rj = lax.broadcasted_iota(I32, (rows, rows), 1)
    earlier_row_same_seq = ((ri // nrow) == (rj // nrow)) & (rj < ri)
    row_off = _dot(earlier_row_same_seq.astype(BF16),
                   jnp.broadcast_to(row_tot, (rows, LANES)).astype(BF16))[:, 0:1]
    cnt3 = cnt_in_row.astype(BF16).reshape(nb, nrow, w)
    off3 = row_off.reshape(nb, nrow, 1)
    incl3 = (row_off + row_tot).reshape(nb, nrow, 1)
    jl = lax.broadcasted_iota(I32, (1, 1, n_keep), 2).astype(F32)
    row_j = jnp.sum((incl3 <= jl).astype(F32), axis=1, keepdims=True)
    onehot = lax.broadcasted_iota(I32, (nb, nrow, n_keep), 1).astype(F32) == row_j
    off_j = jnp.sum(jnp.where(onehot, off3, 0.0), axis=1, keepdims=True)
    local_rank = jl - off_j
    onehot_b = onehot.astype(BF16)
    page_i = lax.broadcasted_iota(I32, (n_pages, n_keep), 0)
    row_i = lax.broadcasted_iota(I32, (SUBLANES, n_keep), 0)
    for b in range(nb):
        cnt_j = lax.dot_general(cnt3[b], onehot_b[b], (((0,), (0,)), ((), ())),
                                preferred_element_type=F32)
        lane_j = jnp.sum((cnt_j <= local_rank[b]).astype(F32), axis=0, keepdims=True)
        pos = jnp.minimum((row_j[b] * w + lane_j).astype(I32), past - 1)
        phys = jnp.sum(jnp.where(page_i == pos // PAGE_SIZE, pt_ref[b], 0.0), axis=0, keepdims=True)
        tile = (phys.astype(I32) * PAGE_SIZE + pos % PAGE_SIZE) // (SUBLANES // N_KV_HEADS)
        meta_o[b] = jnp.where(row_i == 0, tile,
                              jnp.where(row_i == 1, sel_new[b], pos % (SUBLANES // N_KV_HEADS)))


def _decode_pick(keys, knew, pt_f):
    nb, nrow, w = keys.shape
    n_keep = min(TOPK_KEYS, (nrow * w + 1) // 4)
    full = lambda a: pl.BlockSpec(a.shape, lambda i: (0,) * a.ndim)
    return pl.pallas_call(
        functools.partial(_decode_pick_kernel, n_keep=n_keep),
        grid=(1,),
        in_specs=[full(keys), full(knew), full(pt_f)],
        out_specs=pl.BlockSpec((nb, SUBLANES, n_keep), lambda i: (0, 0, 0)),
        out_shape=jax.ShapeDtypeStruct((nb, SUBLANES, n_keep), I32),
        compiler_params=_params(1),
        name="decode_pick",
    )(keys, knew, pt_f)


def _decode_attend_kernel(tile_ref, q_ref, meta_ref, kn_ref, vn_ref, ck_hbm, cv_hbm, o_ref,
                          kbuf, vbuf, expand_ref, sem, *, n_keep):
    b = pl.program_id(0)
    nb = pl.num_programs(0)
    ncol = n_keep * SUBLANES

    def item_copies(bb, j, slot):
        src = pl.ds(pl.multiple_of(tile_ref[bb, j] * SUBLANES, SUBLANES), SUBLANES)
        dst = pl.ds(pl.multiple_of(j * SUBLANES, SUBLANES), SUBLANES)
        return (pltpu.make_async_copy(ck_hbm.at[src], kbuf.at[slot, dst], sem.at[0, slot]),
                pltpu.make_async_copy(cv_hbm.at[src], vbuf.at[slot, dst], sem.at[1, slot]))

    def start_batch(bb, slot):
        def body(j, carry):
            for prio, cp in enumerate(item_copies(bb, j, slot)):
                cp.start(priority=prio)
            return carry
        lax.fori_loop(0, n_keep, body, 0)

    def wait_batch(slot):
        pltpu.make_async_copy(ck_hbm.at[pl.ds(0, ncol)], kbuf.at[slot], sem.at[0, slot]).wait()
        pltpu.make_async_copy(cv_hbm.at[pl.ds(0, ncol)], vbuf.at[slot], sem.at[1, slot]).wait()

    @pl.when(b == 0)
    def _():
        start_batch(0, 0)
        item_of_col = lax.broadcasted_iota(I32, (n_keep, ncol), 1) // SUBLANES
        expand_ref[...] = (item_of_col == lax.broadcasted_iota(I32, (n_keep, ncol), 0)).astype(BF16)

    @pl.when(b + 1 < nb)
    def _():
        start_batch(b + 1, (b + 1) % 2)

    slot = b % 2
    wait_batch(slot)

    q8 = q_ref[0]
    meta = meta_ref[0]
    new_kept = meta[1:2, 0:1] > 0
    sub_col = _dot(meta.astype(BF16), expand_ref[...])[2:3, :]
    head_i = lax.broadcasted_iota(I32, (N_HEADS, ncol), 0)
    col_i = lax.broadcasted_iota(I32, (N_HEADS, ncol), 1)
    want = sub_col * N_KV_HEADS + (head_i // GROUP).astype(F32)
    is_new_item = new_kept & (col_i // SUBLANES == n_keep - 1)
    ok = ((col_i % SUBLANES).astype(F32) == want) & jnp.logical_not(is_new_item)

    s = jnp.where(ok, _dot_nt(q8, kbuf[slot].astype(BF16)), NEG)
    s_new = jnp.sum(q8.astype(F32) * kn_ref[0].astype(F32), axis=1, keepdims=True)
    s_new = jnp.where(new_kept, s_new, NEG)
    m = jnp.maximum(jnp.max(s, axis=1, keepdims=True), s_new)
    p = jnp.exp(s - m)
    p_new = jnp.exp(s_new - m)
    l = jnp.sum(p, axis=1, keepdims=True) + p_new
    acc = _dot(p.astype(BF16), vbuf[slot].astype(BF16)) + p_new.astype(BF16).astype(F32) * vn_ref[0].astype(F32)
    o_ref[0] = (acc / l).astype(BF16)


def _decode_attend(tiles, q8, meta, kn8, vn8, ck2, cv2):
    db, n_keep = tiles.shape
    per_b = lambda a: pl.BlockSpec((1,) + a.shape[1:], lambda b, tl: (b, 0, 0))
    any_spec = pl.BlockSpec(memory_space=pl.ANY)
    ncol = n_keep * SUBLANES
    grid_spec = pltpu.PrefetchScalarGridSpec(
        num_scalar_prefetch=1,
        grid=(db,),
        in_specs=[per_b(q8), per_b(meta), per_b(kn8), per_b(vn8), any_spec, any_spec],
        out_specs=pl.BlockSpec((1, N_HEADS, HEAD_DIM), lambda b, tl: (b, 0, 0)),
        scratch_shapes=[pltpu.VMEM((2, ncol, HEAD_DIM), F32), pltpu.VMEM((2, ncol, HEAD_DIM), F32),
                        pltpu.VMEM((n_keep, ncol), BF16), pltpu.SemaphoreType.DMA((2, 2))],
    )
    return pl.pallas_call(
        functools.partial(_decode_attend_kernel, n_keep=n_keep),
        grid_spec=grid_spec,
        out_shape=jax.ShapeDtypeStruct((db, N_HEADS, HEAD_DIM), BF16),
        compiler_params=_params(1),
        name="decode_attend",
    )(tiles, q8, meta, kn8, vn8, ck2, cv2)


def _layer_norm(r, g, b):
    mu = jnp.mean(r, axis=-1, keepdims=True)
    d = r - mu
    var = jnp.mean(d * d, axis=-1, keepdims=True)
    return d * lax.rsqrt(var + LN_EPS) * g + b


def _route(logits_t, rbias):
    n_exp, tm = logits_t.shape
    epg = n_exp // N_GROUPS
    s = jax.nn.sigmoid(logits_t)
    sb = s + rbias
    ie = lax.broadcasted_iota(I32, (epg, tm), 0)
    gs_rows = []
    for g in range(N_GROUPS):
        blk = sb[g * epg:(g + 1) * epg, :]
        m1 = jnp.max(blk, axis=0, keepdims=True)
        i1 = jnp.min(jnp.where(blk == m1, ie, epg), axis=0, keepdims=True)
        m2 = jnp.max(jnp.where(ie == i1, -jnp.inf, blk), axis=0, keepdims=True)
        gs_rows.append(m1 + m2)
    picked = [jnp.zeros((1, tm), jnp.bool_) for _ in range(N_GROUPS)]
    cur = list(gs_rows)
    for _ in range(TOPK_GROUPS):
        mx = cur[0]
        for g in range(1, N_GROUPS):
            mx = jnp.maximum(mx, cur[g])
        found = jnp.zeros((1, tm), jnp.bool_)
        for g in range(N_GROUPS):
            hit = (cur[g] == mx) & jnp.logical_not(found)
            found = found | hit
            picked[g] = picked[g] | hit
            cur[g] = jnp.where(hit, -jnp.inf, cur[g])
    masked = jnp.concatenate(
        [jnp.where(picked[g], sb[g * epg:(g + 1) * epg, :], NEG) for g in range(N_GROUPS)], axis=0)
    iall = lax.broadcasted_iota(I32, (n_exp, tm), 0)
    e_rows, w_rows = [], []
    for _ in range(TOP_K):
        mx = jnp.max(masked, axis=0, keepdims=True)
        ix = jnp.min(jnp.where(masked == mx, iall, n_exp), axis=0, keepdims=True)
        hit = iall == ix
        w_rows.append(jnp.sum(jnp.where(hit, s, 0.0), axis=0, keepdims=True))
        e_rows.append(ix)
        masked = jnp.where(hit, -jnp.inf, masked)
    wsum = w_rows[0]
    for w in w_rows[1:]:
        wsum = wsum + w
    gates = [w / wsum * ROUTED_SCALE for w in w_rows]
    return jnp.concatenate(e_rows, axis=0), jnp.concatenate(gates, axis=0)


def _post_attn_kernel(xp, xs, ap, as_, cp, cs, gap, gas, gcp, gcs, woa, woc, wo, g1, b1, rwt_hi, rwt_lo, rb,
                      h2_o, e_o, gate_o, *, n_prompt_tiles):
    i = pl.program_id(0)
    is_p = i < n_prompt_tiles
    pick = lambda a, b: jnp.where(is_p, a[...], b[...])
    a = _dot(pick(ap, as_), woa[...])
    c = _dot(pick(cp, cs), woc[...])
    merged = pick(gap, gas).astype(F32) * a + pick(gcp, gcs).astype(F32) * c
    r = ALPHA * pick(xp, xs) + _dot(merged.astype(BF16), wo[...])
    h = _layer_norm(r, g1[...], b1[...])
    tm = h.shape[0]
    for k in range(D_MODEL // LANES):
        h2_o[pl.ds(k, tm, stride=SUBLANES), :] = h[:, k * LANES:(k + 1) * LANES]
    h_hi = h.astype(BF16)
    h_lo = (h - h_hi.astype(F32)).astype(BF16)
    logits_t = _dot_nt(rwt_hi[...], h_hi) + (_dot_nt(rwt_hi[...], h_lo) + _dot_nt(rwt_lo[...], h_hi))
    e_idx, gates = _route(logits_t, rb[...])
    e_o[...] = e_idx
    gate_o[...] = gates


def _post_attn(x_p, x_s, attn_p, attn_s, conv_p, conv_s, ga_p, ga_s, gc_p, gc_s,
               woa, woc, wo, g1, b1, rwt, rb):
    rwt_hi = rwt.astype(BF16)
    rwt_lo = (rwt - rwt_hi.astype(F32)).astype(BF16)
    tp, ts = x_p.shape[0], x_s.shape[0]
    tm = TOKEN_TILE
    npt, nst = tp // tm, ts // tm
    n_tok = tp + ts
    p_spec = pl.BlockSpec((tm, D_MODEL), lambda i: (jnp.minimum(i, npt - 1), 0))
    s_spec = pl.BlockSpec((tm, D_MODEL), lambda i: (jnp.maximum(i - npt, 0), 0))
    consts = [woa, woc, wo, g1, b1, rwt_hi, rwt_lo, rb]
    return pl.pallas_call(
        functools.partial(_post_attn_kernel, n_prompt_tiles=npt),
        grid=(npt + nst,),
        in_specs=[p_spec, s_spec] * 5 + [_const_spec(c.shape) for c in consts],
        out_specs=[pl.BlockSpec((tm * SUBLANES, LANES), lambda i: (i, 0)),
                   pl.BlockSpec((TOP_K, tm), lambda i: (0, i)),
                   pl.BlockSpec((TOP_K, tm), lambda i: (0, i))],
        out_shape=[jax.ShapeDtypeStruct((n_tok * SUBLANES, LANES), F32),
                   jax.ShapeDtypeStruct((TOP_K, n_tok), I32),
                   jax.ShapeDtypeStruct((TOP_K, n_tok), F32)],
        compiler_params=_params(1),
        name="post_attn",
    )(x_p, x_s, attn_p, attn_s, conv_p, conv_s, ga_p, ga_s, gc_p, gc_s, *consts)


def _rank_kernel(e_ref, rank_o, cnt_o, carry_ref):
    i = pl.program_id(0)

    @pl.when(i == 0)
    def _():
        carry_ref[...] = jnp.zeros_like(carry_ref)

    n_exp = carry_ref.shape[0]
    e = e_ref[...]
    tk = e.shape[1]
    ie = lax.broadcasted_iota(I32, (n_exp, tk), 0)
    onehot = jnp.zeros((n_exp, tk), F32)
    for j in range(TOP_K):
        onehot = onehot + (ie == e[j:j + 1, :]).astype(F32)
    before = (lax.broadcasted_iota(I32, (tk, tk), 0) < lax.broadcasted_iota(I32, (tk, tk), 1)).astype(BF16)
    prefix = _dot(onehot.astype(BF16), before) + carry_ref[:, 0:1]
    rows = [jnp.sum(jnp.where(ie == e[j:j + 1, :], prefix, 0.0), axis=0, keepdims=True) for j in range(TOP_K)]
    rank_o[...] = jnp.concatenate(rows, axis=0).astype(I32)
    carry_ref[...] = carry_ref[...] + jnp.sum(onehot, axis=1, keepdims=True)
    cnt_o[...] = carry_ref[...].astype(I32)


def _rank(e_t, n_exp):
    n_tok = e_t.shape[1]
    tk = TOKEN_TILE
    return pl.pallas_call(
        _rank_kernel,
        grid=(n_tok // tk,),
        in_specs=[pl.BlockSpec((TOP_K, tk), lambda i: (0, i))],
        out_specs=[pl.BlockSpec((TOP_K, tk), lambda i: (0, i)),
                   pl.BlockSpec((n_exp, LANES), lambda i: (0, 0))],
        out_shape=[jax.ShapeDtypeStruct((TOP_K, n_tok), I32), jax.ShapeDtypeStruct((n_exp, LANES), I32)],
        scratch_shapes=[pltpu.VMEM((n_exp, LANES), F32)],
        compiler_params=_params(1),
        name="moe_rank",
    )(e_t)


def _slot_kernel(e_ref, rank_ref, pstart_ref, slot_o):
    e = e_ref[...]
    n_exp = pstart_ref.shape[0]
    tk = e.shape[1]
    ie = lax.broadcasted_iota(I32, (n_exp, tk), 0)
    ps = pstart_ref[:, 0:1]
    rows = [jnp.sum(jnp.where(ie == e[j:j + 1, :], ps, 0.0), axis=0, keepdims=True) for j in range(TOP_K)]
    slot_o[...] = rank_ref[...] + jnp.concatenate(rows, axis=0).astype(I32)


def _slots(e_t, rank_t, pstart_f):
    n_tok = e_t.shape[1]
    tk = TOKEN_TILE
    spec = pl.BlockSpec((TOP_K, tk), lambda i: (0, i))
    return pl.pallas_call(
        _slot_kernel,
        grid=(n_tok // tk,),
        in_specs=[spec, spec, _const_spec(pstart_f.shape)],
        out_specs=spec,
        out_shape=jax.ShapeDtypeStruct((TOP_K, n_tok), I32),
        compiler_params=_params(1),
        name="moe_slots",
    )(e_t, rank_t, pstart_f)


def _dispatch_kernel(pend_ref, padded_ref, slot_ref, h2_ref, xs_hbm, zbuf, sem, zsem, *, n_exp):
    i = pl.program_id(0)
    td = slot_ref.shape[0] // TOP_K
    blk_rows = MOE_BLOCK * SUBLANES

    def zero_copy(e):
        dst = pl.ds(pl.multiple_of((pend_ref[e] - MOE_BLOCK) * SUBLANES, SUBLANES), blk_rows)
        return pltpu.make_async_copy(zbuf, xs_hbm.at[dst], zsem)

    @pl.when(i == 0)
    def _():
        zbuf[...] = jnp.zeros_like(zbuf)

        def start(e, carry):
            @pl.when(padded_ref[e] > 0)
            def _():
                zero_copy(e).start()
            return carry

        def wait(e, carry):
            @pl.when(padded_ref[e] > 0)
            def _():
                zero_copy(e).wait()
            return carry

        lax.fori_loop(0, n_exp, start, 0)
        lax.fori_loop(0, n_exp, wait, 0)

    def row_copy(t, j):
        src = pl.ds(pl.multiple_of(t * SUBLANES, SUBLANES), SUBLANES)
        dst = pl.ds(pl.multiple_of(slot_ref[t * TOP_K + j] * SUBLANES, SUBLANES), SUBLANES)
        return pltpu.make_async_copy(h2_ref.at[src], xs_hbm.at[dst], sem)

    def start_tok(t, carry):
        for j in range(TOP_K):
            row_copy(t, j).start(priority=j % 2)
        return carry

    lax.fori_loop(0, td, start_tok, 0)
    for j in range(TOP_K):
        pltpu.make_async_copy(h2_ref, xs_hbm.at[pl.ds(0, td * SUBLANES)], sem).wait()


def _dispatch(pend, padded, slot_flat, h2, m_pad):
    n_tok = slot_flat.shape[0] // TOP_K
    td = TOKEN_TILE
    n_exp = pend.shape[0]
    grid_spec = pltpu.PrefetchScalarGridSpec(
        num_scalar_prefetch=2,
        grid=(n_tok // td,),
        in_specs=[pl.BlockSpec((td * TOP_K,), lambda i, *_: (i,), memory_space=pltpu.SMEM),
                  pl.BlockSpec((td * SUBLANES, LANES), lambda i, *_: (i, 0))],
        out_specs=pl.BlockSpec(memory_space=pl.ANY),
        scratch_shapes=[pltpu.VMEM((MOE_BLOCK * SUBLANES, LANES), F32),
                        pltpu.SemaphoreType.DMA(()), pltpu.SemaphoreType.DMA(())],
    )
    return pl.pallas_call(
        functools.partial(_dispatch_kernel, n_exp=n_exp),
        grid_spec=grid_spec,
        out_shape=jax.ShapeDtypeStruct((m_pad * SUBLANES, LANES), F32),
        compiler_params=_params(1),
        name="moe_dispatch",
    )(pend, padded, slot_flat, h2)


def _tile_rows(ref, n):
    return jnp.concatenate([ref[pl.ds(k, n, stride=SUBLANES), :] for k in range(D_MODEL // LANES)], axis=1)


PLAN_EXPERT, PLAN_SLOT, PLAN_NEXT, PLAN_HAS_NEXT = range(4)


def _expert_kernel(plan_ref, nused_ref, x_ref, wg_hbm, wu_hbm, wd_hbm, y_ref, wg_f, wu_f, wd_f, wgu_b, wd_b, sem):
    i = pl.program_id(0)
    d_exp = wd_b.shape[0]

    def weight_copies(e, s):
        return (pltpu.make_async_copy(wg_hbm.at[e], wg_f.at[s], sem.at[s]),
                pltpu.make_async_copy(wu_hbm.at[e], wu_f.at[s], sem.at[s]),
                pltpu.make_async_copy(wd_hbm.at[e], wd_f.at[s], sem.at[s]))

    @pl.when(i < nused_ref[0])
    def _():
        e = plan_ref[PLAN_EXPERT, i]
        changed = (i == 0) | (e != plan_ref[PLAN_EXPERT, jnp.maximum(i - 1, 0)])

        @pl.when(changed)
        def _():
            s = plan_ref[PLAN_SLOT, i]

            @pl.when(i == 0)
            def _():
                for cp in weight_copies(e, s):
                    cp.start()

            for cp in weight_copies(e, s):
                cp.wait()

            @pl.when(plan_ref[PLAN_HAS_NEXT, i] > 0)
            def _():
                for cp in weight_copies(plan_ref[PLAN_NEXT, i], 1 - s):
                    cp.start()

            wgu_b[:, :d_exp] = wg_f[s].astype(BF16)
            wgu_b[:, d_exp:] = wu_f[s].astype(BF16)
            wd_b[...] = wd_f[s].astype(BF16)

        x = _tile_rows(x_ref, MOE_BLOCK).astype(BF16)
        gu = _dot(x, wgu_b[...])
        hh = jax.nn.silu(gu[:, :d_exp]) * gu[:, d_exp:]
        y = _dot(hh.astype(BF16), wd_b[...])
        for k in range(D_MODEL // LANES):
            y_ref[pl.ds(k, MOE_BLOCK, stride=SUBLANES), :] = y[:, k * LANES:(k + 1) * LANES]


def _expert_plan(block_e, nused):
    n_blocks = block_e.shape[0]
    idx = jnp.arange(n_blocks, dtype=I32)
    changed = (idx == 0) | (block_e != jnp.roll(block_e, 1))
    slot = (jnp.cumsum(changed.astype(I32)) - 1) % 2
    change_pos = jnp.where(changed & (idx < nused), idx, n_blocks)
    next_pos = jnp.concatenate([lax.cummin(change_pos[::-1])[::-1][1:], jnp.full((1,), n_blocks, I32)])
    has_next = (next_pos < n_blocks).astype(I32)
    nxt = block_e[jnp.minimum(next_pos, n_blocks - 1)]
    return jnp.stack([block_e, slot, nxt, has_next]).astype(I32)


def _experts(block_e, nused, xs, w_gate, w_up, w_down):
    n_blocks = block_e.shape[0]
    d_exp = w_gate.shape[2]
    rows = MOE_BLOCK * SUBLANES
    last = lambda i, nu: jnp.minimum(i, nu[0] - 1)
    any_spec = pl.BlockSpec(memory_space=pl.ANY)
    grid_spec = pltpu.PrefetchScalarGridSpec(
        num_scalar_prefetch=2,
        grid=(n_blocks,),
        in_specs=[pl.BlockSpec((rows, LANES), lambda i, plan, nu: (last(i, nu), 0)), any_spec, any_spec, any_spec],
        out_specs=pl.BlockSpec((rows, LANES), lambda i, plan, nu: (last(i, nu), 0)),
        scratch_shapes=[pltpu.VMEM((2, D_MODEL, d_exp), F32), pltpu.VMEM((2, D_MODEL, d_exp), F32),
                        pltpu.VMEM((2, d_exp, D_MODEL), F32),
                        pltpu.VMEM((D_MODEL, 2 * d_exp), BF16), pltpu.VMEM((d_exp, D_MODEL), BF16),
                        pltpu.SemaphoreType.DMA((2,))],
    )
    return pl.pallas_call(
        _expert_kernel,
        grid_spec=grid_spec,
        out_shape=jax.ShapeDtypeStruct(xs.shape, F32),
        compiler_params=_params(1),
        name="moe_experts",
    )(_expert_plan(block_e, nused[0]), nused, xs, w_gate, w_up, w_down)


def _combine_kernel(slot_ref, slot_next_ref, gate_ref, h2_ref, ys_hbm, shg, shu, shd, g2, b2, yp_o, ys_o,
                    buf, base_ref, y_ref, sem, *, n_prompt_tiles):
    i = pl.program_id(0)
    n = pl.num_programs(0)
    tc = gate_ref.shape[0]

    def start_tile(s_ref, bslot):
        def body(t, carry):
            dst = pl.ds(pl.multiple_of(t * SUBLANES, SUBLANES), SUBLANES)
            for j in range(TOP_K):
                src = pl.ds(pl.multiple_of(s_ref[t * TOP_K + j] * SUBLANES, SUBLANES), SUBLANES)
                pltpu.make_async_copy(ys_hbm.at[src], buf.at[bslot, j, dst], sem.at[bslot]).start(priority=j % 2)
            return carry
        lax.fori_loop(0, tc, body, 0)

    def wait_tile(bslot):
        for j in range(TOP_K):
            pltpu.make_async_copy(ys_hbm.at[pl.ds(0, tc * SUBLANES)], buf.at[bslot, j], sem.at[bslot]).wait()

    @pl.when(i == 0)
    def _():
        start_tile(slot_ref, 0)

    cur = i % 2
    h = _tile_rows(h2_ref, tc)
    hb = h.astype(BF16)
    shared = _dot((jax.nn.silu(_dot(hb, shg[...])) * _dot(hb, shu[...])).astype(BF16), shd[...])
    base_ref[...] = ALPHA * h + shared
    wait_tile(cur)
    gain, bias = g2[...], b2[...]
    grp = SUBLANES

    def finish_group(g):
        r0 = pl.multiple_of(g * grp, grp)
        t0 = pl.multiple_of(g * grp * SUBLANES, grp * SUBLANES)
        routed = jnp.zeros((grp, D_MODEL), F32)
        for j in range(TOP_K):
            rows = jnp.concatenate([buf[cur, j, pl.ds(t0 + k, grp, stride=SUBLANES), :]
                                    for k in range(D_MODEL // LANES)], axis=1)
            routed = routed + rows * gate_ref[pl.ds(r0, grp), j:j + 1]
        y_ref[pl.ds(r0, grp), :] = base_ref[pl.ds(r0, grp), :] + routed

    @pl.when(i + 1 < n)
    def _():
        nxt = (i + 1) % 2

        def body(g, carry):
            for tt in range(grp):
                t = g * grp + tt
                dst = pl.ds(pl.multiple_of(t * SUBLANES, SUBLANES), SUBLANES)
                for j in range(TOP_K):
                    src = pl.ds(pl.multiple_of(slot_next_ref[t * TOP_K + j] * SUBLANES, SUBLANES), SUBLANES)
                    pltpu.make_async_copy(ys_hbm.at[src], buf.at[nxt, j, dst], sem.at[nxt]).start(priority=j % 2)
            finish_group(g)
            return carry

        lax.fori_loop(0, tc // grp, body, 0)

    @pl.when(i + 1 >= n)
    def _():
        def body(g, carry):
            finish_group(g)
            return carry

        lax.fori_loop(0, tc // grp, body, 0)

    y = _layer_norm(y_ref[...], gain, bias)

    @pl.when(i < n_prompt_tiles)
    def _():
        yp_o[...] = y

    @pl.when(i >= n_prompt_tiles)
    def _():
        ys_o[...] = y


def _combine(slot_flat, gates, h2, ys, shg, shu, shd, g2, b2, n_prompt, n_decode):
    tc = TOKEN_TILE
    npt, nst = n_prompt // tc, n_decode // tc
    consts = [shg, shu, shd, g2, b2]
    return pl.pallas_call(
        functools.partial(_combine_kernel, n_prompt_tiles=npt),
        grid=(npt + nst,),
        in_specs=[pl.BlockSpec((tc * TOP_K,), lambda i: (i,), memory_space=pltpu.SMEM),
                  pl.BlockSpec((tc * TOP_K,), lambda i: (jnp.minimum(i + 1, npt + nst - 1),),
                               memory_space=pltpu.SMEM),
                  pl.BlockSpec((tc, TOP_K), lambda i: (i, 0)),
                  pl.BlockSpec((tc * SUBLANES, LANES), lambda i: (i, 0)),
                  pl.BlockSpec(memory_space=pl.ANY)] + [_const_spec(c.shape) for c in consts],
        out_specs=[pl.BlockSpec((tc, D_MODEL), lambda i: (jnp.minimum(i, npt - 1), 0)),
                   pl.BlockSpec((tc, D_MODEL), lambda i: (jnp.maximum(i - npt, 0), 0))],
        out_shape=[jax.ShapeDtypeStruct((n_prompt, D_MODEL), F32),
                   jax.ShapeDtypeStruct((n_decode, D_MODEL), F32)],
        scratch_shapes=[pltpu.VMEM((2, TOP_K, tc * SUBLANES, LANES), F32), pltpu.VMEM((tc, D_MODEL), F32),
                        pltpu.VMEM((tc, D_MODEL), F32), pltpu.SemaphoreType.DMA((2,))],
        compiler_params=_params(1),
        name="moe_combine",
    )(slot_flat, slot_flat, gates, h2, ys, *consts)


def _pad_rows(a, n):
    return jnp.pad(a, ((0, n - a.shape[0]),) + ((0, 0),) * (a.ndim - 1))


def kernel(x_prompt, x_sample, cache_k, cache_v, cache_kidx, page_table, state_conv, w_in, conv_w, w_o_attn,
           w_o_conv, w_o, ln1_g, ln1_b, router_w, router_bias, moe_w_gate, moe_w_up, moe_w_down,
           shared_w_gate, shared_w_up, shared_w_down, ln2_g, ln2_b):
    nb, seq, _ = x_prompt.shape
    db = x_sample.shape[0]
    n_pool = cache_k.shape[1]
    n_exp = router_w.shape[-1]
    tp = nb * seq
    ts = TOKEN_TILE
    nkv = N_KV_HEADS * HEAD_DIM

    weights = _prep_in_weights(w_in[0])
    cw = conv_w[0]

    xp2 = x_prompt.reshape(tp, D_MODEL)
    (q_p, k_p, v_p, kvb_p, qi_p, idx_p, kib_p, conv_p, ga_p, gc_p, ulast_p) = _in_proj_prompt(xp2, weights, cw, nb)
    xs2 = _pad_rows(x_sample.reshape(db, D_MODEL), ts)
    prev2 = _pad_rows(state_conv[0, :, 0, :], ts)
    prev1 = _pad_rows(state_conv[0, :, 1, :], ts)
    (q_s, k_s, v_s, _, qi_s, idx_s, _, conv_s, ga_s, gc_s, u_s) = _in_proj_decode(xs2, prev2, prev1, weights, cw)

    attn_p = _attn_prompt(q_p, kvb_p, qi_p, kib_p, idx_p, nb)
    q8 = q_s[:db].reshape(db, N_HEADS, HEAD_DIM)
    qi8 = qi_s[:db].reshape(db, IDX_HEADS, LANES)[:, :, :IDX_DIM]
    wi8 = idx_s[:db, IDX_DIM:IDX_DIM + IDX_HEADS].reshape(db, IDX_HEADS, 1)
    kin = idx_s[:db, :IDX_DIM].astype(BF16).reshape(db, 1, IDX_DIM)
    k_new = k_s[:db * N_KV_HEADS].reshape(db, N_KV_HEADS, HEAD_DIM)
    v_new = v_s[:db * N_KV_HEADS].reshape(db, N_KV_HEADS, HEAD_DIM)
    kn8 = jnp.repeat(k_new, GROUP, axis=1).astype(BF16)
    vn8 = jnp.repeat(v_new, GROUP, axis=1).astype(BF16)
    ck2 = cache_k[0].reshape(n_pool * PAGE_SIZE * N_KV_HEADS, HEAD_DIM)
    cv2 = cache_v[0].reshape(n_pool * PAGE_SIZE * N_KV_HEADS, HEAD_DIM)
    keys_s, knew_s = _decode_scores(page_table, qi8, wi8, kin, jnp.swapaxes(cache_kidx[0], 1, 2))
    meta = _decode_pick(keys_s, knew_s, page_table.astype(F32)[:, :, None])
    attn_s8 = _decode_attend(meta[:, 0, :], q8, meta, kn8, vn8, ck2, cv2)
    attn_s = _pad_rows(attn_s8.reshape(db, N_HEADS * HEAD_DIM), ts)

    h2, e_t, gate_t = _post_attn(
        xp2, xs2, attn_p, attn_s, conv_p, conv_s, ga_p, ga_s, gc_p, gc_s,
        w_o_attn[0].astype(BF16), w_o_conv[0].astype(BF16), w_o[0].astype(BF16),
        ln1_g[0].reshape(1, D_MODEL), ln1_b[0].reshape(1, D_MODEL),
        router_w[0].T, router_bias[0].reshape(n_exp, 1))

    n_tok = tp + ts
    rank_t, cnt = _rank(e_t, n_exp)
    counts = cnt[:, 0]
    padded = (counts + MOE_BLOCK - 1) // MOE_BLOCK * MOE_BLOCK
    pend = jnp.cumsum(padded)
    pstart = pend - padded
    n_blocks = (n_tok * TOP_K + n_exp * (MOE_BLOCK - 1) + MOE_BLOCK - 1) // MOE_BLOCK
    nused = (pend[-1] // MOE_BLOCK).astype(I32)
    blk = jnp.minimum(jnp.arange(n_blocks, dtype=I32), nused - 1)
    block_e = jnp.minimum(jnp.sum((pend[None, :] <= (blk * MOE_BLOCK)[:, None]).astype(I32), axis=1), n_exp - 1)
    slot_t = _slots(e_t, rank_t, jnp.broadcast_to(pstart.astype(F32)[:, None], (n_exp, LANES)))
    slot_flat = slot_t.T.reshape(-1)
    xs = _dispatch(pend.astype(I32), padded.astype(I32), slot_flat, h2, n_blocks * MOE_BLOCK)
    ys = _experts(block_e, nused.reshape(1), xs, moe_w_gate[0], moe_w_up[0], moe_w_down[0])
    y_p, y_s = _combine(slot_flat, gate_t.T, h2, ys,
                        shared_w_gate[0].astype(BF16), shared_w_up[0].astype(BF16),
                        shared_w_down[0].astype(BF16),
                        ln2_g[0].reshape(1, D_MODEL), ln2_b[0].reshape(1, D_MODEL), tp, ts)

    conv_sample = jnp.stack([state_conv[0, :, 1, :], u_s[:db]], axis=1)[None]
    return (y_p.reshape(nb, seq, D_MODEL),
            y_s[:db].reshape(db, 1, D_MODEL),
            k_p.reshape(1, nb, seq, N_KV_HEADS, HEAD_DIM),
            v_p.reshape(1, nb, seq, N_KV_HEADS, HEAD_DIM),
            idx_p[:, :IDX_DIM].reshape(1, nb, seq, IDX_DIM),
            ulast_p[:, SUBLANES - (CONV_W - 1):, :][None],
            k_new.reshape(1, db, 1, N_KV_HEADS, HEAD_DIM),
            v_new.reshape(1, db, 1, N_KV_HEADS, HEAD_DIM),
            idx_s[:db, :IDX_DIM].reshape(1, db, 1, IDX_DIM),
            conv_sample)
```

```python
import functools

import jax
import jax.numpy as jnp
import numpy as np
from jax import lax
from jax.experimental import pallas as pl
from jax.experimental.pallas import tpu as pltpu

F32 = jnp.float32
BF16 = jnp.bfloat16
I32 = jnp.int32

D_MODEL = 1024
N_HEADS = 8
HEAD_DIM = 128
N_KV_HEADS = 2
GROUP = N_HEADS // N_KV_HEADS
IDX_HEADS = 8
IDX_DIM = 64
TOPK_KEYS = 256
IDX_SCALE = IDX_DIM ** -0.5 * IDX_HEADS ** -0.5
Q_SCALE = HEAD_DIM ** -0.5
PAGE_SIZE = 128
D_CONV = D_MODEL
CONV_W = 3
TOP_K = 8
N_GROUPS = 8
TOPK_GROUPS = 4
ROUTED_SCALE = 2.5
DEPTH = 1
ALPHA = (2 * DEPTH) ** 0.25
LN_EPS = 1e-5
NEG = -1e30
IN_WIDTHS = (N_HEADS * HEAD_DIM, N_KV_HEADS * HEAD_DIM, N_KV_HEADS * HEAD_DIM,
             IDX_HEADS * IDX_DIM, IDX_DIM, IDX_HEADS,
             D_CONV, D_CONV, D_CONV, D_MODEL, D_MODEL)

LANES = 128
SUBLANES = 8
VMEM_LIMIT_BYTES = 56 * 1024 * 1024

PROJ_ROWS = 512
ATT_Q = 256
TOKEN_TILE = 256
MOE_BLOCK = 256
CONV_CHUNK = 256
INT_MIN = -2 ** 31
SOFTMAX_DENOM_MIN = 1e-25
F32_MIN_NORMAL = float(np.finfo(np.float32).tiny)
BF16_MIN_NORMAL_BITS = 0x0080


def _np_key(v):
    b = int(np.float32(v).view(np.int32))
    return b ^ ((b >> 31) & 0x7FFFFFFF)


KEY_HALF = _np_key(NEG * 0.5)


def _sort_key(s):
    b = lax.bitcast_convert_type(s + 0.0, I32)
    return b ^ ((b >> 31) & jnp.int32(0x7FFFFFFF))


def _dot(a, b):
    return jnp.dot(a, b, preferred_element_type=F32)


def _dot_nt(a, b):
    return lax.dot_general(a, b, (((1,), (1,)), ((), ())), preferred_element_type=F32)


def _params(n_grid):
    return pltpu.CompilerParams(dimension_semantics=("arbitrary",) * n_grid,
                                vmem_limit_bytes=VMEM_LIMIT_BYTES)


def _const_spec(shape):
    nd = len(shape)
    return pl.BlockSpec(shape, lambda *_: (0,) * nd, pipeline_mode=pl.Buffered(1))


def _proj_common(xb, w1, w2, w3, wga, wgc, q_o, k_o, v_o, kv_o, qi_o, idx_o, kib_o, sga_o, sgc_o):
    nq = N_HEADS * HEAD_DIM
    nkv = N_KV_HEADS * HEAD_DIM
    z1 = _dot(xb, w1[...])
    q_o[...] = (z1[:, :nq] * Q_SCALE).astype(BF16)
    k = z1[:, nq:nq + nkv]
    v = z1[:, nq + nkv:nq + 2 * nkv]
    for g in range(N_KV_HEADS):
        k_o[pl.ds(g, k.shape[0], stride=N_KV_HEADS), :] = k[:, g * HEAD_DIM:(g + 1) * HEAD_DIM]
        v_o[pl.ds(g, v.shape[0], stride=N_KV_HEADS), :] = v[:, g * HEAD_DIM:(g + 1) * HEAD_DIM]
    kv_o[:, :nkv] = k.astype(BF16)
    kv_o[:, nkv:] = v.astype(BF16)
    qi_o[...] = _dot(xb, w2[...]).astype(BF16)
    z3 = _dot(xb, w3[...])
    idx_o[...] = z3
    kib_o[...] = z3.astype(BF16)
    sga_o[...] = jax.nn.sigmoid(_dot(xb, wga[...])).astype(BF16)
    sgc_o[...] = jax.nn.sigmoid(_dot(xb, wgc[...])).astype(BF16)


def _in_proj_prompt_kernel(x_ref, w1, w2, w3, wb, wc, wx, wga, wgc, cw_ref,
                           q_o, k_o, v_o, kv_o, qi_o, idx_o, kib_o, conv_o, sga_o, sgc_o, ulast_o,
                           tail_ref):
    j = pl.program_id(1)

    @pl.when(j == 0)
    def _():
        tail_ref[...] = jnp.zeros_like(tail_ref)

    xb = x_ref[...].astype(BF16)
    _proj_common(xb, w1, w2, w3, wga, wgc, q_o, k_o, v_o, kv_o, qi_o, idx_o, kib_o, sga_o, sgc_o)
    tm = xb.shape[0]
    row = lax.broadcasted_iota(I32, (tm, CONV_CHUNK), 0)
    for c in range(D_CONV // CONV_CHUNK):
        cs = slice(c * CONV_CHUNK, (c + 1) * CONV_CHUNK)
        gb = _dot(xb, wb[:, cs])
        u = _dot(xb, wc[:, cs]) * _dot(xb, wx[:, cs])
        p1 = tail_ref[SUBLANES - 1:SUBLANES, cs]
        p2 = tail_ref[SUBLANES - 2:SUBLANES - 1, cs]
        u1 = jnp.where(row == 0, p1, pltpu.roll(u, 1, 0))
        u2 = jnp.where(row == 0, p2, jnp.where(row == 1, p1, pltpu.roll(u, 2, 0)))
        conv = cw_ref[0:1, cs] * u2 + cw_ref[1:2, cs] * u1 + cw_ref[2:3, cs] * u
        conv_o[:, cs] = (gb * conv).astype(BF16)
        tail_ref[:, cs] = u[tm - SUBLANES:, :]
    ulast_o[0] = tail_ref[...]


def _in_proj_decode_kernel(x_ref, p2_ref, p1_ref, w1, w2, w3, wb, wc, wx, wga, wgc, cw_ref,
                           q_o, k_o, v_o, kv_o, qi_o, idx_o, kib_o, conv_o, sga_o, sgc_o, u_o):
    xb = x_ref[...].astype(BF16)
    _proj_common(xb, w1, w2, w3, wga, wgc, q_o, k_o, v_o, kv_o, qi_o, idx_o, kib_o, sga_o, sgc_o)
    for c in range(D_CONV // CONV_CHUNK):
        cs = slice(c * CONV_CHUNK, (c + 1) * CONV_CHUNK)
        gb = _dot(xb, wb[:, cs])
        u = _dot(xb, wc[:, cs]) * _dot(xb, wx[:, cs])
        conv = cw_ref[0:1, cs] * p2_ref[:, cs] + cw_ref[1:2, cs] * p1_ref[:, cs] + cw_ref[2:3, cs] * u
        conv_o[:, cs] = (gb * conv).astype(BF16)
        u_o[:, cs] = u


def _prep_in_weights(w_in):
    w = w_in.astype(BF16)
    offs = np.concatenate([[0], np.cumsum(IN_WIDTHS)])
    part = [w[:, int(offs[i]):int(offs[i + 1])] for i in range(len(IN_WIDTHS))]
    q, k, v, qi, ki, wi, gb, gc, xv, ga, gcv = part
    d = w.shape[0]
    w1 = jnp.concatenate([q, k, v], axis=1)
    w2 = jnp.pad(qi.reshape(d, IDX_HEADS, IDX_DIM),
                 ((0, 0), (0, 0), (0, LANES - IDX_DIM))).reshape(d, IDX_HEADS * LANES)
    w3 = jnp.concatenate([ki, wi, jnp.zeros((d, LANES - IDX_DIM - IDX_HEADS), BF16)], axis=1)
    return (w1, w2, w3, gb, gc, xv, ga, gcv)


def _proj_out_shapes(t):
    nkv = N_KV_HEADS * HEAD_DIM
    sd = jax.ShapeDtypeStruct
    return [sd((t, N_HEADS * HEAD_DIM), BF16), sd((t * N_KV_HEADS, HEAD_DIM), F32),
            sd((t * N_KV_HEADS, HEAD_DIM), F32), sd((t, 2 * nkv), BF16),
            sd((t, IDX_HEADS * LANES), BF16), sd((t, LANES), F32), sd((t, LANES), BF16),
            sd((t, D_CONV), BF16), sd((t, D_MODEL), BF16), sd((t, D_MODEL), BF16)]


def _in_proj_prompt(x2d, weights, conv_w, nb):
    t = x2d.shape[0]
    s = t // nb
    tm = min(PROJ_ROWS, s)
    nj = s // tm
    row_spec = lambda w, rep=1: pl.BlockSpec((tm * rep, w), lambda b, j: (b * nj + j, 0))
    out_shapes = _proj_out_shapes(t) + [jax.ShapeDtypeStruct((nb, SUBLANES, D_CONV), F32)]
    out_specs = [row_spec(o.shape[1], o.shape[0] // t) for o in out_shapes[:-1]]
    out_specs.append(pl.BlockSpec((1, SUBLANES, D_CONV), lambda b, j: (b, 0, 0)))
    return pl.pallas_call(
        _in_proj_prompt_kernel,
        grid=(nb, nj),
        in_specs=[row_spec(D_MODEL)] + [_const_spec(w.shape) for w in weights] + [_const_spec(conv_w.shape)],
        out_specs=out_specs,
        out_shape=out_shapes,
        scratch_shapes=[pltpu.VMEM((SUBLANES, D_CONV), F32)],
        compiler_params=_params(2),
        name="in_proj_prompt",
    )(x2d, *weights, conv_w)


def _in_proj_decode(x2d, p2, p1, weights, conv_w):
    t = x2d.shape[0]
    row_spec = lambda w, rep=1: pl.BlockSpec((t * rep, w), lambda i: (0, 0))
    out_shapes = _proj_out_shapes(t) + [jax.ShapeDtypeStruct((t, D_CONV), F32)]
    return pl.pallas_call(
        _in_proj_decode_kernel,
        grid=(1,),
        in_specs=[row_spec(D_MODEL)] * 3 + [_const_spec(w.shape) for w in weights] + [_const_spec(conv_w.shape)],
        out_specs=[row_spec(o.shape[1], o.shape[0] // t) for o in out_shapes],
        out_shape=out_shapes,
        compiler_params=_params(1),
        name="in_proj_decode",
    )(x2d, p2, p1, *weights, conv_w)


def _attn_prompt_kernel(q_ref, kv_ref, qi_ref, kib_ref, idx_ref, o_ref,
                        key_ref, hi_ref, bias_ref, t_ref, j_ref, need_ref, m_ref, acc_ref, knorm_ref,
                        *, n_keep, seq_len):
    tq = q_ref.shape[0]
    i = pl.program_id(1)
    nchunk = i + 1
    nsub = tq // LANES

    idx_t = jnp.transpose(idx_ref[...])
    wi_rows = [idx_t[IDX_DIM + h:IDX_DIM + h + 1, :] * IDX_SCALE for h in range(IDX_HEADS)]
    key_i = lax.broadcasted_iota(I32, (tq, tq), 0)
    qry_i = lax.broadcasted_iota(I32, (tq, tq), 1)

    def score_chunk(c, carry):
        kic = kib_ref[pl.ds(pl.multiple_of(c * tq, tq), tq), :]
        acc = jnp.zeros((tq, tq), F32)
        for h in range(IDX_HEADS):
            d = _dot_nt(kic, qi_ref[:, h * LANES:(h + 1) * LANES])
            acc = acc + jnp.maximum(d, 0.0) * wi_rows[h]
        acc = jnp.where(jnp.abs(acc) < F32_MIN_NORMAL, 0.0, acc)
        s = jnp.where((c < i) | (key_i <= qry_i), acc, NEG)
        bits = lax.bitcast_convert_type(s, I32)
        key_ref[c] = bits ^ ((bits >> 31) & jnp.int32(0x7FFFFFFF))
        hi_ref[c] = lax.bitcast_convert_type(bits & jnp.int32(-65536), F32).astype(BF16)
        return carry

    lax.fori_loop(0, nchunk, score_chunk, 0)

    sub_i = lax.broadcasted_iota(I32, (SUBLANES, tq), 0)

    def counts(preds, with_pos=False):
        def body(c, cnts):
            for kk in range(tq // SUBLANES):
                blk = key_ref[c, kk * SUBLANES:(kk + 1) * SUBLANES, :]
                pos = (c * tq + kk * SUBLANES + sub_i) if with_pos else None
                cnts = tuple(cnt + pred(blk, pos).astype(I32) for cnt, pred in zip(cnts, preds))
            return cnts
        cnts = lax.fori_loop(0, nchunk, body, tuple(jnp.zeros((SUBLANES, tq), I32) for _ in preds))
        return tuple(jnp.sum(cnt, axis=0, keepdims=True) for cnt in cnts)

    def count_hi(cand):
        rows16 = 2 * SUBLANES
        def body(c, cnt):
            for kk in range(tq // rows16):
                blk = hi_ref[c, kk * rows16:(kk + 1) * rows16, :]
                cnt = cnt + jnp.where(blk >= cand, jnp.ones_like(cnt), jnp.zeros_like(cnt))
            return cnt
        cnt = lax.fori_loop(0, nchunk, body, jnp.zeros((rows16, tq), BF16))
        return jnp.sum(cnt.astype(F32), axis=0, keepdims=True)

    def hi_bit_body(bi, p16):
        cand = p16 + lax.shift_left(jnp.int32(1), 15 - bi)
        raw = cand ^ ((cand >> 31) & jnp.int32(0x7FFF))
        raw = jnp.where((raw > 0) & (raw < BF16_MIN_NORMAL_BITS), BF16_MIN_NORMAL_BITS, raw)
        cand_f = lax.bitcast_convert_type(lax.shift_left(raw, 16), F32)
        total = count_hi(jnp.broadcast_to(cand_f, (2 * SUBLANES, tq)).astype(BF16))
        return jnp.where(total >= n_keep, cand, p16)

    p16 = lax.fori_loop(0, 16, hi_bit_body, jnp.full((1, tq), -2 ** 15, I32))

    def bit_body(bi, t):
        cand = t + lax.shift_left(jnp.int32(1), 15 - bi)
        cand_b = jnp.broadcast_to(cand, (SUBLANES, tq))
        total, = counts((lambda blk, _: blk >= cand_b,))
        return jnp.where(total >= n_keep, cand, t)

    t = lax.fori_loop(0, 16, bit_body, lax.shift_left(p16, 16))
    t_b = jnp.broadcast_to(t, (SUBLANES, tq))
    cnt_gt, cnt_eq = counts((lambda blk, _: blk > t_b, lambda blk, _: blk == t_b))
    need = n_keep - cnt_gt
    any_excess = jnp.max(((cnt_eq > need) & (t > KEY_HALF)).astype(I32))
    t_ref[...] = t_b
    need_ref[...] = jnp.broadcast_to(need, (SUBLANES, tq))
    j_ref[...] = jnp.full(j_ref.shape, seq_len, I32)

    @pl.when(any_excess > 0)
    def _():
        nbits = max(1, (seq_len - 1).bit_length())
        t_b = t_ref[...]
        need = need_ref[0:1, :]

        def jbit(bi, jj):
            cand = jj | lax.shift_left(jnp.int32(1), nbits - 1 - bi)
            cand_b = jnp.broadcast_to(cand, (SUBLANES, tq))
            g, = counts((lambda blk, pos: (blk == t_b) & (pos < cand_b),), with_pos=True)
            return jnp.where(g < need, cand, jj)

        jj = lax.fori_loop(0, nbits, jbit, jnp.zeros((1, tq), I32))
        j_ref[...] = jnp.broadcast_to(jj, (SUBLANES, tq))

    t_row = t_ref[0:1, :]
    j_row = j_ref[0:1, :]

    def bias_chunk(c, carry):
        blk = key_ref[c]
        sel = (blk > KEY_HALF) & ((blk > t_row) | ((blk == t_row) & ((c * tq + key_i) <= j_row)))
        bias_ref[c] = jnp.transpose(jnp.where(sel, 0.0, NEG))
        return carry

    lax.fori_loop(0, nchunk, bias_chunk, 0)

    nkv = N_KV_HEADS * HEAD_DIM
    gq = GROUP * tq

    def group_logits(c, g):
        rows = pl.ds(pl.multiple_of(c * tq, tq), tq)
        kc = kv_ref[rows, g * HEAD_DIM:(g + 1) * HEAD_DIM]
        qg = jnp.concatenate([q_ref[:, h * HEAD_DIM:(h + 1) * HEAD_DIM]
                              for h in range(g * GROUP, (g + 1) * GROUP)], axis=0)
        s = _dot_nt(qg, kc).reshape(GROUP, tq, tq) + bias_ref[c][None]
        return s.reshape(gq, tq)

    def exact_row_max():
        m_ref[...] = jnp.full(m_ref.shape, NEG, F32)

        def max_chunk(c, carry):
            for g in range(N_KV_HEADS):
                s = group_logits(c, g)
                mx = s[:, :LANES]
                for k in range(1, nsub):
                    mx = jnp.maximum(mx, s[:, k * LANES:(k + 1) * LANES])
                m_ref[g] = jnp.maximum(m_ref[g], mx)
            return carry

        lax.fori_loop(0, nchunk, max_chunk, 0)
        for g in range(N_KV_HEADS):
            m_ref[g] = jnp.broadcast_to(jnp.max(m_ref[g], axis=1, keepdims=True), (gq, LANES))

    ones = jnp.ones((tq, HEAD_DIM), BF16)

    def accumulate():
        acc_ref[...] = jnp.zeros_like(acc_ref)

        def pv_chunk(c, carry):
            rows = pl.ds(pl.multiple_of(c * tq, tq), tq)
            for g in range(N_KV_HEADS):
                s = group_logits(c, g)
                m = m_ref[g]
                p = jnp.concatenate([jnp.exp(s[:, k * LANES:(k + 1) * LANES] - m) for k in range(nsub)], axis=1)
                vc = kv_ref[rows, nkv + g * HEAD_DIM:nkv + (g + 1) * HEAD_DIM]
                acc_ref[g] = acc_ref[g] + _dot(p.astype(BF16), jnp.concatenate([vc, ones], axis=1))
            return carry

        lax.fori_loop(0, nchunk, pv_chunk, 0)

    @pl.when(i == 0)
    def _():
        for g in range(N_KV_HEADS):
            def norm_chunk(c, mx):
                kc = kv_ref[pl.ds(pl.multiple_of(c * tq, tq), tq), g * HEAD_DIM:(g + 1) * HEAD_DIM].astype(F32)
                return jnp.maximum(mx, jnp.sum(kc * kc, axis=1, keepdims=True))
            k2 = lax.fori_loop(0, seq_len // tq, norm_chunk, jnp.zeros((tq, 1), F32))
            knorm_ref[g] = jnp.broadcast_to(jnp.sqrt(jnp.max(k2, axis=0, keepdims=True)), (SUBLANES, LANES))

    for h in range(N_HEADS):
        g, r = h // GROUP, h % GROUP
        qh = q_ref[:, h * HEAD_DIM:(h + 1) * HEAD_DIM].astype(F32)
        qn = jnp.sqrt(jnp.sum(qh * qh, axis=1, keepdims=True))
        m_ref[g, r * tq:(r + 1) * tq, :] = qn * knorm_ref[g][0:1, :]
    accumulate()
    denom_min = jnp.min(acc_ref[:, :, HEAD_DIM:HEAD_DIM + 1])

    @pl.when(denom_min < SOFTMAX_DENOM_MIN)
    def _():
        exact_row_max()
        accumulate()

    for h in range(N_HEADS):
        a = acc_ref[h // GROUP, (h % GROUP) * tq:(h % GROUP + 1) * tq, :]
        o_ref[:, h * HEAD_DIM:(h + 1) * HEAD_DIM] = (a[:, :HEAD_DIM] / a[:, HEAD_DIM:]).astype(BF16)


def _attn_prompt(q, kvb, qi, kib, idx, nb):
    t = q.shape[0]
    s = t // nb
    tq = min(ATT_Q, s)
    nq = s // tq
    n_keep = min(TOPK_KEYS, s // 4)
    blk = lambda w: pl.BlockSpec((tq, w), lambda b, i: (b * nq + i, 0))
    seq = lambda w: pl.BlockSpec((s, w), lambda b, i: (b, 0))
    return pl.pallas_call(
        functools.partial(_attn_prompt_kernel, n_keep=n_keep, seq_len=s),
        grid=(nb, nq),
        in_specs=[blk(q.shape[1]), seq(kvb.shape[1]), blk(qi.shape[1]), seq(kib.shape[1]), blk(idx.shape[1])],
        out_specs=blk(N_HEADS * HEAD_DIM),
        out_shape=jax.ShapeDtypeStruct((t, N_HEADS * HEAD_DIM), BF16),
        scratch_shapes=[pltpu.VMEM((nq, tq, tq), I32), pltpu.VMEM((nq, tq, tq), BF16),
                        pltpu.VMEM((nq, tq, tq), F32),
                        pltpu.VMEM((SUBLANES, tq), I32), pltpu.VMEM((SUBLANES, tq), I32),
                        pltpu.VMEM((SUBLANES, tq), I32),
                        pltpu.VMEM((N_KV_HEADS, GROUP * tq, LANES), F32),
                        pltpu.VMEM((N_KV_HEADS, GROUP * tq, 2 * HEAD_DIM), F32),
                        pltpu.VMEM((N_KV_HEADS, SUBLANES, LANES), F32)],
        compiler_params=_params(2),
        name="attn_prompt",
    )(q, kvb, qi, kib, idx)


DECODE_PAGES_PER_ROW = 4


def _decode_scores_kernel(pt_ref, qi_ref, wi_ref, kin_ref, cx_hbm, key_o, knew_o, xbuf, sem, *, n_pages):
    b = pl.program_id(0)
    nb = pl.num_programs(0)
    ppr = DECODE_PAGES_PER_ROW
    nrow = n_pages // ppr

    def page_copy(bb, p, slot):
        return pltpu.make_async_copy(cx_hbm.at[pt_ref[bb, p]], xbuf.at[slot, p], sem.at[slot])

    def start_batch(bb, slot):
        def body(p, carry):
            page_copy(bb, p, slot).start()
            return carry
        lax.fori_loop(0, n_pages, body, 0)

    def wait_batch(bb, slot):
        def body(p, carry):
            page_copy(bb, p, slot).wait()
            return carry
        lax.fori_loop(0, n_pages, body, 0)

    @pl.when(b == 0)
    def _():
        start_batch(0, 0)

    @pl.when(b + 1 < nb)
    def _():
        start_batch(b + 1, (b + 1) % 2)

    slot = b % 2
    wait_batch(b, slot)

    qi8 = qi_ref[0]
    wi8 = wi_ref[0] * IDX_SCALE

    def score_row(r, carry):
        kx = jnp.concatenate([xbuf[slot, r * ppr + k] for k in range(ppr)], axis=1).astype(BF16)
        d = _dot(qi8, kx)
        sc = jnp.sum(jnp.maximum(d, 0.0) * wi8, axis=0, keepdims=True)
        key_o[0, pl.ds(r, 1), :] = _sort_key(sc)
        return carry

    lax.fori_loop(0, nrow, score_row, 0, unroll=4)
    dn = jnp.sum(qi8.astype(F32) * kin_ref[0].astype(F32), axis=1, keepdims=True)
    key_new = _sort_key(jnp.sum(jnp.maximum(dn, 0.0) * wi8, axis=0, keepdims=True))
    knew_o[0] = jnp.broadcast_to(key_new, (1, LANES))


def _decode_scores(page_table, qi8, wi8, kin, cxt):
    db, n_pages = page_table.shape
    nrow = n_pages // DECODE_PAGES_PER_ROW
    w = DECODE_PAGES_PER_ROW * PAGE_SIZE
    per_b = lambda a: pl.BlockSpec((1,) + a.shape[1:], lambda b, pt: (b, 0, 0))
    grid_spec = pltpu.PrefetchScalarGridSpec(
        num_scalar_prefetch=1,
        grid=(db,),
        in_specs=[per_b(qi8), per_b(wi8), per_b(kin), pl.BlockSpec(memory_space=pl.ANY)],
        out_specs=[pl.BlockSpec((1, nrow, w), lambda b, pt: (b, 0, 0)),
                   pl.BlockSpec((1, 1, LANES), lambda b, pt: (b, 0, 0))],
        scratch_shapes=[pltpu.VMEM((2, n_pages, IDX_DIM, PAGE_SIZE), F32), pltpu.SemaphoreType.DMA((2,))],
    )
    return pl.pallas_call(
        functools.partial(_decode_scores_kernel, n_pages=n_pages),
        grid_spec=grid_spec,
        out_shape=[jax.ShapeDtypeStruct((db, nrow, w), I32), jax.ShapeDtypeStruct((db, 1, LANES), I32)],
        compiler_params=_params(1),
        name="decode_scores",
    )(page_table, qi8, wi8, kin, cxt)


def _decode_pick_kernel(keys_ref, knew_ref, pt_ref, meta_o, *, n_keep):
    nb, nrow, w = keys_ref.shape
    past = nrow * w
    n_pages = pt_ref.shape[1]
    keys = keys_ref[...]
    key_new = knew_ref[...][:, :, 0:1]
    col = (lax.broadcasted_iota(I32, keys.shape, 1) * w + lax.broadcasted_iota(I32, keys.shape, 2))

    def total(x):
        c = x.astype(I32)
        part = c[:, :, :LANES]
        for k in range(1, w // LANES):
            part = part + c[:, :, k * LANES:(k + 1) * LANES]
        return jnp.sum(jnp.sum(part, axis=1, keepdims=True), axis=2, keepdims=True)

    def bit_body(bi, t):
        cand = t + lax.shift_left(jnp.int32(1), 31 - bi)
        cnt = total(keys >= cand) + (key_new >= cand).astype(I32)
        return jnp.where(cnt >= n_keep, cand, t)

    t = lax.fori_loop(0, 32, bit_body, jnp.full((nb, 1, 1), INT_MIN, I32))
    need = n_keep - (total(keys > t) + (key_new > t).astype(I32))
    nbits = past.bit_length()

    def jbit(bi, jj):
        cand = jj | lax.shift_left(jnp.int32(1), nbits - 1 - bi)
        g = total((keys == t) & (col < cand)) + ((key_new == t) & (past < cand)).astype(I32)
        return jnp.where(g < need, cand, jj)

    jj = lax.fori_loop(0, nbits, jbit, jnp.zeros((nb, 1, 1), I32))
    sel = (keys > t) | ((keys == t) & (col <= jj))
    sel_new = ((key_new > t) | ((key_new == t) & (past <= jj))).astype(I32)

    rows = nb * nrow
    incl_lane = (lax.broadcasted_iota(I32, (w, w), 0) <= lax.broadcasted_iota(I32, (w, w), 1)).astype(BF16)
    cnt_in_row = _dot(sel.astype(BF16).reshape(rows, w), incl_lane)
    row_tot = cnt_in_row[:, w - 1:w]
    ri = lax.broadcasted_iota(I32, (rows, rows), 0)
    rj = lax.broadcasted_iota(I32, (rows, rows), 1)
    earlier_row_same_seq = ((ri // nrow) == (rj // nrow)) & (rj < ri)
    row_off = _dot(earlier_row_same_seq.astype(BF16),
                   jnp.broadcast_to(row_tot, (rows, LANES)).astype(BF16))[:, 0:1]
    cnt3 = cnt_in_row.astype(BF16).reshape(nb, nrow, w)
    off3 = row_off.reshape(nb, nrow, 1)
    incl3 = (row_off + row_tot).reshape(nb, nrow, 1)
    jl = lax.broadcasted_iota(I32, (1, 1, n_keep), 2).astype(F32)
    row_j = jnp.sum((incl3 <= jl).astype(F32), axis=1, keepdims=True)
    onehot = lax.broadcasted_iota(I32, (nb, nrow, n_keep), 1).astype(F32) == row_j
    off_j = jnp.sum(jnp.where(onehot, off3, 0.0), axis=1, keepdims=True)
    local_rank = jl - off_j
    onehot_b = onehot.astype(BF16)
    page_i = lax.broadcasted_iota(I32, (n_pages, n_keep), 0)
    row_i = lax.broadcasted_iota(I32, (SUBLANES, n_keep), 0)
    for b in range(nb):
        cnt_j = lax.dot_general(cnt3[b], onehot_b[b], (((0,), (0,)), ((), ())),
                                preferred_element_type=F32)
        lane_j = jnp.sum((cnt_j <= local_rank[b]).astype(F32), axis=0, keepdims=True)
        pos = jnp.minimum((row_j[b] * w + lane_j).astype(I32), past - 1)
        phys = jnp.sum(jnp.where(page_i == pos // PAGE_SIZE, pt_ref[b], 0.0), axis=0, keepdims=True)
        tile = (phys.astype(I32) * PAGE_SIZE + pos % PAGE_SIZE) // (SUBLANES // N_KV_HEADS)
        meta_o[b] = jnp.where(row_i == 0, tile,
                              jnp.where(row_i == 1, sel_new[b], pos % (SUBLANES // N_KV_HEADS)))


def _decode_pick(keys, knew, pt_f):
    nb, nrow, w = keys.shape
    n_keep = min(TOPK_KEYS, (nrow * w + 1) // 4)
    full = lambda a: pl.BlockSpec(a.shape, lambda i: (0,) * a.ndim)
    return pl.pallas_call(
        functools.partial(_decode_pick_kernel, n_keep=n_keep),
        grid=(1,),
        in_specs=[full(keys), full(knew), full(pt_f)],
        out_specs=pl.BlockSpec((nb, SUBLANES, n_keep), lambda i: (0, 0, 0)),
        out_shape=jax.ShapeDtypeStruct((nb, SUBLANES, n_keep), I32),
        compiler_params=_params(1),
        name="decode_pick",
    )(keys, knew, pt_f)


def _decode_attend_kernel(tile_ref, q_ref, meta_ref, kn_ref, vn_ref, ck_hbm, cv_hbm, o_ref,
                          kbuf, vbuf, expand_ref, sem, *, n_keep):
    b = pl.program_id(0)
    nb = pl.num_programs(0)
    ncol = n_keep * SUBLANES

    def item_copies(bb, j, slot):
        src = pl.ds(pl.multiple_of(tile_ref[bb, j] * SUBLANES, SUBLANES), SUBLANES)
        dst = pl.ds(pl.multiple_of(j * SUBLANES, SUBLANES), SUBLANES)
        return (pltpu.make_async_copy(ck_hbm.at[src], kbuf.at[slot, dst], sem.at[0, slot]),
                pltpu.make_async_copy(cv_hbm.at[src], vbuf.at[slot, dst], sem.at[1, slot]))

    def start_batch(bb, slot):
        def body(j, carry):
            for prio, cp in enumerate(item_copies(bb, j, slot)):
                cp.start(priority=prio)
            return carry
        lax.fori_loop(0, n_keep, body, 0)

    def wait_batch(slot):
        pltpu.make_async_copy(ck_hbm.at[pl.ds(0, ncol)], kbuf.at[slot], sem.at[0, slot]).wait()
        pltpu.make_async_copy(cv_hbm.at[pl.ds(0, ncol)], vbuf.at[slot], sem.at[1, slot]).wait()

    @pl.when(b == 0)
    def _():
        start_batch(0, 0)
        item_of_col = lax.broadcasted_iota(I32, (n_keep, ncol), 1) // SUBLANES
        expand_ref[...] = (item_of_col == lax.broadcasted_iota(I32, (n_keep, ncol), 0)).astype(BF16)

    @pl.when(b + 1 < nb)
    def _():
        start_batch(b + 1, (b + 1) % 2)

    slot = b % 2
    wait_batch(slot)

    q8 = q_ref[0]
    meta = meta_ref[0]
    new_kept = meta[1:2, 0:1] > 0
    sub_col = _dot(meta.astype(BF16), expand_ref[...])[2:3, :]
    head_i = lax.broadcasted_iota(I32, (N_HEADS, ncol), 0)
    col_i = lax.broadcasted_iota(I32, (N_HEADS, ncol), 1)
    want = sub_col * N_KV_HEADS + (head_i // GROUP).astype(F32)
    is_new_item = new_kept & (col_i // SUBLANES == n_keep - 1)
    ok = ((col_i % SUBLANES).astype(F32) == want) & jnp.logical_not(is_new_item)

    s = jnp.where(ok, _dot_nt(q8, kbuf[slot].astype(BF16)), NEG)
    s_new = jnp.sum(q8.astype(F32) * kn_ref[0].astype(F32), axis=1, keepdims=True)
    s_new = jnp.where(new_kept, s_new, NEG)
    m = jnp.maximum(jnp.max(s, axis=1, keepdims=True), s_new)
    p = jnp.exp(s - m)
    p_new = jnp.exp(s_new - m)
    l = jnp.sum(p, axis=1, keepdims=True) + p_new
    acc = _dot(p.astype(BF16), vbuf[slot].astype(BF16)) + p_new.astype(BF16).astype(F32) * vn_ref[0].astype(F32)
    o_ref[0] = (acc / l).astype(BF16)


def _decode_attend(tiles, q8, meta, kn8, vn8, ck2, cv2):
    db, n_keep = tiles.shape
    per_b = lambda a: pl.BlockSpec((1,) + a.shape[1:], lambda b, tl: (b, 0, 0))
    any_spec = pl.BlockSpec(memory_space=pl.ANY)
    ncol = n_keep * SUBLANES
    grid_spec = pltpu.PrefetchScalarGridSpec(
        num_scalar_prefetch=1,
        grid=(db,),
        in_specs=[per_b(q8), per_b(meta), per_b(kn8), per_b(vn8), any_spec, any_spec],
        out_specs=pl.BlockSpec((1, N_HEADS, HEAD_DIM), lambda b, tl: (b, 0, 0)),
        scratch_shapes=[pltpu.VMEM((2, ncol, HEAD_DIM), F32), pltpu.VMEM((2, ncol, HEAD_DIM), F32),
                        pltpu.VMEM((n_keep, ncol), BF16), pltpu.SemaphoreType.DMA((2, 2))],
    )
    return pl.pallas_call(
        functools.partial(_decode_attend_kernel, n_keep=n_keep),
        grid_spec=grid_spec,
        out_shape=jax.ShapeDtypeStruct((db, N_HEADS, HEAD_DIM), BF16),
        compiler_params=_params(1),
        name="decode_attend",
    )(tiles, q8, meta, kn8, vn8, ck2, cv2)


def _layer_norm(r, g, b):
    mu = jnp.mean(r, axis=-1, keepdims=True)
    d = r - mu
    var = jnp.mean(d * d, axis=-1, keepdims=True)
    return d * lax.rsqrt(var + LN_EPS) * g + b


def _route(logits_t, rbias):
    n_exp, tm = logits_t.shape
    epg = n_exp // N_GROUPS
    s = jax.nn.sigmoid(logits_t)
    sb = s + rbias
    ie = lax.broadcasted_iota(I32, (epg, tm), 0)
    gs_rows = []
    for g in range(N_GROUPS):
        blk = sb[g * epg:(g + 1) * epg, :]
        m1 = jnp.max(blk, axis=0, keepdims=True)
        i1 = jnp.min(jnp.where(blk == m1, ie, epg), axis=0, keepdims=True)
        m2 = jnp.max(jnp.where(ie == i1, -jnp.inf, blk), axis=0, keepdims=True)
        gs_rows.append(m1 + m2)
    picked = [jnp.zeros((1, tm), jnp.bool_) for _ in range(N_GROUPS)]
    cur = list(gs_rows)
    for _ in range(TOPK_GROUPS):
        mx = cur[0]
        for g in range(1, N_GROUPS):
            mx = jnp.maximum(mx, cur[g])
        found = jnp.zeros((1, tm), jnp.bool_)
        for g in range(N_GROUPS):
            hit = (cur[g] == mx) & jnp.logical_not(found)
            found = found | hit
            picked[g] = picked[g] | hit
            cur[g] = jnp.where(hit, -jnp.inf, cur[g])
    masked = jnp.concatenate(
        [jnp.where(picked[g], sb[g * epg:(g + 1) * epg, :], NEG) for g in range(N_GROUPS)], axis=0)
    iall = lax.broadcasted_iota(I32, (n_exp, tm), 0)
    e_rows, w_rows = [], []
    for _ in range(TOP_K):
        mx = jnp.max(masked, axis=0, keepdims=True)
        ix = jnp.min(jnp.where(masked == mx, iall, n_exp), axis=0, keepdims=True)
        hit = iall == ix
        w_rows.append(jnp.sum(jnp.where(hit, s, 0.0), axis=0, keepdims=True))
        e_rows.append(ix)
        masked = jnp.where(hit, -jnp.inf, masked)
    wsum = w_rows[0]
    for w in w_rows[1:]:
        wsum = wsum + w
    gates = [w / wsum * ROUTED_SCALE for w in w_rows]
    return jnp.concatenate(e_rows, axis=0), jnp.concatenate(gates, axis=0)


def _post_attn_kernel(xp, xs, ap, as_, cp, cs, gap, gas, gcp, gcs, woa, woc, wo, g1, b1, rwt_hi, rwt_lo, rb,
                      h2_o, e_o, gate_o, *, n_prompt_tiles):
    i = pl.program_id(0)
    is_p = i < n_prompt_tiles
    pick = lambda a, b: jnp.where(is_p, a[...], b[...])
    a = _dot(pick(ap, as_), woa[...])
    c = _dot(pick(cp, cs), woc[...])
    merged = pick(gap, gas).astype(F32) * a + pick(gcp, gcs).astype(F32) * c
    r = ALPHA * pick(xp, xs) + _dot(merged.astype(BF16), wo[...])
    h = _layer_norm(r, g1[...], b1[...])
    tm = h.shape[0]
    for k in range(D_MODEL // LANES):
        h2_o[pl.ds(k, tm, stride=SUBLANES), :] = h[:, k * LANES:(k + 1) * LANES]
    h_hi = h.astype(BF16)
    h_lo = (h - h_hi.astype(F32)).astype(BF16)
    logits_t = _dot_nt(rwt_hi[...], h_hi) + (_dot_nt(rwt_hi[...], h_lo) + _dot_nt(rwt_lo[...], h_hi))
    e_idx, gates = _route(logits_t, rb[...])
    e_o[...] = e_idx
    gate_o[...] = gates


def _post_attn(x_p, x_s, attn_p, attn_s, conv_p, conv_s, ga_p, ga_s, gc_p, gc_s,
               woa, woc, wo, g1, b1, rwt, rb):
    rwt_hi = rwt.astype(BF16)
    rwt_lo = (rwt - rwt_hi.astype(F32)).astype(BF16)
    tp, ts = x_p.shape[0], x_s.shape[0]
    tm = TOKEN_TILE
    npt, nst = tp // tm, ts // tm
    n_tok = tp + ts
    p_spec = pl.BlockSpec((tm, D_MODEL), lambda i: (jnp.minimum(i, npt - 1), 0))
    s_spec = pl.BlockSpec((tm, D_MODEL), lambda i: (jnp.maximum(i - npt, 0), 0))
    consts = [woa, woc, wo, g1, b1, rwt_hi, rwt_lo, rb]
    return pl.pallas_call(
        functools.partial(_post_attn_kernel, n_prompt_tiles=npt),
        grid=(npt + nst,),
        in_specs=[p_spec, s_spec] * 5 + [_const_spec(c.shape) for c in consts],
        out_specs=[pl.BlockSpec((tm * SUBLANES, LANES), lambda i: (i, 0)),
                   pl.BlockSpec((TOP_K, tm), lambda i: (0, i)),
                   pl.BlockSpec((TOP_K, tm), lambda i: (0, i))],
        out_shape=[jax.ShapeDtypeStruct((n_tok * SUBLANES, LANES), F32),
                   jax.ShapeDtypeStruct((TOP_K, n_tok), I32),
                   jax.ShapeDtypeStruct((TOP_K, n_tok), F32)],
        compiler_params=_params(1),
        name="post_attn",
    )(x_p, x_s, attn_p, attn_s, conv_p, conv_s, ga_p, ga_s, gc_p, gc_s, *consts)


def _rank_kernel(e_ref, rank_o, cnt_o, carry_ref):
    i = pl.program_id(0)

    @pl.when(i == 0)
    def _():
        carry_ref[...] = jnp.zeros_like(carry_ref)

    n_exp = carry_ref.shape[0]
    e = e_ref[...]
    tk = e.shape[1]
    ie = lax.broadcasted_iota(I32, (n_exp, tk), 0)
    onehot = jnp.zeros((n_exp, tk), F32)
    for j in range(TOP_K):
        onehot = onehot + (ie == e[j:j + 1, :]).astype(F32)
    before = (lax.broadcasted_iota(I32, (tk, tk), 0) < lax.broadcasted_iota(I32, (tk, tk), 1)).astype(BF16)
    prefix = _dot(onehot.astype(BF16), before) + carry_ref[:, 0:1]
    rows = [jnp.sum(jnp.where(ie == e[j:j + 1, :], prefix, 0.0), axis=0, keepdims=True) for j in range(TOP_K)]
    rank_o[...] = jnp.concatenate(rows, axis=0).astype(I32)
    carry_ref[...] = carry_ref[...] + jnp.sum(onehot, axis=1, keepdims=True)
    cnt_o[...] = carry_ref[...].astype(I32)


def _rank(e_t, n_exp):
    n_tok = e_t.shape[1]
    tk = TOKEN_TILE
    return pl.pallas_call(
        _rank_kernel,
        grid=(n_tok // tk,),
        in_specs=[pl.BlockSpec((TOP_K, tk), lambda i: (0, i))],
        out_specs=[pl.BlockSpec((TOP_K, tk), lambda i: (0, i)),
                   pl.BlockSpec((n_exp, LANES), lambda i: (0, 0))],
        out_shape=[jax.ShapeDtypeStruct((TOP_K, n_tok), I32), jax.ShapeDtypeStruct((n_exp, LANES), I32)],
        scratch_shapes=[pltpu.VMEM((n_exp, LANES), F32)],
        compiler_params=_params(1),
        name="moe_rank",
    )(e_t)


def _slot_kernel(e_ref, rank_ref, pstart_ref, slot_o):
    e = e_ref[...]
    n_exp = pstart_ref.shape[0]
    tk = e.shape[1]
    ie = lax.broadcasted_iota(I32, (n_exp, tk), 0)
    ps = pstart_ref[:, 0:1]
    rows = [jnp.sum(jnp.where(ie == e[j:j + 1, :], ps, 0.0), axis=0, keepdims=True) for j in range(TOP_K)]
    slot_o[...] = rank_ref[...] + jnp.concatenate(rows, axis=0).astype(I32)


def _slots(e_t, rank_t, pstart_f):
    n_tok = e_t.shape[1]
    tk = TOKEN_TILE
    spec = pl.BlockSpec((TOP_K, tk), lambda i: (0, i))
    return pl.pallas_call(
        _slot_kernel,
        grid=(n_tok // tk,),
        in_specs=[spec, spec, _const_spec(pstart_f.shape)],
        out_specs=spec,
        out_shape=jax.ShapeDtypeStruct((TOP_K, n_tok), I32),
        compiler_params=_params(1),
        name="moe_slots",
    )(e_t, rank_t, pstart_f)


def _dispatch_kernel(pend_ref, padded_ref, slot_ref, h2_ref, xs_hbm, zbuf, sem, zsem, *, n_exp):
    i = pl.program_id(0)
    td = slot_ref.shape[0] // TOP_K
    blk_rows = MOE_BLOCK * SUBLANES

    def zero_copy(e):
        dst = pl.ds(pl.multiple_of((pend_ref[e] - MOE_BLOCK) * SUBLANES, SUBLANES), blk_rows)
        return pltpu.make_async_copy(zbuf, xs_hbm.at[dst], zsem)

    @pl.when(i == 0)
    def _():
        zbuf[...] = jnp.zeros_like(zbuf)

        def start(e, carry):
            @pl.when(padded_ref[e] > 0)
            def _():
                zero_copy(e).start()
            return carry

        def wait(e, carry):
            @pl.when(padded_ref[e] > 0)
            def _():
                zero_copy(e).wait()
            return carry

        lax.fori_loop(0, n_exp, start, 0)
        lax.fori_loop(0, n_exp, wait, 0)

        def tail_copy(b):
            return pltpu.make_async_copy(zbuf, xs_hbm.at[pl.ds(pl.multiple_of(b * blk_rows, blk_rows), blk_rows)], zsem)

        def tail_start(b, carry):
            tail_copy(b).start()
            return carry

        def tail_wait(b, carry):
            tail_copy(b).wait()
            return carry

        first_unused = pend_ref[n_exp - 1] // MOE_BLOCK
        lax.fori_loop(first_unused, xs_hbm.shape[0] // blk_rows, tail_start, 0)
        lax.fori_loop(first_unused, xs_hbm.shape[0] // blk_rows, tail_wait, 0)

    def row_copy(t, j):
        src = pl.ds(pl.multiple_of(t * SUBLANES, SUBLANES), SUBLANES)
        dst = pl.ds(pl.multiple_of(slot_ref[t * TOP_K + j] * SUBLANES, SUBLANES), SUBLANES)
        return pltpu.make_async_copy(h2_ref.at[src], xs_hbm.at[dst], sem)

    def start_tok(t, carry):
        for j in range(TOP_K):
            row_copy(t, j).start(priority=j % 2)
        return carry

    lax.fori_loop(0, td, start_tok, 0)
    for j in range(TOP_K):
        pltpu.make_async_copy(h2_ref, xs_hbm.at[pl.ds(0, td * SUBLANES)], sem).wait()


def _dispatch(pend, padded, slot_flat, h2, m_pad):
    n_tok = slot_flat.shape[0] // TOP_K
    td = TOKEN_TILE
    n_exp = pend.shape[0]
    grid_spec = pltpu.PrefetchScalarGridSpec(
        num_scalar_prefetch=2,
        grid=(n_tok // td,),
        in_specs=[pl.BlockSpec((td * TOP_K,), lambda i, *_: (i,), memory_space=pltpu.SMEM),
                  pl.BlockSpec((td * SUBLANES, LANES), lambda i, *_: (i, 0))],
        out_specs=pl.BlockSpec(memory_space=pl.ANY),
        scratch_shapes=[pltpu.VMEM((MOE_BLOCK * SUBLANES, LANES), F32),
                        pltpu.SemaphoreType.DMA(()), pltpu.SemaphoreType.DMA(())],
    )
    return pl.pallas_call(
        functools.partial(_dispatch_kernel, n_exp=n_exp),
        grid_spec=grid_spec,
        out_shape=jax.ShapeDtypeStruct((m_pad * SUBLANES, LANES), F32),
        compiler_params=_params(1),
        name="moe_dispatch",
    )(pend, padded, slot_flat, h2)


def _tile_rows(ref, n):
    return jnp.concatenate([ref[pl.ds(k, n, stride=SUBLANES), :] for k in range(D_MODEL // LANES)], axis=1)


PLAN_EXPERT, PLAN_SLOT, PLAN_NEXT, PLAN_HAS_NEXT = range(4)


def _expert_kernel(plan_ref, nused_ref, x_ref, wg_hbm, wu_hbm, wd_hbm, y_ref, wg_f, wu_f, wd_f, wgu_b, wd_b, sem):
    i = pl.program_id(0)
    d_exp = wd_b.shape[0]

    def weight_copies(e, s):
        return (pltpu.make_async_copy(wg_hbm.at[e], wg_f.at[s], sem.at[s]),
                pltpu.make_async_copy(wu_hbm.at[e], wu_f.at[s], sem.at[s]),
                pltpu.make_async_copy(wd_hbm.at[e], wd_f.at[s], sem.at[s]))

    @pl.when(i < nused_ref[0])
    def _():
        e = plan_ref[PLAN_EXPERT, i]
        changed = (i == 0) | (e != plan_ref[PLAN_EXPERT, jnp.maximum(i - 1, 0)])

        @pl.when(changed)
        def _():
            s = plan_ref[PLAN_SLOT, i]

            @pl.when(i == 0)
            def _():
                for cp in weight_copies(e, s):
                    cp.start()

            for cp in weight_copies(e, s):
                cp.wait()

            @pl.when(plan_ref[PLAN_HAS_NEXT, i] > 0)
            def _():
                for cp in weight_copies(plan_ref[PLAN_NEXT, i], 1 - s):
                    cp.start()

            wgu_b[:, :d_exp] = wg_f[s].astype(BF16)
            wgu_b[:, d_exp:] = wu_f[s].astype(BF16)
            wd_b[...] = wd_f[s].astype(BF16)

        x = _tile_rows(x_ref, MOE_BLOCK).astype(BF16)
        gu = _dot(x, wgu_b[...])
        hh = jax.nn.silu(gu[:, :d_exp]) * gu[:, d_exp:]
        y = _dot(hh.astype(BF16), wd_b[...])
        for k in range(D_MODEL // LANES):
            y_ref[pl.ds(k, MOE_BLOCK, stride=SUBLANES), :] = y[:, k * LANES:(k + 1) * LANES]

    @pl.when(i >= nused_ref[0])
    def _():
        y_ref[...] = jnp.zeros_like(y_ref)


def _expert_plan(block_e, nused):
    n_blocks = block_e.shape[0]
    idx = jnp.arange(n_blocks, dtype=I32)
    changed = (idx == 0) | (block_e != jnp.roll(block_e, 1))
    slot = (jnp.cumsum(changed.astype(I32)) - 1) % 2
    change_pos = jnp.where(changed & (idx < nused), idx, n_blocks)
    next_pos = jnp.concatenate([lax.cummin(change_pos[::-1])[::-1][1:], jnp.full((1,), n_blocks, I32)])
    has_next = (next_pos < n_blocks).astype(I32)
    nxt = block_e[jnp.minimum(next_pos, n_blocks - 1)]
    return jnp.stack([block_e, slot, nxt, has_next]).astype(I32)


def _experts(block_e, nused, xs, w_gate, w_up, w_down):
    n_blocks = block_e.shape[0]
    d_exp = w_gate.shape[2]
    rows = MOE_BLOCK * SUBLANES
    last = lambda i, nu: jnp.minimum(i, nu[0] - 1)
    any_spec = pl.BlockSpec(memory_space=pl.ANY)
    grid_spec = pltpu.PrefetchScalarGridSpec(
        num_scalar_prefetch=2,
        grid=(n_blocks,),
        in_specs=[pl.BlockSpec((rows, LANES), lambda i, plan, nu: (last(i, nu), 0)), any_spec, any_spec, any_spec],
        out_specs=pl.BlockSpec((rows, LANES), lambda i, plan, nu: (i, 0)),
        scratch_shapes=[pltpu.VMEM((2, D_MODEL, d_exp), F32), pltpu.VMEM((2, D_MODEL, d_exp), F32),
                        pltpu.VMEM((2, d_exp, D_MODEL), F32),
                        pltpu.VMEM((D_MODEL, 2 * d_exp), BF16), pltpu.VMEM((d_exp, D_MODEL), BF16),
                        pltpu.SemaphoreType.DMA((2,))],
    )
    return pl.pallas_call(
        _expert_kernel,
        grid_spec=grid_spec,
        out_shape=jax.ShapeDtypeStruct(xs.shape, F32),
        compiler_params=_params(1),
        name="moe_experts",
    )(_expert_plan(block_e, nused[0]), nused, xs, w_gate, w_up, w_down)


def _combine_kernel(slot_ref, slot_next_ref, gate_ref, h2_ref, ys_hbm, shg, shu, shd, g2, b2, yp_o, ys_o,
                    buf, base_ref, y_ref, sem, *, n_prompt_tiles):
    i = pl.program_id(0)
    n = pl.num_programs(0)
    tc = gate_ref.shape[0]

    def start_tile(s_ref, bslot):
        def body(t, carry):
            dst = pl.ds(pl.multiple_of(t * SUBLANES, SUBLANES), SUBLANES)
            for j in range(TOP_K):
                src = pl.ds(pl.multiple_of(s_ref[t * TOP_K + j] * SUBLANES, SUBLANES), SUBLANES)
                pltpu.make_async_copy(ys_hbm.at[src], buf.at[bslot, j, dst], sem.at[bslot]).start(priority=j % 2)
            return carry
        lax.fori_loop(0, tc, body, 0)

    def wait_tile(bslot):
        for j in range(TOP_K):
            pltpu.make_async_copy(ys_hbm.at[pl.ds(0, tc * SUBLANES)], buf.at[bslot, j], sem.at[bslot]).wait()

    @pl.when(i == 0)
    def _():
        start_tile(slot_ref, 0)

    cur = i % 2
    h = _tile_rows(h2_ref, tc)
    hb = h.astype(BF16)
    shared = _dot((jax.nn.silu(_dot(hb, shg[...])) * _dot(hb, shu[...])).astype(BF16), shd[...])
    base_ref[...] = ALPHA * h + shared
    wait_tile(cur)
    gain, bias = g2[...], b2[...]
    grp = SUBLANES

    def finish_group(g):
        r0 = pl.multiple_of(g * grp, grp)
        t0 = pl.multiple_of(g * grp * SUBLANES, grp * SUBLANES)
        routed = jnp.zeros((grp, D_MODEL), F32)
        for j in range(TOP_K):
            rows = jnp.concatenate([buf[cur, j, pl.ds(t0 + k, grp, stride=SUBLANES), :]
                                    for k in range(D_MODEL // LANES)], axis=1)
            routed = routed + rows * gate_ref[pl.ds(r0, grp), j:j + 1]
        y_ref[pl.ds(r0, grp), :] = base_ref[pl.ds(r0, grp), :] + routed

    @pl.when(i + 1 < n)
    def _():
        nxt = (i + 1) % 2

        def body(g, carry):
            for tt in range(grp):
                t = g * grp + tt
                dst = pl.ds(pl.multiple_of(t * SUBLANES, SUBLANES), SUBLANES)
                for j in range(TOP_K):
                    src = pl.ds(pl.multiple_of(slot_next_ref[t * TOP_K + j] * SUBLANES, SUBLANES), SUBLANES)
                    pltpu.make_async_copy(ys_hbm.at[src], buf.at[nxt, j, dst], sem.at[nxt]).start(priority=j % 2)
            finish_group(g)
            return carry

        lax.fori_loop(0, tc // grp, body, 0)

    @pl.when(i + 1 >= n)
    def _():
        def body(g, carry):
            finish_group(g)
            return carry

        lax.fori_loop(0, tc // grp, body, 0)

    y = _layer_norm(y_ref[...], gain, bias)

    @pl.when(i < n_prompt_tiles)
    def _():
        yp_o[...] = y

    @pl.when(i >= n_prompt_tiles)
    def _():
        ys_o[...] = y


def _combine(slot_flat, gates, h2, ys, shg, shu, shd, g2, b2, n_prompt, n_decode):
    tc = TOKEN_TILE
    npt, nst = n_prompt // tc, n_decode // tc
    consts = [shg, shu, shd, g2, b2]
    return pl.pallas_call(
        functools.partial(_combine_kernel, n_prompt_tiles=npt),
        grid=(npt + nst,),
        in_specs=[pl.BlockSpec((tc * TOP_K,), lambda i: (i,), memory_space=pltpu.SMEM),
                  pl.BlockSpec((tc * TOP_K,), lambda i: (jnp.minimum(i + 1, npt + nst - 1),),
                               memory_space=pltpu.SMEM),
                  pl.BlockSpec((tc, TOP_K), lambda i: (i, 0)),
                  pl.BlockSpec((tc * SUBLANES, LANES), lambda i: (i, 0)),
                  pl.BlockSpec(memory_space=pl.ANY)] + [_const_spec(c.shape) for c in consts],
        out_specs=[pl.BlockSpec((tc, D_MODEL), lambda i: (jnp.minimum(i, npt - 1), 0)),
                   pl.BlockSpec((tc, D_MODEL), lambda i: (jnp.maximum(i - npt, 0), 0))],
        out_shape=[jax.ShapeDtypeStruct((n_prompt, D_MODEL), F32),
                   jax.ShapeDtypeStruct((n_decode, D_MODEL), F32)],
        scratch_shapes=[pltpu.VMEM((2, TOP_K, tc * SUBLANES, LANES), F32), pltpu.VMEM((tc, D_MODEL), F32),
                        pltpu.VMEM((tc, D_MODEL), F32), pltpu.SemaphoreType.DMA((2,))],
        compiler_params=_params(1),
        name="moe_combine",
    )(slot_flat, slot_flat, gates, h2, ys, *consts)


def _pad_rows(a, n):
    return jnp.pad(a, ((0, n - a.shape[0]),) + ((0, 0),) * (a.ndim - 1))


def kernel(x_prompt, x_sample, cache_k, cache_v, cache_kidx, page_table, state_conv, w_in, conv_w, w_o_attn,
           w_o_conv, w_o, ln1_g, ln1_b, router_w, router_bias, moe_w_gate, moe_w_up, moe_w_down,
           shared_w_gate, shared_w_up, shared_w_down, ln2_g, ln2_b):
    nb, seq, _ = x_prompt.shape
    db = x_sample.shape[0]
    n_pool = cache_k.shape[1]
    n_exp = router_w.shape[-1]
    tp = nb * seq
    ts = TOKEN_TILE
    nkv = N_KV_HEADS * HEAD_DIM

    weights = _prep_in_weights(w_in[0])
    cw = conv_w[0]

    xp2 = x_prompt.reshape(tp, D_MODEL)
    (q_p, k_p, v_p, kvb_p, qi_p, idx_p, kib_p, conv_p, ga_p, gc_p, ulast_p) = _in_proj_prompt(xp2, weights, cw, nb)
    xs2 = _pad_rows(x_sample.reshape(db, D_MODEL), ts)
    prev2 = _pad_rows(state_conv[0, :, 0, :], ts)
    prev1 = _pad_rows(state_conv[0, :, 1, :], ts)
    (q_s, k_s, v_s, _, qi_s, idx_s, _, conv_s, ga_s, gc_s, u_s) = _in_proj_decode(xs2, prev2, prev1, weights, cw)

    attn_p = _attn_prompt(q_p, kvb_p, qi_p, kib_p, idx_p, nb)
    q8 = q_s[:db].reshape(db, N_HEADS, HEAD_DIM)
    qi8 = qi_s[:db].reshape(db, IDX_HEADS, LANES)[:, :, :IDX_DIM]
    wi8 = idx_s[:db, IDX_DIM:IDX_DIM + IDX_HEADS].reshape(db, IDX_HEADS, 1)
    kin = idx_s[:db, :IDX_DIM].astype(BF16).reshape(db, 1, IDX_DIM)
    k_new = k_s[:db * N_KV_HEADS].reshape(db, N_KV_HEADS, HEAD_DIM)
    v_new = v_s[:db * N_KV_HEADS].reshape(db, N_KV_HEADS, HEAD_DIM)
    kn8 = jnp.repeat(k_new, GROUP, axis=1).astype(BF16)
    vn8 = jnp.repeat(v_new, GROUP, axis=1).astype(BF16)
    ck2 = cache_k[0].reshape(n_pool * PAGE_SIZE * N_KV_HEADS, HEAD_DIM)
    cv2 = cache_v[0].reshape(n_pool * PAGE_SIZE * N_KV_HEADS, HEAD_DIM)
    keys_s, knew_s = _decode_scores(page_table, qi8, wi8, kin, jnp.swapaxes(cache_kidx[0], 1, 2))
    meta = _decode_pick(keys_s, knew_s, page_table.astype(F32)[:, :, None])
    attn_s8 = _decode_attend(meta[:, 0, :], q8, meta, kn8, vn8, ck2, cv2)
    attn_s = _pad_rows(attn_s8.reshape(db, N_HEADS * HEAD_DIM), ts)

    h2, e_t, gate_t = _post_attn(
        xp2, xs2, attn_p, attn_s, conv_p, conv_s, ga_p, ga_s, gc_p, gc_s,
        w_o_attn[0].astype(BF16), w_o_conv[0].astype(BF16), w_o[0].astype(BF16),
        ln1_g[0].reshape(1, D_MODEL), ln1_b[0].reshape(1, D_MODEL),
        router_w[0].T, router_bias[0].reshape(n_exp, 1))

    n_tok = tp + ts
    rank_t, cnt = _rank(e_t, n_exp)
    counts = cnt[:, 0]
    padded = (counts + MOE_BLOCK - 1) // MOE_BLOCK * MOE_BLOCK
    pend = jnp.cumsum(padded)
    pstart = pend - padded
    n_blocks = (n_tok * TOP_K + n_exp * (MOE_BLOCK - 1) + MOE_BLOCK - 1) // MOE_BLOCK
    nused = (pend[-1] // MOE_BLOCK).astype(I32)
    blk = jnp.minimum(jnp.arange(n_blocks, dtype=I32), nused - 1)
    block_e = jnp.minimum(jnp.sum((pend[None, :] <= (blk * MOE_BLOCK)[:, None]).astype(I32), axis=1), n_exp - 1)
    slot_t = _slots(e_t, rank_t, jnp.broadcast_to(pstart.astype(F32)[:, None], (n_exp, LANES)))
    slot_flat = slot_t.T.reshape(-1)
    xs = _dispatch(pend.astype(I32), padded.astype(I32), slot_flat, h2, n_blocks * MOE_BLOCK)
    ys = _experts(block_e, nused.reshape(1), xs, moe_w_gate[0], moe_w_up[0], moe_w_down[0])
    y_p, y_s = _combine(slot_flat, gate_t.T, h2, ys,
                        shared_w_gate[0].astype(BF16), shared_w_up[0].astype(BF16),
                        shared_w_down[0].astype(BF16),
                        ln2_g[0].reshape(1, D_MODEL), ln2_b[0].reshape(1, D_MODEL), tp, ts)

    conv_sample = jnp.stack([state_conv[0, :, 1, :], u_s[:db]], axis=1)[None]
    return (y_p.reshape(nb, seq, D_MODEL),
            y_s[:db].reshape(db, 1, D_MODEL),
            k_p.reshape(1, nb, seq, N_KV_HEADS, HEAD_DIM),
            v_p.reshape(1, nb, seq, N_KV_HEADS, HEAD_DIM),
            idx_p[:, :IDX_DIM].reshape(1, nb, seq, IDX_DIM),
            ulast_p[:, SUBLANES - (CONV_W - 1):, :][None],
            k_new.reshape(1, db, 1, N_KV_HEADS, HEAD_DIM),
            v_new.reshape(1, db, 1, N_KV_HEADS, HEAD_DIM),
            idx_s[:db, :IDX_DIM].reshape(1, db, 1, IDX_DIM),
            conv_sample)
```

```python
import functools

import jax
import jax.numpy as jnp
import numpy as np
from jax import lax
from jax.experimental import pallas as pl
from jax.experimental.pallas import tpu as pltpu

F32 = jnp.float32
BF16 = jnp.bfloat16
I32 = jnp.int32

D_MODEL = 1024
N_HEADS = 8
HEAD_DIM = 128
N_KV_HEADS = 2
GROUP = N_HEADS // N_KV_HEADS
IDX_HEADS = 8
IDX_DIM = 64
TOPK_KEYS = 256
IDX_SCALE = IDX_DIM ** -0.5 * IDX_HEADS ** -0.5
Q_SCALE = HEAD_DIM ** -0.5
PAGE_SIZE = 128
D_CONV = D_MODEL
CONV_W = 3
TOP_K = 8
N_GROUPS = 8
TOPK_GROUPS = 4
ROUTED_SCALE = 2.5
DEPTH = 1
ALPHA = (2 * DEPTH) ** 0.25
LN_EPS = 1e-5
NEG = -1e30
IN_WIDTHS = (N_HEADS * HEAD_DIM, N_KV_HEADS * HEAD_DIM, N_KV_HEADS * HEAD_DIM,
             IDX_HEADS * IDX_DIM, IDX_DIM, IDX_HEADS,
             D_CONV, D_CONV, D_CONV, D_MODEL, D_MODEL)

LANES = 128
SUBLANES = 8
VMEM_LIMIT_BYTES = 56 * 1024 * 1024

PROJ_ROWS = 512
ATT_Q = 256
TOKEN_TILE = 256
MOE_BLOCK = 256
CONV_CHUNK = 256
INT_MIN = -2 ** 31
SOFTMAX_DENOM_MIN = 1e-25
F32_MIN_NORMAL = float(np.finfo(np.float32).tiny)
BF16_MIN_NORMAL_BITS = 0x0080


def _np_key(v):
    b = int(np.float32(v).view(np.int32))
    return b ^ ((b >> 31) & 0x7FFFFFFF)


KEY_HALF = _np_key(NEG * 0.5)


def _sort_key(s):
    b = lax.bitcast_convert_type(s + 0.0, I32)
    return b ^ ((b >> 31) & jnp.int32(0x7FFFFFFF))


def _dot(a, b):
    return jnp.dot(a, b, preferred_element_type=F32)


def _dot_nt(a, b):
    return lax.dot_general(a, b, (((1,), (1,)), ((), ())), preferred_element_type=F32)


def _params(n_grid):
    return pltpu.CompilerParams(dimension_semantics=("arbitrary",) * n_grid,
                                vmem_limit_bytes=VMEM_LIMIT_BYTES)


def _const_spec(shape):
    nd = len(shape)
    return pl.BlockSpec(shape, lambda *_: (0,) * nd, pipeline_mode=pl.Buffered(1))


def _proj_common(xb, w1, w2, w3, wga, wgc, q_o, k_o, v_o, kv_o, qi_o, idx_o, kib_o, sga_o, sgc_o):
    nq = N_HEADS * HEAD_DIM
    nkv = N_KV_HEADS * HEAD_DIM
    z1 = _dot(xb, w1[...])
    q_o[...] = (z1[:, :nq] * Q_SCALE).astype(BF16)
    k = z1[:, nq:nq + nkv]
    v = z1[:, nq + nkv:nq + 2 * nkv]
    for g in range(N_KV_HEADS):
        k_o[pl.ds(g, k.shape[0], stride=N_KV_HEADS), :] = k[:, g * HEAD_DIM:(g + 1) * HEAD_DIM]
        v_o[pl.ds(g, v.shape[0], stride=N_KV_HEADS), :] = v[:, g * HEAD_DIM:(g + 1) * HEAD_DIM]
    kv_o[:, :nkv] = k.astype(BF16)
    kv_o[:, nkv:] = v.astype(BF16)
    qi_o[...] = _dot(xb, w2[...]).astype(BF16)
    z3 = _dot(xb, w3[...])
    idx_o[...] = z3
    kib_o[...] = z3.astype(BF16)
    sga_o[...] = jax.nn.sigmoid(_dot(xb, wga[...])).astype(BF16)
    sgc_o[...] = jax.nn.sigmoid(_dot(xb, wgc[...])).astype(BF16)


def _in_proj_prompt_kernel(x_ref, w1, w2, w3, wb, wc, wx, wga, wgc, cw_ref,
                           q_o, k_o, v_o, kv_o, qi_o, idx_o, kib_o, conv_o, sga_o, sgc_o, ulast_o,
                           tail_ref):
    j = pl.program_id(1)

    @pl.when(j == 0)
    def _():
        tail_ref[...] = jnp.zeros_like(tail_ref)

    xb = x_ref[...].astype(BF16)
    _proj_common(xb, w1, w2, w3, wga, wgc, q_o, k_o, v_o, kv_o, qi_o, idx_o, kib_o, sga_o, sgc_o)
    tm = xb.shape[0]
    row = lax.broadcasted_iota(I32, (tm, CONV_CHUNK), 0)
    for c in range(D_CONV // CONV_CHUNK):
        cs = slice(c * CONV_CHUNK, (c + 1) * CONV_CHUNK)
        gb = _dot(xb, wb[:, cs])
        u = _dot(xb, wc[:, cs]) * _dot(xb, wx[:, cs])
        p1 = tail_ref[SUBLANES - 1:SUBLANES, cs]
        p2 = tail_ref[SUBLANES - 2:SUBLANES - 1, cs]
        u1 = jnp.where(row == 0, p1, pltpu.roll(u, 1, 0))
        u2 = jnp.where(row == 0, p2, jnp.where(row == 1, p1, pltpu.roll(u, 2, 0)))
        conv = cw_ref[0:1, cs] * u2 + cw_ref[1:2, cs] * u1 + cw_ref[2:3, cs] * u
        conv_o[:, cs] = (gb * conv).astype(BF16)
        tail_ref[:, cs] = u[tm - SUBLANES:, :]
    ulast_o[0] = tail_ref[...]


def _in_proj_decode_kernel(x_ref, p2_ref, p1_ref, w1, w2, w3, wb, wc, wx, wga, wgc, cw_ref,
                           q_o, k_o, v_o, kv_o, qi_o, idx_o, kib_o, conv_o, sga_o, sgc_o, u_o):
    xb = x_ref[...].astype(BF16)
    _proj_common(xb, w1, w2, w3, wga, wgc, q_o, k_o, v_o, kv_o, qi_o, idx_o, kib_o, sga_o, sgc_o)
    for c in range(D_CONV // CONV_CHUNK):
        cs = slice(c * CONV_CHUNK, (c + 1) * CONV_CHUNK)
        gb = _dot(xb, wb[:, cs])
        u = _dot(xb, wc[:, cs]) * _dot(xb, wx[:, cs])
        conv = cw_ref[0:1, cs] * p2_ref[:, cs] + cw_ref[1:2, cs] * p1_ref[:, cs] + cw_ref[2:3, cs] * u
        conv_o[:, cs] = (gb * conv).astype(BF16)
        u_o[:, cs] = u


def _prep_in_weights(w_in):
    w = w_in.astype(BF16)
    offs = np.concatenate([[0], np.cumsum(IN_WIDTHS)])
    part = [w[:, int(offs[i]):int(offs[i + 1])] for i in range(len(IN_WIDTHS))]
    q, k, v, qi, ki, wi, gb, gc, xv, ga, gcv = part
    d = w.shape[0]
    w1 = jnp.concatenate([q, k, v], axis=1)
    w2 = jnp.pad(qi.reshape(d, IDX_HEADS, IDX_DIM),
                 ((0, 0), (0, 0), (0, LANES - IDX_DIM))).reshape(d, IDX_HEADS * LANES)
    w3 = jnp.concatenate([ki, wi, jnp.zeros((d, LANES - IDX_DIM - IDX_HEADS), BF16)], axis=1)
    return (w1, w2, w3, gb, gc, xv, ga, gcv)


def _proj_out_shapes(t):
    nkv = N_KV_HEADS * HEAD_DIM
    sd = jax.ShapeDtypeStruct
    return [sd((t, N_HEADS * HEAD_DIM), BF16), sd((t * N_KV_HEADS, HEAD_DIM), F32),
            sd((t * N_KV_HEADS, HEAD_DIM), F32), sd((t, 2 * nkv), BF16),
            sd((t, IDX_HEADS * LANES), BF16), sd((t, LANES), F32), sd((t, LANES), BF16),
            sd((t, D_CONV), BF16), sd((t, D_MODEL), BF16), sd((t, D_MODEL), BF16)]


def _in_proj_prompt(x2d, weights, conv_w, nb):
    t = x2d.shape[0]
    s = t // nb
    tm = min(PROJ_ROWS, s)
    nj = s // tm
    row_spec = lambda w, rep=1: pl.BlockSpec((tm * rep, w), lambda b, j: (b * nj + j, 0))
    out_shapes = _proj_out_shapes(t) + [jax.ShapeDtypeStruct((nb, SUBLANES, D_CONV), F32)]
    out_specs = [row_spec(o.shape[1], o.shape[0] // t) for o in out_shapes[:-1]]
    out_specs.append(pl.BlockSpec((1, SUBLANES, D_CONV), lambda b, j: (b, 0, 0)))
    return pl.pallas_call(
        _in_proj_prompt_kernel,
        grid=(nb, nj),
        in_specs=[row_spec(D_MODEL)] + [_const_spec(w.shape) for w in weights] + [_const_spec(conv_w.shape)],
        out_specs=out_specs,
        out_shape=out_shapes,
        scratch_shapes=[pltpu.VMEM((SUBLANES, D_CONV), F32)],
        compiler_params=_params(2),
        name="in_proj_prompt",
    )(x2d, *weights, conv_w)


def _in_proj_decode(x2d, p2, p1, weights, conv_w):
    t = x2d.shape[0]
    row_spec = lambda w, rep=1: pl.BlockSpec((t * rep, w), lambda i: (0, 0))
    out_shapes = _proj_out_shapes(t) + [jax.ShapeDtypeStruct((t, D_CONV), F32)]
    return pl.pallas_call(
        _in_proj_decode_kernel,
        grid=(1,),
        in_specs=[row_spec(D_MODEL)] * 3 + [_const_spec(w.shape) for w in weights] + [_const_spec(conv_w.shape)],
        out_specs=[row_spec(o.shape[1], o.shape[0] // t) for o in out_shapes],
        out_shape=out_shapes,
        compiler_params=_params(1),
        name="in_proj_decode",
    )(x2d, p2, p1, *weights, conv_w)


def _attn_prompt_kernel(q_ref, kv_ref, qi_ref, kib_ref, idx_ref, o_ref,
                        key_ref, hi_ref, bias_ref, t_ref, j_ref, need_ref, m_ref, acc_ref, knorm_ref,
                        *, n_keep, seq_len):
    tq = q_ref.shape[0]
    i = pl.program_id(1)
    nchunk = i + 1
    nsub = tq // LANES

    idx_t = jnp.transpose(idx_ref[...])
    wi_rows = [idx_t[IDX_DIM + h:IDX_DIM + h + 1, :] * IDX_SCALE for h in range(IDX_HEADS)]
    key_i = lax.broadcasted_iota(I32, (tq, tq), 0)
    qry_i = lax.broadcasted_iota(I32, (tq, tq), 1)

    def score_chunk(c, carry):
        kic = kib_ref[pl.ds(pl.multiple_of(c * tq, tq), tq), :]
        acc = jnp.zeros((tq, tq), F32)
        for h in range(IDX_HEADS):
            d = _dot_nt(kic, qi_ref[:, h * LANES:(h + 1) * LANES])
            acc = acc + jnp.maximum(d, 0.0) * wi_rows[h]
        acc = jnp.where(jnp.abs(acc) < F32_MIN_NORMAL, 0.0, acc)
        s = jnp.where((c < i) | (key_i <= qry_i), acc, NEG)
        bits = lax.bitcast_convert_type(s, I32)
        key_ref[c] = bits ^ ((bits >> 31) & jnp.int32(0x7FFFFFFF))
        hi_ref[c] = lax.bitcast_convert_type(bits & jnp.int32(-65536), F32).astype(BF16)
        return carry

    lax.fori_loop(0, nchunk, score_chunk, 0)

    sub_i = lax.broadcasted_iota(I32, (SUBLANES, tq), 0)

    def counts(preds, with_pos=False):
        def body(c, cnts):
            for kk in range(tq // SUBLANES):
                blk = key_ref[c, kk * SUBLANES:(kk + 1) * SUBLANES, :]
                pos = (c * tq + kk * SUBLANES + sub_i) if with_pos else None
                cnts = tuple(cnt + pred(blk, pos).astype(I32) for cnt, pred in zip(cnts, preds))
            return cnts
        cnts = lax.fori_loop(0, nchunk, body, tuple(jnp.zeros((SUBLANES, tq), I32) for _ in preds))
        return tuple(jnp.sum(cnt, axis=0, keepdims=True) for cnt in cnts)

    def count_hi(cand):
        rows16 = 2 * SUBLANES
        def body(c, cnt):
            for kk in range(tq // rows16):
                blk = hi_ref[c, kk * rows16:(kk + 1) * rows16, :]
                cnt = cnt + jnp.where(blk >= cand, jnp.ones_like(cnt), jnp.zeros_like(cnt))
            return cnt
        cnt = lax.fori_loop(0, nchunk, body, jnp.zeros((rows16, tq), BF16))
        return jnp.sum(cnt.astype(F32), axis=0, keepdims=True)

    def hi_bit_body(bi, p16):
        cand = p16 + lax.shift_left(jnp.int32(1), 15 - bi)
        raw = cand ^ ((cand >> 31) & jnp.int32(0x7FFF))
        raw = jnp.where((raw > 0) & (raw < BF16_MIN_NORMAL_BITS), BF16_MIN_NORMAL_BITS, raw)
        cand_f = lax.bitcast_convert_type(lax.shift_left(raw, 16), F32)
        total = count_hi(jnp.broadcast_to(cand_f, (2 * SUBLANES, tq)).astype(BF16))
        return jnp.where(total >= n_keep, cand, p16)

    p16 = lax.fori_loop(0, 16, hi_bit_body, jnp.full((1, tq), -2 ** 15, I32))

    def bit_body(bi, t):
        cand = t + lax.shift_left(jnp.int32(1), 15 - bi)
        cand_b = jnp.broadcast_to(cand, (SUBLANES, tq))
        total, = counts((lambda blk, _: blk >= cand_b,))
        return jnp.where(total >= n_keep, cand, t)

    t = lax.fori_loop(0, 16, bit_body, lax.shift_left(p16, 16))
    t_b = jnp.broadcast_to(t, (SUBLANES, tq))
    cnt_gt, cnt_eq = counts((lambda blk, _: blk > t_b, lambda blk, _: blk == t_b))
    need = n_keep - cnt_gt
    any_excess = jnp.max(((cnt_eq > need) & (t > KEY_HALF)).astype(I32))
    t_ref[...] = t_b
    need_ref[...] = jnp.broadcast_to(need, (SUBLANES, tq))
    j_ref[...] = jnp.full(j_ref.shape, seq_len, I32)

    @pl.when(any_excess > 0)
    def _():
        nbits = max(1, (seq_len - 1).bit_length())
        t_b = t_ref[...]
        need = need_ref[0:1, :]

        def jbit(bi, jj):
            cand = jj | lax.shift_left(jnp.int32(1), nbits - 1 - bi)
            cand_b = jnp.broadcast_to(cand, (SUBLANES, tq))
            g, = counts((lambda blk, pos: (blk == t_b) & (pos < cand_b),), with_pos=True)
            return jnp.where(g < need, cand, jj)

        jj = lax.fori_loop(0, nbits, jbit, jnp.zeros((1, tq), I32))
        j_ref[...] = jnp.broadcast_to(jj, (SUBLANES, tq))

    t_row = t_ref[0:1, :]
    j_row = j_ref[0:1, :]

    def bias_chunk(c, carry):
        blk = key_ref[c]
        sel = (blk > KEY_HALF) & ((blk > t_row) | ((blk == t_row) & ((c * tq + key_i) <= j_row)))
        bias_ref[c] = jnp.transpose(jnp.where(sel, 0.0, NEG))
        return carry

    lax.fori_loop(0, nchunk, bias_chunk, 0)

    nkv = N_KV_HEADS * HEAD_DIM
    gq = GROUP * tq

    def group_logits(c, g):
        rows = pl.ds(pl.multiple_of(c * tq, tq), tq)
        kc = kv_ref[rows, g * HEAD_DIM:(g + 1) * HEAD_DIM]
        qg = jnp.concatenate([q_ref[:, h * HEAD_DIM:(h + 1) * HEAD_DIM]
                              for h in range(g * GROUP, (g + 1) * GROUP)], axis=0)
        s = _dot_nt(qg, kc).reshape(GROUP, tq, tq) + bias_ref[c][None]
        return s.reshape(gq, tq)

    def exact_row_max():
        m_ref[...] = jnp.full(m_ref.shape, NEG, F32)

        def max_chunk(c, carry):
            for g in range(N_KV_HEADS):
                s = group_logits(c, g)
                mx = s[:, :LANES]
                for k in range(1, nsub):
                    mx = jnp.maximum(mx, s[:, k * LANES:(k + 1) * LANES])
                m_ref[g] = jnp.maximum(m_ref[g], mx)
            return carry

        lax.fori_loop(0, nchunk, max_chunk, 0)
        for g in range(N_KV_HEADS):
            m_ref[g] = jnp.broadcast_to(jnp.max(m_ref[g], axis=1, keepdims=True), (gq, LANES))

    ones = jnp.ones((tq, HEAD_DIM), BF16)

    def accumulate():
        acc_ref[...] = jnp.zeros_like(acc_ref)

        def pv_chunk(c, carry):
            rows = pl.ds(pl.multiple_of(c * tq, tq), tq)
            for g in range(N_KV_HEADS):
                s = group_logits(c, g)
                m = m_ref[g]
                p = jnp.concatenate([jnp.exp(s[:, k * LANES:(k + 1) * LANES] - m) for k in range(nsub)], axis=1)
                vc = kv_ref[rows, nkv + g * HEAD_DIM:nkv + (g + 1) * HEAD_DIM]
                acc_ref[g] = acc_ref[g] + _dot(p.astype(BF16), jnp.concatenate([vc, ones], axis=1))
            return carry

        lax.fori_loop(0, nchunk, pv_chunk, 0)

    @pl.when(i == 0)
    def _():
        for g in range(N_KV_HEADS):
            def norm_chunk(c, mx):
                kc = kv_ref[pl.ds(pl.multiple_of(c * tq, tq), tq), g * HEAD_DIM:(g + 1) * HEAD_DIM].astype(F32)
                return jnp.maximum(mx, jnp.sum(kc * kc, axis=1, keepdims=True))
            k2 = lax.fori_loop(0, seq_len // tq, norm_chunk, jnp.zeros((tq, 1), F32))
            knorm_ref[g] = jnp.broadcast_to(jnp.sqrt(jnp.max(k2, axis=0, keepdims=True)), (SUBLANES, LANES))

    for h in range(N_HEADS):
        g, r = h // GROUP, h % GROUP
        qh = q_ref[:, h * HEAD_DIM:(h + 1) * HEAD_DIM].astype(F32)
        qn = jnp.sqrt(jnp.sum(qh * qh, axis=1, keepdims=True))
        m_ref[g, r * tq:(r + 1) * tq, :] = qn * knorm_ref[g][0:1, :]
    accumulate()
    denom_min = jnp.min(acc_ref[:, :, HEAD_DIM:HEAD_DIM + 1])

    @pl.when(denom_min < SOFTMAX_DENOM_MIN)
    def _():
        exact_row_max()
        accumulate()

    for h in range(N_HEADS):
        a = acc_ref[h // GROUP, (h % GROUP) * tq:(h % GROUP + 1) * tq, :]
        o_ref[:, h * HEAD_DIM:(h + 1) * HEAD_DIM] = (a[:, :HEAD_DIM] / a[:, HEAD_DIM:]).astype(BF16)


def _attn_prompt(q, kvb, qi, kib, idx, nb):
    t = q.shape[0]
    s = t // nb
    tq = min(ATT_Q, s)
    nq = s // tq
    n_keep = min(TOPK_KEYS, s // 4)
    blk = lambda w: pl.BlockSpec((tq, w), lambda b, i: (b * nq + i, 0))
    seq = lambda w: pl.BlockSpec((s, w), lambda b, i: (b, 0))
    return pl.pallas_call(
        functools.partial(_attn_prompt_kernel, n_keep=n_keep, seq_len=s),
        grid=(nb, nq),
        in_specs=[blk(q.shape[1]), seq(kvb.shape[1]), blk(qi.shape[1]), seq(kib.shape[1]), blk(idx.shape[1])],
        out_specs=blk(N_HEADS * HEAD_DIM),
        out_shape=jax.ShapeDtypeStruct((t, N_HEADS * HEAD_DIM), BF16),
        scratch_shapes=[pltpu.VMEM((nq, tq, tq), I32), pltpu.VMEM((nq, tq, tq), BF16),
                        pltpu.VMEM((nq, tq, tq), F32),
                        pltpu.VMEM((SUBLANES, tq), I32), pltpu.VMEM((SUBLANES, tq), I32),
                        pltpu.VMEM((SUBLANES, tq), I32),
                        pltpu.VMEM((N_KV_HEADS, GROUP * tq, LANES), F32),
                        pltpu.VMEM((N_KV_HEADS, GROUP * tq, 2 * HEAD_DIM), F32),
                        pltpu.VMEM((N_KV_HEADS, SUBLANES, LANES), F32)],
        compiler_params=_params(2),
        name="attn_prompt",
    )(q, kvb, qi, kib, idx)


DECODE_PAGES_PER_ROW = 4


def _decode_scores_kernel(pt_ref, qi_ref, wi_ref, kin_ref, cx_hbm, key_o, knew_o, xbuf, sem, *, n_pages):
    b = pl.program_id(0)
    nb = pl.num_programs(0)
    ppr = DECODE_PAGES_PER_ROW
    nrow = n_pages // ppr

    def page_copy(bb, p, slot):
        return pltpu.make_async_copy(cx_hbm.at[pt_ref[bb, p]], xbuf.at[slot, p], sem.at[slot])

    def start_batch(bb, slot):
        def body(p, carry):
            page_copy(bb, p, slot).start()
            return carry
        lax.fori_loop(0, n_pages, body, 0)

    def wait_batch(bb, slot):
        def body(p, carry):
            page_copy(bb, p, slot).wait()
            return carry
        lax.fori_loop(0, n_pages, body, 0)

    @pl.when(b == 0)
    def _():
        start_batch(0, 0)

    @pl.when(b + 1 < nb)
    def _():
        start_batch(b + 1, (b + 1) % 2)

    slot = b % 2
    wait_batch(b, slot)

    qi8 = qi_ref[0]
    wi8 = wi_ref[0] * IDX_SCALE

    def score_row(r, carry):
        kx = jnp.concatenate([xbuf[slot, r * ppr + k] for k in range(ppr)], axis=1).astype(BF16)
        d = _dot(qi8, kx)
        sc = jnp.sum(jnp.maximum(d, 0.0) * wi8, axis=0, keepdims=True)
        key_o[0, pl.ds(r, 1), :] = _sort_key(sc)
        return carry

    lax.fori_loop(0, nrow, score_row, 0, unroll=4)
    dn = jnp.sum(qi8.astype(F32) * kin_ref[0].astype(F32), axis=1, keepdims=True)
    key_new = _sort_key(jnp.sum(jnp.maximum(dn, 0.0) * wi8, axis=0, keepdims=True))
    knew_o[0] = jnp.broadcast_to(key_new, (1, LANES))


def _decode_scores(page_table, qi8, wi8, kin, cxt):
    db, n_pages = page_table.shape
    nrow = n_pages // DECODE_PAGES_PER_ROW
    w = DECODE_PAGES_PER_ROW * PAGE_SIZE
    per_b = lambda a: pl.BlockSpec((1,) + a.shape[1:], lambda b, pt: (b, 0, 0))
    grid_spec = pltpu.PrefetchScalarGridSpec(
        num_scalar_prefetch=1,
        grid=(db,),
        in_specs=[per_b(qi8), per_b(wi8), per_b(kin), pl.BlockSpec(memory_space=pl.ANY)],
        out_specs=[pl.BlockSpec((1, nrow, w), lambda b, pt: (b, 0, 0)),
                   pl.BlockSpec((1, 1, LANES), lambda b, pt: (b, 0, 0))],
        scratch_shapes=[pltpu.VMEM((2, n_pages, IDX_DIM, PAGE_SIZE), F32), pltpu.SemaphoreType.DMA((2,))],
    )
    return pl.pallas_call(
        functools.partial(_decode_scores_kernel, n_pages=n_pages),
        grid_spec=grid_spec,
        out_shape=[jax.ShapeDtypeStruct((db, nrow, w), I32), jax.ShapeDtypeStruct((db, 1, LANES), I32)],
        compiler_params=_params(1),
        name="decode_scores",
    )(page_table, qi8, wi8, kin, cxt)


def _decode_pick_kernel(keys_ref, knew_ref, pt_ref, meta_o, *, n_keep):
    nb, nrow, w = keys_ref.shape
    past = nrow * w
    n_pages = pt_ref.shape[1]
    keys = keys_ref[...]
    key_new = knew_ref[...][:, :, 0:1]
    col = (lax.broadcasted_iota(I32, keys.shape, 1) * w + lax.broadcasted_iota(I32, keys.shape, 2))

    def total(x):
        c = x.astype(I32)
        part = c[:, :, :LANES]
        for k in range(1, w // LANES):
            part = part + c[:, :, k * LANES:(k + 1) * LANES]
        return jnp.sum(jnp.sum(part, axis=1, keepdims=True), axis=2, keepdims=True)

    def bit_body(bi, t):
        cand = t + lax.shift_left(jnp.int32(1), 31 - bi)
        cnt = total(keys >= cand) + (key_new >= cand).astype(I32)
        return jnp.where(cnt >= n_keep, cand, t)

    t = lax.fori_loop(0, 32, bit_body, jnp.full((nb, 1, 1), INT_MIN, I32))
    need = n_keep - (total(keys > t) + (key_new > t).astype(I32))
    nbits = past.bit_length()

    def jbit(bi, jj):
        cand = jj | lax.shift_left(jnp.int32(1), nbits - 1 - bi)
        g = total((keys == t) & (col < cand)) + ((key_new == t) & (past < cand)).astype(I32)
        return jnp.where(g < need, cand, jj)

    jj = lax.fori_loop(0, nbits, jbit, jnp.zeros((nb, 1, 1), I32))
    sel = (keys > t) | ((keys == t) & (col <= jj))
    sel_new = ((key_new > t) | ((key_new == t) & (past <= jj))).astype(I32)

    rows = nb * nrow
    incl_lane = (lax.broadcasted_iota(I32, (w, w), 0) <= lax.broadcasted_iota(I32, (w, w), 1)).astype(BF16)
    cnt_in_row = _dot(sel.astype(BF16).reshape(rows, w), incl_lane)
    row_tot = cnt_in_row[:, w - 1:w]
    ri = lax.broadcasted_iota(I32, (rows, rows), 0)
    rj = lax.broadcasted_iota(I32, (rows, rows), 1)
    earlier_row_same_seq = ((ri // nrow) == (rj // nrow)) & (rj < ri)
    row_off = _dot(earlier_row_same_seq.astype(BF16),
                   jnp.broadcast_to(row_tot, (rows, LANES)).astype(BF16))[:, 0:1]
    cnt3 = cnt_in_row.astype(BF16).reshape(nb, nrow, w)
    off3 = row_off.reshape(nb, nrow, 1)
    incl3 = (row_off + row_tot).reshape(nb, nrow, 1)
    jl = lax.broadcasted_iota(I32, (1, 1, n_keep), 2).astype(F32)
    row_j = jnp.sum((incl3 <= jl).astype(F32), axis=1, keepdims=True)
    onehot = lax.broadcasted_iota(I32, (nb, nrow, n_keep), 1).astype(F32) == row_j
    off_j = jnp.sum(jnp.where(onehot, off3, 0.0), axis=1, keepdims=True)
    local_rank = jl - off_j
    onehot_b = onehot.astype(BF16)
    page_i = lax.broadcasted_iota(I32, (n_pages, n_keep), 0)
    row_i = lax.broadcasted_iota(I32, (SUBLANES, n_keep), 0)
    for b in range(nb):
        cnt_j = lax.dot_general(cnt3[b], onehot_b[b], (((0,), (0,)), ((), ())),
                                preferred_element_type=F32)
        lane_j = jnp.sum((cnt_j <= local_rank[b]).astype(F32), axis=0, keepdims=True)
        pos = jnp.minimum((row_j[b] * w + lane_j).astype(I32), past - 1)
        phys = jnp.sum(jnp.where(page_i == pos // PAGE_SIZE, pt_ref[b], 0.0), axis=0, keepdims=True)
        tile = (phys.astype(I32) * PAGE_SIZE + pos % PAGE_SIZE) // (SUBLANES // N_KV_HEADS)
        meta_o[b] = jnp.where(row_i == 0, tile,
                              jnp.where(row_i == 1, sel_new[b], pos % (SUBLANES // N_KV_HEADS)))


def _decode_pick(keys, knew, pt_f):
    nb, nrow, w = keys.shape
    n_keep = min(TOPK_KEYS, (nrow * w + 1) // 4)
    full = lambda a: pl.BlockSpec(a.shape, lambda i: (0,) * a.ndim)
    return pl.pallas_call(
        functools.partial(_decode_pick_kernel, n_keep=n_keep),
        grid=(1,),
        in_specs=[full(keys), full(knew), full(pt_f)],
        out_specs=pl.BlockSpec((nb, SUBLANES, n_keep), lambda i: (0, 0, 0)),
        out_shape=jax.ShapeDtypeStruct((nb, SUBLANES, n_keep), I32),
        compiler_params=_params(1),
        name="decode_pick",
    )(keys, knew, pt_f)


def _decode_attend_kernel(tile_ref, q_ref, meta_ref, kn_ref, vn_ref, ck_hbm, cv_hbm, o_ref,
                          kbuf, vbuf, expand_ref, sem, *, n_keep):
    b = pl.program_id(0)
    nb = pl.num_programs(0)
    ncol = n_keep * SUBLANES

    def item_copies(bb, j, slot):
        src = pl.ds(pl.multiple_of(tile_ref[bb, j] * SUBLANES, SUBLANES), SUBLANES)
        dst = pl.ds(pl.multiple_of(j * SUBLANES, SUBLANES), SUBLANES)
        return (pltpu.make_async_copy(ck_hbm.at[src], kbuf.at[slot, dst], sem.at[0, slot]),
                pltpu.make_async_copy(cv_hbm.at[src], vbuf.at[slot, dst], sem.at[1, slot]))

    def start_batch(bb, slot):
        def body(j, carry):
            for prio, cp in enumerate(item_copies(bb, j, slot)):
                cp.start(priority=prio)
            return carry
        lax.fori_loop(0, n_keep, body, 0)

    def wait_batch(slot):
        pltpu.make_async_copy(ck_hbm.at[pl.ds(0, ncol)], kbuf.at[slot], sem.at[0, slot]).wait()
        pltpu.make_async_copy(cv_hbm.at[pl.ds(0, ncol)], vbuf.at[slot], sem.at[1, slot]).wait()

    @pl.when(b == 0)
    def _():
        start_batch(0, 0)
        item_of_col = lax.broadcasted_iota(I32, (n_keep, ncol), 1) // SUBLANES
        expand_ref[...] = (item_of_col == lax.broadcasted_iota(I32, (n_keep, ncol), 0)).astype(BF16)

    @pl.when(b + 1 < nb)
    def _():
        start_batch(b + 1, (b + 1) % 2)

    slot = b % 2
    wait_batch(slot)

    q8 = q_ref[0]
    meta = meta_ref[0]
    new_kept = meta[1:2, 0:1] > 0
    sub_col = _dot(meta.astype(BF16), expand_ref[...])[2:3, :]
    head_i = lax.broadcasted_iota(I32, (N_HEADS, ncol), 0)
    col_i = lax.broadcasted_iota(I32, (N_HEADS, ncol), 1)
    want = sub_col * N_KV_HEADS + (head_i // GROUP).astype(F32)
    is_new_item = new_kept & (col_i // SUBLANES == n_keep - 1)
    ok = ((col_i % SUBLANES).astype(F32) == want) & jnp.logical_not(is_new_item)

    s = jnp.where(ok, _dot_nt(q8, kbuf[slot].astype(BF16)), NEG)
    s_new = jnp.sum(q8.astype(F32) * kn_ref[0].astype(F32), axis=1, keepdims=True)
    s_new = jnp.where(new_kept, s_new, NEG)
    m = jnp.maximum(jnp.max(s, axis=1, keepdims=True), s_new)
    p = jnp.exp(s - m)
    p_new = jnp.exp(s_new - m)
    l = jnp.sum(p, axis=1, keepdims=True) + p_new
    acc = _dot(p.astype(BF16), vbuf[slot].astype(BF16)) + p_new.astype(BF16).astype(F32) * vn_ref[0].astype(F32)
    o_ref[0] = (acc / l).astype(BF16)


def _decode_attend(tiles, q8, meta, kn8, vn8, ck2, cv2):
    db, n_keep = tiles.shape
    per_b = lambda a: pl.BlockSpec((1,) + a.shape[1:], lambda b, tl: (b, 0, 0))
    any_spec = pl.BlockSpec(memory_space=pl.ANY)
    ncol = n_keep * SUBLANES
    grid_spec = pltpu.PrefetchScalarGridSpec(
        num_scalar_prefetch=1,
        grid=(db,),
        in_specs=[per_b(q8), per_b(meta), per_b(kn8), per_b(vn8), any_spec, any_spec],
        out_specs=pl.BlockSpec((1, N_HEADS, HEAD_DIM), lambda b, tl: (b, 0, 0)),
        scratch_shapes=[pltpu.VMEM((2, ncol, HEAD_DIM), F32), pltpu.VMEM((2, ncol, HEAD_DIM), F32),
                        pltpu.VMEM((n_keep, ncol), BF16), pltpu.SemaphoreType.DMA((2, 2))],
    )
    return pl.pallas_call(
        functools.partial(_decode_attend_kernel, n_keep=n_keep),
        grid_spec=grid_spec,
        out_shape=jax.ShapeDtypeStruct((db, N_HEADS, HEAD_DIM), BF16),
        compiler_params=_params(1),
        name="decode_attend",
    )(tiles, q8, meta, kn8, vn8, ck2, cv2)


def _layer_norm(r, g, b):
    mu = jnp.mean(r, axis=-1, keepdims=True)
    d = r - mu
    var = jnp.mean(d * d, axis=-1, keepdims=True)
    return d * lax.rsqrt(var + LN_EPS) * g + b


def _route(logits_t, rbias):
    n_exp, tm = logits_t.shape
    epg = n_exp // N_GROUPS
    s = jax.nn.sigmoid(logits_t)
    sb = s + rbias
    ie = lax.broadcasted_iota(I32, (epg, tm), 0)
    gs_rows = []
    for g in range(N_GROUPS):
        blk = sb[g * epg:(g + 1) * epg, :]
        m1 = jnp.max(blk, axis=0, keepdims=True)
        i1 = jnp.min(jnp.where(blk == m1, ie, epg), axis=0, keepdims=True)
        m2 = jnp.max(jnp.where(ie == i1, -jnp.inf, blk), axis=0, keepdims=True)
        gs_rows.append(m1 + m2)
    picked = [jnp.zeros((1, tm), jnp.bool_) for _ in range(N_GROUPS)]
    cur = list(gs_rows)
    for _ in range(TOPK_GROUPS):
        mx = cur[0]
        for g in range(1, N_GROUPS):
            mx = jnp.maximum(mx, cur[g])
        found = jnp.zeros((1, tm), jnp.bool_)
        for g in range(N_GROUPS):
            hit = (cur[g] == mx) & jnp.logical_not(found)
            found = found | hit
            picked[g] = picked[g] | hit
            cur[g] = jnp.where(hit, -jnp.inf, cur[g])
    masked = jnp.concatenate(
        [jnp.where(picked[g], sb[g * epg:(g + 1) * epg, :], NEG) for g in range(N_GROUPS)], axis=0)
    iall = lax.broadcasted_iota(I32, (n_exp, tm), 0)
    e_rows, w_rows = [], []
    for _ in range(TOP_K):
        mx = jnp.max(masked, axis=0, keepdims=True)
        ix = jnp.min(jnp.where(masked == mx, iall, n_exp), axis=0, keepdims=True)
        hit = iall == ix
        w_rows.append(jnp.sum(jnp.where(hit, s, 0.0), axis=0, keepdims=True))
        e_rows.append(ix)
        masked = jnp.where(hit, -jnp.inf, masked)
    wsum = w_rows[0]
    for w in w_rows[1:]:
        wsum = wsum + w
    gates = [w / wsum * ROUTED_SCALE for w in w_rows]
    return jnp.concatenate(e_rows, axis=0), jnp.concatenate(gates, axis=0)


def _post_attn_kernel(xp, xs, ap, as_, cp, cs, gap, gas, gcp, gcs, woa, woc, wo, g1, b1, rwt_hi, rwt_lo, rb,
                      h2_o, e_o, gate_o, *, n_prompt_tiles):
    i = pl.program_id(0)
    is_p = i < n_prompt_tiles
    pick = lambda a, b: jnp.where(is_p, a[...], b[...])
    a = _dot(pick(ap, as_), woa[...])
    c = _dot(pick(cp, cs), woc[...])
    merged = pick(gap, gas).astype(F32) * a + pick(gcp, gcs).astype(F32) * c
    r = ALPHA * pick(xp, xs) + _dot(merged.astype(BF16), wo[...])
    h = _layer_norm(r, g1[...], b1[...])
    tm = h.shape[0]
    for k in range(D_MODEL // LANES):
        h2_o[pl.ds(k, tm, stride=SUBLANES), :] = h[:, k * LANES:(k + 1) * LANES]
    h_hi = h.astype(BF16)
    h_lo = (h - h_hi.astype(F32)).astype(BF16)
    logits_t = _dot_nt(rwt_hi[...], h_hi) + (_dot_nt(rwt_hi[...], h_lo) + _dot_nt(rwt_lo[...], h_hi))
    e_idx, gates = _route(logits_t, rb[...])
    e_o[...] = e_idx
    gate_o[...] = gates


def _post_attn(x_p, x_s, attn_p, attn_s, conv_p, conv_s, ga_p, ga_s, gc_p, gc_s,
               woa, woc, wo, g1, b1, rwt, rb):
    rwt_hi = rwt.astype(BF16)
    rwt_lo = (rwt - rwt_hi.astype(F32)).astype(BF16)
    tp, ts = x_p.shape[0], x_s.shape[0]
    tm = TOKEN_TILE
    npt, nst = tp // tm, ts // tm
    n_tok = tp + ts
    p_spec = pl.BlockSpec((tm, D_MODEL), lambda i: (jnp.minimum(i, npt - 1), 0))
    s_spec = pl.BlockSpec((tm, D_MODEL), lambda i: (jnp.maximum(i - npt, 0), 0))
    consts = [woa, woc, wo, g1, b1, rwt_hi, rwt_lo, rb]
    return pl.pallas_call(
        functools.partial(_post_attn_kernel, n_prompt_tiles=npt),
        grid=(npt + nst,),
        in_specs=[p_spec, s_spec] * 5 + [_const_spec(c.shape) for c in consts],
        out_specs=[pl.BlockSpec((tm * SUBLANES, LANES), lambda i: (i, 0)),
                   pl.BlockSpec((TOP_K, tm), lambda i: (0, i)),
                   pl.BlockSpec((TOP_K, tm), lambda i: (0, i))],
        out_shape=[jax.ShapeDtypeStruct((n_tok * SUBLANES, LANES), F32),
                   jax.ShapeDtypeStruct((TOP_K, n_tok), I32),
                   jax.ShapeDtypeStruct((TOP_K, n_tok), F32)],
        compiler_params=_params(1),
        name="post_attn",
    )(x_p, x_s, attn_p, attn_s, conv_p, conv_s, ga_p, ga_s, gc_p, gc_s, *consts)


def _rank_kernel(e_ref, rank_o, cnt_o, carry_ref):
    i = pl.program_id(0)

    @pl.when(i == 0)
    def _():
        carry_ref[...] = jnp.zeros_like(carry_ref)

    n_exp = carry_ref.shape[0]
    e = e_ref[...]
    tk = e.shape[1]
    ie = lax.broadcasted_iota(I32, (n_exp, tk), 0)
    onehot = jnp.zeros((n_exp, tk), F32)
    for j in range(TOP_K):
        onehot = onehot + (ie == e[j:j + 1, :]).astype(F32)
    before = (lax.broadcasted_iota(I32, (tk, tk), 0) < lax.broadcasted_iota(I32, (tk, tk), 1)).astype(BF16)
    prefix = _dot(onehot.astype(BF16), before) + carry_ref[:, 0:1]
    rows = [jnp.sum(jnp.where(ie == e[j:j + 1, :], prefix, 0.0), axis=0, keepdims=True) for j in range(TOP_K)]
    rank_o[...] = jnp.concatenate(rows, axis=0).astype(I32)
    carry_ref[...] = carry_ref[...] + jnp.sum(onehot, axis=1, keepdims=True)
    cnt_o[...] = carry_ref[...].astype(I32)


def _rank(e_t, n_exp):
    n_tok = e_t.shape[1]
    tk = TOKEN_TILE
    return pl.pallas_call(
        _rank_kernel,
        grid=(n_tok // tk,),
        in_specs=[pl.BlockSpec((TOP_K, tk), lambda i: (0, i))],
        out_specs=[pl.BlockSpec((TOP_K, tk), lambda i: (0, i)),
                   pl.BlockSpec((n_exp, LANES), lambda i: (0, 0))],
        out_shape=[jax.ShapeDtypeStruct((TOP_K, n_tok), I32), jax.ShapeDtypeStruct((n_exp, LANES), I32)],
        scratch_shapes=[pltpu.VMEM((n_exp, LANES), F32)],
        compiler_params=_params(1),
        name="moe_rank",
    )(e_t)


def _slot_kernel(e_ref, rank_ref, pstart_ref, slot_o):
    e = e_ref[...]
    n_exp = pstart_ref.shape[0]
    tk = e.shape[1]
    ie = lax.broadcasted_iota(I32, (n_exp, tk), 0)
    ps = pstart_ref[:, 0:1]
    rows = [jnp.sum(jnp.where(ie == e[j:j + 1, :], ps, 0.0), axis=0, keepdims=True) for j in range(TOP_K)]
    slot_o[...] = rank_ref[...] + jnp.concatenate(rows, axis=0).astype(I32)


def _slots(e_t, rank_t, pstart_f):
    n_tok = e_t.shape[1]
    tk = TOKEN_TILE
    spec = pl.BlockSpec((TOP_K, tk), lambda i: (0, i))
    return pl.pallas_call(
        _slot_kernel,
        grid=(n_tok // tk,),
        in_specs=[spec, spec, _const_spec(pstart_f.shape)],
        out_specs=spec,
        out_shape=jax.ShapeDtypeStruct((TOP_K, n_tok), I32),
        compiler_params=_params(1),
        name="moe_slots",
    )(e_t, rank_t, pstart_f)


def _dispatch_kernel(pend_ref, pad_ref, slot_ref, h2_ref, xs_hbm, zbuf, sem, zsem, *, n_exp):
    i = pl.program_id(0)
    td = slot_ref.shape[0] // TOP_K
    blk_rows = MOE_BLOCK * SUBLANES
    pieces = [1 << b for b in reversed(range((MOE_BLOCK - 1).bit_length()))]

    def pad_copies(e, act):
        pad = pad_ref[e]
        off = pend_ref[e] - pad
        for piece in pieces:
            dst = pl.ds(pl.multiple_of(off * SUBLANES, SUBLANES), piece * SUBLANES)
            cp = pltpu.make_async_copy(zbuf.at[pl.ds(0, piece * SUBLANES)], xs_hbm.at[dst], zsem)

            @pl.when((pad & piece) != 0)
            def _():
                act(cp)

            off = off + (pad & piece)

    @pl.when(i == 0)
    def _():
        zbuf[...] = jnp.zeros_like(zbuf)

        def start(e, carry):
            pad_copies(e, lambda cp: cp.start())
            return carry

        def wait(e, carry):
            pad_copies(e, lambda cp: cp.wait())
            return carry

        lax.fori_loop(0, n_exp, start, 0)
        lax.fori_loop(0, n_exp, wait, 0)

        def tail_copy(b):
            return pltpu.make_async_copy(zbuf, xs_hbm.at[pl.ds(pl.multiple_of(b * blk_rows, blk_rows), blk_rows)], zsem)

        def tail_start(b, carry):
            tail_copy(b).start()
            return carry

        def tail_wait(b, carry):
            tail_copy(b).wait()
            return carry

        first_unused = pend_ref[n_exp - 1] // MOE_BLOCK
        lax.fori_loop(first_unused, xs_hbm.shape[0] // blk_rows, tail_start, 0)
        lax.fori_loop(first_unused, xs_hbm.shape[0] // blk_rows, tail_wait, 0)

    def row_copy(t, j):
        src = pl.ds(pl.multiple_of(t * SUBLANES, SUBLANES), SUBLANES)
        dst = pl.ds(pl.multiple_of(slot_ref[t * TOP_K + j] * SUBLANES, SUBLANES), SUBLANES)
        return pltpu.make_async_copy(h2_ref.at[src], xs_hbm.at[dst], sem)

    def start_tok(t, carry):
        for j in range(TOP_K):
            row_copy(t, j).start(priority=j % 2)
        return carry

    lax.fori_loop(0, td, start_tok, 0)
    for j in range(TOP_K):
        pltpu.make_async_copy(h2_ref, xs_hbm.at[pl.ds(0, td * SUBLANES)], sem).wait()


def _dispatch(pend, pad, slot_flat, h2, m_pad):
    n_tok = slot_flat.shape[0] // TOP_K
    td = TOKEN_TILE
    n_exp = pend.shape[0]
    grid_spec = pltpu.PrefetchScalarGridSpec(
        num_scalar_prefetch=2,
        grid=(n_tok // td,),
        in_specs=[pl.BlockSpec((td * TOP_K,), lambda i, *_: (i,), memory_space=pltpu.SMEM),
                  pl.BlockSpec((td * SUBLANES, LANES), lambda i, *_: (i, 0))],
        out_specs=pl.BlockSpec(memory_space=pl.ANY),
        scratch_shapes=[pltpu.VMEM((MOE_BLOCK * SUBLANES, LANES), F32),
                        pltpu.SemaphoreType.DMA(()), pltpu.SemaphoreType.DMA(())],
    )
    return pl.pallas_call(
        functools.partial(_dispatch_kernel, n_exp=n_exp),
        grid_spec=grid_spec,
        out_shape=jax.ShapeDtypeStruct((m_pad * SUBLANES, LANES), F32),
        compiler_params=_params(1),
        name="moe_dispatch",
    )(pend, pad, slot_flat, h2)


def _tile_rows(ref, n):
    return jnp.concatenate([ref[pl.ds(k, n, stride=SUBLANES), :] for k in range(D_MODEL // LANES)], axis=1)


PLAN_EXPERT, PLAN_SLOT, PLAN_NEXT, PLAN_HAS_NEXT = range(4)


def _expert_kernel(plan_ref, nused_ref, x_ref, wg_hbm, wu_hbm, wd_hbm, y_ref, wg_f, wu_f, wd_f, wgu_b, wd_b, sem):
    i = pl.program_id(0)
    d_exp = wd_b.shape[0]

    def weight_copies(e, s):
        return (pltpu.make_async_copy(wg_hbm.at[e], wg_f.at[s], sem.at[s]),
                pltpu.make_async_copy(wu_hbm.at[e], wu_f.at[s], sem.at[s]),
                pltpu.make_async_copy(wd_hbm.at[e], wd_f.at[s], sem.at[s]))

    @pl.when(i < nused_ref[0])
    def _():
        e = plan_ref[PLAN_EXPERT, i]
        changed = (i == 0) | (e != plan_ref[PLAN_EXPERT, jnp.maximum(i - 1, 0)])

        @pl.when(changed)
        def _():
            s = plan_ref[PLAN_SLOT, i]

            @pl.when(i == 0)
            def _():
                for cp in weight_copies(e, s):
                    cp.start()

            for cp in weight_copies(e, s):
                cp.wait()

            @pl.when(plan_ref[PLAN_HAS_NEXT, i] > 0)
            def _():
                for cp in weight_copies(plan_ref[PLAN_NEXT, i], 1 - s):
                    cp.start()

            wgu_b[:, :d_exp] = wg_f[s].astype(BF16)
            wgu_b[:, d_exp:] = wu_f[s].astype(BF16)
            wd_b[...] = wd_f[s].astype(BF16)

        x = _tile_rows(x_ref, MOE_BLOCK).astype(BF16)
        gu = _dot(x, wgu_b[...])
        hh = jax.nn.silu(gu[:, :d_exp]) * gu[:, d_exp:]
        y = _dot(hh.astype(BF16), wd_b[...])
        for k in range(D_MODEL // LANES):
            y_ref[pl.ds(k, MOE_BLOCK, stride=SUBLANES), :] = y[:, k * LANES:(k + 1) * LANES]

    @pl.when(i >= nused_ref[0])
    def _():
        y_ref[...] = jnp.zeros_like(y_ref)


def _expert_plan(block_e, nused):
    n_blocks = block_e.shape[0]
    idx = jnp.arange(n_blocks, dtype=I32)
    changed = (idx == 0) | (block_e != jnp.roll(block_e, 1))
    slot = (jnp.cumsum(changed.astype(I32)) - 1) % 2
    change_pos = jnp.where(changed & (idx < nused), idx, n_blocks)
    next_pos = jnp.concatenate([lax.cummin(change_pos[::-1])[::-1][1:], jnp.full((1,), n_blocks, I32)])
    has_next = (next_pos < n_blocks).astype(I32)
    nxt = block_e[jnp.minimum(next_pos, n_blocks - 1)]
    return jnp.stack([block_e, slot, nxt, has_next]).astype(I32)


def _experts(block_e, nused, xs, w_gate, w_up, w_down):
    n_blocks = block_e.shape[0]
    d_exp = w_gate.shape[2]
    rows = MOE_BLOCK * SUBLANES
    last = lambda i, nu: jnp.minimum(i, nu[0] - 1)
    any_spec = pl.BlockSpec(memory_space=pl.ANY)
    grid_spec = pltpu.PrefetchScalarGridSpec(
        num_scalar_prefetch=2,
        grid=(n_blocks,),
        in_specs=[pl.BlockSpec((rows, LANES), lambda i, plan, nu: (last(i, nu), 0)), any_spec, any_spec, any_spec],
        out_specs=pl.BlockSpec((rows, LANES), lambda i, plan, nu: (i, 0)),
        scratch_shapes=[pltpu.VMEM((2, D_MODEL, d_exp), F32), pltpu.VMEM((2, D_MODEL, d_exp), F32),
                        pltpu.VMEM((2, d_exp, D_MODEL), F32),
                        pltpu.VMEM((D_MODEL, 2 * d_exp), BF16), pltpu.VMEM((d_exp, D_MODEL), BF16),
                        pltpu.SemaphoreType.DMA((2,))],
    )
    return pl.pallas_call(
        _expert_kernel,
        grid_spec=grid_spec,
        out_shape=jax.ShapeDtypeStruct(xs.shape, F32),
        compiler_params=_params(1),
        name="moe_experts",
    )(_expert_plan(block_e, nused[0]), nused, xs, w_gate, w_up, w_down)


def _combine_kernel(slot_ref, slot_next_ref, gate_ref, h2_ref, ys_hbm, shg, shu, shd, g2, b2, yp_o, ys_o,
                    buf, base_ref, y_ref, sem, *, n_prompt_tiles):
    i = pl.program_id(0)
    n = pl.num_programs(0)
    tc = gate_ref.shape[0]

    def start_tile(s_ref, bslot):
        def body(t, carry):
            dst = pl.ds(pl.multiple_of(t * SUBLANES, SUBLANES), SUBLANES)
            for j in range(TOP_K):
                src = pl.ds(pl.multiple_of(s_ref[t * TOP_K + j] * SUBLANES, SUBLANES), SUBLANES)
                pltpu.make_async_copy(ys_hbm.at[src], buf.at[bslot, j, dst], sem.at[bslot]).start(priority=j % 2)
            return carry
        lax.fori_loop(0, tc, body, 0)

    def wait_tile(bslot):
        for j in range(TOP_K):
            pltpu.make_async_copy(ys_hbm.at[pl.ds(0, tc * SUBLANES)], buf.at[bslot, j], sem.at[bslot]).wait()

    @pl.when(i == 0)
    def _():
        start_tile(slot_ref, 0)

    cur = i % 2
    h = _tile_rows(h2_ref, tc)
    hb = h.astype(BF16)
    shared = _dot((jax.nn.silu(_dot(hb, shg[...])) * _dot(hb, shu[...])).astype(BF16), shd[...])
    base_ref[...] = ALPHA * h + shared
    wait_tile(cur)
    gain, bias = g2[...], b2[...]
    grp = SUBLANES

    def finish_group(g):
        r0 = pl.multiple_of(g * grp, grp)
        t0 = pl.multiple_of(g * grp * SUBLANES, grp * SUBLANES)
        routed = jnp.zeros((grp, D_MODEL), F32)
        for j in range(TOP_K):
            rows = jnp.concatenate([buf[cur, j, pl.ds(t0 + k, grp, stride=SUBLANES), :]
                                    for k in range(D_MODEL // LANES)], axis=1)
            routed = routed + rows * gate_ref[pl.ds(r0, grp), j:j + 1]
        y_ref[pl.ds(r0, grp), :] = base_ref[pl.ds(r0, grp), :] + routed

    @pl.when(i + 1 < n)
    def _():
        nxt = (i + 1) % 2

        def body(g, carry):
            for tt in range(grp):
                t = g * grp + tt
                dst = pl.ds(pl.multiple_of(t * SUBLANES, SUBLANES), SUBLANES)
                for j in range(TOP_K):
                    src = pl.ds(pl.multiple_of(slot_next_ref[t * TOP_K + j] * SUBLANES, SUBLANES), SUBLANES)
                    pltpu.make_async_copy(ys_hbm.at[src], buf.at[nxt, j, dst], sem.at[nxt]).start(priority=j % 2)
            finish_group(g)
            return carry

        lax.fori_loop(0, tc // grp, body, 0)

    @pl.when(i + 1 >= n)
    def _():
        def body(g, carry):
            finish_group(g)
            return carry

        lax.fori_loop(0, tc // grp, body, 0)

    y = _layer_norm(y_ref[...], gain, bias)

    @pl.when(i < n_prompt_tiles)
    def _():
        yp_o[...] = y

    @pl.when(i >= n_prompt_tiles)
    def _():
        ys_o[...] = y


def _combine(slot_flat, gates, h2, ys, shg, shu, shd, g2, b2, n_prompt, n_decode):
    tc = TOKEN_TILE
    npt, nst = n_prompt // tc, n_decode // tc
    consts = [shg, shu, shd, g2, b2]
    return pl.pallas_call(
        functools.partial(_combine_kernel, n_prompt_tiles=npt),
        grid=(npt + nst,),
        in_specs=[pl.BlockSpec((tc * TOP_K,), lambda i: (i,), memory_space=pltpu.SMEM),
                  pl.BlockSpec((tc * TOP_K,), lambda i: (jnp.minimum(i + 1, npt + nst - 1),),
                               memory_space=pltpu.SMEM),
                  pl.BlockSpec((tc, TOP_K), lambda i: (i, 0)),
                  pl.BlockSpec((tc * SUBLANES, LANES), lambda i: (i, 0)),
                  pl.BlockSpec(memory_space=pl.ANY)] + [_const_spec(c.shape) for c in consts],
        out_specs=[pl.BlockSpec((tc, D_MODEL), lambda i: (jnp.minimum(i, npt - 1), 0)),
                   pl.BlockSpec((tc, D_MODEL), lambda i: (jnp.maximum(i - npt, 0), 0))],
        out_shape=[jax.ShapeDtypeStruct((n_prompt, D_MODEL), F32),
                   jax.ShapeDtypeStruct((n_decode, D_MODEL), F32)],
        scratch_shapes=[pltpu.VMEM((2, TOP_K, tc * SUBLANES, LANES), F32), pltpu.VMEM((tc, D_MODEL), F32),
                        pltpu.VMEM((tc, D_MODEL), F32), pltpu.SemaphoreType.DMA((2,))],
        compiler_params=_params(1),
        name="moe_combine",
    )(slot_flat, slot_flat, gates, h2, ys, *consts)


def _pad_rows(a, n):
    return jnp.pad(a, ((0, n - a.shape[0]),) + ((0, 0),) * (a.ndim - 1))


def kernel(x_prompt, x_sample, cache_k, cache_v, cache_kidx, page_table, state_conv, w_in, conv_w, w_o_attn,
           w_o_conv, w_o, ln1_g, ln1_b, router_w, router_bias, moe_w_gate, moe_w_up, moe_w_down,
           shared_w_gate, shared_w_up, shared_w_down, ln2_g, ln2_b):
    nb, seq, _ = x_prompt.shape
    db = x_sample.shape[0]
    n_pool = cache_k.shape[1]
    n_exp = router_w.shape[-1]
    tp = nb * seq
    ts = TOKEN_TILE
    nkv = N_KV_HEADS * HEAD_DIM

    weights = _prep_in_weights(w_in[0])
    cw = conv_w[0]

    xp2 = x_prompt.reshape(tp, D_MODEL)
    (q_p, k_p, v_p, kvb_p, qi_p, idx_p, kib_p, conv_p, ga_p, gc_p, ulast_p) = _in_proj_prompt(xp2, weights, cw, nb)
    xs2 = _pad_rows(x_sample.reshape(db, D_MODEL), ts)
    prev2 = _pad_rows(state_conv[0, :, 0, :], ts)
    prev1 = _pad_rows(state_conv[0, :, 1, :], ts)
    (q_s, k_s, v_s, _, qi_s, idx_s, _, conv_s, ga_s, gc_s, u_s) = _in_proj_decode(xs2, prev2, prev1, weights, cw)

    attn_p = _attn_prompt(q_p, kvb_p, qi_p, kib_p, idx_p, nb)
    q8 = q_s[:db].reshape(db, N_HEADS, HEAD_DIM)
    qi8 = qi_s[:db].reshape(db, IDX_HEADS, LANES)[:, :, :IDX_DIM]
    wi8 = idx_s[:db, IDX_DIM:IDX_DIM + IDX_HEADS].reshape(db, IDX_HEADS, 1)
    kin = idx_s[:db, :IDX_DIM].astype(BF16).reshape(db, 1, IDX_DIM)
    k_new = k_s[:db * N_KV_HEADS].reshape(db, N_KV_HEADS, HEAD_DIM)
    v_new = v_s[:db * N_KV_HEADS].reshape(db, N_KV_HEADS, HEAD_DIM)
    kn8 = jnp.repeat(k_new, GROUP, axis=1).astype(BF16)
    vn8 = jnp.repeat(v_new, GROUP, axis=1).astype(BF16)
    ck2 = cache_k[0].reshape(n_pool * PAGE_SIZE * N_KV_HEADS, HEAD_DIM)
    cv2 = cache_v[0].reshape(n_pool * PAGE_SIZE * N_KV_HEADS, HEAD_DIM)
    keys_s, knew_s = _decode_scores(page_table, qi8, wi8, kin, jnp.swapaxes(cache_kidx[0], 1, 2))
    meta = _decode_pick(keys_s, knew_s, page_table.astype(F32)[:, :, None])
    attn_s8 = _decode_attend(meta[:, 0, :], q8, meta, kn8, vn8, ck2, cv2)
    attn_s = _pad_rows(attn_s8.reshape(db, N_HEADS * HEAD_DIM), ts)

    h2, e_t, gate_t = _post_attn(
        xp2, xs2, attn_p, attn_s, conv_p, conv_s, ga_p, ga_s, gc_p, gc_s,
        w_o_attn[0].astype(BF16), w_o_conv[0].astype(BF16), w_o[0].astype(BF16),
        ln1_g[0].reshape(1, D_MODEL), ln1_b[0].reshape(1, D_MODEL),
        router_w[0].T, router_bias[0].reshape(n_exp, 1))

    n_tok = tp + ts
    rank_t, cnt = _rank(e_t, n_exp)
    counts = cnt[:, 0]
    padded = (counts + MOE_BLOCK - 1) // MOE_BLOCK * MOE_BLOCK
    pend = jnp.cumsum(padded)
    pstart = pend - padded
    n_blocks = (n_tok * TOP_K + n_exp * (MOE_BLOCK - 1) + MOE_BLOCK - 1) // MOE_BLOCK
    nused = (pend[-1] // MOE_BLOCK).astype(I32)
    blk = jnp.minimum(jnp.arange(n_blocks, dtype=I32), nused - 1)
    block_e = jnp.minimum(jnp.sum((pend[None, :] <= (blk * MOE_BLOCK)[:, None]).astype(I32), axis=1), n_exp - 1)
    slot_t = _slots(e_t, rank_t, jnp.broadcast_to(pstart.astype(F32)[:, None], (n_exp, LANES)))
    slot_flat = slot_t.T.reshape(-1)
    xs = _dispatch(pend.astype(I32), (padded - counts).astype(I32), slot_flat, h2, n_blocks * MOE_BLOCK)
    ys = _experts(block_e, nused.reshape(1), xs, moe_w_gate[0], moe_w_up[0], moe_w_down[0])
    y_p, y_s = _combine(slot_flat, gate_t.T, h2, ys,
                        shared_w_gate[0].astype(BF16), shared_w_up[0].astype(BF16),
                        shared_w_down[0].astype(BF16),
                        ln2_g[0].reshape(1, D_MODEL), ln2_b[0].reshape(1, D_MODEL), tp, ts)

    conv_sample = jnp.stack([state_conv[0, :, 1, :], u_s[:db]], axis=1)[None]
    return (y_p.reshape(nb, seq, D_MODEL),
            y_s[:db].reshape(db, 1, D_MODEL),
            k_p.reshape(1, nb, seq, N_KV_HEADS, HEAD_DIM),
            v_p.reshape(1, nb, seq, N_KV_HEADS, HEAD_DIM),
            idx_p[:, :IDX_DIM].reshape(1, nb, seq, IDX_DIM),
            ulast_p[:, SUBLANES - (CONV_W - 1):, :][None],
            k_new.reshape(1, db, 1, N_KV_HEADS, HEAD_DIM),
            v_new.reshape(1, db, 1, N_KV_HEADS, HEAD_DIM),
            idx_s[:db, :IDX_DIM].reshape(1, db, 1, IDX_DIM),
            conv_sample)
```

```python
import functools

import jax
import jax.numpy as jnp
import numpy as np
from jax import lax
from jax.experimental import pallas as pl
from jax.experimental.pallas import tpu as pltpu

F32 = jnp.float32
BF16 = jnp.bfloat16
I32 = jnp.int32

D_MODEL = 1024
N_HEADS = 8
HEAD_DIM = 128
N_KV_HEADS = 2
GROUP = N_HEADS // N_KV_HEADS
IDX_HEADS = 8
IDX_DIM = 64
TOPK_KEYS = 256
IDX_SCALE = IDX_DIM ** -0.5 * IDX_HEADS ** -0.5
Q_SCALE = HEAD_DIM ** -0.5
PAGE_SIZE = 128
D_CONV = D_MODEL
CONV_W = 3
TOP_K = 8
N_GROUPS = 8
TOPK_GROUPS = 4
ROUTED_SCALE = 2.5
DEPTH = 1
ALPHA = (2 * DEPTH) ** 0.25
LN_EPS = 1e-5
NEG = -1e30
IN_WIDTHS = (N_HEADS * HEAD_DIM, N_KV_HEADS * HEAD_DIM, N_KV_HEADS * HEAD_DIM,
             IDX_HEADS * IDX_DIM, IDX_DIM, IDX_HEADS,
             D_CONV, D_CONV, D_CONV, D_MODEL, D_MODEL)

LANES = 128
SUBLANES = 8
VMEM_LIMIT_BYTES = 56 * 1024 * 1024

PROJ_ROWS = 512
ATT_Q = 256
TOKEN_TILE = 256
MOE_BLOCK = 512
CONV_CHUNK = 256
INT_MIN = -2 ** 31
SOFTMAX_DENOM_MIN = 1e-25
F32_MIN_NORMAL = float(np.finfo(np.float32).tiny)
BF16_MIN_NORMAL_BITS = 0x0080


def _np_key(v):
    b = int(np.float32(v).view(np.int32))
    return b ^ ((b >> 31) & 0x7FFFFFFF)


KEY_HALF = _np_key(NEG * 0.5)


def _sort_key(s):
    b = lax.bitcast_convert_type(s + 0.0, I32)
    return b ^ ((b >> 31) & jnp.int32(0x7FFFFFFF))


def _dot(a, b):
    return jnp.dot(a, b, preferred_element_type=F32)


def _dot_nt(a, b):
    return lax.dot_general(a, b, (((1,), (1,)), ((), ())), preferred_element_type=F32)


def _params(n_grid):
    return pltpu.CompilerParams(dimension_semantics=("arbitrary",) * n_grid,
                                vmem_limit_bytes=VMEM_LIMIT_BYTES)


def _const_spec(shape):
    nd = len(shape)
    return pl.BlockSpec(shape, lambda *_: (0,) * nd, pipeline_mode=pl.Buffered(1))


def _proj_common(xb, w1, w2, w3, wga, wgc, q_o, k_o, v_o, kv_o, qi_o, idx_o, kib_o, sga_o, sgc_o):
    nq = N_HEADS * HEAD_DIM
    nkv = N_KV_HEADS * HEAD_DIM
    z1 = _dot(xb, w1[...])
    q_o[...] = (z1[:, :nq] * Q_SCALE).astype(BF16)
    k = z1[:, nq:nq + nkv]
    v = z1[:, nq + nkv:nq + 2 * nkv]
    for g in range(N_KV_HEADS):
        k_o[pl.ds(g, k.shape[0], stride=N_KV_HEADS), :] = k[:, g * HEAD_DIM:(g + 1) * HEAD_DIM]
        v_o[pl.ds(g, v.shape[0], stride=N_KV_HEADS), :] = v[:, g * HEAD_DIM:(g + 1) * HEAD_DIM]
    kv_o[:, :nkv] = k.astype(BF16)
    kv_o[:, nkv:] = v.astype(BF16)
    qi_o[...] = _dot(xb, w2[...]).astype(BF16)
    z3 = _dot(xb, w3[...])
    idx_o[...] = z3
    kib_o[...] = z3.astype(BF16)
    sga_o[...] = jax.nn.sigmoid(_dot(xb, wga[...])).astype(BF16)
    sgc_o[...] = jax.nn.sigmoid(_dot(xb, wgc[...])).astype(BF16)


def _in_proj_prompt_kernel(x_ref, w1, w2, w3, wb, wc, wx, wga, wgc, cw_ref,
                           q_o, k_o, v_o, kv_o, qi_o, idx_o, kib_o, conv_o, sga_o, sgc_o, ulast_o,
                           tail_ref):
    j = pl.program_id(1)

    @pl.when(j == 0)
    def _():
        tail_ref[...] = jnp.zeros_like(tail_ref)

    xb = x_ref[...].astype(BF16)
    _proj_common(xb, w1, w2, w3, wga, wgc, q_o, k_o, v_o, kv_o, qi_o, idx_o, kib_o, sga_o, sgc_o)
    tm = xb.shape[0]
    row = lax.broadcasted_iota(I32, (tm, CONV_CHUNK), 0)
    for c in range(D_CONV // CONV_CHUNK):
        cs = slice(c * CONV_CHUNK, (c + 1) * CONV_CHUNK)
        gb = _dot(xb, wb[:, cs])
        u = _dot(xb, wc[:, cs]) * _dot(xb, wx[:, cs])
        p1 = tail_ref[SUBLANES - 1:SUBLANES, cs]
        p2 = tail_ref[SUBLANES - 2:SUBLANES - 1, cs]
        u1 = jnp.where(row == 0, p1, pltpu.roll(u, 1, 0))
        u2 = jnp.where(row == 0, p2, jnp.where(row == 1, p1, pltpu.roll(u, 2, 0)))
        conv = cw_ref[0:1, cs] * u2 + cw_ref[1:2, cs] * u1 + cw_ref[2:3, cs] * u
        conv_o[:, cs] = (gb * conv).astype(BF16)
        tail_ref[:, cs] = u[tm - SUBLANES:, :]
    ulast_o[0] = tail_ref[...]


def _in_proj_decode_kernel(x_ref, p2_ref, p1_ref, w1, w2, w3, wb, wc, wx, wga, wgc, cw_ref,
                           q_o, k_o, v_o, kv_o, qi_o, idx_o, kib_o, conv_o, sga_o, sgc_o, u_o):
    xb = x_ref[...].astype(BF16)
    _proj_common(xb, w1, w2, w3, wga, wgc, q_o, k_o, v_o, kv_o, qi_o, idx_o, kib_o, sga_o, sgc_o)
    for c in range(D_CONV // CONV_CHUNK):
        cs = slice(c * CONV_CHUNK, (c + 1) * CONV_CHUNK)
        gb = _dot(xb, wb[:, cs])
        u = _dot(xb, wc[:, cs]) * _dot(xb, wx[:, cs])
        conv = cw_ref[0:1, cs] * p2_ref[:, cs] + cw_ref[1:2, cs] * p1_ref[:, cs] + cw_ref[2:3, cs] * u
        conv_o[:, cs] = (gb * conv).astype(BF16)
        u_o[:, cs] = u


def _prep_in_weights(w_in):
    w = w_in.astype(BF16)
    offs = np.concatenate([[0], np.cumsum(IN_WIDTHS)])
    part = [w[:, int(offs[i]):int(offs[i + 1])] for i in range(len(IN_WIDTHS))]
    q, k, v, qi, ki, wi, gb, gc, xv, ga, gcv = part
    d = w.shape[0]
    w1 = jnp.concatenate([q, k, v], axis=1)
    w2 = jnp.pad(qi.reshape(d, IDX_HEADS, IDX_DIM),
                 ((0, 0), (0, 0), (0, LANES - IDX_DIM))).reshape(d, IDX_HEADS * LANES)
    w3 = jnp.concatenate([ki, wi, jnp.zeros((d, LANES - IDX_DIM - IDX_HEADS), BF16)], axis=1)
    return (w1, w2, w3, gb, gc, xv, ga, gcv)


def _proj_out_shapes(t):
    nkv = N_KV_HEADS * HEAD_DIM
    sd = jax.ShapeDtypeStruct
    return [sd((t, N_HEADS * HEAD_DIM), BF16), sd((t * N_KV_HEADS, HEAD_DIM), F32),
            sd((t * N_KV_HEADS, HEAD_DIM), F32), sd((t, 2 * nkv), BF16),
            sd((t, IDX_HEADS * LANES), BF16), sd((t, LANES), F32), sd((t, LANES), BF16),
            sd((t, D_CONV), BF16), sd((t, D_MODEL), BF16), sd((t, D_MODEL), BF16)]


def _in_proj_prompt(x2d, weights, conv_w, nb):
    t = x2d.shape[0]
    s = t // nb
    tm = min(PROJ_ROWS, s)
    nj = s // tm
    row_spec = lambda w, rep=1: pl.BlockSpec((tm * rep, w), lambda b, j: (b * nj + j, 0))
    out_shapes = _proj_out_shapes(t) + [jax.ShapeDtypeStruct((nb, SUBLANES, D_CONV), F32)]
    out_specs = [row_spec(o.shape[1], o.shape[0] // t) for o in out_shapes[:-1]]
    out_specs.append(pl.BlockSpec((1, SUBLANES, D_CONV), lambda b, j: (b, 0, 0)))
    return pl.pallas_call(
        _in_proj_prompt_kernel,
        grid=(nb, nj),
        in_specs=[row_spec(D_MODEL)] + [_const_spec(w.shape) for w in weights] + [_const_spec(conv_w.shape)],
        out_specs=out_specs,
        out_shape=out_shapes,
        scratch_shapes=[pltpu.VMEM((SUBLANES, D_CONV), F32)],
        compiler_params=_params(2),
        name="in_proj_prompt",
    )(x2d, *weights, conv_w)


def _in_proj_decode(x2d, p2, p1, weights, conv_w):
    t = x2d.shape[0]
    row_spec = lambda w, rep=1: pl.BlockSpec((t * rep, w), lambda i: (0, 0))
    out_shapes = _proj_out_shapes(t) + [jax.ShapeDtypeStruct((t, D_CONV), F32)]
    return pl.pallas_call(
        _in_proj_decode_kernel,
        grid=(1,),
        in_specs=[row_spec(D_MODEL)] * 3 + [_const_spec(w.shape) for w in weights] + [_const_spec(conv_w.shape)],
        out_specs=[row_spec(o.shape[1], o.shape[0] // t) for o in out_shapes],
        out_shape=out_shapes,
        compiler_params=_params(1),
        name="in_proj_decode",
    )(x2d, p2, p1, *weights, conv_w)


def _attn_prompt_kernel(q_ref, kv_ref, qi_ref, kib_ref, idx_ref, o_ref,
                        key_ref, hi_ref, bias_ref, t_ref, j_ref, need_ref, m_ref, acc_ref, knorm_ref,
                        *, n_keep, seq_len):
    tq = q_ref.shape[0]
    i = pl.program_id(1)
    nchunk = i + 1
    nsub = tq // LANES

    idx_t = jnp.transpose(idx_ref[...])
    wi_rows = [idx_t[IDX_DIM + h:IDX_DIM + h + 1, :] * IDX_SCALE for h in range(IDX_HEADS)]
    key_i = lax.broadcasted_iota(I32, (tq, tq), 0)
    qry_i = lax.broadcasted_iota(I32, (tq, tq), 1)

    def score_chunk(c, carry):
        kic = kib_ref[pl.ds(pl.multiple_of(c * tq, tq), tq), :]
        acc = jnp.zeros((tq, tq), F32)
        for h in range(IDX_HEADS):
            d = _dot_nt(kic, qi_ref[:, h * LANES:(h + 1) * LANES])
            acc = acc + jnp.maximum(d, 0.0) * wi_rows[h]
        acc = jnp.where(jnp.abs(acc) < F32_MIN_NORMAL, 0.0, acc)
        s = jnp.where((c < i) | (key_i <= qry_i), acc, NEG)
        bits = lax.bitcast_convert_type(s, I32)
        key_ref[c] = bits ^ ((bits >> 31) & jnp.int32(0x7FFFFFFF))
        hi_ref[c] = lax.bitcast_convert_type(bits & jnp.int32(-65536), F32).astype(BF16)
        return carry

    lax.fori_loop(0, nchunk, score_chunk, 0)

    sub_i = lax.broadcasted_iota(I32, (SUBLANES, tq), 0)

    def counts(preds, with_pos=False):
        def body(c, cnts):
            for kk in range(tq // SUBLANES):
                blk = key_ref[c, kk * SUBLANES:(kk + 1) * SUBLANES, :]
                pos = (c * tq + kk * SUBLANES + sub_i) if with_pos else None
                cnts = tuple(cnt + pred(blk, pos).astype(I32) for cnt, pred in zip(cnts, preds))
            return cnts
        cnts = lax.fori_loop(0, nchunk, body, tuple(jnp.zeros((SUBLANES, tq), I32) for _ in preds))
        return tuple(jnp.sum(cnt, axis=0, keepdims=True) for cnt in cnts)

    def count_hi(cand):
        rows16 = 2 * SUBLANES
        def body(c, cnt):
            for kk in range(tq // rows16):
                blk = hi_ref[c, kk * rows16:(kk + 1) * rows16, :]
                cnt = cnt + jnp.where(blk >= cand, jnp.ones_like(cnt), jnp.zeros_like(cnt))
            return cnt
        cnt = lax.fori_loop(0, nchunk, body, jnp.zeros((rows16, tq), BF16))
        return jnp.sum(cnt.astype(F32), axis=0, keepdims=True)

    def hi_bit_body(bi, p16):
        cand = p16 + lax.shift_left(jnp.int32(1), 15 - bi)
        raw = cand ^ ((cand >> 31) & jnp.int32(0x7FFF))
        raw = jnp.where((raw > 0) & (raw < BF16_MIN_NORMAL_BITS), BF16_MIN_NORMAL_BITS, raw)
        cand_f = lax.bitcast_convert_type(lax.shift_left(raw, 16), F32)
        total = count_hi(jnp.broadcast_to(cand_f, (2 * SUBLANES, tq)).astype(BF16))
        return jnp.where(total >= n_keep, cand, p16)

    p16 = lax.fori_loop(0, 16, hi_bit_body, jnp.full((1, tq), -2 ** 15, I32))

    def bit_body(bi, t):
        cand = t + lax.shift_left(jnp.int32(1), 15 - bi)
        cand_b = jnp.broadcast_to(cand, (SUBLANES, tq))
        total, = counts((lambda blk, _: blk >= cand_b,))
        return jnp.where(total >= n_keep, cand, t)

    t = lax.fori_loop(0, 16, bit_body, lax.shift_left(p16, 16))
    t_b = jnp.broadcast_to(t, (SUBLANES, tq))
    cnt_gt, cnt_eq = counts((lambda blk, _: blk > t_b, lambda blk, _: blk == t_b))
    need = n_keep - cnt_gt
    any_excess = jnp.max(((cnt_eq > need) & (t > KEY_HALF)).astype(I32))
    t_ref[...] = t_b
    need_ref[...] = jnp.broadcast_to(need, (SUBLANES, tq))
    j_ref[...] = jnp.full(j_ref.shape, seq_len, I32)

    @pl.when(any_excess > 0)
    def _():
        nbits = max(1, (seq_len - 1).bit_length())
        t_b = t_ref[...]
        need = need_ref[0:1, :]

        def jbit(bi, jj):
            cand = jj | lax.shift_left(jnp.int32(1), nbits - 1 - bi)
            cand_b = jnp.broadcast_to(cand, (SUBLANES, tq))
            g, = counts((lambda blk, pos: (blk == t_b) & (pos < cand_b),), with_pos=True)
            return jnp.where(g < need, cand, jj)

        jj = lax.fori_loop(0, nbits, jbit, jnp.zeros((1, tq), I32))
        j_ref[...] = jnp.broadcast_to(jj, (SUBLANES, tq))

    t_row = t_ref[0:1, :]
    j_row = j_ref[0:1, :]

    def bias_chunk(c, carry):
        blk = key_ref[c]
        sel = (blk > KEY_HALF) & ((blk > t_row) | ((blk == t_row) & ((c * tq + key_i) <= j_row)))
        bias_ref[c] = jnp.transpose(jnp.where(sel, 0.0, NEG))
        return carry

    lax.fori_loop(0, nchunk, bias_chunk, 0)

    nkv = N_KV_HEADS * HEAD_DIM
    gq = GROUP * tq

    def group_logits(c, g):
        rows = pl.ds(pl.multiple_of(c * tq, tq), tq)
        kc = kv_ref[rows, g * HEAD_DIM:(g + 1) * HEAD_DIM]
        qg = jnp.concatenate([q_ref[:, h * HEAD_DIM:(h + 1) * HEAD_DIM]
                              for h in range(g * GROUP, (g + 1) * GROUP)], axis=0)
        s = _dot_nt(qg, kc).reshape(GROUP, tq, tq) + bias_ref[c][None]
        return s.reshape(gq, tq)

    def exact_row_max():
        m_ref[...] = jnp.full(m_ref.shape, NEG, F32)

        def max_chunk(c, carry):
            for g in range(N_KV_HEADS):
                s = group_logits(c, g)
                mx = s[:, :LANES]
                for k in range(1, nsub):
                    mx = jnp.maximum(mx, s[:, k * LANES:(k + 1) * LANES])
                m_ref[g] = jnp.maximum(m_ref[g], mx)
            return carry

        lax.fori_loop(0, nchunk, max_chunk, 0)
        for g in range(N_KV_HEADS):
            m_ref[g] = jnp.broadcast_to(jnp.max(m_ref[g], axis=1, keepdims=True), (gq, LANES))

    ones = jnp.ones((tq, HEAD_DIM), BF16)

    def accumulate():
        acc_ref[...] = jnp.zeros_like(acc_ref)

        def pv_chunk(c, carry):
            rows = pl.ds(pl.multiple_of(c * tq, tq), tq)
            for g in range(N_KV_HEADS):
                s = group_logits(c, g)
                m = m_ref[g]
                p = jnp.concatenate([jnp.exp(s[:, k * LANES:(k + 1) * LANES] - m) for k in range(nsub)], axis=1)
                vc = kv_ref[rows, nkv + g * HEAD_DIM:nkv + (g + 1) * HEAD_DIM]
                acc_ref[g] = acc_ref[g] + _dot(p.astype(BF16), jnp.concatenate([vc, ones], axis=1))
            return carry

        lax.fori_loop(0, nchunk, pv_chunk, 0)

    @pl.when(i == 0)
    def _():
        for g in range(N_KV_HEADS):
            def norm_chunk(c, mx):
                kc = kv_ref[pl.ds(pl.multiple_of(c * tq, tq), tq), g * HEAD_DIM:(g + 1) * HEAD_DIM].astype(F32)
                return jnp.maximum(mx, jnp.sum(kc * kc, axis=1, keepdims=True))
            k2 = lax.fori_loop(0, seq_len // tq, norm_chunk, jnp.zeros((tq, 1), F32))
            knorm_ref[g] = jnp.broadcast_to(jnp.sqrt(jnp.max(k2, axis=0, keepdims=True)), (SUBLANES, LANES))

    for h in range(N_HEADS):
        g, r = h // GROUP, h % GROUP
        qh = q_ref[:, h * HEAD_DIM:(h + 1) * HEAD_DIM].astype(F32)
        qn = jnp.sqrt(jnp.sum(qh * qh, axis=1, keepdims=True))
        m_ref[g, r * tq:(r + 1) * tq, :] = qn * knorm_ref[g][0:1, :]
    accumulate()
    denom_min = jnp.min(acc_ref[:, :, HEAD_DIM:HEAD_DIM + 1])

    @pl.when(denom_min < SOFTMAX_DENOM_MIN)
    def _():
        exact_row_max()
        accumulate()

    for h in range(N_HEADS):
        a = acc_ref[h // GROUP, (h % GROUP) * tq:(h % GROUP + 1) * tq, :]
        o_ref[:, h * HEAD_DIM:(h + 1) * HEAD_DIM] = (a[:, :HEAD_DIM] / a[:, HEAD_DIM:]).astype(BF16)


def _attn_prompt(q, kvb, qi, kib, idx, nb):
    t = q.shape[0]
    s = t // nb
    tq = min(ATT_Q, s)
    nq = s // tq
    n_keep = min(TOPK_KEYS, s // 4)
    blk = lambda w: pl.BlockSpec((tq, w), lambda b, i: (b * nq + i, 0))
    seq = lambda w: pl.BlockSpec((s, w), lambda b, i: (b, 0))
    return pl.pallas_call(
        functools.partial(_attn_prompt_kernel, n_keep=n_keep, seq_len=s),
        grid=(nb, nq),
        in_specs=[blk(q.shape[1]), seq(kvb.shape[1]), blk(qi.shape[1]), seq(kib.shape[1]), blk(idx.shape[1])],
        out_specs=blk(N_HEADS * HEAD_DIM),
        out_shape=jax.ShapeDtypeStruct((t, N_HEADS * HEAD_DIM), BF16),
        scratch_shapes=[pltpu.VMEM((nq, tq, tq), I32), pltpu.VMEM((nq, tq, tq), BF16),
                        pltpu.VMEM((nq, tq, tq), F32),
                        pltpu.VMEM((SUBLANES, tq), I32), pltpu.VMEM((SUBLANES, tq), I32),
                        pltpu.VMEM((SUBLANES, tq), I32),
                        pltpu.VMEM((N_KV_HEADS, GROUP * tq, LANES), F32),
                        pltpu.VMEM((N_KV_HEADS, GROUP * tq, 2 * HEAD_DIM), F32),
                        pltpu.VMEM((N_KV_HEADS, SUBLANES, LANES), F32)],
        compiler_params=_params(2),
        name="attn_prompt",
    )(q, kvb, qi, kib, idx)


DECODE_PAGES_PER_ROW = 4


def _decode_scores_kernel(pt_ref, qi_ref, wi_ref, kin_ref, cx_hbm, key_o, knew_o, xbuf, sem, *, n_pages):
    b = pl.program_id(0)
    nb = pl.num_programs(0)
    ppr = DECODE_PAGES_PER_ROW
    nrow = n_pages // ppr

    def page_copy(bb, p, slot):
        return pltpu.make_async_copy(cx_hbm.at[pt_ref[bb, p]], xbuf.at[slot, p], sem.at[slot])

    def start_batch(bb, slot):
        def body(p, carry):
            page_copy(bb, p, slot).start()
            return carry
        lax.fori_loop(0, n_pages, body, 0)

    def wait_batch(bb, slot):
        def body(p, carry):
            page_copy(bb, p, slot).wait()
            return carry
        lax.fori_loop(0, n_pages, body, 0)

    @pl.when(b == 0)
    def _():
        start_batch(0, 0)

    @pl.when(b + 1 < nb)
    def _():
        start_batch(b + 1, (b + 1) % 2)

    slot = b % 2
    wait_batch(b, slot)

    qi8 = qi_ref[0]
    wi8 = wi_ref[0] * IDX_SCALE

    def score_row(r, carry):
        kx = jnp.concatenate([xbuf[slot, r * ppr + k] for k in range(ppr)], axis=1).astype(BF16)
        d = _dot(qi8, kx)
        sc = jnp.sum(jnp.maximum(d, 0.0) * wi8, axis=0, keepdims=True)
        key_o[0, pl.ds(r, 1), :] = _sort_key(sc)
        return carry

    lax.fori_loop(0, nrow, score_row, 0, unroll=4)
    dn = jnp.sum(qi8.astype(F32) * kin_ref[0].astype(F32), axis=1, keepdims=True)
    key_new = _sort_key(jnp.sum(jnp.maximum(dn, 0.0) * wi8, axis=0, keepdims=True))
    knew_o[0] = jnp.broadcast_to(key_new, (1, LANES))


def _decode_scores(page_table, qi8, wi8, kin, cxt):
    db, n_pages = page_table.shape
    nrow = n_pages // DECODE_PAGES_PER_ROW
    w = DECODE_PAGES_PER_ROW * PAGE_SIZE
    per_b = lambda a: pl.BlockSpec((1,) + a.shape[1:], lambda b, pt: (b, 0, 0))
    grid_spec = pltpu.PrefetchScalarGridSpec(
        num_scalar_prefetch=1,
        grid=(db,),
        in_specs=[per_b(qi8), per_b(wi8), per_b(kin), pl.BlockSpec(memory_space=pl.ANY)],
        out_specs=[pl.BlockSpec((1, nrow, w), lambda b, pt: (b, 0, 0)),
                   pl.BlockSpec((1, 1, LANES), lambda b, pt: (b, 0, 0))],
        scratch_shapes=[pltpu.VMEM((2, n_pages, IDX_DIM, PAGE_SIZE), F32), pltpu.SemaphoreType.DMA((2,))],
    )
    return pl.pallas_call(
        functools.partial(_decode_scores_kernel, n_pages=n_pages),
        grid_spec=grid_spec,
        out_shape=[jax.ShapeDtypeStruct((db, nrow, w), I32), jax.ShapeDtypeStruct((db, 1, LANES), I32)],
        compiler_params=_params(1),
        name="decode_scores",
    )(page_table, qi8, wi8, kin, cxt)


def _decode_pick_kernel(keys_ref, knew_ref, pt_ref, meta_o, *, n_keep):
    nb, nrow, w = keys_ref.shape
    past = nrow * w
    n_pages = pt_ref.shape[1]
    keys = keys_ref[...]
    key_new = knew_ref[...][:, :, 0:1]
    col = (lax.broadcasted_iota(I32, keys.shape, 1) * w + lax.broadcasted_iota(I32, keys.shape, 2))

    def total(x):
        c = x.astype(I32)
        part = c[:, :, :LANES]
        for k in range(1, w // LANES):
            part = part + c[:, :, k * LANES:(k + 1) * LANES]
        return jnp.sum(jnp.sum(part, axis=1, keepdims=True), axis=2, keepdims=True)

    def bit_body(bi, t):
        cand = t + lax.shift_left(jnp.int32(1), 31 - bi)
        cnt = total(keys >= cand) + (key_new >= cand).astype(I32)
        return jnp.where(cnt >= n_keep, cand, t)

    t = lax.fori_loop(0, 32, bit_body, jnp.full((nb, 1, 1), INT_MIN, I32))
    need = n_keep - (total(keys > t) + (key_new > t).astype(I32))
    nbits = past.bit_length()

    def jbit(bi, jj):
        cand = jj | lax.shift_left(jnp.int32(1), nbits - 1 - bi)
        g = total((keys == t) & (col < cand)) + ((key_new == t) & (past < cand)).astype(I32)
        return jnp.where(g < need, cand, jj)

    jj = lax.fori_loop(0, nbits, jbit, jnp.zeros((nb, 1, 1), I32))
    sel = (keys > t) | ((keys == t) & (col <= jj))
    sel_new = ((key_new > t) | ((key_new == t) & (past <= jj))).astype(I32)

    rows = nb * nrow
    incl_lane = (lax.broadcasted_iota(I32, (w, w), 0) <= lax.broadcasted_iota(I32, (w, w), 1)).astype(BF16)
    cnt_in_row = _dot(sel.astype(BF16).reshape(rows, w), incl_lane)
    row_tot = cnt_in_row[:, w - 1:w]
    ri = lax.broadcasted_iota(I32, (rows, rows), 0)
    rj = lax.broadcasted_iota(I32, (rows, rows), 1)
    earlier_row_same_seq = ((ri // nrow) == (rj // nrow)) & (rj < ri)
    row_off = _dot(earlier_row_same_seq.astype(BF16),
                   jnp.broadcast_to(row_tot, (rows, LANES)).astype(BF16))[:, 0:1]
    cnt3 = cnt_in_row.astype(BF16).reshape(nb, nrow, w)
    off3 = row_off.reshape(nb, nrow, 1)
    incl3 = (row_off + row_tot).reshape(nb, nrow, 1)
    jl = lax.broadcasted_iota(I32, (1, 1, n_keep), 2).astype(F32)
    row_j = jnp.sum((incl3 <= jl).astype(F32), axis=1, keepdims=True)
    onehot = lax.broadcasted_iota(I32, (nb, nrow, n_keep), 1).astype(F32) == row_j
    off_j = jnp.sum(jnp.where(onehot, off3, 0.0), axis=1, keepdims=True)
    local_rank = jl - off_j
    onehot_b = onehot.astype(BF16)
    page_i = lax.broadcasted_iota(I32, (n_pages, n_keep), 0)
    row_i = lax.broadcasted_iota(I32, (SUBLANES, n_keep), 0)
    for b in range(nb):
        cnt_j = lax.dot_general(cnt3[b], onehot_b[b], (((0,), (0,)), ((), ())),
                                preferred_element_type=F32)
        lane_j = jnp.sum((cnt_j <= local_rank[b]).astype(F32), axis=0, keepdims=True)
        pos = jnp.minimum((row_j[b] * w + lane_j).astype(I32), past - 1)
        phys = jnp.sum(jnp.where(page_i == pos // PAGE_SIZE, pt_ref[b], 0.0), axis=0, keepdims=True)
        tile = (phys.astype(I32) * PAGE_SIZE + pos % PAGE_SIZE) // (SUBLANES // N_KV_HEADS)
        meta_o[b] = jnp.where(row_i == 0, tile,
                              jnp.where(row_i == 1, sel_new[b], pos % (SUBLANES // N_KV_HEADS)))


def _decode_pick(keys, knew, pt_f):
    nb, nrow, w = keys.shape
    n_keep = min(TOPK_KEYS, (nrow * w + 1) // 4)
    full = lambda a: pl.BlockSpec(a.shape, lambda i: (0,) * a.ndim)
    return pl.pallas_call(
        functools.partial(_decode_pick_kernel, n_keep=n_keep),
        grid=(1,),
        in_specs=[full(keys), full(knew), full(pt_f)],
        out_specs=pl.BlockSpec((nb, SUBLANES, n_keep), lambda i: (0, 0, 0)),
        out_shape=jax.ShapeDtypeStruct((nb, SUBLANES, n_keep), I32),
        compiler_params=_params(1),
        name="decode_pick",
    )(keys, knew, pt_f)


def _decode_attend_kernel(tile_ref, q_ref, meta_ref, kn_ref, vn_ref, ck_hbm, cv_hbm, o_ref,
                          kbuf, vbuf, expand_ref, sem, *, n_keep):
    b = pl.program_id(0)
    nb = pl.num_programs(0)
    ncol = n_keep * SUBLANES

    def item_copies(bb, j, slot):
        src = pl.ds(pl.multiple_of(tile_ref[bb, j] * SUBLANES, SUBLANES), SUBLANES)
        dst = pl.ds(pl.multiple_of(j * SUBLANES, SUBLANES), SUBLANES)
        return (pltpu.make_async_copy(ck_hbm.at[src], kbuf.at[slot, dst], sem.at[0, slot]),
                pltpu.make_async_copy(cv_hbm.at[src], vbuf.at[slot, dst], sem.at[1, slot]))

    def start_batch(bb, slot):
        def body(j, carry):
            for prio, cp in enumerate(item_copies(bb, j, slot)):
                cp.start(priority=prio)
            return carry
        lax.fori_loop(0, n_keep, body, 0)

    def wait_batch(slot):
        pltpu.make_async_copy(ck_hbm.at[pl.ds(0, ncol)], kbuf.at[slot], sem.at[0, slot]).wait()
        pltpu.make_async_copy(cv_hbm.at[pl.ds(0, ncol)], vbuf.at[slot], sem.at[1, slot]).wait()

    @pl.when(b == 0)
    def _():
        start_batch(0, 0)
        item_of_col = lax.broadcasted_iota(I32, (n_keep, ncol), 1) // SUBLANES
        expand_ref[...] = (item_of_col == lax.broadcasted_iota(I32, (n_keep, ncol), 0)).astype(BF16)

    @pl.when(b + 1 < nb)
    def _():
        start_batch(b + 1, (b + 1) % 2)

    slot = b % 2
    wait_batch(slot)

    q8 = q_ref[0]
    meta = meta_ref[0]
    new_kept = meta[1:2, 0:1] > 0
    sub_col = _dot(meta.astype(BF16), expand_ref[...])[2:3, :]
    head_i = lax.broadcasted_iota(I32, (N_HEADS, ncol), 0)
    col_i = lax.broadcasted_iota(I32, (N_HEADS, ncol), 1)
    want = sub_col * N_KV_HEADS + (head_i // GROUP).astype(F32)
    is_new_item = new_kept & (col_i // SUBLANES == n_keep - 1)
    ok = ((col_i % SUBLANES).astype(F32) == want) & jnp.logical_not(is_new_item)

    s = jnp.where(ok, _dot_nt(q8, kbuf[slot].astype(BF16)), NEG)
    s_new = jnp.sum(q8.astype(F32) * kn_ref[0].astype(F32), axis=1, keepdims=True)
    s_new = jnp.where(new_kept, s_new, NEG)
    m = jnp.maximum(jnp.max(s, axis=1, keepdims=True), s_new)
    p = jnp.exp(s - m)
    p_new = jnp.exp(s_new - m)
    l = jnp.sum(p, axis=1, keepdims=True) + p_new
    acc = _dot(p.astype(BF16), vbuf[slot].astype(BF16)) + p_new.astype(BF16).astype(F32) * vn_ref[0].astype(F32)
    o_ref[0] = (acc / l).astype(BF16)


def _decode_attend(tiles, q8, meta, kn8, vn8, ck2, cv2):
    db, n_keep = tiles.shape
    per_b = lambda a: pl.BlockSpec((1,) + a.shape[1:], lambda b, tl: (b, 0, 0))
    any_spec = pl.BlockSpec(memory_space=pl.ANY)
    ncol = n_keep * SUBLANES
    grid_spec = pltpu.PrefetchScalarGridSpec(
        num_scalar_prefetch=1,
        grid=(db,),
        in_specs=[per_b(q8), per_b(meta), per_b(kn8), per_b(vn8), any_spec, any_spec],
        out_specs=pl.BlockSpec((1, N_HEADS, HEAD_DIM), lambda b, tl: (b, 0, 0)),
        scratch_shapes=[pltpu.VMEM((2, ncol, HEAD_DIM), F32), pltpu.VMEM((2, ncol, HEAD_DIM), F32),
                        pltpu.VMEM((n_keep, ncol), BF16), pltpu.SemaphoreType.DMA((2, 2))],
    )
    return pl.pallas_call(
        functools.partial(_decode_attend_kernel, n_keep=n_keep),
        grid_spec=grid_spec,
        out_shape=jax.ShapeDtypeStruct((db, N_HEADS, HEAD_DIM), BF16),
        compiler_params=_params(1),
        name="decode_attend",
    )(tiles, q8, meta, kn8, vn8, ck2, cv2)


def _layer_norm(r, g, b):
    mu = jnp.mean(r, axis=-1, keepdims=True)
    d = r - mu
    var = jnp.mean(d * d, axis=-1, keepdims=True)
    return d * lax.rsqrt(var + LN_EPS) * g + b


def _route(logits_t, rbias):
    n_exp, tm = logits_t.shape
    epg = n_exp // N_GROUPS
    s = jax.nn.sigmoid(logits_t)
    sb = s + rbias
    ie = lax.broadcasted_iota(I32, (epg, tm), 0)
    gs_rows = []
    for g in range(N_GROUPS):
        blk = sb[g * epg:(g + 1) * epg, :]
        m1 = jnp.max(blk, axis=0, keepdims=True)
        i1 = jnp.min(jnp.where(blk == m1, ie, epg), axis=0, keepdims=True)
        m2 = jnp.max(jnp.where(ie == i1, -jnp.inf, blk), axis=0, keepdims=True)
        gs_rows.append(m1 + m2)
    picked = [jnp.zeros((1, tm), jnp.bool_) for _ in range(N_GROUPS)]
    cur = list(gs_rows)
    for _ in range(TOPK_GROUPS):
        mx = cur[0]
        for g in range(1, N_GROUPS):
            mx = jnp.maximum(mx, cur[g])
        found = jnp.zeros((1, tm), jnp.bool_)
        for g in range(N_GROUPS):
            hit = (cur[g] == mx) & jnp.logical_not(found)
            found = found | hit
            picked[g] = picked[g] | hit
            cur[g] = jnp.where(hit, -jnp.inf, cur[g])
    masked = jnp.concatenate(
        [jnp.where(picked[g], sb[g * epg:(g + 1) * epg, :], NEG) for g in range(N_GROUPS)], axis=0)
    iall = lax.broadcasted_iota(I32, (n_exp, tm), 0)
    e_rows, w_rows = [], []
    for _ in range(TOP_K):
        mx = jnp.max(masked, axis=0, keepdims=True)
        ix = jnp.min(jnp.where(masked == mx, iall, n_exp), axis=0, keepdims=True)
        hit = iall == ix
        w_rows.append(jnp.sum(jnp.where(hit, s, 0.0), axis=0, keepdims=True))
        e_rows.append(ix)
        masked = jnp.where(hit, -jnp.inf, masked)
    wsum = w_rows[0]
    for w in w_rows[1:]:
        wsum = wsum + w
    gates = [w / wsum * ROUTED_SCALE for w in w_rows]
    return jnp.concatenate(e_rows, axis=0), jnp.concatenate(gates, axis=0)


def _post_attn_kernel(xp, xs, ap, as_, cp, cs, gap, gas, gcp, gcs, woa, woc, wo, g1, b1, rwt_hi, rwt_lo, rb,
                      h2_o, e_o, gate_o, *, n_prompt_tiles):
    i = pl.program_id(0)
    is_p = i < n_prompt_tiles
    pick = lambda a, b: jnp.where(is_p, a[...], b[...])
    a = _dot(pick(ap, as_), woa[...])
    c = _dot(pick(cp, cs), woc[...])
    merged = pick(gap, gas).astype(F32) * a + pick(gcp, gcs).astype(F32) * c
    r = ALPHA * pick(xp, xs) + _dot(merged.astype(BF16), wo[...])
    h = _layer_norm(r, g1[...], b1[...])
    tm = h.shape[0]
    for k in range(D_MODEL // LANES):
        h2_o[pl.ds(k, tm, stride=SUBLANES), :] = h[:, k * LANES:(k + 1) * LANES]
    h_hi = h.astype(BF16)
    h_lo = (h - h_hi.astype(F32)).astype(BF16)
    logits_t = _dot_nt(rwt_hi[...], h_hi) + (_dot_nt(rwt_hi[...], h_lo) + _dot_nt(rwt_lo[...], h_hi))
    e_idx, gates = _route(logits_t, rb[...])
    e_o[...] = e_idx
    gate_o[...] = gates


def _post_attn(x_p, x_s, attn_p, attn_s, conv_p, conv_s, ga_p, ga_s, gc_p, gc_s,
               woa, woc, wo, g1, b1, rwt, rb):
    rwt_hi = rwt.astype(BF16)
    rwt_lo = (rwt - rwt_hi.astype(F32)).astype(BF16)
    tp, ts = x_p.shape[0], x_s.shape[0]
    tm = TOKEN_TILE
    npt, nst = tp // tm, ts // tm
    n_tok = tp + ts
    p_spec = pl.BlockSpec((tm, D_MODEL), lambda i: (jnp.minimum(i, npt - 1), 0))
    s_spec = pl.BlockSpec((tm, D_MODEL), lambda i: (jnp.maximum(i - npt, 0), 0))
    consts = [woa, woc, wo, g1, b1, rwt_hi, rwt_lo, rb]
    return pl.pallas_call(
        functools.partial(_post_attn_kernel, n_prompt_tiles=npt),
        grid=(npt + nst,),
        in_specs=[p_spec, s_spec] * 5 + [_const_spec(c.shape) for c in consts],
        out_specs=[pl.BlockSpec((tm * SUBLANES, LANES), lambda i: (i, 0)),
                   pl.BlockSpec((TOP_K, tm), lambda i: (0, i)),
                   pl.BlockSpec((TOP_K, tm), lambda i: (0, i))],
        out_shape=[jax.ShapeDtypeStruct((n_tok * SUBLANES, LANES), F32),
                   jax.ShapeDtypeStruct((TOP_K, n_tok), I32),
                   jax.ShapeDtypeStruct((TOP_K, n_tok), F32)],
        compiler_params=_params(1),
        name="post_attn",
    )(x_p, x_s, attn_p, attn_s, conv_p, conv_s, ga_p, ga_s, gc_p, gc_s, *consts)


def _rank_kernel(e_ref, rank_o, cnt_o, carry_ref):
    i = pl.program_id(0)

    @pl.when(i == 0)
    def _():
        carry_ref[...] = jnp.zeros_like(carry_ref)

    n_exp = carry_ref.shape[0]
    e = e_ref[...]
    tk = e.shape[1]
    ie = lax.broadcasted_iota(I32, (n_exp, tk), 0)
    onehot = jnp.zeros((n_exp, tk), F32)
    for j in range(TOP_K):
        onehot = onehot + (ie == e[j:j + 1, :]).astype(F32)
    before = (lax.broadcasted_iota(I32, (tk, tk), 0) < lax.broadcasted_iota(I32, (tk, tk), 1)).astype(BF16)
    prefix = _dot(onehot.astype(BF16), before) + carry_ref[:, 0:1]
    rows = [jnp.sum(jnp.where(ie == e[j:j + 1, :], prefix, 0.0), axis=0, keepdims=True) for j in range(TOP_K)]
    rank_o[...] = jnp.concatenate(rows, axis=0).astype(I32)
    carry_ref[...] = carry_ref[...] + jnp.sum(onehot, axis=1, keepdims=True)
    cnt_o[...] = carry_ref[...].astype(I32)


def _rank(e_t, n_exp):
    n_tok = e_t.shape[1]
    tk = TOKEN_TILE
    return pl.pallas_call(
        _rank_kernel,
        grid=(n_tok // tk,),
        in_specs=[pl.BlockSpec((TOP_K, tk), lambda i: (0, i))],
        out_specs=[pl.BlockSpec((TOP_K, tk), lambda i: (0, i)),
                   pl.BlockSpec((n_exp, LANES), lambda i: (0, 0))],
        out_shape=[jax.ShapeDtypeStruct((TOP_K, n_tok), I32), jax.ShapeDtypeStruct((n_exp, LANES), I32)],
        scratch_shapes=[pltpu.VMEM((n_exp, LANES), F32)],
        compiler_params=_params(1),
        name="moe_rank",
    )(e_t)


def _slot_kernel(e_ref, rank_ref, pstart_ref, slot_o):
    e = e_ref[...]
    n_exp = pstart_ref.shape[0]
    tk = e.shape[1]
    ie = lax.broadcasted_iota(I32, (n_exp, tk), 0)
    ps = pstart_ref[:, 0:1]
    rows = [jnp.sum(jnp.where(ie == e[j:j + 1, :], ps, 0.0), axis=0, keepdims=True) for j in range(TOP_K)]
    slot_o[...] = rank_ref[...] + jnp.concatenate(rows, axis=0).astype(I32)


def _slots(e_t, rank_t, pstart_f):
    n_tok = e_t.shape[1]
    tk = TOKEN_TILE
    spec = pl.BlockSpec((TOP_K, tk), lambda i: (0, i))
    return pl.pallas_call(
        _slot_kernel,
        grid=(n_tok // tk,),
        in_specs=[spec, spec, _const_spec(pstart_f.shape)],
        out_specs=spec,
        out_shape=jax.ShapeDtypeStruct((TOP_K, n_tok), I32),
        compiler_params=_params(1),
        name="moe_slots",
    )(e_t, rank_t, pstart_f)


def _dispatch_kernel(pend_ref, pad_ref, slot_ref, h2_ref, xs_hbm, zbuf, sem, zsem, *, n_exp):
    i = pl.program_id(0)
    td = slot_ref.shape[0] // TOP_K
    blk_rows = MOE_BLOCK * SUBLANES
    pieces = [1 << b for b in reversed(range((MOE_BLOCK - 1).bit_length()))]

    def pad_copies(e, act):
        pad = pad_ref[e]
        off = pend_ref[e] - pad
        for piece in pieces:
            dst = pl.ds(pl.multiple_of(off * SUBLANES, SUBLANES), piece * SUBLANES)
            cp = pltpu.make_async_copy(zbuf.at[pl.ds(0, piece * SUBLANES)], xs_hbm.at[dst], zsem)

            @pl.when((pad & piece) != 0)
            def _():
                act(cp)

            off = off + (pad & piece)

    @pl.when(i == 0)
    def _():
        zbuf[...] = jnp.zeros_like(zbuf)

        def start(e, carry):
            pad_copies(e, lambda cp: cp.start())
            return carry

        def wait(e, carry):
            pad_copies(e, lambda cp: cp.wait())
            return carry

        lax.fori_loop(0, n_exp, start, 0)
        lax.fori_loop(0, n_exp, wait, 0)

        def tail_copy(b):
            return pltpu.make_async_copy(zbuf, xs_hbm.at[pl.ds(pl.multiple_of(b * blk_rows, blk_rows), blk_rows)], zsem)

        def tail_start(b, carry):
            tail_copy(b).start()
            return carry

        def tail_wait(b, carry):
            tail_copy(b).wait()
            return carry

        first_unused = pend_ref[n_exp - 1] // MOE_BLOCK
        lax.fori_loop(first_unused, xs_hbm.shape[0] // blk_rows, tail_start, 0)
        lax.fori_loop(first_unused, xs_hbm.shape[0] // blk_rows, tail_wait, 0)

    def row_copy(t, j):
        src = pl.ds(pl.multiple_of(t * SUBLANES, SUBLANES), SUBLANES)
        dst = pl.ds(pl.multiple_of(slot_ref[t * TOP_K + j] * SUBLANES, SUBLANES), SUBLANES)
        return pltpu.make_async_copy(h2_ref.at[src], xs_hbm.at[dst], sem)

    def start_tok(t, carry):
        for j in range(TOP_K):
            row_copy(t, j).start(priority=j % 2)
        return carry

    lax.fori_loop(0, td, start_tok, 0)
    for j in range(TOP_K):
        pltpu.make_async_copy(h2_ref, xs_hbm.at[pl.ds(0, td * SUBLANES)], sem).wait()


def _dispatch(pend, pad, slot_flat, h2, m_pad):
    n_tok = slot_flat.shape[0] // TOP_K
    td = TOKEN_TILE
    n_exp = pend.shape[0]
    grid_spec = pltpu.PrefetchScalarGridSpec(
        num_scalar_prefetch=2,
        grid=(n_tok // td,),
        in_specs=[pl.BlockSpec((td * TOP_K,), lambda i, *_: (i,), memory_space=pltpu.SMEM),
                  pl.BlockSpec((td * SUBLANES, LANES), lambda i, *_: (i, 0))],
        out_specs=pl.BlockSpec(memory_space=pl.ANY),
        scratch_shapes=[pltpu.VMEM((MOE_BLOCK * SUBLANES, LANES), F32),
                        pltpu.SemaphoreType.DMA(()), pltpu.SemaphoreType.DMA(())],
    )
    return pl.pallas_call(
        functools.partial(_dispatch_kernel, n_exp=n_exp),
        grid_spec=grid_spec,
        out_shape=jax.ShapeDtypeStruct((m_pad * SUBLANES, LANES), F32),
        compiler_params=_params(1),
        name="moe_dispatch",
    )(pend, pad, slot_flat, h2)


def _tile_rows(ref, n):
    return jnp.concatenate([ref[pl.ds(k, n, stride=SUBLANES), :] for k in range(D_MODEL // LANES)], axis=1)


PLAN_EXPERT, PLAN_SLOT, PLAN_NEXT, PLAN_HAS_NEXT = range(4)


def _expert_kernel(plan_ref, nused_ref, x_ref, wg_hbm, wu_hbm, wd_hbm, y_ref, wg_f, wu_f, wd_f, wgu_b, wd_b, sem):
    i = pl.program_id(0)
    d_exp = wd_b.shape[0]

    def weight_copies(e, s):
        return (pltpu.make_async_copy(wg_hbm.at[e], wg_f.at[s], sem.at[s]),
                pltpu.make_async_copy(wu_hbm.at[e], wu_f.at[s], sem.at[s]),
                pltpu.make_async_copy(wd_hbm.at[e], wd_f.at[s], sem.at[s]))

    @pl.when(i < nused_ref[0])
    def _():
        e = plan_ref[PLAN_EXPERT, i]
        changed = (i == 0) | (e != plan_ref[PLAN_EXPERT, jnp.maximum(i - 1, 0)])

        @pl.when(changed)
        def _():
            s = plan_ref[PLAN_SLOT, i]

            @pl.when(i == 0)
            def _():
                for cp in weight_copies(e, s):
                    cp.start()

            for cp in weight_copies(e, s):
                cp.wait()

            @pl.when(plan_ref[PLAN_HAS_NEXT, i] > 0)
            def _():
                for cp in weight_copies(plan_ref[PLAN_NEXT, i], 1 - s):
                    cp.start()

            wgu_b[:, :d_exp] = wg_f[s].astype(BF16)
            wgu_b[:, d_exp:] = wu_f[s].astype(BF16)
            wd_b[...] = wd_f[s].astype(BF16)

        x = _tile_rows(x_ref, MOE_BLOCK).astype(BF16)
        gu = _dot(x, wgu_b[...])
        hh = jax.nn.silu(gu[:, :d_exp]) * gu[:, d_exp:]
        y = _dot(hh.astype(BF16), wd_b[...])
        for k in range(D_MODEL // LANES):
            y_ref[pl.ds(k, MOE_BLOCK, stride=SUBLANES), :] = y[:, k * LANES:(k + 1) * LANES]

    @pl.when(i >= nused_ref[0])
    def _():
        y_ref[...] = jnp.zeros_like(y_ref)


def _expert_plan(block_e, nused):
    n_blocks = block_e.shape[0]
    idx = jnp.arange(n_blocks, dtype=I32)
    changed = (idx == 0) | (block_e != jnp.roll(block_e, 1))
    slot = (jnp.cumsum(changed.astype(I32)) - 1) % 2
    change_pos = jnp.where(changed & (idx < nused), idx, n_blocks)
    next_pos = jnp.concatenate([lax.cummin(change_pos[::-1])[::-1][1:], jnp.full((1,), n_blocks, I32)])
    has_next = (next_pos < n_blocks).astype(I32)
    nxt = block_e[jnp.minimum(next_pos, n_blocks - 1)]
    return jnp.stack([block_e, slot, nxt, has_next]).astype(I32)


def _experts(block_e, nused, xs, w_gate, w_up, w_down):
    n_blocks = block_e.shape[0]
    d_exp = w_gate.shape[2]
    rows = MOE_BLOCK * SUBLANES
    last = lambda i, nu: jnp.minimum(i, nu[0] - 1)
    any_spec = pl.BlockSpec(memory_space=pl.ANY)
    grid_spec = pltpu.PrefetchScalarGridSpec(
        num_scalar_prefetch=2,
        grid=(n_blocks,),
        in_specs=[pl.BlockSpec((rows, LANES), lambda i, plan, nu: (last(i, nu), 0)), any_spec, any_spec, any_spec],
        out_specs=pl.BlockSpec((rows, LANES), lambda i, plan, nu: (i, 0)),
        scratch_shapes=[pltpu.VMEM((2, D_MODEL, d_exp), F32), pltpu.VMEM((2, D_MODEL, d_exp), F32),
                        pltpu.VMEM((2, d_exp, D_MODEL), F32),
                        pltpu.VMEM((D_MODEL, 2 * d_exp), BF16), pltpu.VMEM((d_exp, D_MODEL), BF16),
                        pltpu.SemaphoreType.DMA((2,))],
    )
    return pl.pallas_call(
        _expert_kernel,
        grid_spec=grid_spec,
        out_shape=jax.ShapeDtypeStruct(xs.shape, F32),
        compiler_params=_params(1),
        name="moe_experts",
    )(_expert_plan(block_e, nused[0]), nused, xs, w_gate, w_up, w_down)


def _combine_kernel(slot_ref, slot_next_ref, gate_ref, h2_ref, ys_hbm, shg, shu, shd, g2, b2, yp_o, ys_o,
                    buf, base_ref, y_ref, sem, *, n_prompt_tiles):
    i = pl.program_id(0)
    n = pl.num_programs(0)
    tc = gate_ref.shape[0]

    def start_tile(s_ref, bslot):
        def body(t, carry):
            dst = pl.ds(pl.multiple_of(t * SUBLANES, SUBLANES), SUBLANES)
            for j in range(TOP_K):
                src = pl.ds(pl.multiple_of(s_ref[t * TOP_K + j] * SUBLANES, SUBLANES), SUBLANES)
                pltpu.make_async_copy(ys_hbm.at[src], buf.at[bslot, j, dst], sem.at[bslot]).start(priority=j % 2)
            return carry
        lax.fori_loop(0, tc, body, 0)

    def wait_tile(bslot):
        for j in range(TOP_K):
            pltpu.make_async_copy(ys_hbm.at[pl.ds(0, tc * SUBLANES)], buf.at[bslot, j], sem.at[bslot]).wait()

    @pl.when(i == 0)
    def _():
        start_tile(slot_ref, 0)

    cur = i % 2
    h = _tile_rows(h2_ref, tc)
    hb = h.astype(BF16)
    shared = _dot((jax.nn.silu(_dot(hb, shg[...])) * _dot(hb, shu[...])).astype(BF16), shd[...])
    base_ref[...] = ALPHA * h + shared
    wait_tile(cur)
    gain, bias = g2[...], b2[...]
    grp = SUBLANES

    def finish_group(g):
        r0 = pl.multiple_of(g * grp, grp)
        t0 = pl.multiple_of(g * grp * SUBLANES, grp * SUBLANES)
        routed = jnp.zeros((grp, D_MODEL), F32)
        for j in range(TOP_K):
            rows = jnp.concatenate([buf[cur, j, pl.ds(t0 + k, grp, stride=SUBLANES), :]
                                    for k in range(D_MODEL // LANES)], axis=1)
            routed = routed + rows * gate_ref[pl.ds(r0, grp), j:j + 1]
        y_ref[pl.ds(r0, grp), :] = base_ref[pl.ds(r0, grp), :] + routed

    @pl.when(i + 1 < n)
    def _():
        nxt = (i + 1) % 2

        def body(g, carry):
            for tt in range(grp):
                t = g * grp + tt
                dst = pl.ds(pl.multiple_of(t * SUBLANES, SUBLANES), SUBLANES)
                for j in range(TOP_K):
                    src = pl.ds(pl.multiple_of(slot_next_ref[t * TOP_K + j] * SUBLANES, SUBLANES), SUBLANES)
                    pltpu.make_async_copy(ys_hbm.at[src], buf.at[nxt, j, dst], sem.at[nxt]).start(priority=j % 2)
            finish_group(g)
            return carry

        lax.fori_loop(0, tc // grp, body, 0)

    @pl.when(i + 1 >= n)
    def _():
        def body(g, carry):
            finish_group(g)
            return carry

        lax.fori_loop(0, tc // grp, body, 0)

    y = _layer_norm(y_ref[...], gain, bias)

    @pl.when(i < n_prompt_tiles)
    def _():
        yp_o[...] = y

    @pl.when(i >= n_prompt_tiles)
    def _():
        ys_o[...] = y


def _combine(slot_flat, gates, h2, ys, shg, shu, shd, g2, b2, n_prompt, n_decode):
    tc = TOKEN_TILE
    npt, nst = n_prompt // tc, n_decode // tc
    consts = [shg, shu, shd, g2, b2]
    return pl.pallas_call(
        functools.partial(_combine_kernel, n_prompt_tiles=npt),
        grid=(npt + nst,),
        in_specs=[pl.BlockSpec((tc * TOP_K,), lambda i: (i,), memory_space=pltpu.SMEM),
                  pl.BlockSpec((tc * TOP_K,), lambda i: (jnp.minimum(i + 1, npt + nst - 1),),
                               memory_space=pltpu.SMEM),
                  pl.BlockSpec((tc, TOP_K), lambda i: (i, 0)),
                  pl.BlockSpec((tc * SUBLANES, LANES), lambda i: (i, 0)),
                  pl.BlockSpec(memory_space=pl.ANY)] + [_const_spec(c.shape) for c in consts],
        out_specs=[pl.BlockSpec((tc, D_MODEL), lambda i: (jnp.minimum(i, npt - 1), 0)),
                   pl.BlockSpec((tc, D_MODEL), lambda i: (jnp.maximum(i - npt, 0), 0))],
        out_shape=[jax.ShapeDtypeStruct((n_prompt, D_MODEL), F32),
                   jax.ShapeDtypeStruct((n_decode, D_MODEL), F32)],
        scratch_shapes=[pltpu.VMEM((2, TOP_K, tc * SUBLANES, LANES), F32), pltpu.VMEM((tc, D_MODEL), F32),
                        pltpu.VMEM((tc, D_MODEL), F32), pltpu.SemaphoreType.DMA((2,))],
        compiler_params=_params(1),
        name="moe_combine",
    )(slot_flat, slot_flat, gates, h2, ys, *consts)


def _pad_rows(a, n):
    return jnp.pad(a, ((0, n - a.shape[0]),) + ((0, 0),) * (a.ndim - 1))


def kernel(x_prompt, x_sample, cache_k, cache_v, cache_kidx, page_table, state_conv, w_in, conv_w, w_o_attn,
           w_o_conv, w_o, ln1_g, ln1_b, router_w, router_bias, moe_w_gate, moe_w_up, moe_w_down,
           shared_w_gate, shared_w_up, shared_w_down, ln2_g, ln2_b):
    nb, seq, _ = x_prompt.shape
    db = x_sample.shape[0]
    n_pool = cache_k.shape[1]
    n_exp = router_w.shape[-1]
    tp = nb * seq
    ts = TOKEN_TILE
    nkv = N_KV_HEADS * HEAD_DIM

    weights = _prep_in_weights(w_in[0])
    cw = conv_w[0]

    xp2 = x_prompt.reshape(tp, D_MODEL)
    (q_p, k_p, v_p, kvb_p, qi_p, idx_p, kib_p, conv_p, ga_p, gc_p, ulast_p) = _in_proj_prompt(xp2, weights, cw, nb)
    xs2 = _pad_rows(x_sample.reshape(db, D_MODEL), ts)
    prev2 = _pad_rows(state_conv[0, :, 0, :], ts)
    prev1 = _pad_rows(state_conv[0, :, 1, :], ts)
    (q_s, k_s, v_s, _, qi_s, idx_s, _, conv_s, ga_s, gc_s, u_s) = _in_proj_decode(xs2, prev2, prev1, weights, cw)

    attn_p = _attn_prompt(q_p, kvb_p, qi_p, kib_p, idx_p, nb)
    q8 = q_s[:db].reshape(db, N_HEADS, HEAD_DIM)
    qi8 = qi_s[:db].reshape(db, IDX_HEADS, LANES)[:, :, :IDX_DIM]
    wi8 = idx_s[:db, IDX_DIM:IDX_DIM + IDX_HEADS].reshape(db, IDX_HEADS, 1)
    kin = idx_s[:db, :IDX_DIM].astype(BF16).reshape(db, 1, IDX_DIM)
    k_new = k_s[:db * N_KV_HEADS].reshape(db, N_KV_HEADS, HEAD_DIM)
    v_new = v_s[:db * N_KV_HEADS].reshape(db, N_KV_HEADS, HEAD_DIM)
    kn8 = jnp.repeat(k_new, GROUP, axis=1).astype(BF16)
    vn8 = jnp.repeat(v_new, GROUP, axis=1).astype(BF16)
    ck2 = cache_k[0].reshape(n_pool * PAGE_SIZE * N_KV_HEADS, HEAD_DIM)
    cv2 = cache_v[0].reshape(n_pool * PAGE_SIZE * N_KV_HEADS, HEAD_DIM)
    keys_s, knew_s = _decode_scores(page_table, qi8, wi8, kin, jnp.swapaxes(cache_kidx[0], 1, 2))
    meta = _decode_pick(keys_s, knew_s, page_table.astype(F32)[:, :, None])
    attn_s8 = _decode_attend(meta[:, 0, :], q8, meta, kn8, vn8, ck2, cv2)
    attn_s = _pad_rows(attn_s8.reshape(db, N_HEADS * HEAD_DIM), ts)

    h2, e_t, gate_t = _post_attn(
        xp2, xs2, attn_p, attn_s, conv_p, conv_s, ga_p, ga_s, gc_p, gc_s,
        w_o_attn[0].astype(BF16), w_o_conv[0].astype(BF16), w_o[0].astype(BF16),
        ln1_g[0].reshape(1, D_MODEL), ln1_b[0].reshape(1, D_MODEL),
        router_w[0].T, router_bias[0].reshape(n_exp, 1))

    n_tok = tp + ts
    rank_t, cnt = _rank(e_t, n_exp)
    counts = cnt[:, 0]
    padded = (counts + MOE_BLOCK - 1) // MOE_BLOCK * MOE_BLOCK
    pend = jnp.cumsum(padded)
    pstart = pend - padded
    n_blocks = (n_tok * TOP_K + n_exp * (MOE_BLOCK - 1) + MOE_BLOCK - 1) // MOE_BLOCK
    nused = (pend[-1] // MOE_BLOCK).astype(I32)
    blk = jnp.minimum(jnp.arange(n_blocks, dtype=I32), nused - 1)
    block_e = jnp.minimum(jnp.sum((pend[None, :] <= (blk * MOE_BLOCK)[:, None]).astype(I32), axis=1), n_exp - 1)
    slot_t = _slots(e_t, rank_t, jnp.broadcast_to(pstart.astype(F32)[:, None], (n_exp, LANES)))
    slot_flat = slot_t.T.reshape(-1)
    xs = _dispatch(pend.astype(I32), (padded - counts).astype(I32), slot_flat, h2, n_blocks * MOE_BLOCK)
    ys = _experts(block_e, nused.reshape(1), xs, moe_w_gate[0], moe_w_up[0], moe_w_down[0])
    y_p, y_s = _combine(slot_flat, gate_t.T, h2, ys,
                        shared_w_gate[0].astype(BF16), shared_w_up[0].astype(BF16),
                        shared_w_down[0].astype(BF16),
                        ln2_g[0].reshape(1, D_MODEL), ln2_b[0].reshape(1, D_MODEL), tp, ts)

    conv_sample = jnp.stack([state_conv[0, :, 1, :], u_s[:db]], axis=1)[None]
    return (y_p.reshape(nb, seq, D_MODEL),
            y_s[:db].reshape(db, 1, D_MODEL),
            k_p.reshape(1, nb, seq, N_KV_HEADS, HEAD_DIM),
            v_p.reshape(1, nb, seq, N_KV_HEADS, HEAD_DIM),
            idx_p[:, :IDX_DIM].reshape(1, nb, seq, IDX_DIM),
            ulast_p[:, SUBLANES - (CONV_W - 1):, :][None],
            k_new.reshape(1, db, 1, N_KV_HEADS, HEAD_DIM),
            v_new.reshape(1, db, 1, N_KV_HEADS, HEAD_DIM),
            idx_s[:db, :IDX_DIM].reshape(1, db, 1, IDX_DIM),
            conv_sample)
```

```python
import functools

import jax
import jax.numpy as jnp
import numpy as np
from jax import lax
from jax.experimental import pallas as pl
from jax.experimental.pallas import tpu as pltpu

F32 = jnp.float32
BF16 = jnp.bfloat16
I32 = jnp.int32

D_MODEL = 1024
N_HEADS = 8
HEAD_DIM = 128
N_KV_HEADS = 2
GROUP = N_HEADS // N_KV_HEADS
IDX_HEADS = 8
IDX_DIM = 64
TOPK_KEYS = 256
IDX_SCALE = IDX_DIM ** -0.5 * IDX_HEADS ** -0.5
Q_SCALE = HEAD_DIM ** -0.5
PAGE_SIZE = 128
D_CONV = D_MODEL
CONV_W = 3
TOP_K = 8
N_GROUPS = 8
TOPK_GROUPS = 4
ROUTED_SCALE = 2.5
DEPTH = 1
ALPHA = (2 * DEPTH) ** 0.25
LN_EPS = 1e-5
NEG = -1e30
IN_WIDTHS = (N_HEADS * HEAD_DIM, N_KV_HEADS * HEAD_DIM, N_KV_HEADS * HEAD_DIM,
             IDX_HEADS * IDX_DIM, IDX_DIM, IDX_HEADS,
             D_CONV, D_CONV, D_CONV, D_MODEL, D_MODEL)

LANES = 128
SUBLANES = 8
VMEM_LIMIT_BYTES = 56 * 1024 * 1024

PROJ_ROWS = 512
ATT_Q = 256
TOKEN_TILE = 256
MOE_BLOCK = 256
CONV_CHUNK = 256
INT_MIN = -2 ** 31
SOFTMAX_DENOM_MIN = 1e-25
F32_MIN_NORMAL = float(np.finfo(np.float32).tiny)
BF16_MIN_NORMAL_BITS = 0x0080


def _np_key(v):
    b = int(np.float32(v).view(np.int32))
    return b ^ ((b >> 31) & 0x7FFFFFFF)


KEY_HALF = _np_key(NEG * 0.5)


def _sort_key(s):
    b = lax.bitcast_convert_type(s + 0.0, I32)
    return b ^ ((b >> 31) & jnp.int32(0x7FFFFFFF))


def _dot(a, b):
    return jnp.dot(a, b, preferred_element_type=F32)


def _dot_nt(a, b):
    return lax.dot_general(a, b, (((1,), (1,)), ((), ())), preferred_element_type=F32)


def _params(n_grid):
    return pltpu.CompilerParams(dimension_semantics=("arbitrary",) * n_grid,
                                vmem_limit_bytes=VMEM_LIMIT_BYTES)


def _const_spec(shape):
    nd = len(shape)
    return pl.BlockSpec(shape, lambda *_: (0,) * nd, pipeline_mode=pl.Buffered(1))


def _proj_common(xb, w1, w2, w3, wga, wgc, q_o, k_o, v_o, kv_o, qi_o, idx_o, kib_o, sga_o, sgc_o):
    nq = N_HEADS * HEAD_DIM
    nkv = N_KV_HEADS * HEAD_DIM
    z1 = _dot(xb, w1[...])
    q_o[...] = (z1[:, :nq] * Q_SCALE).astype(BF16)
    k = z1[:, nq:nq + nkv]
    v = z1[:, nq + nkv:nq + 2 * nkv]
    for g in range(N_KV_HEADS):
        k_o[pl.ds(g, k.shape[0], stride=N_KV_HEADS), :] = k[:, g * HEAD_DIM:(g + 1) * HEAD_DIM]
        v_o[pl.ds(g, v.shape[0], stride=N_KV_HEADS), :] = v[:, g * HEAD_DIM:(g + 1) * HEAD_DIM]
    kv_o[:, :nkv] = k.astype(BF16)
    kv_o[:, nkv:] = v.astype(BF16)
    qi_o[...] = _dot(xb, w2[...]).astype(BF16)
    z3 = _dot(xb, w3[...])
    idx_o[...] = z3
    kib_o[...] = z3.astype(BF16)
    sga_o[...] = jax.nn.sigmoid(_dot(xb, wga[...])).astype(BF16)
    sgc_o[...] = jax.nn.sigmoid(_dot(xb, wgc[...])).astype(BF16)


def _in_proj_prompt_kernel(x_ref, w1, w2, w3, wb, wc, wx, wga, wgc, cw_ref,
                           q_o, k_o, v_o, kv_o, qi_o, idx_o, kib_o, conv_o, sga_o, sgc_o, ulast_o,
                           tail_ref):
    j = pl.program_id(1)

    @pl.when(j == 0)
    def _():
        tail_ref[...] = jnp.zeros_like(tail_ref)

    xb = x_ref[...].astype(BF16)
    _proj_common(xb, w1, w2, w3, wga, wgc, q_o, k_o, v_o, kv_o, qi_o, idx_o, kib_o, sga_o, sgc_o)
    tm = xb.shape[0]
    row = lax.broadcasted_iota(I32, (tm, CONV_CHUNK), 0)
    for c in range(D_CONV // CONV_CHUNK):
        cs = slice(c * CONV_CHUNK, (c + 1) * CONV_CHUNK)
        gb = _dot(xb, wb[:, cs])
        u = _dot(xb, wc[:, cs]) * _dot(xb, wx[:, cs])
        p1 = tail_ref[SUBLANES - 1:SUBLANES, cs]
        p2 = tail_ref[SUBLANES - 2:SUBLANES - 1, cs]
        u1 = jnp.where(row == 0, p1, pltpu.roll(u, 1, 0))
        u2 = jnp.where(row == 0, p2, jnp.where(row == 1, p1, pltpu.roll(u, 2, 0)))
        conv = cw_ref[0:1, cs] * u2 + cw_ref[1:2, cs] * u1 + cw_ref[2:3, cs] * u
        conv_o[:, cs] = (gb * conv).astype(BF16)
        tail_ref[:, cs] = u[tm - SUBLANES:, :]
    ulast_o[0] = tail_ref[...]


def _in_proj_decode_kernel(x_ref, p2_ref, p1_ref, w1, w2, w3, wb, wc, wx, wga, wgc, cw_ref,
                           q_o, k_o, v_o, kv_o, qi_o, idx_o, kib_o, conv_o, sga_o, sgc_o, u_o):
    xb = x_ref[...].astype(BF16)
    _proj_common(xb, w1, w2, w3, wga, wgc, q_o, k_o, v_o, kv_o, qi_o, idx_o, kib_o, sga_o, sgc_o)
    for c in range(D_CONV // CONV_CHUNK):
        cs = slice(c * CONV_CHUNK, (c + 1) * CONV_CHUNK)
        gb = _dot(xb, wb[:, cs])
        u = _dot(xb, wc[:, cs]) * _dot(xb, wx[:, cs])
        conv = cw_ref[0:1, cs] * p2_ref[:, cs] + cw_ref[1:2, cs] * p1_ref[:, cs] + cw_ref[2:3, cs] * u
        conv_o[:, cs] = (gb * conv).astype(BF16)
        u_o[:, cs] = u


def _prep_in_weights(w_in):
    w = w_in.astype(BF16)
    offs = np.concatenate([[0], np.cumsum(IN_WIDTHS)])
    part = [w[:, int(offs[i]):int(offs[i + 1])] for i in range(len(IN_WIDTHS))]
    q, k, v, qi, ki, wi, gb, gc, xv, ga, gcv = part
    d = w.shape[0]
    w1 = jnp.concatenate([q, k, v], axis=1)
    w2 = jnp.pad(qi.reshape(d, IDX_HEADS, IDX_DIM),
                 ((0, 0), (0, 0), (0, LANES - IDX_DIM))).reshape(d, IDX_HEADS * LANES)
    w3 = jnp.concatenate([ki, wi, jnp.zeros((d, LANES - IDX_DIM - IDX_HEADS), BF16)], axis=1)
    return (w1, w2, w3, gb, gc, xv, ga, gcv)


def _proj_out_shapes(t):
    nkv = N_KV_HEADS * HEAD_DIM
    sd = jax.ShapeDtypeStruct
    return [sd((t, N_HEADS * HEAD_DIM), BF16), sd((t * N_KV_HEADS, HEAD_DIM), F32),
            sd((t * N_KV_HEADS, HEAD_DIM), F32), sd((t, 2 * nkv), BF16),
            sd((t, IDX_HEADS * LANES), BF16), sd((t, LANES), F32), sd((t, LANES), BF16),
            sd((t, D_CONV), BF16), sd((t, D_MODEL), BF16), sd((t, D_MODEL), BF16)]


def _in_proj_prompt(x2d, weights, conv_w, nb):
    t = x2d.shape[0]
    s = t // nb
    tm = min(PROJ_ROWS, s)
    nj = s // tm
    row_spec = lambda w, rep=1: pl.BlockSpec((tm * rep, w), lambda b, j: (b * nj + j, 0))
    out_shapes = _proj_out_shapes(t) + [jax.ShapeDtypeStruct((nb, SUBLANES, D_CONV), F32)]
    out_specs = [row_spec(o.shape[1], o.shape[0] // t) for o in out_shapes[:-1]]
    out_specs.append(pl.BlockSpec((1, SUBLANES, D_CONV), lambda b, j: (b, 0, 0)))
    return pl.pallas_call(
        _in_proj_prompt_kernel,
        grid=(nb, nj),
        in_specs=[row_spec(D_MODEL)] + [_const_spec(w.shape) for w in weights] + [_const_spec(conv_w.shape)],
        out_specs=out_specs,
        out_shape=out_shapes,
        scratch_shapes=[pltpu.VMEM((SUBLANES, D_CONV), F32)],
        compiler_params=_params(2),
        name="in_proj_prompt",
    )(x2d, *weights, conv_w)


def _in_proj_decode(x2d, p2, p1, weights, conv_w):
    t = x2d.shape[0]
    row_spec = lambda w, rep=1: pl.BlockSpec((t * rep, w), lambda i: (0, 0))
    out_shapes = _proj_out_shapes(t) + [jax.ShapeDtypeStruct((t, D_CONV), F32)]
    return pl.pallas_call(
        _in_proj_decode_kernel,
        grid=(1,),
        in_specs=[row_spec(D_MODEL)] * 3 + [_const_spec(w.shape) for w in weights] + [_const_spec(conv_w.shape)],
        out_specs=[row_spec(o.shape[1], o.shape[0] // t) for o in out_shapes],
        out_shape=out_shapes,
        compiler_params=_params(1),
        name="in_proj_decode",
    )(x2d, p2, p1, *weights, conv_w)


def _attn_prompt_kernel(q_ref, kv_ref, qi_ref, kib_ref, idx_ref, o_ref,
                        key_ref, hi_ref, bias_ref, t_ref, j_ref, need_ref, m_ref, acc_ref, knorm_ref,
                        *, n_keep, seq_len):
    tq = q_ref.shape[0]
    i = pl.program_id(1)
    nchunk = i + 1
    nsub = tq // LANES

    idx_t = jnp.transpose(idx_ref[...])
    wi_rows = [idx_t[IDX_DIM + h:IDX_DIM + h + 1, :] * IDX_SCALE for h in range(IDX_HEADS)]
    key_i = lax.broadcasted_iota(I32, (tq, tq), 0)
    qry_i = lax.broadcasted_iota(I32, (tq, tq), 1)

    def score_chunk(c, carry):
        kic = kib_ref[pl.ds(pl.multiple_of(c * tq, tq), tq), :]
        acc = jnp.zeros((tq, tq), F32)
        for h in range(IDX_HEADS):
            d = _dot_nt(kic, qi_ref[:, h * LANES:(h + 1) * LANES])
            acc = acc + jnp.maximum(d, 0.0) * wi_rows[h]
        acc = jnp.where(jnp.abs(acc) < F32_MIN_NORMAL, 0.0, acc)
        s = jnp.where((c < i) | (key_i <= qry_i), acc, NEG)
        bits = lax.bitcast_convert_type(s, I32)
        key_ref[c] = bits ^ ((bits >> 31) & jnp.int32(0x7FFFFFFF))
        hi_ref[c] = lax.bitcast_convert_type(bits & jnp.int32(-65536), F32).astype(BF16)
        return carry

    lax.fori_loop(0, nchunk, score_chunk, 0)

    sub_i = lax.broadcasted_iota(I32, (SUBLANES, tq), 0)

    def counts(preds, with_pos=False):
        def body(c, cnts):
            for kk in range(tq // SUBLANES):
                blk = key_ref[c, kk * SUBLANES:(kk + 1) * SUBLANES, :]
                pos = (c * tq + kk * SUBLANES + sub_i) if with_pos else None
                cnts = tuple(cnt + pred(blk, pos).astype(I32) for cnt, pred in zip(cnts, preds))
            return cnts
        cnts = lax.fori_loop(0, nchunk, body, tuple(jnp.zeros((SUBLANES, tq), I32) for _ in preds))
        return tuple(jnp.sum(cnt, axis=0, keepdims=True) for cnt in cnts)

    def count_hi(cand):
        rows16 = 2 * SUBLANES
        def body(c, cnt):
            for kk in range(tq // rows16):
                blk = hi_ref[c, kk * rows16:(kk + 1) * rows16, :]
                cnt = cnt + jnp.where(blk >= cand, jnp.ones_like(cnt), jnp.zeros_like(cnt))
            return cnt
        cnt = lax.fori_loop(0, nchunk, body, jnp.zeros((rows16, tq), BF16))
        return jnp.sum(cnt.astype(F32), axis=0, keepdims=True)

    def hi_bit_body(bi, p16):
        cand = p16 + lax.shift_left(jnp.int32(1), 15 - bi)
        raw = cand ^ ((cand >> 31) & jnp.int32(0x7FFF))
        raw = jnp.where((raw > 0) & (raw < BF16_MIN_NORMAL_BITS), BF16_MIN_NORMAL_BITS, raw)
        cand_f = lax.bitcast_convert_type(lax.shift_left(raw, 16), F32)
        total = count_hi(jnp.broadcast_to(cand_f, (2 * SUBLANES, tq)).astype(BF16))
        return jnp.where(total >= n_keep, cand, p16)

    p16 = lax.fori_loop(0, 16, hi_bit_body, jnp.full((1, tq), -2 ** 15, I32))

    def bit_body(bi, t):
        cand = t + lax.shift_left(jnp.int32(1), 15 - bi)
        cand_b = jnp.broadcast_to(cand, (SUBLANES, tq))
        total, = counts((lambda blk, _: blk >= cand_b,))
        return jnp.where(total >= n_keep, cand, t)

    t = lax.fori_loop(0, 16, bit_body, lax.shift_left(p16, 16))
    t_b = jnp.broadcast_to(t, (SUBLANES, tq))
    cnt_gt, cnt_eq = counts((lambda blk, _: blk > t_b, lambda blk, _: blk == t_b))
    need = n_keep - cnt_gt
    any_excess = jnp.max(((cnt_eq > need) & (t > KEY_HALF)).astype(I32))
    t_ref[...] = t_b
    need_ref[...] = jnp.broadcast_to(need, (SUBLANES, tq))
    j_ref[...] = jnp.full(j_ref.shape, seq_len, I32)

    @pl.when(any_excess > 0)
    def _():
        nbits = max(1, (seq_len - 1).bit_length())
        t_b = t_ref[...]
        need = need_ref[0:1, :]

        def jbit(bi, jj):
            cand = jj | lax.shift_left(jnp.int32(1), nbits - 1 - bi)
            cand_b = jnp.broadcast_to(cand, (SUBLANES, tq))
            g, = counts((lambda blk, pos: (blk == t_b) & (pos < cand_b),), with_pos=True)
            return jnp.where(g < need, cand, jj)

        jj = lax.fori_loop(0, nbits, jbit, jnp.zeros((1, tq), I32))
        j_ref[...] = jnp.broadcast_to(jj, (SUBLANES, tq))

    t_row = t_ref[0:1, :]
    j_row = j_ref[0:1, :]

    def bias_chunk(c, carry):
        blk = key_ref[c]
        sel = (blk > KEY_HALF) & ((blk > t_row) | ((blk == t_row) & ((c * tq + key_i) <= j_row)))
        bias_ref[c] = jnp.transpose(jnp.where(sel, 0.0, NEG))
        return carry

    lax.fori_loop(0, nchunk, bias_chunk, 0)

    nkv = N_KV_HEADS * HEAD_DIM
    gq = GROUP * tq

    def group_logits(c, g):
        rows = pl.ds(pl.multiple_of(c * tq, tq), tq)
        kc = kv_ref[rows, g * HEAD_DIM:(g + 1) * HEAD_DIM]
        qg = jnp.concatenate([q_ref[:, h * HEAD_DIM:(h + 1) * HEAD_DIM]
                              for h in range(g * GROUP, (g + 1) * GROUP)], axis=0)
        s = _dot_nt(qg, kc).reshape(GROUP, tq, tq) + bias_ref[c][None]
        return s.reshape(gq, tq)

    def exact_row_max():
        m_ref[...] = jnp.full(m_ref.shape, NEG, F32)

        def max_chunk(c, carry):
            for g in range(N_KV_HEADS):
                s = group_logits(c, g)
                mx = s[:, :LANES]
                for k in range(1, nsub):
                    mx = jnp.maximum(mx, s[:, k * LANES:(k + 1) * LANES])
                m_ref[g] = jnp.maximum(m_ref[g], mx)
            return carry

        lax.fori_loop(0, nchunk, max_chunk, 0)
        for g in range(N_KV_HEADS):
            m_ref[g] = jnp.broadcast_to(jnp.max(m_ref[g], axis=1, keepdims=True), (gq, LANES))

    ones = jnp.ones((tq, HEAD_DIM), BF16)

    def accumulate():
        acc_ref[...] = jnp.zeros_like(acc_ref)

        def pv_chunk(c, carry):
            rows = pl.ds(pl.multiple_of(c * tq, tq), tq)
            for g in range(N_KV_HEADS):
                s = group_logits(c, g)
                m = m_ref[g]
                p = jnp.concatenate([jnp.exp(s[:, k * LANES:(k + 1) * LANES] - m) for k in range(nsub)], axis=1)
                vc = kv_ref[rows, nkv + g * HEAD_DIM:nkv + (g + 1) * HEAD_DIM]
                acc_ref[g] = acc_ref[g] + _dot(p.astype(BF16), jnp.concatenate([vc, ones], axis=1))
            return carry

        lax.fori_loop(0, nchunk, pv_chunk, 0)

    @pl.when(i == 0)
    def _():
        for g in range(N_KV_HEADS):
            def norm_chunk(c, mx):
                kc = kv_ref[pl.ds(pl.multiple_of(c * tq, tq), tq), g * HEAD_DIM:(g + 1) * HEAD_DIM].astype(F32)
                return jnp.maximum(mx, jnp.sum(kc * kc, axis=1, keepdims=True))
            k2 = lax.fori_loop(0, seq_len // tq, norm_chunk, jnp.zeros((tq, 1), F32))
            knorm_ref[g] = jnp.broadcast_to(jnp.sqrt(jnp.max(k2, axis=0, keepdims=True)), (SUBLANES, LANES))

    for h in range(N_HEADS):
        g, r = h // GROUP, h % GROUP
        qh = q_ref[:, h * HEAD_DIM:(h + 1) * HEAD_DIM].astype(F32)
        qn = jnp.sqrt(jnp.sum(qh * qh, axis=1, keepdims=True))
        m_ref[g, r * tq:(r + 1) * tq, :] = qn * knorm_ref[g][0:1, :]
    accumulate()
    denom_min = jnp.min(acc_ref[:, :, HEAD_DIM:HEAD_DIM + 1])

    @pl.when(denom_min < SOFTMAX_DENOM_MIN)
    def _():
        exact_row_max()
        accumulate()

    for h in range(N_HEADS):
        a = acc_ref[h // GROUP, (h % GROUP) * tq:(h % GROUP + 1) * tq, :]
        o_ref[:, h * HEAD_DIM:(h + 1) * HEAD_DIM] = (a[:, :HEAD_DIM] / a[:, HEAD_DIM:]).astype(BF16)


def _attn_prompt(q, kvb, qi, kib, idx, nb):
    t = q.shape[0]
    s = t // nb
    tq = min(ATT_Q, s)
    nq = s // tq
    n_keep = min(TOPK_KEYS, s // 4)
    blk = lambda w: pl.BlockSpec((tq, w), lambda b, i: (b * nq + i, 0))
    seq = lambda w: pl.BlockSpec((s, w), lambda b, i: (b, 0))
    return pl.pallas_call(
        functools.partial(_attn_prompt_kernel, n_keep=n_keep, seq_len=s),
        grid=(nb, nq),
        in_specs=[blk(q.shape[1]), seq(kvb.shape[1]), blk(qi.shape[1]), seq(kib.shape[1]), blk(idx.shape[1])],
        out_specs=blk(N_HEADS * HEAD_DIM),
        out_shape=jax.ShapeDtypeStruct((t, N_HEADS * HEAD_DIM), BF16),
        scratch_shapes=[pltpu.VMEM((nq, tq, tq), I32), pltpu.VMEM((nq, tq, tq), BF16),
                        pltpu.VMEM((nq, tq, tq), F32),
                        pltpu.VMEM((SUBLANES, tq), I32), pltpu.VMEM((SUBLANES, tq), I32),
                        pltpu.VMEM((SUBLANES, tq), I32),
                        pltpu.VMEM((N_KV_HEADS, GROUP * tq, LANES), F32),
                        pltpu.VMEM((N_KV_HEADS, GROUP * tq, 2 * HEAD_DIM), F32),
                        pltpu.VMEM((N_KV_HEADS, SUBLANES, LANES), F32)],
        compiler_params=_params(2),
        name="attn_prompt",
    )(q, kvb, qi, kib, idx)


DECODE_PAGES_PER_ROW = 4


def _decode_scores_kernel(pt_ref, qi_ref, wi_ref, kin_ref, cx_hbm, key_o, knew_o, xbuf, sem, *, n_pages):
    b = pl.program_id(0)
    nb = pl.num_programs(0)
    ppr = DECODE_PAGES_PER_ROW
    nrow = n_pages // ppr

    def page_copy(bb, p, slot):
        return pltpu.make_async_copy(cx_hbm.at[pt_ref[bb, p]], xbuf.at[slot, p], sem.at[slot])

    def start_batch(bb, slot):
        def body(p, carry):
            page_copy(bb, p, slot).start()
            return carry
        lax.fori_loop(0, n_pages, body, 0)

    def wait_batch(bb, slot):
        def body(p, carry):
            page_copy(bb, p, slot).wait()
            return carry
        lax.fori_loop(0, n_pages, body, 0)

    @pl.when(b == 0)
    def _():
        start_batch(0, 0)

    @pl.when(b + 1 < nb)
    def _():
        start_batch(b + 1, (b + 1) % 2)

    slot = b % 2
    wait_batch(b, slot)

    qi8 = qi_ref[0]
    wi8 = wi_ref[0] * IDX_SCALE

    def score_row(r, carry):
        kx = jnp.concatenate([xbuf[slot, r * ppr + k] for k in range(ppr)], axis=1).astype(BF16)
        d = _dot(qi8, kx)
        sc = jnp.sum(jnp.maximum(d, 0.0) * wi8, axis=0, keepdims=True)
        key_o[0, pl.ds(r, 1), :] = _sort_key(sc)
        return carry

    lax.fori_loop(0, nrow, score_row, 0, unroll=4)
    dn = jnp.sum(qi8.astype(F32) * kin_ref[0].astype(F32), axis=1, keepdims=True)
    key_new = _sort_key(jnp.sum(jnp.maximum(dn, 0.0) * wi8, axis=0, keepdims=True))
    knew_o[0] = jnp.broadcast_to(key_new, (1, LANES))


def _decode_scores(page_table, qi8, wi8, kin, cxt):
    db, n_pages = page_table.shape
    nrow = n_pages // DECODE_PAGES_PER_ROW
    w = DECODE_PAGES_PER_ROW * PAGE_SIZE
    per_b = lambda a: pl.BlockSpec((1,) + a.shape[1:], lambda b, pt: (b, 0, 0))
    grid_spec = pltpu.PrefetchScalarGridSpec(
        num_scalar_prefetch=1,
        grid=(db,),
        in_specs=[per_b(qi8), per_b(wi8), per_b(kin), pl.BlockSpec(memory_space=pl.ANY)],
        out_specs=[pl.BlockSpec((1, nrow, w), lambda b, pt: (b, 0, 0)),
                   pl.BlockSpec((1, 1, LANES), lambda b, pt: (b, 0, 0))],
        scratch_shapes=[pltpu.VMEM((2, n_pages, IDX_DIM, PAGE_SIZE), F32), pltpu.SemaphoreType.DMA((2,))],
    )
    return pl.pallas_call(
        functools.partial(_decode_scores_kernel, n_pages=n_pages),
        grid_spec=grid_spec,
        out_shape=[jax.ShapeDtypeStruct((db, nrow, w), I32), jax.ShapeDtypeStruct((db, 1, LANES), I32)],
        compiler_params=_params(1),
        name="decode_scores",
    )(page_table, qi8, wi8, kin, cxt)


def _decode_pick_kernel(keys_ref, knew_ref, pt_ref, meta_o, *, n_keep):
    nb, nrow, w = keys_ref.shape
    past = nrow * w
    n_pages = pt_ref.shape[1]
    keys = keys_ref[...]
    key_new = knew_ref[...][:, :, 0:1]
    col = (lax.broadcasted_iota(I32, keys.shape, 1) * w + lax.broadcasted_iota(I32, keys.shape, 2))

    def total(x):
        c = x.astype(I32)
        part = c[:, :, :LANES]
        for k in range(1, w // LANES):
            part = part + c[:, :, k * LANES:(k + 1) * LANES]
        return jnp.sum(jnp.sum(part, axis=1, keepdims=True), axis=2, keepdims=True)

    def bit_body(bi, t):
        cand = t + lax.shift_left(jnp.int32(1), 31 - bi)
        cnt = total(keys >= cand) + (key_new >= cand).astype(I32)
        return jnp.where(cnt >= n_keep, cand, t)

    t = lax.fori_loop(0, 32, bit_body, jnp.full((nb, 1, 1), INT_MIN, I32))
    need = n_keep - (total(keys > t) + (key_new > t).astype(I32))
    nbits = past.bit_length()

    def jbit(bi, jj):
        cand = jj | lax.shift_left(jnp.int32(1), nbits - 1 - bi)
        g = total((keys == t) & (col < cand)) + ((key_new == t) & (past < cand)).astype(I32)
        return jnp.where(g < need, cand, jj)

    jj = lax.fori_loop(0, nbits, jbit, jnp.zeros((nb, 1, 1), I32))
    sel = (keys > t) | ((keys == t) & (col <= jj))
    sel_new = ((key_new > t) | ((key_new == t) & (past <= jj))).astype(I32)

    rows = nb * nrow
    incl_lane = (lax.broadcasted_iota(I32, (w, w), 0) <= lax.broadcasted_iota(I32, (w, w), 1)).astype(BF16)
    cnt_in_row = _dot(sel.astype(BF16).reshape(rows, w), incl_lane)
    row_tot = cnt_in_row[:, w - 1:w]
    ri = lax.broadcasted_iota(I32, (rows, rows), 0)
    rj = lax.broadcasted_iota(I32, (rows, rows), 1)
    earlier_row_same_seq = ((ri // nrow) == (rj // nrow)) & (rj < ri)
    row_off = _dot(earlier_row_same_seq.astype(BF16),
                   jnp.broadcast_to(row_tot, (rows, LANES)).astype(BF16))[:, 0:1]
    cnt3 = cnt_in_row.astype(BF16).reshape(nb, nrow, w)
    off3 = row_off.reshape(nb, nrow, 1)
    incl3 = (row_off + row_tot).reshape(nb, nrow, 1)
    jl = lax.broadcasted_iota(I32, (1, 1, n_keep), 2).astype(F32)
    row_j = jnp.sum((incl3 <= jl).astype(F32), axis=1, keepdims=True)
    onehot = lax.broadcasted_iota(I32, (nb, nrow, n_keep), 1).astype(F32) == row_j
    off_j = jnp.sum(jnp.where(onehot, off3, 0.0), axis=1, keepdims=True)
    local_rank = jl - off_j
    onehot_b = onehot.astype(BF16)
    page_i = lax.broadcasted_iota(I32, (n_pages, n_keep), 0)
    row_i = lax.broadcasted_iota(I32, (SUBLANES, n_keep), 0)
    for b in range(nb):
        cnt_j = lax.dot_general(cnt3[b], onehot_b[b], (((0,), (0,)), ((), ())),
                                preferred_element_type=F32)
        lane_j = jnp.sum((cnt_j <= local_rank[b]).astype(F32), axis=0, keepdims=True)
        pos = jnp.minimum((row_j[b] * w + lane_j).astype(I32), past - 1)
        phys = jnp.sum(jnp.where(page_i == pos // PAGE_SIZE, pt_ref[b], 0.0), axis=0, keepdims=True)
        tile = (phys.astype(I32) * PAGE_SIZE + pos % PAGE_SIZE) // (SUBLANES // N_KV_HEADS)
        meta_o[b] = jnp.where(row_i == 0, tile,
                              jnp.where(row_i == 1, sel_new[b], pos % (SUBLANES // N_KV_HEADS)))


def _decode_pick(keys, knew, pt_f):
    nb, nrow, w = keys.shape
    n_keep = min(TOPK_KEYS, (nrow * w + 1) // 4)
    full = lambda a: pl.BlockSpec(a.shape, lambda i: (0,) * a.ndim)
    return pl.pallas_call(
        functools.partial(_decode_pick_kernel, n_keep=n_keep),
        grid=(1,),
        in_specs=[full(keys), full(knew), full(pt_f)],
        out_specs=pl.BlockSpec((nb, SUBLANES, n_keep), lambda i: (0, 0, 0)),
        out_shape=jax.ShapeDtypeStruct((nb, SUBLANES, n_keep), I32),
        compiler_params=_params(1),
        name="decode_pick",
    )(keys, knew, pt_f)


def _decode_attend_kernel(tile_ref, q_ref, meta_ref, kn_ref, vn_ref, ck_hbm, cv_hbm, o_ref,
                          kbuf, vbuf, expand_ref, sem, *, n_keep):
    b = pl.program_id(0)
    nb = pl.num_programs(0)
    ncol = n_keep * SUBLANES

    def item_copies(bb, j, slot):
        src = pl.ds(pl.multiple_of(tile_ref[bb, j] * SUBLANES, SUBLANES), SUBLANES)
        dst = pl.ds(pl.multiple_of(j * SUBLANES, SUBLANES), SUBLANES)
        return (pltpu.make_async_copy(ck_hbm.at[src], kbuf.at[slot, dst], sem.at[0, slot]),
                pltpu.make_async_copy(cv_hbm.at[src], vbuf.at[slot, dst], sem.at[1, slot]))

    def start_batch(bb, slot):
        def body(j, carry):
            for prio, cp in enumerate(item_copies(bb, j, slot)):
                cp.start(priority=prio)
            return carry
        lax.fori_loop(0, n_keep, body, 0)

    def wait_batch(slot):
        pltpu.make_async_copy(ck_hbm.at[pl.ds(0, ncol)], kbuf.at[slot], sem.at[0, slot]).wait()
        pltpu.make_async_copy(cv_hbm.at[pl.ds(0, ncol)], vbuf.at[slot], sem.at[1, slot]).wait()

    @pl.when(b == 0)
    def _():
        start_batch(0, 0)
        item_of_col = lax.broadcasted_iota(I32, (n_keep, ncol), 1) // SUBLANES
        expand_ref[...] = (item_of_col == lax.broadcasted_iota(I32, (n_keep, ncol), 0)).astype(BF16)

    @pl.when(b + 1 < nb)
    def _():
        start_batch(b + 1, (b + 1) % 2)

    slot = b % 2
    wait_batch(slot)

    q8 = q_ref[0]
    meta = meta_ref[0]
    new_kept = meta[1:2, 0:1] > 0
    sub_col = _dot(meta.astype(BF16), expand_ref[...])[2:3, :]
    head_i = lax.broadcasted_iota(I32, (N_HEADS, ncol), 0)
    col_i = lax.broadcasted_iota(I32, (N_HEADS, ncol), 1)
    want = sub_col * N_KV_HEADS + (head_i // GROUP).astype(F32)
    is_new_item = new_kept & (col_i // SUBLANES == n_keep - 1)
    ok = ((col_i % SUBLANES).astype(F32) == want) & jnp.logical_not(is_new_item)

    s = jnp.where(ok, _dot_nt(q8, kbuf[slot].astype(BF16)), NEG)
    s_new = jnp.sum(q8.astype(F32) * kn_ref[0].astype(F32), axis=1, keepdims=True)
    s_new = jnp.where(new_kept, s_new, NEG)
    m = jnp.maximum(jnp.max(s, axis=1, keepdims=True), s_new)
    p = jnp.exp(s - m)
    p_new = jnp.exp(s_new - m)
    l = jnp.sum(p, axis=1, keepdims=True) + p_new
    acc = _dot(p.astype(BF16), vbuf[slot].astype(BF16)) + p_new.astype(BF16).astype(F32) * vn_ref[0].astype(F32)
    o_ref[0] = (acc / l).astype(BF16)


def _decode_attend(tiles, q8, meta, kn8, vn8, ck2, cv2):
    db, n_keep = tiles.shape
    per_b = lambda a: pl.BlockSpec((1,) + a.shape[1:], lambda b, tl: (b, 0, 0))
    any_spec = pl.BlockSpec(memory_space=pl.ANY)
    ncol = n_keep * SUBLANES
    grid_spec = pltpu.PrefetchScalarGridSpec(
        num_scalar_prefetch=1,
        grid=(db,),
        in_specs=[per_b(q8), per_b(meta), per_b(kn8), per_b(vn8), any_spec, any_spec],
        out_specs=pl.BlockSpec((1, N_HEADS, HEAD_DIM), lambda b, tl: (b, 0, 0)),
        scratch_shapes=[pltpu.VMEM((2, ncol, HEAD_DIM), F32), pltpu.VMEM((2, ncol, HEAD_DIM), F32),
                        pltpu.VMEM((n_keep, ncol), BF16), pltpu.SemaphoreType.DMA((2, 2))],
    )
    return pl.pallas_call(
        functools.partial(_decode_attend_kernel, n_keep=n_keep),
        grid_spec=grid_spec,
        out_shape=jax.ShapeDtypeStruct((db, N_HEADS, HEAD_DIM), BF16),
        compiler_params=_params(1),
        name="decode_attend",
    )(tiles, q8, meta, kn8, vn8, ck2, cv2)


def _layer_norm(r, g, b):
    mu = jnp.mean(r, axis=-1, keepdims=True)
    d = r - mu
    var = jnp.mean(d * d, axis=-1, keepdims=True)
    return d * lax.rsqrt(var + LN_EPS) * g + b


def _route(logits_t, rbias):
    n_exp, tm = logits_t.shape
    epg = n_exp // N_GROUPS
    s = jax.nn.sigmoid(logits_t)
    sb = s + rbias
    ie = lax.broadcasted_iota(I32, (epg, tm), 0)
    gs_rows = []
    for g in range(N_GROUPS):
        blk = sb[g * epg:(g + 1) * epg, :]
        m1 = jnp.max(blk, axis=0, keepdims=True)
        i1 = jnp.min(jnp.where(blk == m1, ie, epg), axis=0, keepdims=True)
        m2 = jnp.max(jnp.where(ie == i1, -jnp.inf, blk), axis=0, keepdims=True)
        gs_rows.append(m1 + m2)
    picked = [jnp.zeros((1, tm), jnp.bool_) for _ in range(N_GROUPS)]
    cur = list(gs_rows)
    for _ in range(TOPK_GROUPS):
        mx = cur[0]
        for g in range(1, N_GROUPS):
            mx = jnp.maximum(mx, cur[g])
        found = jnp.zeros((1, tm), jnp.bool_)
        for g in range(N_GROUPS):
            hit = (cur[g] == mx) & jnp.logical_not(found)
            found = found | hit
            picked[g] = picked[g] | hit
            cur[g] = jnp.where(hit, -jnp.inf, cur[g])
    masked = jnp.concatenate(
        [jnp.where(picked[g], sb[g * epg:(g + 1) * epg, :], NEG) for g in range(N_GROUPS)], axis=0)
    iall = lax.broadcasted_iota(I32, (n_exp, tm), 0)
    e_rows, w_rows = [], []
    for _ in range(TOP_K):
        mx = jnp.max(masked, axis=0, keepdims=True)
        ix = jnp.min(jnp.where(masked == mx, iall, n_exp), axis=0, keepdims=True)
        hit = iall == ix
        w_rows.append(jnp.sum(jnp.where(hit, s, 0.0), axis=0, keepdims=True))
        e_rows.append(ix)
        masked = jnp.where(hit, -jnp.inf, masked)
    wsum = w_rows[0]
    for w in w_rows[1:]:
        wsum = wsum + w
    gates = [w / wsum * ROUTED_SCALE for w in w_rows]
    return jnp.concatenate(e_rows, axis=0), jnp.concatenate(gates, axis=0)


def _post_attn_kernel(xp, xs, ap, as_, cp, cs, gap, gas, gcp, gcs, woa, woc, wo, g1, b1, rwt_hi, rwt_lo, rb,
                      h2_o, e_o, gate_o, *, n_prompt_tiles):
    i = pl.program_id(0)
    is_p = i < n_prompt_tiles
    pick = lambda a, b: jnp.where(is_p, a[...], b[...])
    a = _dot(pick(ap, as_), woa[...])
    c = _dot(pick(cp, cs), woc[...])
    merged = pick(gap, gas).astype(F32) * a + pick(gcp, gcs).astype(F32) * c
    r = ALPHA * pick(xp, xs) + _dot(merged.astype(BF16), wo[...])
    h = _layer_norm(r, g1[...], b1[...])
    tm = h.shape[0]
    for k in range(D_MODEL // LANES):
        h2_o[pl.ds(k, tm, stride=SUBLANES), :] = h[:, k * LANES:(k + 1) * LANES]
    h_hi = h.astype(BF16)
    h_lo = (h - h_hi.astype(F32)).astype(BF16)
    logits_t = _dot_nt(rwt_hi[...], h_hi) + (_dot_nt(rwt_hi[...], h_lo) + _dot_nt(rwt_lo[...], h_hi))
    e_idx, gates = _route(logits_t, rb[...])
    e_o[...] = e_idx
    gate_o[...] = gates


def _post_attn(x_p, x_s, attn_p, attn_s, conv_p, conv_s, ga_p, ga_s, gc_p, gc_s,
               woa, woc, wo, g1, b1, rwt, rb):
    rwt_hi = rwt.astype(BF16)
    rwt_lo = (rwt - rwt_hi.astype(F32)).astype(BF16)
    tp, ts = x_p.shape[0], x_s.shape[0]
    tm = TOKEN_TILE
    npt, nst = tp // tm, ts // tm
    n_tok = tp + ts
    p_spec = pl.BlockSpec((tm, D_MODEL), lambda i: (jnp.minimum(i, npt - 1), 0))
    s_spec = pl.BlockSpec((tm, D_MODEL), lambda i: (jnp.maximum(i - npt, 0), 0))
    consts = [woa, woc, wo, g1, b1, rwt_hi, rwt_lo, rb]
    return pl.pallas_call(
        functools.partial(_post_attn_kernel, n_prompt_tiles=npt),
        grid=(npt + nst,),
        in_specs=[p_spec, s_spec] * 5 + [_const_spec(c.shape) for c in consts],
        out_specs=[pl.BlockSpec((tm * SUBLANES, LANES), lambda i: (i, 0)),
                   pl.BlockSpec((TOP_K, tm), lambda i: (0, i)),
                   pl.BlockSpec((TOP_K, tm), lambda i: (0, i))],
        out_shape=[jax.ShapeDtypeStruct((n_tok * SUBLANES, LANES), F32),
                   jax.ShapeDtypeStruct((TOP_K, n_tok), I32),
                   jax.ShapeDtypeStruct((TOP_K, n_tok), F32)],
        compiler_params=_params(1),
        name="post_attn",
    )(x_p, x_s, attn_p, attn_s, conv_p, conv_s, ga_p, ga_s, gc_p, gc_s, *consts)


def _rank_kernel(e_ref, rank_o, cnt_o, carry_ref):
    i = pl.program_id(0)

    @pl.when(i == 0)
    def _():
        carry_ref[...] = jnp.zeros_like(carry_ref)

    n_exp = carry_ref.shape[0]
    e = e_ref[...]
    tk = e.shape[1]
    ie = lax.broadcasted_iota(I32, (n_exp, tk), 0)
    onehot = jnp.zeros((n_exp, tk), F32)
    for j in range(TOP_K):
        onehot = onehot + (ie == e[j:j + 1, :]).astype(F32)
    before = (lax.broadcasted_iota(I32, (tk, tk), 0) < lax.broadcasted_iota(I32, (tk, tk), 1)).astype(BF16)
    prefix = _dot(onehot.astype(BF16), before) + carry_ref[:, 0:1]
    rows = [jnp.sum(jnp.where(ie == e[j:j + 1, :], prefix, 0.0), axis=0, keepdims=True) for j in range(TOP_K)]
    rank_o[...] = jnp.concatenate(rows, axis=0).astype(I32)
    carry_ref[...] = carry_ref[...] + jnp.sum(onehot, axis=1, keepdims=True)
    cnt_o[...] = carry_ref[...].astype(I32)


def _rank(e_t, n_exp):
    n_tok = e_t.shape[1]
    tk = TOKEN_TILE
    return pl.pallas_call(
        _rank_kernel,
        grid=(n_tok // tk,),
        in_specs=[pl.BlockSpec((TOP_K, tk), lambda i: (0, i))],
        out_specs=[pl.BlockSpec((TOP_K, tk), lambda i: (0, i)),
                   pl.BlockSpec((n_exp, LANES), lambda i: (0, 0))],
        out_shape=[jax.ShapeDtypeStruct((TOP_K, n_tok), I32), jax.ShapeDtypeStruct((n_exp, LANES), I32)],
        scratch_shapes=[pltpu.VMEM((n_exp, LANES), F32)],
        compiler_params=_params(1),
        name="moe_rank",
    )(e_t)


def _slot_kernel(e_ref, rank_ref, pstart_ref, slot_o):
    e = e_ref[...]
    n_exp = pstart_ref.shape[0]
    tk = e.shape[1]
    ie = lax.broadcasted_iota(I32, (n_exp, tk), 0)
    ps = pstart_ref[:, 0:1]
    rows = [jnp.sum(jnp.where(ie == e[j:j + 1, :], ps, 0.0), axis=0, keepdims=True) for j in range(TOP_K)]
    slot_o[...] = rank_ref[...] + jnp.concatenate(rows, axis=0).astype(I32)


def _slots(e_t, rank_t, pstart_f):
    n_tok = e_t.shape[1]
    tk = TOKEN_TILE
    spec = pl.BlockSpec((TOP_K, tk), lambda i: (0, i))
    return pl.pallas_call(
        _slot_kernel,
        grid=(n_tok // tk,),
        in_specs=[spec, spec, _const_spec(pstart_f.shape)],
        out_specs=spec,
        out_shape=jax.ShapeDtypeStruct((TOP_K, n_tok), I32),
        compiler_params=_params(1),
        name="moe_slots",
    )(e_t, rank_t, pstart_f)


def _dispatch_kernel(pend_ref, pad_ref, slot_ref, h2_ref, xs_hbm, zbuf, sem, zsem, *, n_exp):
    i = pl.program_id(0)
    td = slot_ref.shape[0] // TOP_K
    blk_rows = MOE_BLOCK * SUBLANES
    pieces = [1 << b for b in reversed(range((MOE_BLOCK - 1).bit_length()))]

    def pad_copies(e, act):
        pad = pad_ref[e]
        off = pend_ref[e] - pad
        for piece in pieces:
            dst = pl.ds(pl.multiple_of(off * SUBLANES, SUBLANES), piece * SUBLANES)
            cp = pltpu.make_async_copy(zbuf.at[pl.ds(0, piece * SUBLANES)], xs_hbm.at[dst], zsem)

            @pl.when((pad & piece) != 0)
            def _():
                act(cp)

            off = off + (pad & piece)

    @pl.when(i == 0)
    def _():
        zbuf[...] = jnp.zeros_like(zbuf)

        def start(e, carry):
            pad_copies(e, lambda cp: cp.start())
            return carry

        def wait(e, carry):
            pad_copies(e, lambda cp: cp.wait())
            return carry

        lax.fori_loop(0, n_exp, start, 0)
        lax.fori_loop(0, n_exp, wait, 0)

        def tail_copy(b):
            return pltpu.make_async_copy(zbuf, xs_hbm.at[pl.ds(pl.multiple_of(b * blk_rows, blk_rows), blk_rows)], zsem)

        def tail_start(b, carry):
            tail_copy(b).start()
            return carry

        def tail_wait(b, carry):
            tail_copy(b).wait()
            return carry

        first_unused = pend_ref[n_exp - 1] // MOE_BLOCK
        lax.fori_loop(first_unused, xs_hbm.shape[0] // blk_rows, tail_start, 0)
        lax.fori_loop(first_unused, xs_hbm.shape[0] // blk_rows, tail_wait, 0)

    def row_copy(t, j):
        src = pl.ds(pl.multiple_of(t * SUBLANES, SUBLANES), SUBLANES)
        dst = pl.ds(pl.multiple_of(slot_ref[t * TOP_K + j] * SUBLANES, SUBLANES), SUBLANES)
        return pltpu.make_async_copy(h2_ref.at[src], xs_hbm.at[dst], sem)

    def start_tok(t, carry):
        for j in range(TOP_K):
            row_copy(t, j).start(priority=j % 2)
        return carry

    lax.fori_loop(0, td, start_tok, 0)
    for j in range(TOP_K):
        pltpu.make_async_copy(h2_ref, xs_hbm.at[pl.ds(0, td * SUBLANES)], sem).wait()


def _dispatch(pend, pad, slot_flat, h2, m_pad):
    n_tok = slot_flat.shape[0] // TOP_K
    td = TOKEN_TILE
    n_exp = pend.shape[0]
    grid_spec = pltpu.PrefetchScalarGridSpec(
        num_scalar_prefetch=2,
        grid=(n_tok // td,),
        in_specs=[pl.BlockSpec((td * TOP_K,), lambda i, *_: (i,), memory_space=pltpu.SMEM),
                  pl.BlockSpec((td * SUBLANES, LANES), lambda i, *_: (i, 0))],
        out_specs=pl.BlockSpec(memory_space=pl.ANY),
        scratch_shapes=[pltpu.VMEM((MOE_BLOCK * SUBLANES, LANES), F32),
                        pltpu.SemaphoreType.DMA(()), pltpu.SemaphoreType.DMA(())],
    )
    return pl.pallas_call(
        functools.partial(_dispatch_kernel, n_exp=n_exp),
        grid_spec=grid_spec,
        out_shape=jax.ShapeDtypeStruct((m_pad * SUBLANES, LANES), F32),
        compiler_params=_params(1),
        name="moe_dispatch",
    )(pend, pad, slot_flat, h2)


def _tile_rows(ref, n):
    return jnp.concatenate([ref[pl.ds(k, n, stride=SUBLANES), :] for k in range(D_MODEL // LANES)], axis=1)


PLAN_EXPERT, PLAN_SLOT, PLAN_NEXT, PLAN_HAS_NEXT = range(4)


X_RING = 3


def _expert_kernel(plan_ref, nused_ref, x_hbm, wg_hbm, wu_hbm, wd_hbm, y_ref, xbuf, wg_f, wu_f, wd_f, wgu_b, wd_b,
                   sem, xsem):
    i = pl.program_id(0)
    d_exp = wd_b.shape[0]
    nused = nused_ref[0]
    blk_rows = MOE_BLOCK * SUBLANES

    def x_copy(b):
        src = pl.ds(pl.multiple_of(b * blk_rows, blk_rows), blk_rows)
        return pltpu.make_async_copy(x_hbm.at[src], xbuf.at[b % X_RING], xsem.at[b % X_RING])

    @pl.when(i == 0)
    def _():
        x_copy(0).start()

        @pl.when(1 < nused)
        def _():
            x_copy(1).start()

    @pl.when(i + 2 < nused)
    def _():
        x_copy(i + 2).start()

    def weight_copies(e, s):
        return (pltpu.make_async_copy(wg_hbm.at[e], wg_f.at[s], sem.at[s]),
                pltpu.make_async_copy(wu_hbm.at[e], wu_f.at[s], sem.at[s]),
                pltpu.make_async_copy(wd_hbm.at[e], wd_f.at[s], sem.at[s]))

    @pl.when(i < nused_ref[0])
    def _():
        e = plan_ref[PLAN_EXPERT, i]
        changed = (i == 0) | (e != plan_ref[PLAN_EXPERT, jnp.maximum(i - 1, 0)])

        @pl.when(changed)
        def _():
            s = plan_ref[PLAN_SLOT, i]

            @pl.when(i == 0)
            def _():
                for cp in weight_copies(e, s):
                    cp.start()

            for cp in weight_copies(e, s):
                cp.wait()

            @pl.when(plan_ref[PLAN_HAS_NEXT, i] > 0)
            def _():
                for cp in weight_copies(plan_ref[PLAN_NEXT, i], 1 - s):
                    cp.start()

            wgu_b[:, :d_exp] = wg_f[s].astype(BF16)
            wgu_b[:, d_exp:] = wu_f[s].astype(BF16)
            wd_b[...] = wd_f[s].astype(BF16)

        x_copy(i).wait()
        x = _tile_rows(xbuf.at[i % X_RING], MOE_BLOCK).astype(BF16)
        gu = _dot(x, wgu_b[...])
        hh = jax.nn.silu(gu[:, :d_exp]) * gu[:, d_exp:]
        y = _dot(hh.astype(BF16), wd_b[...])
        for k in range(D_MODEL // LANES):
            y_ref[pl.ds(k, MOE_BLOCK, stride=SUBLANES), :] = y[:, k * LANES:(k + 1) * LANES]

    @pl.when(i >= nused_ref[0])
    def _():
        y_ref[...] = jnp.zeros_like(y_ref)


def _expert_plan(block_e, nused):
    n_blocks = block_e.shape[0]
    idx = jnp.arange(n_blocks, dtype=I32)
    changed = (idx == 0) | (block_e != jnp.roll(block_e, 1))
    slot = (jnp.cumsum(changed.astype(I32)) - 1) % 2
    change_pos = jnp.where(changed & (idx < nused), idx, n_blocks)
    next_pos = jnp.concatenate([lax.cummin(change_pos[::-1])[::-1][1:], jnp.full((1,), n_blocks, I32)])
    has_next = (next_pos < n_blocks).astype(I32)
    nxt = block_e[jnp.minimum(next_pos, n_blocks - 1)]
    return jnp.stack([block_e, slot, nxt, has_next]).astype(I32)


def _experts(block_e, nused, xs, w_gate, w_up, w_down):
    n_blocks = block_e.shape[0]
    d_exp = w_gate.shape[2]
    rows = MOE_BLOCK * SUBLANES
    any_spec = pl.BlockSpec(memory_space=pl.ANY)
    grid_spec = pltpu.PrefetchScalarGridSpec(
        num_scalar_prefetch=2,
        grid=(n_blocks,),
        in_specs=[any_spec, any_spec, any_spec, any_spec],
        out_specs=pl.BlockSpec((rows, LANES), lambda i, plan, nu: (i, 0)),
        scratch_shapes=[pltpu.VMEM((X_RING, rows, LANES), F32),
                        pltpu.VMEM((2, D_MODEL, d_exp), F32), pltpu.VMEM((2, D_MODEL, d_exp), F32),
                        pltpu.VMEM((2, d_exp, D_MODEL), F32),
                        pltpu.VMEM((D_MODEL, 2 * d_exp), BF16), pltpu.VMEM((d_exp, D_MODEL), BF16),
                        pltpu.SemaphoreType.DMA((2,)), pltpu.SemaphoreType.DMA((X_RING,))],
    )
    return pl.pallas_call(
        _expert_kernel,
        grid_spec=grid_spec,
        out_shape=jax.ShapeDtypeStruct(xs.shape, F32),
        compiler_params=_params(1),
        name="moe_experts",
    )(_expert_plan(block_e, nused[0]), nused, xs, w_gate, w_up, w_down)


def _combine_kernel(slot_ref, slot_next_ref, gate_ref, h2_ref, ys_hbm, shg, shu, shd, g2, b2, yp_o, ys_o,
                    buf, base_ref, y_ref, sem, *, n_prompt_tiles):
    i = pl.program_id(0)
    n = pl.num_programs(0)
    tc = gate_ref.shape[0]

    def start_tile(s_ref, bslot):
        def body(t, carry):
            dst = pl.ds(pl.multiple_of(t * SUBLANES, SUBLANES), SUBLANES)
            for j in range(TOP_K):
                src = pl.ds(pl.multiple_of(s_ref[t * TOP_K + j] * SUBLANES, SUBLANES), SUBLANES)
                pltpu.make_async_copy(ys_hbm.at[src], buf.at[bslot, j, dst], sem.at[bslot]).start(priority=j % 2)
            return carry
        lax.fori_loop(0, tc, body, 0)

    def wait_tile(bslot):
        for j in range(TOP_K):
            pltpu.make_async_copy(ys_hbm.at[pl.ds(0, tc * SUBLANES)], buf.at[bslot, j], sem.at[bslot]).wait()

    @pl.when(i == 0)
    def _():
        start_tile(slot_ref, 0)

    cur = i % 2
    h = _tile_rows(h2_ref, tc)
    hb = h.astype(BF16)
    shared = _dot((jax.nn.silu(_dot(hb, shg[...])) * _dot(hb, shu[...])).astype(BF16), shd[...])
    base_ref[...] = ALPHA * h + shared
    wait_tile(cur)
    gain, bias = g2[...], b2[...]
    grp = SUBLANES

    def finish_group(g):
        r0 = pl.multiple_of(g * grp, grp)
        t0 = pl.multiple_of(g * grp * SUBLANES, grp * SUBLANES)
        routed = jnp.zeros((grp, D_MODEL), F32)
        for j in range(TOP_K):
            rows = jnp.concatenate([buf[cur, j, pl.ds(t0 + k, grp, stride=SUBLANES), :]
                                    for k in range(D_MODEL // LANES)], axis=1)
            routed = routed + rows * gate_ref[pl.ds(r0, grp), j:j + 1]
        y_ref[pl.ds(r0, grp), :] = base_ref[pl.ds(r0, grp), :] + routed

    @pl.when(i + 1 < n)
    def _():
        nxt = (i + 1) % 2

        def body(g, carry):
            for tt in range(grp):
                t = g * grp + tt
                dst = pl.ds(pl.multiple_of(t * SUBLANES, SUBLANES), SUBLANES)
                for j in range(TOP_K):
                    src = pl.ds(pl.multiple_of(slot_next_ref[t * TOP_K + j] * SUBLANES, SUBLANES), SUBLANES)
                    pltpu.make_async_copy(ys_hbm.at[src], buf.at[nxt, j, dst], sem.at[nxt]).start(priority=j % 2)
            finish_group(g)
            return carry

        lax.fori_loop(0, tc // grp, body, 0)

    @pl.when(i + 1 >= n)
    def _():
        def body(g, carry):
            finish_group(g)
            return carry

        lax.fori_loop(0, tc // grp, body, 0)

    y = _layer_norm(y_ref[...], gain, bias)

    @pl.when(i < n_prompt_tiles)
    def _():
        yp_o[...] = y

    @pl.when(i >= n_prompt_tiles)
    def _():
        ys_o[...] = y


def _combine(slot_flat, gates, h2, ys, shg, shu, shd, g2, b2, n_prompt, n_decode):
    tc = TOKEN_TILE
    npt, nst = n_prompt // tc, n_decode // tc
    consts = [shg, shu, shd, g2, b2]
    return pl.pallas_call(
        functools.partial(_combine_kernel, n_prompt_tiles=npt),
        grid=(npt + nst,),
        in_specs=[pl.BlockSpec((tc * TOP_K,), lambda i: (i,), memory_space=pltpu.SMEM),
                  pl.BlockSpec((tc * TOP_K,), lambda i: (jnp.minimum(i + 1, npt + nst - 1),),
                               memory_space=pltpu.SMEM),
                  pl.BlockSpec((tc, TOP_K), lambda i: (i, 0)),
                  pl.BlockSpec((tc * SUBLANES, LANES), lambda i: (i, 0)),
                  pl.BlockSpec(memory_space=pl.ANY)] + [_const_spec(c.shape) for c in consts],
        out_specs=[pl.BlockSpec((tc, D_MODEL), lambda i: (jnp.minimum(i, npt - 1), 0)),
                   pl.BlockSpec((tc, D_MODEL), lambda i: (jnp.maximum(i - npt, 0), 0))],
        out_shape=[jax.ShapeDtypeStruct((n_prompt, D_MODEL), F32),
                   jax.ShapeDtypeStruct((n_decode, D_MODEL), F32)],
        scratch_shapes=[pltpu.VMEM((2, TOP_K, tc * SUBLANES, LANES), F32), pltpu.VMEM((tc, D_MODEL), F32),
                        pltpu.VMEM((tc, D_MODEL), F32), pltpu.SemaphoreType.DMA((2,))],
        compiler_params=_params(1),
        name="moe_combine",
    )(slot_flat, slot_flat, gates, h2, ys, *consts)


def _pad_rows(a, n):
    return jnp.pad(a, ((0, n - a.shape[0]),) + ((0, 0),) * (a.ndim - 1))


def kernel(x_prompt, x_sample, cache_k, cache_v, cache_kidx, page_table, state_conv, w_in, conv_w, w_o_attn,
           w_o_conv, w_o, ln1_g, ln1_b, router_w, router_bias, moe_w_gate, moe_w_up, moe_w_down,
           shared_w_gate, shared_w_up, shared_w_down, ln2_g, ln2_b):
    nb, seq, _ = x_prompt.shape
    db = x_sample.shape[0]
    n_pool = cache_k.shape[1]
    n_exp = router_w.shape[-1]
    tp = nb * seq
    ts = TOKEN_TILE
    nkv = N_KV_HEADS * HEAD_DIM

    weights = _prep_in_weights(w_in[0])
    cw = conv_w[0]

    xp2 = x_prompt.reshape(tp, D_MODEL)
    (q_p, k_p, v_p, kvb_p, qi_p, idx_p, kib_p, conv_p, ga_p, gc_p, ulast_p) = _in_proj_prompt(xp2, weights, cw, nb)
    xs2 = _pad_rows(x_sample.reshape(db, D_MODEL), ts)
    prev2 = _pad_rows(state_conv[0, :, 0, :], ts)
    prev1 = _pad_rows(state_conv[0, :, 1, :], ts)
    (q_s, k_s, v_s, _, qi_s, idx_s, _, conv_s, ga_s, gc_s, u_s) = _in_proj_decode(xs2, prev2, prev1, weights, cw)

    attn_p = _attn_prompt(q_p, kvb_p, qi_p, kib_p, idx_p, nb)
    q8 = q_s[:db].reshape(db, N_HEADS, HEAD_DIM)
    qi8 = qi_s[:db].reshape(db, IDX_HEADS, LANES)[:, :, :IDX_DIM]
    wi8 = idx_s[:db, IDX_DIM:IDX_DIM + IDX_HEADS].reshape(db, IDX_HEADS, 1)
    kin = idx_s[:db, :IDX_DIM].astype(BF16).reshape(db, 1, IDX_DIM)
    k_new = k_s[:db * N_KV_HEADS].reshape(db, N_KV_HEADS, HEAD_DIM)
    v_new = v_s[:db * N_KV_HEADS].reshape(db, N_KV_HEADS, HEAD_DIM)
    kn8 = jnp.repeat(k_new, GROUP, axis=1).astype(BF16)
    vn8 = jnp.repeat(v_new, GROUP, axis=1).astype(BF16)
    ck2 = cache_k[0].reshape(n_pool * PAGE_SIZE * N_KV_HEADS, HEAD_DIM)
    cv2 = cache_v[0].reshape(n_pool * PAGE_SIZE * N_KV_HEADS, HEAD_DIM)
    keys_s, knew_s = _decode_scores(page_table, qi8, wi8, kin, jnp.swapaxes(cache_kidx[0], 1, 2))
    meta = _decode_pick(keys_s, knew_s, page_table.astype(F32)[:, :, None])
    attn_s8 = _decode_attend(meta[:, 0, :], q8, meta, kn8, vn8, ck2, cv2)
    attn_s = _pad_rows(attn_s8.reshape(db, N_HEADS * HEAD_DIM), ts)

    h2, e_t, gate_t = _post_attn(
        xp2, xs2, attn_p, attn_s, conv_p, conv_s, ga_p, ga_s, gc_p, gc_s,
        w_o_attn[0].astype(BF16), w_o_conv[0].astype(BF16), w_o[0].astype(BF16),
        ln1_g[0].reshape(1, D_MODEL), ln1_b[0].reshape(1, D_MODEL),
        router_w[0].T, router_bias[0].reshape(n_exp, 1))

    n_tok = tp + ts
    rank_t, cnt = _rank(e_t, n_exp)
    counts = cnt[:, 0]
    padded = (counts + MOE_BLOCK - 1) // MOE_BLOCK * MOE_BLOCK
    pend = jnp.cumsum(padded)
    pstart = pend - padded
    n_blocks = (n_tok * TOP_K + n_exp * (MOE_BLOCK - 1) + MOE_BLOCK - 1) // MOE_BLOCK
    nused = (pend[-1] // MOE_BLOCK).astype(I32)
    blk = jnp.minimum(jnp.arange(n_blocks, dtype=I32), nused - 1)
    block_e = jnp.minimum(jnp.sum((pend[None, :] <= (blk * MOE_BLOCK)[:, None]).astype(I32), axis=1), n_exp - 1)
    slot_t = _slots(e_t, rank_t, jnp.broadcast_to(pstart.astype(F32)[:, None], (n_exp, LANES)))
    slot_flat = slot_t.T.reshape(-1)
    xs = _dispatch(pend.astype(I32), (padded - counts).astype(I32), slot_flat, h2, n_blocks * MOE_BLOCK)
    ys = _experts(block_e, nused.reshape(1), xs, moe_w_gate[0], moe_w_up[0], moe_w_down[0])
    y_p, y_s = _combine(slot_flat, gate_t.T, h2, ys,
                        shared_w_gate[0].astype(BF16), shared_w_up[0].astype(BF16),
                        shared_w_down[0].astype(BF16),
                        ln2_g[0].reshape(1, D_MODEL), ln2_b[0].reshape(1, D_MODEL), tp, ts)

    conv_sample = jnp.stack([state_conv[0, :, 1, :], u_s[:db]], axis=1)[None]
    return (y_p.reshape(nb, seq, D_MODEL),
            y_s[:db].reshape(db, 1, D_MODEL),
            k_p.reshape(1, nb, seq, N_KV_HEADS, HEAD_DIM),
            v_p.reshape(1, nb, seq, N_KV_HEADS, HEAD_DIM),
            idx_p[:, :IDX_DIM].reshape(1, nb, seq, IDX_DIM),
            ulast_p[:, SUBLANES - (CONV_W - 1):, :][None],
            k_new.reshape(1, db, 1, N_KV_HEADS, HEAD_DIM),
            v_new.reshape(1, db, 1, N_KV_HEADS, HEAD_DIM),
            idx_s[:db, :IDX_DIM].reshape(1, db, 1, IDX_DIM),
            conv_sample)
```
